```python
import math
import jax, jax.numpy as jnp
from jax import lax
import numpy as np

D_MODEL = 1024
BATCH = 4
SEQ = 4096
DEPTH = 2
DEC_BATCH = 128
DEC_SEQ = 4
PAST_LEN = 2048
PAGE_SIZE = 128

N_HEADS = 16
N_KV_HEADS = 4
GROUP = N_HEADS // N_KV_HEADS
HEAD_DIM = 64
Q_DIM = N_HEADS * HEAD_DIM
KV_DIM = N_KV_HEADS * HEAD_DIM
IN_COLS = Q_DIM + 6 * KV_DIM + 3 * N_HEADS
CMP_BLK = 32
SEL_BLK = 64
N_SEL = 16
WINDOW = 512
Q_CHUNK = 64
N_BUCKETS = 32
MAX_DISTANCE = 128
POOL_WINDOWS = (2, 4, 8, 16)
N_POOL_GROUPS = len(POOL_WINDOWS)
POOL_GROUP_DIM = D_MODEL // N_POOL_GROUPS
POOL_STATE = max(POOL_WINDOWS) - 1
D_FF = ((8 * D_MODEL + 3 * 256 - 1) // (3 * 256)) * 256
N_NSA_LAYERS = (DEPTH + 1) // 2
N_POOL_LAYERS = DEPTH // 2
RMS_EPS = 1e-6
NEG = -1e30
BIG = 1e9

kernel_name = 'nsa_pool_hybrid_step'


def rmsnorm(x, g):
    xf = x.astype(jnp.float32)
    y = xf * lax.rsqrt(jnp.mean(xf * xf, axis=-1, keepdims=True) + RMS_EPS)
    return (y * g.astype(jnp.float32)).astype(x.dtype)


def ada_modulate(c, w_ada, b_ada):
    m = jax.nn.silu(c) @ w_ada + b_ada
    return jnp.split(m[:, None, :], 6, axis=-1)


def rel_bucket(dist):
    d = jnp.maximum(dist, 0)
    max_exact = N_BUCKETS // 2
    large = max_exact + (jnp.log(jnp.maximum(d, 1).astype(jnp.float32) / max_exact)
                         / math.log(MAX_DISTANCE / max_exact) * (N_BUCKETS - max_exact)).astype(jnp.int32)
    large = jnp.minimum(large, N_BUCKETS - 1)
    return jnp.where(d < max_exact, d, large)


def masked_softmax(logits, mask):
    l = jnp.where(mask, logits.astype(jnp.float32), NEG)
    m = jnp.max(l, axis=-1, keepdims=True)
    p = jnp.where(mask, jnp.exp(l - m), 0.0)
    return p / jnp.maximum(jnp.sum(p, axis=-1, keepdims=True), 1e-30)


def nsa_project(h, w_in):
    b, t = h.shape[:2]
    proj = h @ w_in
    parts = jnp.split(proj, [Q_DIM + i * KV_DIM for i in range(7)], axis=-1)
    q = parts[0].reshape(b, t, N_KV_HEADS, GROUP, HEAD_DIM)
    kvs = [p.reshape(b, t, N_KV_HEADS, HEAD_DIM) for p in parts[1:7]]
    gates = parts[7].reshape(b, t, N_KV_HEADS, GROUP, 3)
    return q, kvs, gates


def compress_blocks(k_full, w_c, pe_c):
    b, L = k_full.shape[:2]
    nc = L // CMP_BLK
    blk = k_full[:, :nc * CMP_BLK].reshape(b, nc, CMP_BLK, N_KV_HEADS, HEAD_DIM) + pe_c[None, None, :, None, :]
    return jnp.einsum('bnjkd,jde->bnke', blk, w_c)


def select_blocks(k_full):
    b, L = k_full.shape[:2]
    ns = -(-L // SEL_BLK)
    kp = jnp.pad(k_full, ((0, 0), (0, ns * SEL_BLK - L), (0, 0), (0, 0)))
    return kp.reshape(b, ns, SEL_BLK, N_KV_HEADS, HEAD_DIM).transpose(0, 3, 1, 2, 4)


def nsa_attend(q, gates, qpos, k_cmp, v_cmp, k_sb, v_sb, kw, vw, kwpos, rel_bias):
    b, c = q.shape[:2]
    scale = HEAD_DIM ** -0.5
    tab = rel_bias.reshape(N_BUCKETS, N_KV_HEADS, GROUP)
    nc = k_cmp.shape[1]
    cend = jnp.arange(nc, dtype=jnp.int32) * CMP_BLK + (CMP_BLK - 1)
    dc = qpos[:, None] - cend[None, :]
    bias_c = jnp.transpose(tab[rel_bucket(dc)], (2, 3, 0, 1))
    lc = jnp.einsum('bckgd,bnkd->bkgcn', q, k_cmp) * scale + bias_c
    p_c = masked_softmax(lc, dc >= 0)
    o_c = jnp.einsum('bkgcn,bnkd->bckgd', p_c.astype(v_cmp.dtype), v_cmp)
    ns = k_sb.shape[2]
    ratio = SEL_BLK // CMP_BLK
    imp = jnp.pad(jnp.sum(p_c, axis=2), ((0, 0), (0, 0), (0, 0), (0, ratio * ns - nc)))
    imp = imp.reshape(b, N_KV_HEADS, c, ns, ratio).sum(-1)
    blk = jnp.arange(ns, dtype=jnp.int32)[None, :]
    qblk = (qpos // SEL_BLK)[:, None]
    forced = (blk == 0) | (blk == qblk) | (blk == qblk - 1)
    eligible = blk <= qblk
    imp = jnp.where(forced, BIG, jnp.where(eligible, imp, -1.0))
    _, idx = lax.top_k(imp, min(N_SEL, ns))
    gather = jax.vmap(jax.vmap(lambda kb, i: kb[i]))
    ks = gather(k_sb, idx)
    vs = gather(v_sb, idx)
    spos = idx[..., None] * SEL_BLK + jnp.arange(SEL_BLK, dtype=jnp.int32)
    ds = qpos[None, None, :, None, None] - spos
    bias_s = tab[rel_bucket(ds), jnp.arange(N_KV_HEADS)[None, :, None, None, None]]
    bias_s = jnp.moveaxis(bias_s, -1, 2)
    ls = jnp.einsum('bckgd,bkcjsd->bkgcjs', q, ks) * scale + bias_s
    ls = ls.reshape(b, N_KV_HEADS, GROUP, c, -1)
    p_s = masked_softmax(ls, (ds >= 0).reshape(b, N_KV_HEADS, 1, c, -1))
    o_s = jnp.einsum('bkgcx,bkcxd->bckgd', p_s.astype(vs.dtype), vs.reshape(b, N_KV_HEADS, c, -1, HEAD_DIM))
    dw = qpos[:, None] - kwpos[None, :]
    bias_w = jnp.transpose(tab[rel_bucket(dw)], (2, 3, 0, 1))
    lw = jnp.einsum('bckgd,blkd->bkgcl', q, kw) * scale + bias_w
    p_w = masked_softmax(lw, (dw >= 0) & (dw < WINDOW) & (kwpos[None, :] >= 0))
    o_w = jnp.einsum('bkgcl,blkd->bckgd', p_w.astype(vw.dtype), vw)
    g = jax.nn.sigmoid(gates.astype(jnp.float32))
    o = g[..., 0:1] * o_c + g[..., 1:2] * o_s + g[..., 2:3] * o_w
    return o.reshape(b, c, Q_DIM).astype(q.dtype)


def nsa_prompt(h, w_in, w_out, wk_c, wv_c, pek_c, pev_c, rel_bias):
    b, t = h.shape[:2]
    q, (kc, vc, ksl, vsl, kw, vw), gates = nsa_project(h, w_in)
    k_cmp = compress_blocks(kc, wk_c, pek_c)
    v_cmp = compress_blocks(vc, wv_c, pev_c)
    k_sb = select_blocks(ksl)
    v_sb = select_blocks(vsl)
    kw_pad = jnp.pad(kw, ((0, 0), (WINDOW, 0), (0, 0), (0, 0)))
    vw_pad = jnp.pad(vw, ((0, 0), (WINDOW, 0), (0, 0), (0, 0)))
    n_chunks = t // Q_CHUNK
    qc = jnp.moveaxis(q.reshape(b, n_chunks, Q_CHUNK, N_KV_HEADS, GROUP, HEAD_DIM), 1, 0)
    gc = jnp.moveaxis(gates.reshape(b, n_chunks, Q_CHUNK, N_KV_HEADS, GROUP, 3), 1, 0)
    starts = jnp.arange(n_chunks, dtype=jnp.int32) * Q_CHUNK

    def one_chunk(args):
        q_i, g_i, s = args
        qpos = s + jnp.arange(Q_CHUNK, dtype=jnp.int32)
        kw_i = lax.dynamic_slice_in_dim(kw_pad, s, WINDOW + Q_CHUNK, axis=1)
        vw_i = lax.dynamic_slice_in_dim(vw_pad, s, WINDOW + Q_CHUNK, axis=1)
        kwpos = s - WINDOW + jnp.arange(WINDOW + Q_CHUNK, dtype=jnp.int32)
        return nsa_attend(q_i, g_i, qpos, k_cmp, v_cmp, k_sb, v_sb, kw_i, vw_i, kwpos, rel_bias)

    o = lax.map(one_chunk, (qc, gc, starts))
    o = jnp.moveaxis(o, 0, 1).reshape(b, t, Q_DIM)
    win = min(WINDOW, t)
    return o @ w_out, (kc, vc, ksl, vsl, kw[:, t - win:], vw[:, t - win:])


def nsa_sample(h, c_kc, c_vc, c_ks, c_vs, page_table, win_k, win_v, w_in, w_out, wk_c, wv_c, pek_c, pev_c, rel_bias):
    b, t = h.shape[:2]
    past = page_table.shape[1] * PAGE_SIZE
    q, (kc, vc, ksl, vsl, kw, vw), gates = nsa_project(h, w_in)

    def paged(cache, new):
        old = cache[page_table].reshape(b, past, N_KV_HEADS, HEAD_DIM)
        return jnp.concatenate([old, new], axis=1)

    k_cmp = compress_blocks(paged(c_kc, kc), wk_c, pek_c)
    v_cmp = compress_blocks(paged(c_vc, vc), wv_c, pev_c)
    k_sb = select_blocks(paged(c_ks, ksl))
    v_sb = select_blocks(paged(c_vs, vsl))
    wbuf = win_k.shape[1]
    kw_ext = jnp.concatenate([win_k, kw], axis=1)
    vw_ext = jnp.concatenate([win_v, vw], axis=1)
    qpos = past + jnp.arange(t, dtype=jnp.int32)
    kwpos = past - wbuf + jnp.arange(wbuf + t, dtype=jnp.int32)
    o = nsa_attend(q, gates, qpos, k_cmp, v_cmp, k_sb, v_sb, kw_ext, vw_ext, kwpos, rel_bias)
    return o @ w_out, (kc, vc, ksl, vsl, kw_ext[:, t:], vw_ext[:, t:])


def pool_mix(u, prev, pos0, w_grp, layer_scale):
    b, t, _ = u.shape
    p = prev.shape[1]
    ext = jnp.concatenate([prev, u], axis=1).astype(jnp.float32)
    cs = jnp.concatenate([jnp.zeros((b, 1, D_MODEL), jnp.float32), lax.cumsum(ext, axis=1)], axis=1)
    pos = pos0 + jnp.arange(t, dtype=jnp.int32)
    outs = []
    for gi, w in enumerate(POOL_WINDOWS):
        sl = slice(gi * POOL_GROUP_DIM, (gi + 1) * POOL_GROUP_DIM)
        hi = cs[:, p + 1:p + 1 + t, sl]
        lo = cs[:, p + 1 - w:p + 1 - w + t, sl]
        cnt = jnp.minimum(pos + 1, w).astype(jnp.float32)[None, :, None]
        outs.append((hi - lo) / cnt)
    pooled = jnp.concatenate(outs, axis=-1) - ext[:, p:]
    mixed = jnp.einsum('btgc,gce->btge', pooled.reshape(b, t, N_POOL_GROUPS, POOL_GROUP_DIM),
                       w_grp.astype(jnp.float32)).reshape(b, t, D_MODEL)
    return (mixed * layer_scale).astype(u.dtype), ext[:, -POOL_STATE:].astype(u.dtype)


def swiglu(h, w_gate, w_up, w_down):
    return (jax.nn.silu(h @ w_gate) * (h @ w_up)) @ w_down


def run_trunk(x, c, mixer_apply, ada_w, ada_b, norm_g, final_g, ffn_wg, ffn_wu, ffn_wd):
    states = []
    for i in range(DEPTH):
        sh1, sc1, g1, sh2, sc2, g2 = ada_modulate(c, ada_w[i], ada_b[i])
        h = rmsnorm(x, norm_g[i, 0]) * (1 + sc1) + sh1
        y, st = mixer_apply(i, h)
        x = x + g1 * y
        h = rmsnorm(x, norm_g[i, 1]) * (1 + sc2) + sh2
        x = x + g2 * swiglu(h, ffn_wg[i], ffn_wu[i], ffn_wd[i])
        states.append(st)
    return rmsnorm(x, final_g), states


def setup_inputs(seed: int = 0) -> dict:
    key = jax.random.key(seed)
    ks = jax.random.split(key, 32)
    n_pages = PAST_LEN // PAGE_SIZE
    n_used = DEC_BATCH * n_pages
    n_phys = n_used + n_used // 4
    wbuf = min(WINDOW, PAST_LEN)
    nrm = lambda k, shape, s: jax.random.normal(k, shape, jnp.float32) * s
    page_table = jax.random.permutation(ks[0], n_phys)[:n_used].reshape(DEC_BATCH, n_pages).astype(jnp.int32)
    cache_shape = (N_NSA_LAYERS, n_phys, PAGE_SIZE, N_KV_HEADS, HEAD_DIM)
    win_shape = (N_NSA_LAYERS, DEC_BATCH, wbuf, N_KV_HEADS, HEAD_DIM)
    return {
        'x_prompt': nrm(ks[1], (BATCH, SEQ, D_MODEL), 1.0),
        'x_sample': nrm(ks[2], (DEC_BATCH, DEC_SEQ, D_MODEL), 1.0),
        'c_prompt': nrm(ks[3], (BATCH, D_MODEL), 1.0),
        'c_sample': nrm(ks[4], (DEC_BATCH, D_MODEL), 1.0),
        'cache_k_cmp': nrm(ks[5], cache_shape, 1.0),
        'cache_v_cmp': nrm(ks[6], cache_shape, 1.0),
        'cache_k_sel': nrm(ks[7], cache_shape, 1.0),
        'cache_v_sel': nrm(ks[8], cache_shape, 1.0),
        'page_table': page_table,
        'state_k_win': nrm(ks[9], win_shape, 1.0),
        'state_v_win': nrm(ks[10], win_shape, 1.0),
        'state_pool': nrm(ks[11], (N_POOL_LAYERS, DEC_BATCH, POOL_STATE, D_MODEL), 1.0),
        'rel_bias': nrm(ks[12], (N_BUCKETS, N_HEADS), 0.2),
        'ada_w': nrm(ks[13], (DEPTH, D_MODEL, 6 * D_MODEL), 0.5 * D_MODEL ** -0.5),
        'ada_b': nrm(ks[14], (DEPTH, 6 * D_MODEL), 0.01),
        'norm_g': 1.0 + nrm(ks[15], (DEPTH, 2, D_MODEL), 0.01),
        'final_g': 1.0 + nrm(ks[16], (D_MODEL,), 0.01),
        'nsa_w_in': nrm(ks[17], (N_NSA_LAYERS, D_MODEL, IN_COLS), D_MODEL ** -0.5),
        'nsa_w_out': nrm(ks[18], (N_NSA_LAYERS, Q_DIM, D_MODEL), Q_DIM ** -0.5),
        'cmp_wk': nrm(ks[19], (N_NSA_LAYERS, CMP_BLK, HEAD_DIM, HEAD_DIM), (CMP_BLK * HEAD_DIM) ** -0.5),
        'cmp_wv': nrm(ks[20], (N_NSA_LAYERS, CMP_BLK, HEAD_DIM, HEAD_DIM), (CMP_BLK * HEAD_DIM) ** -0.5),
        'cmp_pe_k': nrm(ks[21], (N_NSA_LAYERS, CMP_BLK, HEAD_DIM), 0.02),
        'cmp_pe_v': nrm(ks[22], (N_NSA_LAYERS, CMP_BLK, HEAD_DIM), 0.02),
        'pool_w': nrm(ks[23], (N_POOL_LAYERS, N_POOL_GROUPS, POOL_GROUP_DIM, POOL_GROUP_DIM), POOL_GROUP_DIM ** -0.5),
        'pool_scale': 1.0 + nrm(ks[24], (N_POOL_LAYERS, D_MODEL), 0.1),
        'ffn_wg': nrm(ks[25], (DEPTH, D_MODEL, D_FF), D_MODEL ** -0.5),
        'ffn_wu': nrm(ks[26], (DEPTH, D_MODEL, D_FF), D_MODEL ** -0.5),
        'ffn_wd': nrm(ks[27], (DEPTH, D_FF, D_MODEL), D_FF ** -0.5),
    }


def reference(x_prompt, x_sample, c_prompt, c_sample, cache_k_cmp, cache_v_cmp, cache_k_sel, cache_v_sel,
              page_table, state_k_win, state_v_win, state_pool, rel_bias, ada_w, ada_b, norm_g, final_g,
              nsa_w_in, nsa_w_out, cmp_wk, cmp_wv, cmp_pe_k, cmp_pe_v, pool_w, pool_scale,
              ffn_wg, ffn_wu, ffn_wd):
    past = page_table.shape[1] * PAGE_SIZE

    def prompt_mixer(i, h):
        j = i // 2
        if i % 2 == 0:
            return nsa_prompt(h, nsa_w_in[j], nsa_w_out[j], cmp_wk[j], cmp_wv[j], cmp_pe_k[j], cmp_pe_v[j], rel_bias)
        prev = jnp.zeros((h.shape[0], POOL_STATE, D_MODEL), h.dtype)
        return pool_mix(h, prev, 0, pool_w[j], pool_scale[j])

    def sample_mixer(i, h):
        j = i // 2
        if i % 2 == 0:
            return nsa_sample(h, cache_k_cmp[j], cache_v_cmp[j], cache_k_sel[j], cache_v_sel[j], page_table,
                              state_k_win[j], state_v_win[j], nsa_w_in[j], nsa_w_out[j],
                              cmp_wk[j], cmp_wv[j], cmp_pe_k[j], cmp_pe_v[j], rel_bias)
        return pool_mix(h, state_pool[j], past, pool_w[j], pool_scale[j])

    y_prompt, st_p = run_trunk(x_prompt, c_prompt, prompt_mixer, ada_w, ada_b, norm_g, final_g, ffn_wg, ffn_wu, ffn_wd)
    y_sample, st_s = run_trunk(x_sample, c_sample, sample_mixer, ada_w, ada_b, norm_g, final_g, ffn_wg, ffn_wu, ffn_wd)

    nsa_p = [st_p[i] for i in range(0, DEPTH, 2)]
    nsa_s = [st_s[i] for i in range(0, DEPTH, 2)]
    new_pool_p = jnp.stack([st_p[i] for i in range(1, DEPTH, 2)])
    new_pool_s = jnp.stack([st_s[i] for i in range(1, DEPTH, 2)])
    new_k_cmp_p = jnp.stack([s[0] for s in nsa_p])
    new_v_cmp_p = jnp.stack([s[1] for s in nsa_p])
    new_k_sel_p = jnp.stack([s[2] for s in nsa_p])
    new_v_sel_p = jnp.stack([s[3] for s in nsa_p])
    new_k_win_p = jnp.stack([s[4] for s in nsa_p])
    new_v_win_p = jnp.stack([s[5] for s in nsa_p])
    new_k_cmp_s = jnp.stack([s[0] for s in nsa_s])
    new_v_cmp_s = jnp.stack([s[1] for s in nsa_s])
    new_k_sel_s = jnp.stack([s[2] for s in nsa_s])
    new_v_sel_s = jnp.stack([s[3] for s in nsa_s])
    new_k_win_s = jnp.stack([s[4] for s in nsa_s])
    new_v_win_s = jnp.stack([s[5] for s in nsa_s])
    return (y_prompt, y_sample,
            new_k_cmp_p, new_v_cmp_p, new_k_sel_p, new_v_sel_p, new_k_win_p, new_v_win_p, new_pool_p,
            new_k_cmp_s, new_v_cmp_s, new_k_sel_s, new_v_sel_s, new_k_win_s, new_v_win_s, new_pool_s)
```

```python
import functools
import math

import jax
import jax.numpy as jnp
from jax import lax
from jax.experimental import pallas as pl
from jax.experimental.pallas import tpu as pltpu

D_MODEL = 1024
N_HEADS = 16
N_KV = 4
GROUP = N_HEADS // N_KV
HEAD_DIM = 64
Q_DIM = N_HEADS * HEAD_DIM
KV_DIM = N_KV * HEAD_DIM
CMP_BLK = 32
SEL_BLK = 64
N_SEL = 16
WINDOW = 512
Q_CHUNK = 64
N_BUCKETS = 32
MAX_DISTANCE = 128
POOL_WINDOWS = (2, 4, 8, 16)
POOL_GROUP_DIM = D_MODEL // len(POOL_WINDOWS)
POOL_STATE = max(POOL_WINDOWS) - 1
PAGE_SIZE = 128
RMS_EPS = 1e-6
NEG = -1e30
BIG = 1e9

KEY_TILE = 128
LANES = GROUP * Q_CHUNK
SAMPLE_LANES = 128
VMEM_LIMIT_BYTES = 48 * 1024 * 1024

F32 = jnp.float32
BF16 = jnp.bfloat16


def _cparams(*sem):
    return pltpu.CompilerParams(dimension_semantics=sem, vmem_limit_bytes=VMEM_LIMIT_BYTES)


def _silu(x):
    return x * (1.0 / (1.0 + jnp.exp(-x)))


def _sigmoid(x):
    return 1.0 / (1.0 + jnp.exp(-x))


def _normmod(x, g, sc, sh):
    ms = jnp.mean(x * x, axis=-1, keepdims=True)
    return (x * lax.rsqrt(ms + RMS_EPS) * g) * (1.0 + sc) + sh


def _mod_spec(mod, tm, rows_per_batch):
    if mod.ndim == 3:
        return pl.BlockSpec((None, 1, D_MODEL), lambda i, *_: ((i * tm) // rows_per_batch, 0, 0))
    return pl.BlockSpec((tm, D_MODEL), lambda i, *_: (i, 0))


def _mm_kernel(a_ref, w_ref, b_ref, o_ref, *, silu_in):
    a = a_ref[...]
    if silu_in:
        a = _silu(a)
    o_ref[...] = jnp.dot(a.astype(BF16), w_ref[...], preferred_element_type=F32) + b_ref[...]


def _matmul(a, w, bias=None, *, silu_in=False, tm=256, tn=None, name="matmul"):
    m, k = a.shape
    n = w.shape[1]
    tm = min(tm, m)
    tn = n if tn is None else tn
    assert m % tm == 0 and n % tn == 0
    if bias is None:
        bias = jnp.zeros((1, n), F32)
    return pl.pallas_call(
        functools.partial(_mm_kernel, silu_in=silu_in),
        grid=(m // tm, n // tn),
        in_specs=[pl.BlockSpec((tm, k), lambda i, j: (i, 0)),
                  pl.BlockSpec((k, tn), lambda i, j: (0, j)),
                  pl.BlockSpec((1, tn), lambda i, j: (0, j))],
        out_specs=pl.BlockSpec((tm, tn), lambda i, j: (i, j)),
        out_shape=jax.ShapeDtypeStruct((m, n), F32),
        compiler_params=_cparams("parallel", "parallel"),
        name=name,
    )(a, w.astype(BF16), bias)


_PROJ_MAIN = Q_DIM + 6 * KV_DIM
_GATE_PAD = 128


def _proj_kernel(x_ref, g_ref, sc_ref, sh_ref, w_ref, wg_ref, q_ref, *rest):
    kv_refs, gate_ref = rest[:6], rest[6]
    hb = _normmod(x_ref[...], g_ref[...], sc_ref[...], sh_ref[...]).astype(BF16)
    q_ref[...] = jnp.dot(hb, w_ref[:, :Q_DIM], preferred_element_type=F32)
    for n, r in enumerate(kv_refs):
        lo = Q_DIM + n * KV_DIM
        r[...] = jnp.dot(hb, w_ref[:, lo:lo + KV_DIM], preferred_element_type=F32)
    gate_ref[...] = jnp.dot(hb, wg_ref[...], preferred_element_type=F32)


def _nsa_project(x, g, sc, sh, w_in, rows_per_batch, tm):
    m = x.shape[0]
    tm = min(tm, m)
    w_main = w_in[:, :_PROJ_MAIN].astype(BF16)
    w_gate = jnp.pad(w_in[:, _PROJ_MAIN:], ((0, 0), (0, _GATE_PAD - 3 * N_HEADS))).astype(BF16)
    row = lambda n: pl.BlockSpec((tm, n), lambda i: (i, 0))
    outs = pl.pallas_call(
        _proj_kernel,
        grid=(m // tm,),
        in_specs=[row(D_MODEL),
                  pl.BlockSpec((1, D_MODEL), lambda i: (0, 0)),
                  _mod_spec(sc, tm, rows_per_batch), _mod_spec(sh, tm, rows_per_batch),
                  pl.BlockSpec((D_MODEL, _PROJ_MAIN), lambda i: (0, 0)),
                  pl.BlockSpec((D_MODEL, _GATE_PAD), lambda i: (0, 0))],
        out_specs=[row(Q_DIM)] + [row(KV_DIM)] * 6 + [row(_GATE_PAD)],
        out_shape=[jax.ShapeDtypeStruct((m, Q_DIM), F32)]
        + [jax.ShapeDtypeStruct((m, KV_DIM), F32)] * 6
        + [jax.ShapeDtypeStruct((m, _GATE_PAD), F32)],
        compiler_params=_cparams("parallel"),
        name="nsa_project",
    )(x, g.reshape(1, D_MODEL), sc, sh, w_main, w_gate)
    return outs[0], outs[1:7], outs[7]


def _mm_res_kernel(a_ref, w_ref, x_ref, gate_ref, o_ref):
    y = jnp.dot(a_ref[...].astype(BF16), w_ref[...], preferred_element_type=F32)
    o_ref[...] = x_ref[...] + gate_ref[...] * y


def _matmul_residual(a, w, x, gate, rows_per_batch, tm):
    m, k = a.shape
    tm = min(tm, m)
    return pl.pallas_call(
        _mm_res_kernel,
        grid=(m // tm,),
        in_specs=[pl.BlockSpec((tm, k), lambda i: (i, 0)),
                  pl.BlockSpec((k, D_MODEL), lambda i: (0, 0)),
                  pl.BlockSpec((tm, D_MODEL), lambda i: (i, 0)),
                  _mod_spec(gate, tm, rows_per_batch)],
        out_specs=pl.BlockSpec((tm, D_MODEL), lambda i: (i, 0)),
        out_shape=jax.ShapeDtypeStruct((m, D_MODEL), F32),
        compiler_params=_cparams("parallel"),
        name="out_proj_residual",
    )(a, w.astype(BF16), x, gate)


def _ffn_kernel(x_ref, g_ref, sc_ref, sh_ref, gate_ref, wg_ref, wu_ref, wd_ref, fg_ref,
                o_ref, hb_ref, acc_ref, *, final_norm):
    f = pl.program_id(1)

    @pl.when(f == 0)
    def _():
        hb_ref[...] = _normmod(x_ref[...], g_ref[...], sc_ref[...], sh_ref[...]).astype(BF16)
        acc_ref[...] = jnp.zeros_like(acc_ref)

    hb = hb_ref[...]
    a = jnp.dot(hb, wg_ref[...], preferred_element_type=F32)
    u = jnp.dot(hb, wu_ref[...], preferred_element_type=F32)
    act = (_silu(a) * u).astype(BF16)
    acc_ref[...] += jnp.dot(act, wd_ref[...], preferred_element_type=F32)

    @pl.when(f == pl.num_programs(1) - 1)
    def _():
        y = x_ref[...] + gate_ref[...] * acc_ref[...]
        if final_norm:
            ms = jnp.mean(y * y, axis=-1, keepdims=True)
            y = y * lax.rsqrt(ms + RMS_EPS) * fg_ref[...]
        o_ref[...] = y


def _ffn(x, g, sc, sh, gate, wg, wu, wd, final_g, rows_per_batch, tm, tf, final_norm):
    m = x.shape[0]
    d_ff = wg.shape[1]
    tm = min(tm, m)
    assert d_ff % tf == 0
    vec = pl.BlockSpec((1, D_MODEL), lambda i, f: (0, 0))
    return pl.pallas_call(
        functools.partial(_ffn_kernel, final_norm=final_norm),
        grid=(m // tm, d_ff // tf),
        in_specs=[pl.BlockSpec((tm, D_MODEL), lambda i, f: (i, 0)),
                  vec,
                  _mod_spec(sc, tm, rows_per_batch), _mod_spec(sh, tm, rows_per_batch),
                  _mod_spec(gate, tm, rows_per_batch),
                  pl.BlockSpec((D_MODEL, tf), lambda i, f: (0, f)),
                  pl.BlockSpec((D_MODEL, tf), lambda i, f: (0, f)),
                  pl.BlockSpec((tf, D_MODEL), lambda i, f: (f, 0)),
                  vec],
        out_specs=pl.BlockSpec((tm, D_MODEL), lambda i, f: (i, 0)),
        out_shape=jax.ShapeDtypeStruct((m, D_MODEL), F32),
        scratch_shapes=[pltpu.VMEM((tm, D_MODEL), BF16), pltpu.VMEM((tm, D_MODEL), F32)],
        compiler_params=_cparams("parallel", "arbitrary"),
        name="ffn",
    )(x, g.reshape(1, D_MODEL), sc, sh, gate, wg.astype(BF16), wu.astype(BF16), wd.astype(BF16),
      final_g.reshape(1, D_MODEL))


_POOL_HALO = 16


def _pool_kernel(x_ref, xprev_ref, state_ref, g_ref, sc_ref, sh_ref, gate_ref, w_ref, ls_ref,
                 o_ref, st_ref, ext_ref, *, tm, pos0):
    i = pl.program_id(1)
    g, sc, sh = g_ref[...], sc_ref[...], sh_ref[...]
    u = _normmod(x_ref[...], g, sc, sh)
    prev = jnp.where(i == 0, state_ref[...], _normmod(xprev_ref[...], g, sc, sh))
    ext_ref[0:_POOL_HALO, :] = prev
    ext_ref[_POOL_HALO:_POOL_HALO + tm, :] = u
    st_ref[...] = ext_ref[tm:tm + _POOL_HALO, :]

    pos = pos0 + i * tm + lax.broadcasted_iota(jnp.int32, (tm, 1), 0)
    mixed = []
    for gi, w in enumerate(POOL_WINDOWS):
        lo = gi * POOL_GROUP_DIM
        s = u[:, lo:lo + POOL_GROUP_DIM]
        for k in range(1, w):
            s = s + ext_ref[_POOL_HALO - k:_POOL_HALO - k + tm, lo:lo + POOL_GROUP_DIM]
        cnt = jnp.minimum(pos + 1, w).astype(F32)
        pooled = s / cnt - u[:, lo:lo + POOL_GROUP_DIM]
        mixed.append(jnp.dot(pooled.astype(BF16), w_ref[gi], preferred_element_type=F32))
    y = jnp.concatenate(mixed, axis=-1) * ls_ref[...]
    o_ref[...] = x_ref[...] + gate_ref[...] * y


def _pool_mix(x3, state, g, sc, sh, gate, w_grp, layer_scale, pos0, tm):
    b, t, _ = x3.shape
    tm = min(tm, t)
    state16 = jnp.pad(state, ((0, 0), (_POOL_HALO - POOL_STATE, 0), (0, 0)))
    if t >= _POOL_HALO:
        xprev = x3
        nprev = tm // _POOL_HALO
        prev_spec = pl.BlockSpec((None, _POOL_HALO, D_MODEL),
                                 lambda bi, i: (bi, jnp.maximum(i * nprev - 1, 0), 0))
    else:
        xprev = state16
        prev_spec = pl.BlockSpec((None, _POOL_HALO, D_MODEL), lambda bi, i: (bi, 0, 0))
    vec = pl.BlockSpec((1, D_MODEL), lambda bi, i: (0, 0))
    bvec = pl.BlockSpec((None, 1, D_MODEL), lambda bi, i: (bi, 0, 0))
    out, st = pl.pallas_call(
        functools.partial(_pool_kernel, tm=tm, pos0=pos0),
        grid=(b, t // tm),
        in_specs=[pl.BlockSpec((None, tm, D_MODEL), lambda bi, i: (bi, i, 0)),
                  prev_spec,
                  pl.BlockSpec((None, _POOL_HALO, D_MODEL), lambda bi, i: (bi, 0, 0)),
                  vec, bvec, bvec, bvec,
                  pl.BlockSpec((len(POOL_WINDOWS), POOL_GROUP_DIM, POOL_GROUP_DIM),
                               lambda bi, i: (0, 0, 0)),
                  vec],
        out_specs=[pl.BlockSpec((None, tm, D_MODEL), lambda bi, i: (bi, i, 0)),
                   pl.BlockSpec((None, _POOL_HALO, D_MODEL), lambda bi, i: (bi, 0, 0))],
        out_shape=[jax.ShapeDtypeStruct((b, t, D_MODEL), F32),
                   jax.ShapeDtypeStruct((b, _POOL_HALO, D_MODEL), F32)],
        scratch_shapes=[pltpu.VMEM((tm + _POOL_HALO, D_MODEL), F32)],
        compiler_params=_cparams("parallel", "arbitrary"),
        name="pool_mix",
    )(x3, xprev, state16, g.reshape(1, D_MODEL), sc, sh, gate, w_grp.astype(BF16),
      layer_scale.reshape(1, D_MODEL))
    return out, st[:, _POOL_HALO - POOL_STATE:]


def _rel_bucket(dist):
    d = jnp.maximum(dist, 0)
    max_exact = N_BUCKETS // 2
    large = max_exact + (jnp.log(jnp.maximum(d, 1).astype(F32) / max_exact)
                         / math.log(MAX_DISTANCE / max_exact) * (N_BUCKETS - max_exact)).astype(jnp.int32)
    large = jnp.minimum(large, N_BUCKETS - 1)
    return jnp.where(d < max_exact, d, large)


def _bias_table(rel_bias, dist, valid):
    tab = rel_bias.reshape(N_BUCKETS, N_KV, GROUP)
    bias = tab[_rel_bucket(dist)]
    bias = jnp.where(valid[..., None, None], bias, NEG)
    nd = dist.ndim
    bias = jnp.moveaxis(bias, (nd, nd + 1), (nd - 2, nd))
    return bias.reshape(bias.shape[:-2] + (GROUP * dist.shape[-1],))


def _far_bias(rel_bias):
    row = rel_bias[N_BUCKETS - 1].reshape(N_KV, GROUP)
    return jnp.repeat(row, Q_CHUNK, axis=1).reshape(N_KV, 1, LANES)


def _prompt_tables(rel_bias, t):
    kj = jnp.arange(KEY_TILE, dtype=jnp.int32)[:, None]
    qi = jnp.arange(Q_CHUNK, dtype=jnp.int32)[None, :]
    delta = jnp.array([[128, 0], [192, 64]], jnp.int32)[:, :, None, None]
    dist = delta + qi - kj
    near = _bias_table(rel_bias, dist, dist >= 0)
    delta = jnp.array([WINDOW, WINDOW + Q_CHUNK], jnp.int32)[:, None, None]
    dist = delta + qi - kj
    wfirst = _bias_table(rel_bias, dist, dist < WINDOW)
    nc = t // CMP_BLK
    start = (jnp.arange(t // Q_CHUNK, dtype=jnp.int32) * Q_CHUNK)[:, None, None]
    cend = (jnp.arange(nc, dtype=jnp.int32) * CMP_BLK + CMP_BLK - 1)[:, None]
    dist = start + qi - cend
    cmp_tab = _bias_table(rel_bias, dist, dist >= 0)
    return near, wfirst, cmp_tab


def _softmax_step(state, k_tile, v_tile_t, qs, add):
    m, l, acc = state
    s = jnp.dot(k_tile, qs, preferred_element_type=F32) + add
    m_new = jnp.maximum(m, jnp.max(s, axis=0, keepdims=True))
    alpha = jnp.exp(m - m_new)
    p = jnp.exp(s - m_new)
    l = alpha * l + jnp.sum(p, axis=0, keepdims=True)
    acc = alpha * acc + jnp.dot(v_tile_t, p.astype(BF16), preferred_element_type=F32)
    return m_new, l, acc


def _softmax_init():
    return (jnp.full((1, LANES), NEG, F32), jnp.zeros((1, LANES), F32),
            jnp.zeros((HEAD_DIM, LANES), F32))


def _softmax_finish(state):
    _, l, acc = state
    return acc / jnp.maximum(l, 1e-30)


def _prompt_attn_kernel(q_ref, g_ref, kc_ref, vct_ref, cb_ref, ks_ref, vst_ref, kw_ref, vwt_ref,
                        near_ref, wfirst_ref, far_ref, o_ref, imp_ref, val_ref, selrep_ref,
                        *, n_blk, n_sel):
    i = pl.program_id(1)
    par = i % 2
    jd = i // 2
    scale = HEAD_DIM ** -0.5
    lane128 = lax.broadcasted_iota(jnp.int32, (1, 128), 1)

    qs_all, oc_all, halves = [], [], []
    for kv in range(N_KV):
        qs = q_ref[kv] * scale
        qs_all.append(qs)
        bias = cb_ref[kv]
        s = jnp.dot(kc_ref[kv], qs, preferred_element_type=F32) + bias
        valid = bias > 0.5 * NEG
        m = jnp.max(s, axis=0, keepdims=True)
        p = jnp.where(valid, jnp.exp(s - m), 0.0)
        p = p / jnp.maximum(jnp.sum(p, axis=0, keepdims=True), 1e-30)
        oc_all.append(jnp.dot(vct_ref[kv], p.astype(BF16), preferred_element_type=F32))
        t = p[:, :128] + p[:, 128:]
        halves.append(t + pltpu.roll(t, 64, 1))
    lo_half = lane128 < Q_CHUNK
    imp_ref[0] = jnp.where(lo_half, halves[0], halves[1])
    imp_ref[1] = jnp.where(lo_half, halves[2], halves[3])

    imp = jnp.concatenate(
        [imp_ref[h, pl.ds(0, n_blk, stride=2), :] + imp_ref[h, pl.ds(1, n_blk, stride=2), :] for h in range(2)],
        axis=1)
    blk = lax.broadcasted_iota(jnp.int32, (n_blk, LANES), 0)
    forced = (blk == 0) | (blk == i) | (blk == i - 1)
    val_ref[...] = jnp.where(forced, BIG, jnp.where(blk <= i, imp, -1.0))
    val = val_ref[...]
    rank = jnp.zeros((n_blk, LANES), F32)
    for j in range(n_blk):
        row = val_ref[j:j + 1, :]
        rank = rank + jnp.where(blk > j, jnp.where(row >= val, 1.0, 0.0), jnp.where(row > val, 1.0, 0.0))
    selneg = jnp.where(rank < n_sel, 0.0, NEG)
    for kv in range(N_KV):
        x = selneg[:, 128 * (kv // 2):128 * (kv // 2) + 128]
        rolled = pltpu.roll(x, 64, 1)
        y = jnp.where(lo_half, x, rolled) if kv % 2 == 0 else jnp.where(lo_half, rolled, x)
        selrep_ref[kv, :, 0:128] = y
        selrep_ref[kv, :, 128:256] = y

    for kv in range(N_KV):
        qs = qs_all[kv]
        far = far_ref[kv]

        def sel_rows(jt, kv=kv):
            top = selrep_ref[kv, pl.ds(2 * jt, 1), :]
            bot = selrep_ref[kv, pl.ds(2 * jt + 1, 1), :]
            return top, bot

        def sel_mask(jt):
            top, bot = sel_rows(jt)
            return jnp.concatenate([jnp.broadcast_to(top, (SEL_BLK, LANES)),
                                    jnp.broadcast_to(bot, (SEL_BLK, LANES))], axis=0)

        st = _softmax_init()
        st = _softmax_step(st, ks_ref[kv, jd], vst_ref[kv, jd], qs, near_ref[par, 1, kv] + sel_mask(jd))
        jn = jnp.maximum(jd - 1, 0)
        off = jnp.where(jd >= 1, 0.0, NEG)
        st = _softmax_step(st, ks_ref[kv, jn], vst_ref[kv, jn], qs,
                           near_ref[par, 0, kv] + sel_mask(jn) + off)

        def far_body(jt, st, kv=kv, qs=qs, far=far):
            top, bot = sel_rows(jt)
            add = jnp.concatenate([jnp.broadcast_to(top + far, (SEL_BLK, LANES)),
                                   jnp.broadcast_to(bot + far, (SEL_BLK, LANES))], axis=0)
            return _softmax_step(st, ks_ref[kv, jt], vst_ref[kv, jt], qs, add)

        st = lax.fori_loop(0, jnp.maximum(jd - 1, 0), far_body, st)
        o_s = _softmax_finish(st)

        st = _softmax_init()
        st = _softmax_step(st, kw_ref[kv, jd], vwt_ref[kv, jd], qs, near_ref[par, 1, kv])
        for back, add in ((1, near_ref[par, 0, kv]), (2, far), (3, far), (4, wfirst_ref[par, kv])):
            jt = jd - back
            jc = jnp.maximum(jt, 0)
            off = jnp.where(jt >= 0, 0.0, NEG)
            st = _softmax_step(st, kw_ref[kv, jc], vwt_ref[kv, jc], qs, add + off)
        o_w = _softmax_finish(st)

        gate = _sigmoid(g_ref[kv])
        o = gate[0:1] * oc_all[kv] + gate[1:2] * o_s + gate[2:3] * o_w
        o_ref[kv] = o.astype(o_ref.dtype)


def _prompt_attention(q, gates, k_cmp, v_cmp, ksl, vsl, kw, vw, rel_bias, b, t):
    nch, nt, nc, n_blk = t // Q_CHUNK, t // KEY_TILE, t // CMP_BLK, t // SEL_BLK
    q_t = (q.reshape(b, nch, Q_CHUNK, N_KV, GROUP, HEAD_DIM).transpose(0, 1, 3, 5, 4, 2)
           .reshape(b, nch, N_KV, HEAD_DIM, LANES).astype(BF16))
    g_t = (gates[:, :3 * N_HEADS].reshape(b, nch, Q_CHUNK, N_KV, GROUP, 3).transpose(0, 1, 3, 5, 4, 2)
           .reshape(b, nch, N_KV, 3, LANES))
    g_t = jnp.pad(g_t, ((0, 0), (0, 0), (0, 0), (0, 5), (0, 0)))
    kc = k_cmp.reshape(b, nc, N_KV, HEAD_DIM).transpose(0, 2, 1, 3).astype(BF16)
    vct = v_cmp.reshape(b, nc, N_KV, HEAD_DIM).transpose(0, 2, 3, 1).astype(BF16)

    def k_tiles(a):
        return a.reshape(b, nt, KEY_TILE, N_KV, HEAD_DIM).transpose(0, 3, 1, 2, 4).astype(BF16)

    def v_tiles_t(a):
        return a.reshape(b, nt, KEY_TILE, N_KV, HEAD_DIM).transpose(0, 3, 1, 4, 2).astype(BF16)

    near, wfirst, cmp_tab = _prompt_tables(rel_bias, t)
    far = _far_bias(rel_bias)

    per_b = lambda *shape: pl.BlockSpec((None,) + shape, lambda bi, i: (bi,) + (0,) * len(shape))
    const = lambda *shape: pl.BlockSpec(shape, lambda bi, i: (0,) * len(shape))
    chunk = lambda *shape: pl.BlockSpec((None, None) + shape, lambda bi, i: (bi, i) + (0,) * len(shape))
    o_t = pl.pallas_call(
        functools.partial(_prompt_attn_kernel, n_blk=n_blk, n_sel=min(N_SEL, n_blk)),
        grid=(b, nch),
        in_specs=[chunk(N_KV, HEAD_DIM, LANES),
                  chunk(N_KV, 8, LANES),
                  per_b(N_KV, nc, HEAD_DIM),
                  per_b(N_KV, HEAD_DIM, nc),
                  pl.BlockSpec((None, N_KV, nc, LANES), lambda bi, i: (i, 0, 0, 0)),
                  per_b(N_KV, nt, KEY_TILE, HEAD_DIM),
                  per_b(N_KV, nt, HEAD_DIM, KEY_TILE),
                  per_b(N_KV, nt, KEY_TILE, HEAD_DIM),
                  per_b(N_KV, nt, HEAD_DIM, KEY_TILE),
                  const(2, 2, N_KV, KEY_TILE, LANES),
                  const(2, N_KV, KEY_TILE, LANES),
                  const(N_KV, 1, LANES)],
        out_specs=chunk(N_KV, HEAD_DIM, LANES),
        out_shape=jax.ShapeDtypeStruct((b, nch, N_KV, HEAD_DIM, LANES), BF16),
        scratch_shapes=[pltpu.VMEM((2, nc, LANES // 2), F32), pltpu.VMEM((n_blk, LANES), F32),
                        pltpu.VMEM((N_KV, n_blk, LANES), F32)],
        compiler_params=_cparams("parallel", "arbitrary"),
        name="nsa_prompt_attention",
    )(q_t, g_t, kc, vct, cmp_tab, k_tiles(ksl), v_tiles_t(vsl), k_tiles(kw), v_tiles_t(vw),
      near, wfirst, far)
    return (o_t.reshape(b, nch, N_KV, HEAD_DIM, GROUP, Q_CHUNK).transpose(0, 1, 5, 2, 4, 3)
            .reshape(b * t, Q_DIM))


def _sample_tables(rel_bias, past, t, wbuf):
    lane = jnp.arange(SAMPLE_LANES, dtype=jnp.int32)
    g, kvh, qi = lane // (N_KV * t), (lane // t) % N_KV, lane % t
    used = lane < GROUP * N_KV * t
    head = jnp.where(used, kvh * GROUP + g, 0)
    qpos = past + qi

    def tab(kpos, valid):
        dist = qpos[None, :] - kpos[:, None]
        bias = rel_bias[_rel_bucket(dist), head[None, :]]
        return jnp.where(valid(dist) & (kpos[:, None] >= 0), jnp.where(used[None, :], bias, 0.0), NEG)

    pad = lambda n: jnp.concatenate([jnp.arange(n, dtype=jnp.int32), jnp.full((8 - n,), 1 << 20, jnp.int32)])
    causal = lambda d: d >= 0
    window = lambda d: (d >= 0) & (d < WINDOW)
    nc = past // CMP_BLK
    t_cmp = tab(jnp.arange(nc, dtype=jnp.int32) * CMP_BLK + CMP_BLK - 1, causal)
    t_sel = tab(jnp.arange(past, dtype=jnp.int32), causal).reshape(past // KEY_TILE, KEY_TILE, SAMPLE_LANES)
    t_new = tab(past + pad(t), causal)
    t_win = tab(past - wbuf + jnp.arange(wbuf, dtype=jnp.int32), window)
    t_wnew = tab(past + pad(t), window)
    return t_cmp, t_sel, t_new, t_win, t_wnew


def _sample_attn_kernel(pt_ref, *refs, n_pages, n_sel, t, wbuf):
    del pt_ref
    it = iter(refs)
    qbd_ref, g_ref = next(it), next(it)
    kc_pages = [next(it) for _ in range(n_pages)]
    vc_pages = [next(it) for _ in range(n_pages)]
    ks_pages = [next(it) for _ in range(n_pages)]
    vs_pages = [next(it) for _ in range(n_pages)]
    ksn_ref, vsn_ref, kwin_ref, vwin_ref, kwn_ref, vwn_ref = (next(it) for _ in range(6))
    tcmp_ref, tsel_ref, tnew_ref, twin_ref, twnew_ref = (next(it) for _ in range(5))
    o_ref = next(it)
    kc_ref, vc_ref, imp_ref, val_ref, s_ref = (next(it) for _ in range(5))

    scale = HEAD_DIM ** -0.5
    nq = N_KV * t
    qs = qbd_ref[...] * scale
    lane = lax.broadcasted_iota(jnp.int32, (1, SAMPLE_LANES), 1)
    blocks_per_page = PAGE_SIZE // CMP_BLK

    def pv(p, v):
        return lax.dot_general(p.astype(BF16), v.astype(BF16), (((0,), (0,)), ((), ())),
                               preferred_element_type=F32)

    def normalise(acc, l):
        inv = jnp.transpose(jnp.broadcast_to(1.0 / jnp.maximum(l, 1e-30), (SAMPLE_LANES, SAMPLE_LANES)))
        return acc * jnp.concatenate([inv, inv], axis=1)

    for p in range(n_pages):
        kc_ref[p * blocks_per_page:(p + 1) * blocks_per_page, :] = kc_pages[p][...]
        vc_ref[p * blocks_per_page:(p + 1) * blocks_per_page, :] = vc_pages[p][...]
    s = jnp.dot(kc_ref[...].astype(BF16), qs, preferred_element_type=F32) + tcmp_ref[...]
    m = jnp.max(s, axis=0, keepdims=True)
    p_c = jnp.exp(s - m)
    p_c = p_c / jnp.maximum(jnp.sum(p_c, axis=0, keepdims=True), 1e-30)
    o_c = pv(p_c, vc_ref[...])

    imp = p_c
    for g in range(1, GROUP):
        imp = imp + pltpu.roll(p_c, SAMPLE_LANES - g * nq, 1)
    imp_ref[...] = imp
    n_pairs = n_pages * blocks_per_page // 2
    n_blk = n_pairs + 1
    n_rows = val_ref.shape[0]
    val_ref[...] = jnp.full((n_rows, SAMPLE_LANES), -2.0, F32)
    val_ref[0:n_pairs, :] = imp_ref[pl.ds(0, n_pairs, stride=2), :] + imp_ref[pl.ds(1, n_pairs, stride=2), :]
    blk = lax.broadcasted_iota(jnp.int32, (n_rows, SAMPLE_LANES), 0)
    qblk = n_blk - 1
    forced = (blk == 0) | (blk == qblk) | (blk == qblk - 1)
    val = jnp.where(forced, BIG, val_ref[...])
    val = jnp.where(blk < n_blk, val, -2.0)
    val_ref[...] = val
    rank = jnp.zeros((n_rows, SAMPLE_LANES), F32)
    for j in range(n_blk):
        row = val_ref[j:j + 1, :]
        rank = rank + jnp.where(blk > j, jnp.where(row >= val, 1.0, 0.0), jnp.where(row > val, 1.0, 0.0))
    selneg = jnp.where((rank < n_sel) & (lane < nq), 0.0, jnp.where(lane < nq, NEG, 0.0))
    selrep = selneg
    for g in range(1, GROUP):
        selrep = selrep + pltpu.roll(selneg, g * nq, 1)
    val_ref[...] = selrep

    def sel_rows(r):
        return jnp.broadcast_to(val_ref[r:r + 1, :], (SEL_BLK, SAMPLE_LANES))

    for p in range(n_pages):
        add = tsel_ref[p] + jnp.concatenate([sel_rows(2 * p), sel_rows(2 * p + 1)], axis=0)
        s_ref[p * KEY_TILE:(p + 1) * KEY_TILE, :] = (
            jnp.dot(ks_pages[p][...].astype(BF16), qs, preferred_element_type=F32) + add)
    base = n_pages * KEY_TILE
    s_ref[base:base + 8, :] = (jnp.dot(ksn_ref[...].astype(BF16), qs, preferred_element_type=F32)
                               + tnew_ref[...] + jnp.broadcast_to(val_ref[qblk:qblk + 1, :], (8, SAMPLE_LANES)))
    m = jnp.max(s_ref[0:base + 8, :], axis=0, keepdims=True)
    p_new = jnp.exp(s_ref[base:base + 8, :] - m)
    l = jnp.sum(p_new, axis=0, keepdims=True)
    acc = pv(p_new, vsn_ref[...])
    for p in range(n_pages):
        pp = jnp.exp(s_ref[p * KEY_TILE:(p + 1) * KEY_TILE, :] - m)
        l = l + jnp.sum(pp, axis=0, keepdims=True)
        acc = acc + pv(pp, vs_pages[p][...])
    o_s = normalise(acc, l)

    s_w = jnp.dot(kwin_ref[...].astype(BF16), qs, preferred_element_type=F32) + twin_ref[...]
    s_n = jnp.dot(kwn_ref[...].astype(BF16), qs, preferred_element_type=F32) + twnew_ref[...]
    m = jnp.maximum(jnp.max(s_w, axis=0, keepdims=True), jnp.max(s_n, axis=0, keepdims=True))
    p_w, p_n = jnp.exp(s_w - m), jnp.exp(s_n - m)
    l = jnp.sum(p_w, axis=0, keepdims=True) + jnp.sum(p_n, axis=0, keepdims=True)
    o_w = normalise(pv(p_w, vwin_ref[...]) + pv(p_n, vwn_ref[...]), l)

    gate = _sigmoid(g_ref[...])
    o_ref[...] = gate[:, 0:1] * o_c + gate[:, 1:2] * o_s + gate[:, 2:3] * o_w


def _sample_attention(q, gates, kcmp_phys, vcmp_phys, cache_ks, cache_vs, page_table,
                      ksl, vsl, win_k, win_v, kw, vw, rel_bias, b, t):
    n_pages = page_table.shape[1]
    past = n_pages * PAGE_SIZE
    wbuf = win_k.shape[1]
    n_phys = cache_ks.shape[0]
    nq = N_KV * t
    used = GROUP * nq
    blocks_per_page = PAGE_SIZE // CMP_BLK
    n_blk = past // SEL_BLK + 1
    n_rows = -(-n_blk // 8) * 8

    q5 = q.reshape(b, t, N_KV, GROUP, HEAD_DIM).transpose(0, 2, 4, 3, 1)
    qbd = jnp.einsum("bkdgt,kc->bkdgct", q5, jnp.eye(N_KV, dtype=q.dtype))
    qbd = jnp.pad(qbd.reshape(b, KV_DIM, used), ((0, 0), (0, 0), (0, SAMPLE_LANES - used))).astype(BF16)
    g_rows = gates[:, :3 * N_HEADS].reshape(b, t, N_KV, GROUP, 3).transpose(0, 3, 2, 1, 4).reshape(b, used, 3)
    g_rows = jnp.pad(g_rows, ((0, 0), (0, SAMPLE_LANES - used), (0, 128 - 3)))
    pad_rows = lambda a: jnp.pad(a.reshape(b, t, KV_DIM), ((0, 0), (0, 8 - t), (0, 0)))
    tables = _sample_tables(rel_bias, past, t, wbuf)

    per_b = lambda *shape: pl.BlockSpec((None,) + shape, lambda bi, pt: (bi,) + (0,) * len(shape))
    const = lambda *shape: pl.BlockSpec(shape, lambda bi, pt: (0,) * len(shape))

    def page(rows, p):
        return pl.BlockSpec((None, rows, KV_DIM), lambda bi, pt, p=p: (pt[bi, p], 0, 0))

    in_specs = ([per_b(KV_DIM, SAMPLE_LANES), per_b(SAMPLE_LANES, 128)]
                + [page(blocks_per_page, p) for p in range(n_pages)] * 2
                + [page(PAGE_SIZE, p) for p in range(n_pages)] * 2
                + [per_b(8, KV_DIM), per_b(8, KV_DIM), per_b(wbuf, KV_DIM), per_b(wbuf, KV_DIM),
                   per_b(8, KV_DIM), per_b(8, KV_DIM)]
                + [const(*tb.shape) for tb in tables])
    kcp = kcmp_phys.reshape(n_phys, blocks_per_page, KV_DIM)
    vcp = vcmp_phys.reshape(n_phys, blocks_per_page, KV_DIM)
    cks = cache_ks.reshape(n_phys, PAGE_SIZE, KV_DIM)
    cvs = cache_vs.reshape(n_phys, PAGE_SIZE, KV_DIM)
    out = pl.pallas_call(
        functools.partial(_sample_attn_kernel, n_pages=n_pages, n_sel=min(N_SEL, n_blk), t=t, wbuf=wbuf),
        grid_spec=pltpu.PrefetchScalarGridSpec(
            num_scalar_prefetch=1,
            grid=(b,),
            in_specs=in_specs,
            out_specs=per_b(SAMPLE_LANES, KV_DIM),
            scratch_shapes=[pltpu.VMEM((n_pages * blocks_per_page, KV_DIM), F32),
                            pltpu.VMEM((n_pages * blocks_per_page, KV_DIM), F32),
                            pltpu.VMEM((n_pages * blocks_per_page, SAMPLE_LANES), F32),
                            pltpu.VMEM((n_rows, SAMPLE_LANES), F32),
                            pltpu.VMEM((past + 8, SAMPLE_LANES), F32)]),
        out_shape=jax.ShapeDtypeStruct((b, SAMPLE_LANES, KV_DIM), F32),
        compiler_params=_cparams("arbitrary"),
        name="nsa_sample_attention",
    )(page_table, qbd, g_rows, *([kcp] * n_pages), *([vcp] * n_pages), *([cks] * n_pages),
      *([cvs] * n_pages), pad_rows(ksl), pad_rows(vsl),
      win_k.reshape(b, wbuf, KV_DIM), win_v.reshape(b, wbuf, KV_DIM), pad_rows(kw), pad_rows(vw), *tables)
    o = out[:, :used].reshape(b, GROUP, N_KV, t, N_KV, HEAD_DIM)
    o = jnp.einsum("bgktkd->btkgd", o)
    return o.reshape(b * t, Q_DIM)


def _compress_weights(w_c, pe_c):
    eye = jnp.eye(N_KV, dtype=w_c.dtype)
    w_big = jnp.einsum("jde,kc->jkdce", w_c, eye).reshape(CMP_BLK * KV_DIM, KV_DIM)
    pe_flat = jnp.broadcast_to(pe_c[:, None, :], (CMP_BLK, N_KV, HEAD_DIM)).reshape(1, CMP_BLK * KV_DIM)
    pe_rows = jnp.pad(pe_flat, ((0, 7), (0, 0)))
    bias = _matmul(pe_rows, w_big, name="compress_pe")[0:1]
    return w_big, bias


def _compress(x_rows, w_big, bias, tm):
    return _matmul(x_rows, w_big, bias, tm=tm, name="compress")


def _ada(c_all, ada_w, ada_b):
    mods = []
    for i in range(ada_w.shape[0]):
        mods.append(_matmul(c_all, ada_w[i], ada_b[i].reshape(1, -1), silu_in=True, tn=2 * D_MODEL,
                            name="ada_modulate"))
    return mods


def kernel(x_prompt, x_sample, c_prompt, c_sample, cache_k_cmp, cache_v_cmp, cache_k_sel, cache_v_sel,
           page_table, state_k_win, state_v_win, state_pool, rel_bias, ada_w, ada_b, norm_g, final_g,
           nsa_w_in, nsa_w_out, cmp_wk, cmp_wv, cmp_pe_k, cmp_pe_v, pool_w, pool_scale,
           ffn_wg, ffn_wu, ffn_wd):
    bp, tp, _ = x_prompt.shape
    bs, ts, _ = x_sample.shape
    n_phys = cache_k_cmp.shape[1]
    past = page_table.shape[1] * PAGE_SIZE
    depth = ada_w.shape[0]
    assert depth == 2 and tp % KEY_TILE == 0 and ts <= 8 and past % PAGE_SIZE == 0

    n_c = bp + bs
    c_all = jnp.pad(jnp.concatenate([c_prompt, c_sample], axis=0), ((0, -n_c % 8), (0, 0)))
    mods = _ada(c_all, ada_w, ada_b)

    def mod_prompt(i):
        return [v[:bp].reshape(bp, 1, D_MODEL) for v in jnp.split(mods[i], 6, axis=-1)]

    def mod_sample(i):
        return [jnp.repeat(v[bp:n_c], ts, axis=0) for v in jnp.split(mods[i], 6, axis=-1)]

    wk_big, k_bias = _compress_weights(cmp_wk[0], cmp_pe_k[0])
    wv_big, v_bias = _compress_weights(cmp_wv[0], cmp_pe_v[0])

    mp = bp * tp
    x = x_prompt.reshape(mp, D_MODEL)
    sh1, sc1, g1, sh2, sc2, g2 = mod_prompt(0)
    q, (kc, vc, ksl, vsl, kw, vw), gates = _nsa_project(x, norm_g[0, 0], sc1, sh1, nsa_w_in[0], tp, 512)
    blk_rows = lambda a: a.reshape(mp // CMP_BLK, CMP_BLK * KV_DIM)
    k_cmp = _compress(blk_rows(kc), wk_big, k_bias, 256)
    v_cmp = _compress(blk_rows(vc), wv_big, v_bias, 256)
    o = _prompt_attention(q, gates, k_cmp, v_cmp, ksl, vsl, kw, vw, rel_bias, bp, tp)
    x = _matmul_residual(o, nsa_w_out[0], x, g1, tp, 512)
    x = _ffn(x, norm_g[0, 1], sc2, sh2, g2, ffn_wg[0], ffn_wu[0], ffn_wd[0], final_g, tp, 512, 1408, False)
    sh1, sc1, g1, sh2, sc2, g2 = mod_prompt(1)
    x3, pool_p = _pool_mix(x.reshape(bp, tp, D_MODEL), jnp.zeros((bp, POOL_STATE, D_MODEL), F32),
                           norm_g[1, 0], sc1, sh1, g1, pool_w[0], pool_scale[0], 0, 512)
    y_prompt = _ffn(x3.reshape(mp, D_MODEL), norm_g[1, 1], sc2, sh2, g2, ffn_wg[1], ffn_wu[1], ffn_wd[1],
                    final_g, tp, 512, 1408, True).reshape(bp, tp, D_MODEL)
    win = min(WINDOW, tp)
    st5 = lambda a: a.reshape(1, bp, tp, N_KV, HEAD_DIM)
    prompt_states = (st5(kc), st5(vc), st5(ksl), st5(vsl), st5(kw)[:, :, tp - win:], st5(vw)[:, :, tp - win:],
                     pool_p[None])

    ms = bs * ts
    x = x_sample.reshape(ms, D_MODEL)
    sh1, sc1, g1, sh2, sc2, g2 = mod_sample(0)
    q, (kc, vc, ksl, vsl, kw, vw), gates = _nsa_project(x, norm_g[0, 0], sc1, sh1, nsa_w_in[0], ts, 512)
    assert (past + ts) // CMP_BLK == past // CMP_BLK
    page_rows = lambda c: c.reshape(n_phys * (PAGE_SIZE // CMP_BLK), CMP_BLK * KV_DIM)
    kcmp_phys = _compress(page_rows(cache_k_cmp[0]), wk_big, k_bias, 256)
    vcmp_phys = _compress(page_rows(cache_v_cmp[0]), wv_big, v_bias, 256)
    o = _sample_attention(q, gates, kcmp_phys, vcmp_phys, cache_k_sel[0], cache_v_sel[0], page_table,
                          ksl, vsl, state_k_win[0], state_v_win[0], kw, vw, rel_bias, bs, ts)
    x = _matmul_residual(o, nsa_w_out[0], x, g1, ts, 512)
    x = _ffn(x, norm_g[0, 1], sc2, sh2, g2, ffn_wg[0], ffn_wu[0], ffn_wd[0], final_g, ts, 512, 1408, False)
    sh1, sc1, g1, sh2, sc2, g2 = mod_sample(1)
    b3 = lambda v: v.reshape(bs, ts, D_MODEL)[:, :1]
    x3, pool_s = _pool_mix(x.reshape(bs, ts, D_MODEL), state_pool[0], norm_g[1, 0], b3(sc1), b3(sh1), b3(g1),
                           pool_w[0], pool_scale[0], past, 512)
    y_sample = _ffn(x3.reshape(ms, D_MODEL), norm_g[1, 1], sc2, sh2, g2, ffn_wg[1], ffn_wu[1], ffn_wd[1],
                    final_g, ts, 512, 1408, True).reshape(bs, ts, D_MODEL)
    st5 = lambda a: a.reshape(1, bs, ts, N_KV, HEAD_DIM)
    kw_ext = jnp.concatenate([state_k_win[0], st5(kw)[0]], axis=1)[:, ts:]
    vw_ext = jnp.concatenate([state_v_win[0], st5(vw)[0]], axis=1)[:, ts:]
    sample_states = (st5(kc), st5(vc), st5(ksl), st5(vsl), kw_ext[None], vw_ext[None], pool_s[None])

    return (y_prompt, y_sample) + prompt_states + sample_states
```

```python
import functools
import math

import jax
import jax.numpy as jnp
from jax import lax
from jax.experimental import pallas as pl
from jax.experimental.pallas import tpu as pltpu

D_MODEL = 1024
N_HEADS = 16
N_KV = 4
GROUP = N_HEADS // N_KV
HEAD_DIM = 64
Q_DIM = N_HEADS * HEAD_DIM
KV_DIM = N_KV * HEAD_DIM
CMP_BLK = 32
SEL_BLK = 64
N_SEL = 16
WINDOW = 512
Q_CHUNK = 64
N_BUCKETS = 32
MAX_DISTANCE = 128
POOL_WINDOWS = (2, 4, 8, 16)
POOL_GROUP_DIM = D_MODEL // len(POOL_WINDOWS)
POOL_STATE = max(POOL_WINDOWS) - 1
PAGE_SIZE = 128
RMS_EPS = 1e-6
NEG = -1e30
BIG = 1e9

KEY_TILE = 128
CHUNK = 128
LANES = GROUP * CHUNK
SAMPLE_LANES = 128
VMEM_LIMIT_BYTES = 48 * 1024 * 1024

F32 = jnp.float32
BF16 = jnp.bfloat16


def _cparams(*sem):
    return pltpu.CompilerParams(dimension_semantics=sem, vmem_limit_bytes=VMEM_LIMIT_BYTES)


def _silu(x):
    return x * (1.0 / (1.0 + jnp.exp(-x)))


def _sigmoid(x):
    return 1.0 / (1.0 + jnp.exp(-x))


def _normmod(x, g, sc, sh):
    ms = jnp.mean(x * x, axis=-1, keepdims=True)
    return (x * lax.rsqrt(ms + RMS_EPS) * g) * (1.0 + sc) + sh


def _mod_spec(mod, tm, rows_per_batch):
    if mod.ndim == 3:
        return pl.BlockSpec((None, 1, D_MODEL), lambda i, *_: ((i * tm) // rows_per_batch, 0, 0))
    return pl.BlockSpec((tm, D_MODEL), lambda i, *_: (i, 0))


def _mm_kernel(a_ref, w_ref, b_ref, o_ref, *, silu_in):
    a = a_ref[...]
    if silu_in:
        a = _silu(a)
    o_ref[...] = jnp.dot(a.astype(BF16), w_ref[...], preferred_element_type=F32) + b_ref[...]


def _matmul(a, w, bias=None, *, silu_in=False, tm=256, tn=None, name="matmul"):
    m, k = a.shape
    n = w.shape[1]
    tm = min(tm, m)
    tn = n if tn is None else tn
    assert m % tm == 0 and n % tn == 0
    if bias is None:
        bias = jnp.zeros((1, n), F32)
    return pl.pallas_call(
        functools.partial(_mm_kernel, silu_in=silu_in),
        grid=(m // tm, n // tn),
        in_specs=[pl.BlockSpec((tm, k), lambda i, j: (i, 0)),
                  pl.BlockSpec((k, tn), lambda i, j: (0, j)),
                  pl.BlockSpec((1, tn), lambda i, j: (0, j))],
        out_specs=pl.BlockSpec((tm, tn), lambda i, j: (i, j)),
        out_shape=jax.ShapeDtypeStruct((m, n), F32),
        compiler_params=_cparams("parallel", "parallel"),
        name=name,
    )(a, w.astype(BF16), bias)


_PROJ_MAIN = Q_DIM + 6 * KV_DIM
_GATE_PAD = 128


def _proj_kernel(x_ref, g_ref, sc_ref, sh_ref, w_ref, wg_ref, *out_refs, attn_layouts, tm):
    hb = _normmod(x_ref[...], g_ref[...], sc_ref[...], sh_ref[...]).astype(BF16)
    q = jnp.dot(hb, w_ref[:, :Q_DIM], preferred_element_type=F32)
    gates = jnp.dot(hb, wg_ref[...], preferred_element_type=F32)
    kv_refs = out_refs[:6]
    kvs = []
    for n, r in enumerate(kv_refs):
        lo = Q_DIM + n * KV_DIM
        kvs.append(jnp.dot(hb, w_ref[:, lo:lo + KV_DIM], preferred_element_type=F32))
        r[...] = kvs[n]
    if not attn_layouts:
        out_refs[6][...] = q
        out_refs[7][...] = gates
        return
    qt_ref, gt_ref, ks_ref, vst_ref, kw_ref, vwt_ref, q_scr, gate_scr = out_refs[6:]
    n_chunks = tm // CHUNK
    q_scr[...] = q
    gate_scr[...] = gates
    qt = q_scr[...].T
    gt = gate_scr[...].T
    for c in range(n_chunks):
        tok = slice(c * CHUNK, (c + 1) * CHUNK)
        for kv in range(N_KV):
            for gi in range(GROUP):
                h = kv * GROUP + gi
                lanes = slice(gi * CHUNK, (gi + 1) * CHUNK)
                qt_ref[c, kv, :, lanes] = qt[h * HEAD_DIM:(h + 1) * HEAD_DIM, tok].astype(BF16)
                gt_ref[c, kv, :, lanes] = gt[3 * h:3 * h + 3, tok]
    for k_ref, vt_ref, k, v_out in ((ks_ref, vst_ref, kvs[2], kv_refs[3]), (kw_ref, vwt_ref, kvs[4], kv_refs[5])):
        vt = v_out[...].T
        for kv in range(N_KV):
            cols = slice(kv * HEAD_DIM, (kv + 1) * HEAD_DIM)
            k_ref[kv] = k[:, cols].astype(BF16)
            for c in range(n_chunks):
                vt_ref[kv, c] = vt[cols, c * CHUNK:(c + 1) * CHUNK].astype(BF16)


def _nsa_project(x, g, sc, sh, w_in, rows_per_batch, tm, attn_layouts):
    m = x.shape[0]
    tm = min(tm, m)
    w_main = w_in[:, :_PROJ_MAIN].astype(BF16)
    w_gate = jnp.pad(w_in[:, _PROJ_MAIN:], ((0, 0), (0, _GATE_PAD - 3 * N_HEADS))).astype(BF16)
    row = lambda n: pl.BlockSpec((tm, n), lambda i: (i, 0))
    out_specs = [row(KV_DIM)] * 6
    out_shape = [jax.ShapeDtypeStruct((m, KV_DIM), F32)] * 6
    if attn_layouts:
        assert tm % CHUNK == 0
        nck = tm // CHUNK
        chunked = lambda *s: pl.BlockSpec((nck,) + s, lambda i: (i,) + (0,) * len(s))
        k_spec = pl.BlockSpec((N_KV, tm, HEAD_DIM), lambda i: (0, i, 0))
        vt_spec = pl.BlockSpec((N_KV, nck, HEAD_DIM, CHUNK), lambda i: (0, i, 0, 0))
        k_shape = jax.ShapeDtypeStruct((N_KV, m, HEAD_DIM), BF16)
        vt_shape = jax.ShapeDtypeStruct((N_KV, m // CHUNK, HEAD_DIM, CHUNK), BF16)
        out_specs += [chunked(N_KV, HEAD_DIM, LANES), chunked(N_KV, 3, LANES), k_spec, vt_spec, k_spec, vt_spec]
        out_shape += [jax.ShapeDtypeStruct((m // CHUNK, N_KV, HEAD_DIM, LANES), BF16),
                      jax.ShapeDtypeStruct((m // CHUNK, N_KV, 3, LANES), F32),
                      k_shape, vt_shape, k_shape, vt_shape]
    else:
        out_specs += [row(Q_DIM), row(_GATE_PAD)]
        out_shape += [jax.ShapeDtypeStruct((m, Q_DIM), F32), jax.ShapeDtypeStruct((m, _GATE_PAD), F32)]
    outs = pl.pallas_call(
        functools.partial(_proj_kernel, attn_layouts=attn_layouts, tm=tm),
        grid=(m // tm,),
        in_specs=[row(D_MODEL),
                  pl.BlockSpec((1, D_MODEL), lambda i: (0, 0)),
                  _mod_spec(sc, tm, rows_per_batch), _mod_spec(sh, tm, rows_per_batch),
                  pl.BlockSpec((D_MODEL, _PROJ_MAIN), lambda i: (0, 0)),
                  pl.BlockSpec((D_MODEL, _GATE_PAD), lambda i: (0, 0))],
        out_specs=out_specs,
        out_shape=out_shape,
        scratch_shapes=([pltpu.VMEM((tm, Q_DIM), F32), pltpu.VMEM((tm, _GATE_PAD), F32)]
                        if attn_layouts else []),
        compiler_params=_cparams("parallel"),
        name="nsa_project",
    )(x, g.reshape(1, D_MODEL), sc, sh, w_main, w_gate)
    return outs[:6], outs[6:]


def _mm_res_kernel(a_ref, w_ref, x_ref, gate_ref, o_ref):
    y = jnp.dot(a_ref[...].astype(BF16), w_ref[...], preferred_element_type=F32)
    o_ref[...] = x_ref[...] + gate_ref[...] * y


def _matmul_residual(a, w, x, gate, rows_per_batch, tm):
    m, k = a.shape
    tm = min(tm, m)
    return pl.pallas_call(
        _mm_res_kernel,
        grid=(m // tm,),
        in_specs=[pl.BlockSpec((tm, k), lambda i: (i, 0)),
                  pl.BlockSpec((k, D_MODEL), lambda i: (0, 0)),
                  pl.BlockSpec((tm, D_MODEL), lambda i: (i, 0)),
                  _mod_spec(gate, tm, rows_per_batch)],
        out_specs=pl.BlockSpec((tm, D_MODEL), lambda i: (i, 0)),
        out_shape=jax.ShapeDtypeStruct((m, D_MODEL), F32),
        compiler_params=_cparams("parallel"),
        name="out_proj_residual",
    )(a, w.astype(BF16), x, gate)


def _mm_res_t_kernel(ot_ref, w_ref, x_ref, gate_ref, o_ref, *, n_chunks):
    chunks = []
    for c in range(n_chunks):
        rows = [ot_ref[c, kv, :, gi * CHUNK:(gi + 1) * CHUNK] for kv in range(N_KV) for gi in range(GROUP)]
        chunks.append(jnp.concatenate(rows, axis=0).astype(F32).T)
    a = jnp.concatenate(chunks, axis=0).astype(BF16)
    y = jnp.dot(a, w_ref[...], preferred_element_type=F32)
    o_ref[...] = x_ref[...] + gate_ref[...] * y


def _matmul_residual_t(o_t, w, x, gate, rows_per_batch, tm):
    m = x.shape[0]
    nck = tm // CHUNK
    return pl.pallas_call(
        functools.partial(_mm_res_t_kernel, n_chunks=nck),
        grid=(m // tm,),
        in_specs=[pl.BlockSpec((nck, N_KV, HEAD_DIM, LANES), lambda i: (i, 0, 0, 0)),
                  pl.BlockSpec((Q_DIM, D_MODEL), lambda i: (0, 0)),
                  pl.BlockSpec((tm, D_MODEL), lambda i: (i, 0)),
                  _mod_spec(gate, tm, rows_per_batch)],
        out_specs=pl.BlockSpec((tm, D_MODEL), lambda i: (i, 0)),
        out_shape=jax.ShapeDtypeStruct((m, D_MODEL), F32),
        compiler_params=_cparams("parallel"),
        name="out_proj_residual_t",
    )(o_t, w.astype(BF16), x, gate)


def _ffn_kernel(x_ref, g_ref, sc_ref, sh_ref, gate_ref, wg_ref, wu_ref, wd_ref, fg_ref,
                o_ref, hb_ref, acc_ref, *, final_norm):
    f = pl.program_id(1)

    @pl.when(f == 0)
    def _():
        hb_ref[...] = _normmod(x_ref[...], g_ref[...], sc_ref[...], sh_ref[...]).astype(BF16)
        acc_ref[...] = jnp.zeros_like(acc_ref)

    hb = hb_ref[...]
    a = jnp.dot(hb, wg_ref[...], preferred_element_type=F32)
    u = jnp.dot(hb, wu_ref[...], preferred_element_type=F32)
    act = (_silu(a) * u).astype(BF16)
    acc_ref[...] += jnp.dot(act, wd_ref[...], preferred_element_type=F32)

    @pl.when(f == pl.num_programs(1) - 1)
    def _():
        y = x_ref[...] + gate_ref[...] * acc_ref[...]
        if final_norm:
            ms = jnp.mean(y * y, axis=-1, keepdims=True)
            y = y * lax.rsqrt(ms + RMS_EPS) * fg_ref[...]
        o_ref[...] = y


def _ffn(x, g, sc, sh, gate, wg, wu, wd, final_g, rows_per_batch, tm, tf, final_norm):
    m = x.shape[0]
    d_ff = wg.shape[1]
    tm = min(tm, m)
    assert d_ff % tf == 0
    vec = pl.BlockSpec((1, D_MODEL), lambda i, f: (0, 0))
    return pl.pallas_call(
        functools.partial(_ffn_kernel, final_norm=final_norm),
        grid=(m // tm, d_ff // tf),
        in_specs=[pl.BlockSpec((tm, D_MODEL), lambda i, f: (i, 0)),
                  vec,
                  _mod_spec(sc, tm, rows_per_batch), _mod_spec(sh, tm, rows_per_batch),
                  _mod_spec(gate, tm, rows_per_batch),
                  pl.BlockSpec((D_MODEL, tf), lambda i, f: (0, f)),
                  pl.BlockSpec((D_MODEL, tf), lambda i, f: (0, f)),
                  pl.BlockSpec((tf, D_MODEL), lambda i, f: (f, 0)),
                  vec],
        out_specs=pl.BlockSpec((tm, D_MODEL), lambda i, f: (i, 0)),
        out_shape=jax.ShapeDtypeStruct((m, D_MODEL), F32),
        scratch_shapes=[pltpu.VMEM((tm, D_MODEL), BF16), pltpu.VMEM((tm, D_MODEL), F32)],
        compiler_params=_cparams("parallel", "arbitrary"),
        name="ffn",
    )(x, g.reshape(1, D_MODEL), sc, sh, gate, wg.astype(BF16), wu.astype(BF16), wd.astype(BF16),
      final_g.reshape(1, D_MODEL))


_POOL_HALO = 16


def _pool_kernel(x_ref, xprev_ref, state_ref, g_ref, sc_ref, sh_ref, gate_ref, w_ref, ls_ref,
                 o_ref, st_ref, ext_ref, *, tm, pos0):
    i = pl.program_id(1)
    g, sc, sh = g_ref[...], sc_ref[...], sh_ref[...]
    u = _normmod(x_ref[...], g, sc, sh)
    prev = jnp.where(i == 0, state_ref[...], _normmod(xprev_ref[...], g, sc, sh))
    ext_ref[0:_POOL_HALO, :] = prev
    ext_ref[_POOL_HALO:_POOL_HALO + tm, :] = u
    st_ref[...] = ext_ref[tm:tm + _POOL_HALO, :]

    pos = pos0 + i * tm + lax.broadcasted_iota(jnp.int32, (tm, 1), 0)
    mixed = []
    for gi, w in enumerate(POOL_WINDOWS):
        lo = gi * POOL_GROUP_DIM
        s = u[:, lo:lo + POOL_GROUP_DIM]
        for k in range(1, w):
            s = s + ext_ref[_POOL_HALO - k:_POOL_HALO - k + tm, lo:lo + POOL_GROUP_DIM]
        cnt = jnp.minimum(pos + 1, w).astype(F32)
        pooled = s / cnt - u[:, lo:lo + POOL_GROUP_DIM]
        mixed.append(jnp.dot(pooled.astype(BF16), w_ref[gi], preferred_element_type=F32))
    y = jnp.concatenate(mixed, axis=-1) * ls_ref[...]
    o_ref[...] = x_ref[...] + gate_ref[...] * y


def _pool_mix(x3, state, g, sc, sh, gate, w_grp, layer_scale, pos0, tm):
    b, t, _ = x3.shape
    tm = min(tm, t)
    state16 = jnp.pad(state, ((0, 0), (_POOL_HALO - POOL_STATE, 0), (0, 0)))
    if t >= _POOL_HALO:
        xprev = x3
        nprev = tm // _POOL_HALO
        prev_spec = pl.BlockSpec((None, _POOL_HALO, D_MODEL),
                                 lambda bi, i: (bi, jnp.maximum(i * nprev - 1, 0), 0))
    else:
        xprev = state16
        prev_spec = pl.BlockSpec((None, _POOL_HALO, D_MODEL), lambda bi, i: (bi, 0, 0))
    vec = pl.BlockSpec((1, D_MODEL), lambda bi, i: (0, 0))
    bvec = pl.BlockSpec((None, 1, D_MODEL), lambda bi, i: (bi, 0, 0))
    out, st = pl.pallas_call(
        functools.partial(_pool_kernel, tm=tm, pos0=pos0),
        grid=(b, t // tm),
        in_specs=[pl.BlockSpec((None, tm, D_MODEL), lambda bi, i: (bi, i, 0)),
                  prev_spec,
                  pl.BlockSpec((None, _POOL_HALO, D_MODEL), lambda bi, i: (bi, 0, 0)),
                  vec, bvec, bvec, bvec,
                  pl.BlockSpec((len(POOL_WINDOWS), POOL_GROUP_DIM, POOL_GROUP_DIM),
                               lambda bi, i: (0, 0, 0)),
                  vec],
        out_specs=[pl.BlockSpec((None, tm, D_MODEL), lambda bi, i: (bi, i, 0)),
                   pl.BlockSpec((None, _POOL_HALO, D_MODEL), lambda bi, i: (bi, 0, 0))],
        out_shape=[jax.ShapeDtypeStruct((b, t, D_MODEL), F32),
                   jax.ShapeDtypeStruct((b, _POOL_HALO, D_MODEL), F32)],
        scratch_shapes=[pltpu.VMEM((tm + _POOL_HALO, D_MODEL), F32)],
        compiler_params=_cparams("parallel", "arbitrary"),
        name="pool_mix",
    )(x3, xprev, state16, g.reshape(1, D_MODEL), sc, sh, gate, w_grp.astype(BF16),
      layer_scale.reshape(1, D_MODEL))
    return out, st[:, _POOL_HALO - POOL_STATE:]


def _rel_bucket(dist):
    d = jnp.maximum(dist, 0)
    max_exact = N_BUCKETS // 2
    large = max_exact + (jnp.log(jnp.maximum(d, 1).astype(F32) / max_exact)
                         / math.log(MAX_DISTANCE / max_exact) * (N_BUCKETS - max_exact)).astype(jnp.int32)
    large = jnp.minimum(large, N_BUCKETS - 1)
    return jnp.where(d < max_exact, d, large)


def _dist_bias(rel_bias):
    return rel_bias[_rel_bucket(jnp.arange(MAX_DISTANCE + 1, dtype=jnp.int32))]


def _bias_tile(fd, dist, valid):
    k, q = dist.shape
    bias = jnp.take(fd, jnp.clip(dist, 0, MAX_DISTANCE).reshape(-1), axis=0).reshape(k, q, N_KV, GROUP)
    bias = jnp.where(valid[:, :, None, None], bias, NEG)
    return bias.transpose(2, 0, 3, 1).reshape(N_KV, k, GROUP * q)


def _prompt_tables(rel_bias, t):
    fd = _dist_bias(rel_bias)
    kj = jnp.arange(KEY_TILE, dtype=jnp.int32)[:, None]
    qi = jnp.arange(CHUNK, dtype=jnp.int32)[None, :]
    dist = qi - kj
    near = jnp.stack([_bias_tile(fd, dist + KEY_TILE, dist + KEY_TILE >= 0),
                      _bias_tile(fd, dist, dist >= 0)])
    wfirst = _bias_tile(fd, dist + WINDOW, dist < 0)
    far = _bias_tile(fd, jnp.full((1, CHUNK), MAX_DISTANCE, jnp.int32), jnp.ones((1, CHUNK), bool))
    per_chunk = CHUNK // CMP_BLK
    assert CHUNK == 128 and CMP_BLK == 32 and MAX_DISTANCE == 128
    rel0, n_rel = -per_chunk, 2 * per_chunk
    rel = rel0 + jnp.arange(n_rel, dtype=jnp.int32)[:, None]
    dist = qi - CMP_BLK * rel - (CMP_BLK - 1)
    crel = _bias_tile(fd, dist, dist >= 0)
    relm = (jnp.arange(t // CMP_BLK, dtype=jnp.int32)[None, :]
            - per_chunk * jnp.arange(t // CHUNK, dtype=jnp.int32)[:, None])[:, None, :, None]
    cmp_tab = jnp.where(relm >= rel0 + n_rel, NEG, far[None])
    for r in range(n_rel):
        cmp_tab = jnp.where(relm == rel0 + r, crel[None, :, r:r + 1, :], cmp_tab)
    return near, wfirst, far, cmp_tab


def _tile_softmax(tiles, qs):
    scores = [jnp.dot(k, qs, preferred_element_type=F32) + add for k, _, add in tiles]
    m = functools.reduce(jnp.maximum, [jnp.max(s, axis=0, keepdims=True) for s in scores])
    l, acc = 0.0, 0.0
    for s, (_, vt, _) in zip(scores, tiles):
        p = jnp.exp(s - m)
        l = l + jnp.sum(p, axis=0, keepdims=True)
        acc = acc + jnp.dot(vt, p.astype(BF16), preferred_element_type=F32)
    return m, l, acc


def _prompt_attn_kernel(q_ref, g_ref, kc_ref, vct_ref, cb_ref, ks_ref, vst_ref, kw_ref, vwt_ref,
                        near_ref, wfirst_ref, far_ref, o_ref,
                        qs_ref, imp_ref, val_ref, selrep_ref, m_ref, l_ref, alpha_ref, acc_ref, mix_ref,
                        s_ref, p_ref, *, n_blk, n_sel):
    i = pl.program_id(1)
    scale = HEAD_DIM ** -0.5

    def k_rows(ref, kv, jt):
        return ref[kv, pl.ds(pl.multiple_of(jt * KEY_TILE, KEY_TILE), KEY_TILE), :]

    def sel_rows(kv, jt):
        return selrep_ref[kv, pl.ds(2 * jt, 1), :], selrep_ref[kv, pl.ds(2 * jt + 1, 1), :]

    def halves(top, bot):
        return jnp.concatenate([jnp.broadcast_to(top, (SEL_BLK, LANES)),
                                jnp.broadcast_to(bot, (SEL_BLK, LANES))], axis=0)

    for kv in range(N_KV):
        qs = q_ref[kv] * scale
        qs_ref[kv] = qs
        bias = cb_ref[kv]
        s = jnp.dot(kc_ref[kv], qs, preferred_element_type=F32) + bias
        valid = bias > 0.5 * NEG
        m = jnp.max(s, axis=0, keepdims=True)
        p = jnp.where(valid, jnp.exp(s - m), 0.0)
        p = p / jnp.maximum(jnp.sum(p, axis=0, keepdims=True), 1e-30)
        o_c = jnp.dot(vct_ref[kv], p.astype(BF16), preferred_element_type=F32)
        mix_ref[kv] = _sigmoid(g_ref[kv, 0:1, :]) * o_c
        imp_ref[kv] = functools.reduce(
            lambda a, b: a + b, [p[:, gi * CHUNK:(gi + 1) * CHUNK] for gi in range(GROUP)])

    imp = jnp.concatenate(
        [imp_ref[kv, pl.ds(0, n_blk, stride=2), :] + imp_ref[kv, pl.ds(1, n_blk, stride=2), :]
         for kv in range(N_KV)], axis=1)
    blk = lax.broadcasted_iota(jnp.int32, (n_blk, LANES), 0)
    lane = lax.broadcasted_iota(jnp.int32, (1, LANES), 1)
    assert CHUNK == 2 * SEL_BLK
    qblk = 2 * i + jnp.where((lane & (CHUNK - 1)) >= SEL_BLK, 1, 0)
    forced = (blk == 0) | (blk == qblk) | (blk == qblk - 1)
    val_ref[...] = jnp.where(forced, BIG, jnp.where(blk <= qblk, imp, -1.0))
    val = val_ref[...]
    rank = jnp.zeros((n_blk, LANES), F32)
    for j in range(n_blk):
        row = val_ref[j:j + 1, :]
        rank = rank + jnp.where(blk > j, jnp.where(row >= val, 1.0, 0.0), jnp.where(row > val, 1.0, 0.0))
    selneg = jnp.where(rank < n_sel, 0.0, NEG)
    for kv in range(N_KV):
        selrep_ref[kv] = jnp.concatenate([selneg[:, kv * CHUNK:(kv + 1) * CHUNK]] * GROUP, axis=1)

    prev = jnp.maximum(i - 1, 0)
    prev_off = jnp.where(i >= 1, 0.0, NEG)
    for kv in range(N_KV):
        qs = qs_ref[kv]
        far = far_ref[kv]
        tiles = [(k_rows(kw_ref, kv, i), vwt_ref[kv, i], near_ref[1, kv]),
                 (k_rows(kw_ref, kv, prev), vwt_ref[kv, prev], near_ref[0, kv] + prev_off)]
        for back, add in ((2, far), (3, far), (4, wfirst_ref[kv])):
            jt = jnp.maximum(i - back, 0)
            tiles.append((k_rows(kw_ref, kv, jt), vwt_ref[kv, jt], add + jnp.where(i >= back, 0.0, NEG)))
        _, l, acc = _tile_softmax(tiles, qs)
        mix_ref[kv] += _sigmoid(g_ref[kv, 2:3, :]) * (acc / jnp.maximum(l, 1e-30))

        tiles = [(k_rows(ks_ref, kv, i), vst_ref[kv, i], near_ref[1, kv] + halves(*sel_rows(kv, i))),
                 (k_rows(ks_ref, kv, prev), vst_ref[kv, prev],
                  near_ref[0, kv] + halves(*sel_rows(kv, prev)) + prev_off)]
        m_ref[kv], l_ref[kv], acc_ref[kv] = _tile_softmax(tiles, qs)

    def far_body(jt, carry):
        for kv in range(N_KV):
            top, bot = sel_rows(kv, jt)
            far = far_ref[kv]
            s_ref[kv] = (jnp.dot(k_rows(ks_ref, kv, jt), qs_ref[kv], preferred_element_type=F32)
                         + halves(top + far, bot + far))
        for kv in range(N_KV):
            s = s_ref[kv]
            m_old = m_ref[kv]
            m_new = jnp.maximum(m_old, jnp.max(s, axis=0, keepdims=True))
            alpha = jnp.exp(m_old - m_new)
            p = jnp.exp(s - m_new)
            m_ref[kv] = m_new
            alpha_ref[kv] = alpha
            l_ref[kv] = alpha * l_ref[kv] + jnp.sum(p, axis=0, keepdims=True)
            p_ref[kv] = p.astype(BF16)
        for kv in range(N_KV):
            acc_ref[kv] = alpha_ref[kv] * acc_ref[kv] + jnp.dot(vst_ref[kv, jt], p_ref[kv],
                                                                preferred_element_type=F32)
        return carry

    lax.fori_loop(0, jnp.maximum(i - 1, 0), far_body, 0)

    for kv in range(N_KV):
        o_s = acc_ref[kv] / jnp.maximum(l_ref[kv], 1e-30)
        o_ref[kv] = (mix_ref[kv] + _sigmoid(g_ref[kv, 1:2, :]) * o_s).astype(o_ref.dtype)


def _prompt_attention(layouts, k_cmp, v_cmp, rel_bias, b, t):
    q_t, g_t, ks, vst, kw, vwt = layouts
    nch, nt, nc, n_blk = t // CHUNK, t // KEY_TILE, t // CMP_BLK, t // SEL_BLK
    kc = k_cmp.reshape(b, nc, N_KV, HEAD_DIM).transpose(0, 2, 1, 3).astype(BF16)
    vct = v_cmp.reshape(b, nc, N_KV, HEAD_DIM).transpose(0, 2, 3, 1).astype(BF16)
    near, wfirst, far, cmp_tab = _prompt_tables(rel_bias, t)

    once = pl.Buffered(1)
    per_b = lambda *shape: pl.BlockSpec((None,) + shape, lambda bi, i: (bi,) + (0,) * len(shape),
                                        pipeline_mode=once)
    const = lambda *shape: pl.BlockSpec(shape, lambda bi, i: (0,) * len(shape), pipeline_mode=once)
    chunk = lambda *shape: pl.BlockSpec((None,) + shape, lambda bi, i: (bi * nch + i,) + (0,) * len(shape))
    k_spec = pl.BlockSpec((N_KV, t, HEAD_DIM), lambda bi, i: (0, bi, 0), pipeline_mode=once)
    vt_spec = pl.BlockSpec((N_KV, nt, HEAD_DIM, KEY_TILE), lambda bi, i: (0, bi, 0, 0), pipeline_mode=once)
    stat = pltpu.VMEM((N_KV, 1, LANES), F32)
    wide = pltpu.VMEM((N_KV, HEAD_DIM, LANES), F32)
    return pl.pallas_call(
        functools.partial(_prompt_attn_kernel, n_blk=n_blk, n_sel=min(N_SEL, n_blk)),
        grid=(b, nch),
        in_specs=[chunk(N_KV, HEAD_DIM, LANES),
                  chunk(N_KV, 3, LANES),
                  per_b(N_KV, nc, HEAD_DIM),
                  per_b(N_KV, HEAD_DIM, nc),
                  pl.BlockSpec((None, N_KV, nc, LANES), lambda bi, i: (i, 0, 0, 0)),
                  k_spec, vt_spec, k_spec, vt_spec,
                  const(2, N_KV, KEY_TILE, LANES),
                  const(N_KV, KEY_TILE, LANES),
                  const(N_KV, 1, LANES)],
        out_specs=chunk(N_KV, HEAD_DIM, LANES),
        out_shape=jax.ShapeDtypeStruct((b * nch, N_KV, HEAD_DIM, LANES), BF16),
        scratch_shapes=[pltpu.VMEM((N_KV, HEAD_DIM, LANES), BF16),
                        pltpu.VMEM((N_KV, nc, CHUNK), F32),
                        pltpu.VMEM((n_blk, LANES), F32),
                        pltpu.VMEM((N_KV, n_blk, LANES), F32),
                        stat, stat, stat, wide, wide,
                        pltpu.VMEM((N_KV, KEY_TILE, LANES), F32),
                        pltpu.VMEM((N_KV, KEY_TILE, LANES), BF16)],
        compiler_params=_cparams("parallel", "arbitrary"),
        name="nsa_prompt_attention",
    )(q_t, g_t, kc, vct, cmp_tab, ks, vst, kw, vwt, near, wfirst, far)


def _sample_tables(rel_bias, past, t, wbuf):
    lane = jnp.arange(SAMPLE_LANES, dtype=jnp.int32)
    g, kvh, qi = lane // (N_KV * t), (lane // t) % N_KV, lane % t
    used = lane < GROUP * N_KV * t
    head = jnp.where(used, kvh * GROUP + g, 0)
    qpos = past + qi
    fd = _dist_bias(rel_bias)

    def tab(kpos, valid, n_near):
        dist = qpos[None, :] - kpos[:, None]
        n_far = kpos.shape[0] - n_near
        near = fd[jnp.clip(dist[n_far:], 0, MAX_DISTANCE), head[None, :]]
        far = jnp.broadcast_to(fd[MAX_DISTANCE, head][None, :], (n_far, SAMPLE_LANES))
        bias = jnp.where(used[None, :], jnp.concatenate([far, near], axis=0), 0.0)
        return jnp.where(valid(dist) & (kpos[:, None] >= 0), bias, NEG)

    pad = lambda n: jnp.concatenate([jnp.arange(n, dtype=jnp.int32), jnp.full((8 - n,), 1 << 20, jnp.int32)])
    causal = lambda d: d >= 0
    window = lambda d: (d >= 0) & (d < WINDOW)
    nc = past // CMP_BLK
    assert wbuf >= MAX_DISTANCE and past >= MAX_DISTANCE
    t_cmp = tab(jnp.arange(nc, dtype=jnp.int32) * CMP_BLK + CMP_BLK - 1, causal, MAX_DISTANCE // CMP_BLK)
    t_sel = tab(jnp.arange(past, dtype=jnp.int32), causal, MAX_DISTANCE)
    t_sel = t_sel.reshape(past // KEY_TILE, KEY_TILE, SAMPLE_LANES)
    t_new = tab(past + pad(t), causal, 8)
    t_win = tab(past - wbuf + jnp.arange(wbuf, dtype=jnp.int32), window, MAX_DISTANCE)
    t_wnew = tab(past + pad(t), window, 8)
    return t_cmp, t_sel, t_new, t_win, t_wnew


def _sample_attn_kernel(pt_ref, *refs, n_pages, n_sel, t, wbuf):
    del pt_ref
    it = iter(refs)
    qbd_ref, g_ref = next(it), next(it)
    kc_pages = [next(it) for _ in range(n_pages)]
    vc_pages = [next(it) for _ in range(n_pages)]
    ks_pages = [next(it) for _ in range(n_pages)]
    vs_pages = [next(it) for _ in range(n_pages)]
    ksn_ref, vsn_ref, kwin_ref, vwin_ref, kwn_ref, vwn_ref = (next(it) for _ in range(6))
    tcmp_ref, tsel_ref, tnew_ref, twin_ref, twnew_ref = (next(it) for _ in range(5))
    o_ref = next(it)
    kc_ref, vc_ref, imp_ref, val_ref, s_ref = (next(it) for _ in range(5))

    scale = HEAD_DIM ** -0.5
    nq = N_KV * t
    qs = qbd_ref[...] * scale
    lane = lax.broadcasted_iota(jnp.int32, (1, SAMPLE_LANES), 1)
    blocks_per_page = PAGE_SIZE // CMP_BLK

    def pv(p, v):
        return lax.dot_general(p.astype(BF16), v.astype(BF16), (((0,), (0,)), ((), ())),
                               preferred_element_type=F32)

    def normalise(acc, l):
        inv = jnp.transpose(jnp.broadcast_to(1.0 / jnp.maximum(l, 1e-30), (SAMPLE_LANES, SAMPLE_LANES)))
        return acc * jnp.concatenate([inv, inv], axis=1)

    for p in range(n_pages):
        kc_ref[p * blocks_per_page:(p + 1) * blocks_per_page, :] = kc_pages[p][...]
        vc_ref[p * blocks_per_page:(p + 1) * blocks_per_page, :] = vc_pages[p][...]
    s = jnp.dot(kc_ref[...].astype(BF16), qs, preferred_element_type=F32) + tcmp_ref[...]
    m = jnp.max(s, axis=0, keepdims=True)
    p_c = jnp.exp(s - m)
    p_c = p_c / jnp.maximum(jnp.sum(p_c, axis=0, keepdims=True), 1e-30)
    o_c = pv(p_c, vc_ref[...])

    imp = p_c
    for g in range(1, GROUP):
        imp = imp + pltpu.roll(p_c, SAMPLE_LANES - g * nq, 1)
    imp_ref[...] = imp
    n_pairs = n_pages * blocks_per_page // 2
    n_blk = n_pairs + 1
    n_rows = val_ref.shape[0]
    val_ref[...] = jnp.full((n_rows, SAMPLE_LANES), -2.0, F32)
    val_ref[0:n_pairs, :] = imp_ref[pl.ds(0, n_pairs, stride=2), :] + imp_ref[pl.ds(1, n_pairs, stride=2), :]
    blk = lax.broadcasted_iota(jnp.int32, (n_rows, SAMPLE_LANES), 0)
    qblk = n_blk - 1
    forced = (blk == 0) | (blk == qblk) | (blk == qblk - 1)
    val = jnp.where(forced, BIG, val_ref[...])
    val = jnp.where(blk < n_blk, val, -2.0)
    val_ref[...] = val
    rank = jnp.zeros((n_rows, SAMPLE_LANES), F32)
    for j in range(n_blk):
        row = val_ref[j:j + 1, :]
        rank = rank + jnp.where(blk > j, jnp.where(row >= val, 1.0, 0.0), jnp.where(row > val, 1.0, 0.0))
    selneg = jnp.where((rank < n_sel) & (lane < nq), 0.0, jnp.where(lane < nq, NEG, 0.0))
    selrep = selneg
    for g in range(1, GROUP):
        selrep = selrep + pltpu.roll(selneg, g * nq, 1)
    val_ref[...] = selrep

    def sel_rows(r):
        return jnp.broadcast_to(val_ref[r:r + 1, :], (SEL_BLK, SAMPLE_LANES))

    for p in range(n_pages):
        add = tsel_ref[p] + jnp.concatenate([sel_rows(2 * p), sel_rows(2 * p + 1)], axis=0)
        s_ref[p * KEY_TILE:(p + 1) * KEY_TILE, :] = (
            jnp.dot(ks_pages[p][...].astype(BF16), qs, preferred_element_type=F32) + add)
    base = n_pages * KEY_TILE
    s_ref[base:base + 8, :] = (jnp.dot(ksn_ref[...].astype(BF16), qs, preferred_element_type=F32)
                               + tnew_ref[...] + jnp.broadcast_to(val_ref[qblk:qblk + 1, :], (8, SAMPLE_LANES)))
    m = jnp.max(s_ref[0:base + 8, :], axis=0, keepdims=True)
    p_new = jnp.exp(s_ref[base:base + 8, :] - m)
    l = jnp.sum(p_new, axis=0, keepdims=True)
    acc = pv(p_new, vsn_ref[...])
    for p in range(n_pages):
        pp = jnp.exp(s_ref[p * KEY_TILE:(p + 1) * KEY_TILE, :] - m)
        l = l + jnp.sum(pp, axis=0, keepdims=True)
        acc = acc + pv(pp, vs_pages[p][...])
    o_s = normalise(acc, l)

    s_w = jnp.dot(kwin_ref[...].astype(BF16), qs, preferred_element_type=F32) + twin_ref[...]
    s_n = jnp.dot(kwn_ref[...].astype(BF16), qs, preferred_element_type=F32) + twnew_ref[...]
    m = jnp.maximum(jnp.max(s_w, axis=0, keepdims=True), jnp.max(s_n, axis=0, keepdims=True))
    p_w, p_n = jnp.exp(s_w - m), jnp.exp(s_n - m)
    l = jnp.sum(p_w, axis=0, keepdims=True) + jnp.sum(p_n, axis=0, keepdims=True)
    o_w = normalise(pv(p_w, vwin_ref[...]) + pv(p_n, vwn_ref[...]), l)

    gate = _sigmoid(g_ref[...])
    o_ref[...] = gate[:, 0:1] * o_c + gate[:, 1:2] * o_s + gate[:, 2:3] * o_w


def _sample_attention(q, gates, kcmp_phys, vcmp_phys, cache_ks, cache_vs, page_table,
                      ksl, vsl, win_k, win_v, kw, vw, rel_bias, b, t):
    n_pages = page_table.shape[1]
    past = n_pages * PAGE_SIZE
    wbuf = win_k.shape[1]
    n_phys = cache_ks.shape[0]
    nq = N_KV * t
    used = GROUP * nq
    blocks_per_page = PAGE_SIZE // CMP_BLK
    n_blk = past // SEL_BLK + 1
    n_rows = -(-n_blk // 8) * 8

    q5 = q.reshape(b, t, N_KV, GROUP, HEAD_DIM).transpose(0, 2, 4, 3, 1)
    qbd = jnp.einsum("bkdgt,kc->bkdgct", q5, jnp.eye(N_KV, dtype=q.dtype))
    qbd = jnp.pad(qbd.reshape(b, KV_DIM, used), ((0, 0), (0, 0), (0, SAMPLE_LANES - used))).astype(BF16)
    g_rows = gates[:, :3 * N_HEADS].reshape(b, t, N_KV, GROUP, 3).transpose(0, 3, 2, 1, 4).reshape(b, used, 3)
    g_rows = jnp.pad(g_rows, ((0, 0), (0, SAMPLE_LANES - used), (0, 128 - 3)))
    pad_rows = lambda a: jnp.pad(a.reshape(b, t, KV_DIM), ((0, 0), (0, 8 - t), (0, 0)))
    tables = _sample_tables(rel_bias, past, t, wbuf)

    per_b = lambda *shape: pl.BlockSpec((None,) + shape, lambda bi, pt: (bi,) + (0,) * len(shape))
    const = lambda *shape: pl.BlockSpec(shape, lambda bi, pt: (0,) * len(shape))

    def page(rows, p):
        return pl.BlockSpec((None, rows, KV_DIM), lambda bi, pt, p=p: (pt[bi, p], 0, 0))

    in_specs = ([per_b(KV_DIM, SAMPLE_LANES), per_b(SAMPLE_LANES, 128)]
                + [page(blocks_per_page, p) for p in range(n_pages)] * 2
                + [page(PAGE_SIZE, p) for p in range(n_pages)] * 2
                + [per_b(8, KV_DIM), per_b(8, KV_DIM), per_b(wbuf, KV_DIM), per_b(wbuf, KV_DIM),
                   per_b(8, KV_DIM), per_b(8, KV_DIM)]
                + [const(*tb.shape) for tb in tables])
    kcp = kcmp_phys.reshape(n_phys, blocks_per_page, KV_DIM)
    vcp = vcmp_phys.reshape(n_phys, blocks_per_page, KV_DIM)
    cks = cache_ks.reshape(n_phys, PAGE_SIZE, KV_DIM)
    cvs = cache_vs.reshape(n_phys, PAGE_SIZE, KV_DIM)
    out = pl.pallas_call(
        functools.partial(_sample_attn_kernel, n_pages=n_pages, n_sel=min(N_SEL, n_blk), t=t, wbuf=wbuf),
        grid_spec=pltpu.PrefetchScalarGridSpec(
            num_scalar_prefetch=1,
            grid=(b,),
            in_specs=in_specs,
            out_specs=per_b(SAMPLE_LANES, KV_DIM),
            scratch_shapes=[pltpu.VMEM((n_pages * blocks_per_page, KV_DIM), F32),
                            pltpu.VMEM((n_pages * blocks_per_page, KV_DIM), F32),
                            pltpu.VMEM((n_pages * blocks_per_page, SAMPLE_LANES), F32),
                            pltpu.VMEM((n_rows, SAMPLE_LANES), F32),
                            pltpu.VMEM((past + 8, SAMPLE_LANES), F32)]),
        out_shape=jax.ShapeDtypeStruct((b, SAMPLE_LANES, KV_DIM), F32),
        compiler_params=_cparams("arbitrary"),
        name="nsa_sample_attention",
    )(page_table, qbd, g_rows, *([kcp] * n_pages), *([vcp] * n_pages), *([cks] * n_pages),
      *([cvs] * n_pages), pad_rows(ksl), pad_rows(vsl),
      win_k.reshape(b, wbuf, KV_DIM), win_v.reshape(b, wbuf, KV_DIM), pad_rows(kw), pad_rows(vw), *tables)
    o = out[:, :used].reshape(b, GROUP, N_KV, t, N_KV, HEAD_DIM)
    o = jnp.einsum("bgktkd->btkgd", o)
    return o.reshape(b * t, Q_DIM)


def _compress_weights(w_c, pe_c):
    eye = jnp.eye(N_KV, dtype=w_c.dtype)
    w_big = jnp.einsum("jde,kc->jkdce", w_c, eye).reshape(CMP_BLK * KV_DIM, KV_DIM)
    pe_flat = jnp.broadcast_to(pe_c[:, None, :], (CMP_BLK, N_KV, HEAD_DIM)).reshape(1, CMP_BLK * KV_DIM)
    pe_rows = jnp.pad(pe_flat, ((0, 7), (0, 0)))
    bias = _matmul(pe_rows, w_big, name="compress_pe")[0:1]
    return w_big, bias


def _compress(x_rows, w_big, bias, tm):
    return _matmul(x_rows, w_big, bias, tm=tm, name="compress")


def _ada(c_all, ada_w, ada_b):
    mods = []
    for i in range(ada_w.shape[0]):
        mods.append(_matmul(c_all, ada_w[i], ada_b[i].reshape(1, -1), silu_in=True, tn=2 * D_MODEL,
                            name="ada_modulate"))
    return mods


def kernel(x_prompt, x_sample, c_prompt, c_sample, cache_k_cmp, cache_v_cmp, cache_k_sel, cache_v_sel,
           page_table, state_k_win, state_v_win, state_pool, rel_bias, ada_w, ada_b, norm_g, final_g,
           nsa_w_in, nsa_w_out, cmp_wk, cmp_wv, cmp_pe_k, cmp_pe_v, pool_w, pool_scale,
           ffn_wg, ffn_wu, ffn_wd):
    bp, tp, _ = x_prompt.shape
    bs, ts, _ = x_sample.shape
    n_phys = cache_k_cmp.shape[1]
    past = page_table.shape[1] * PAGE_SIZE
    depth = ada_w.shape[0]
    assert depth == 2 and tp % KEY_TILE == 0 and ts <= 8 and past % PAGE_SIZE == 0

    n_c = bp + bs
    c_all = jnp.pad(jnp.concatenate([c_prompt, c_sample], axis=0), ((0, -n_c % 8), (0, 0)))
    mods = _ada(c_all, ada_w, ada_b)

    def mod_prompt(i):
        return [v[:bp].reshape(bp, 1, D_MODEL) for v in jnp.split(mods[i], 6, axis=-1)]

    def mod_sample(i):
        return [jnp.repeat(v[bp:n_c], ts, axis=0) for v in jnp.split(mods[i], 6, axis=-1)]

    wk_big, k_bias = _compress_weights(cmp_wk[0], cmp_pe_k[0])
    wv_big, v_bias = _compress_weights(cmp_wv[0], cmp_pe_v[0])

    mp = bp * tp
    x = x_prompt.reshape(mp, D_MODEL)
    sh1, sc1, g1, sh2, sc2, g2 = mod_prompt(0)
    (kc, vc, ksl, vsl, kw, vw), layouts = _nsa_project(x, norm_g[0, 0], sc1, sh1, nsa_w_in[0], tp, 512, True)
    blk_rows = lambda a: a.reshape(mp // CMP_BLK, CMP_BLK * KV_DIM)
    k_cmp = _compress(blk_rows(kc), wk_big, k_bias, 256)
    v_cmp = _compress(blk_rows(vc), wv_big, v_bias, 256)
    o_t = _prompt_attention(layouts, k_cmp, v_cmp, rel_bias, bp, tp)
    x = _matmul_residual_t(o_t, nsa_w_out[0], x, g1, tp, 512)
    x = _ffn(x, norm_g[0, 1], sc2, sh2, g2, ffn_wg[0], ffn_wu[0], ffn_wd[0], final_g, tp, 512, 1408, False)
    sh1, sc1, g1, sh2, sc2, g2 = mod_prompt(1)
    x3, pool_p = _pool_mix(x.reshape(bp, tp, D_MODEL), jnp.zeros((bp, POOL_STATE, D_MODEL), F32),
                           norm_g[1, 0], sc1, sh1, g1, pool_w[0], pool_scale[0], 0, 512)
    y_prompt = _ffn(x3.reshape(mp, D_MODEL), norm_g[1, 1], sc2, sh2, g2, ffn_wg[1], ffn_wu[1], ffn_wd[1],
                    final_g, tp, 512, 1408, True).reshape(bp, tp, D_MODEL)
    win = min(WINDOW, tp)
    st5 = lambda a: a.reshape(1, bp, tp, N_KV, HEAD_DIM)
    prompt_states = (st5(kc), st5(vc), st5(ksl), st5(vsl), st5(kw)[:, :, tp - win:], st5(vw)[:, :, tp - win:],
                     pool_p[None])

    ms = bs * ts
    x = x_sample.reshape(ms, D_MODEL)
    sh1, sc1, g1, sh2, sc2, g2 = mod_sample(0)
    (kc, vc, ksl, vsl, kw, vw), (q, gates) = _nsa_project(x, norm_g[0, 0], sc1, sh1, nsa_w_in[0], ts, 512, False)
    assert (past + ts) // CMP_BLK == past // CMP_BLK
    page_rows = lambda c: c.reshape(n_phys * (PAGE_SIZE // CMP_BLK), CMP_BLK * KV_DIM)
    kcmp_phys = _compress(page_rows(cache_k_cmp[0]), wk_big, k_bias, 256)
    vcmp_phys = _compress(page_rows(cache_v_cmp[0]), wv_big, v_bias, 256)
    o = _sample_attention(q, gates, kcmp_phys, vcmp_phys, cache_k_sel[0], cache_v_sel[0], page_table,
                          ksl, vsl, state_k_win[0], state_v_win[0], kw, vw, rel_bias, bs, ts)
    x = _matmul_residual(o, nsa_w_out[0], x, g1, ts, 512)
    x = _ffn(x, norm_g[0, 1], sc2, sh2, g2, ffn_wg[0], ffn_wu[0], ffn_wd[0], final_g, ts, 512, 1408, False)
    sh1, sc1, g1, sh2, sc2, g2 = mod_sample(1)
    b3 = lambda v: v.reshape(bs, ts, D_MODEL)[:, :1]
    x3, pool_s = _pool_mix(x.reshape(bs, ts, D_MODEL), state_pool[0], norm_g[1, 0], b3(sc1), b3(sh1), b3(g1),
                           pool_w[0], pool_scale[0], past, 512)
    y_sample = _ffn(x3.reshape(ms, D_MODEL), norm_g[1, 1], sc2, sh2, g2, ffn_wg[1], ffn_wu[1], ffn_wd[1],
                    final_g, ts, 512, 1408, True).reshape(bs, ts, D_MODEL)
    st5 = lambda a: a.reshape(1, bs, ts, N_KV, HEAD_DIM)
    kw_ext = jnp.concatenate([state_k_win[0], st5(kw)[0]], axis=1)[:, ts:]
    vw_ext = jnp.concatenate([state_v_win[0], st5(vw)[0]], axis=1)[:, ts:]
    sample_states = (st5(kc), st5(vc), st5(ksl), st5(vsl), kw_ext[None], vw_ext[None], pool_s[None])

    return (y_prompt, y_sample) + prompt_states + sample_states
```

```python
import functools
import math

import jax
import jax.numpy as jnp
from jax import lax
from jax.experimental import pallas as pl
from jax.experimental.pallas import tpu as pltpu

D_MODEL = 1024
N_HEADS = 16
N_KV = 4
GROUP = N_HEADS // N_KV
HEAD_DIM = 64
Q_DIM = N_HEADS * HEAD_DIM
KV_DIM = N_KV * HEAD_DIM
CMP_BLK = 32
SEL_BLK = 64
N_SEL = 16
WINDOW = 512
Q_CHUNK = 64
N_BUCKETS = 32
MAX_DISTANCE = 128
POOL_WINDOWS = (2, 4, 8, 16)
POOL_GROUP_DIM = D_MODEL // len(POOL_WINDOWS)
POOL_STATE = max(POOL_WINDOWS) - 1
PAGE_SIZE = 128
RMS_EPS = 1e-6
NEG = -1e30
BIG = 1e9

KEY_TILE = 128
CHUNK = 128
LANES = GROUP * CHUNK
SAMPLE_LANES = 128
VMEM_LIMIT_BYTES = 48 * 1024 * 1024

F32 = jnp.float32
BF16 = jnp.bfloat16


def _cparams(*sem):
    return pltpu.CompilerParams(dimension_semantics=sem, vmem_limit_bytes=VMEM_LIMIT_BYTES)


def _silu(x):
    return x * (1.0 / (1.0 + jnp.exp(-x)))


def _sigmoid(x):
    return 1.0 / (1.0 + jnp.exp(-x))


def _normmod(x, g, sc, sh):
    ms = jnp.mean(x * x, axis=-1, keepdims=True)
    return (x * lax.rsqrt(ms + RMS_EPS) * g) * (1.0 + sc) + sh


def _mod_spec(mod, tm, rows_per_batch):
    if mod.ndim == 3:
        return pl.BlockSpec((None, 1, D_MODEL), lambda i, *_: ((i * tm) // rows_per_batch, 0, 0))
    return pl.BlockSpec((tm, D_MODEL), lambda i, *_: (i, 0))


def _mm_kernel(a_ref, w_ref, b_ref, o_ref, *, silu_in):
    a = a_ref[...]
    if silu_in:
        a = _silu(a)
    o_ref[...] = jnp.dot(a.astype(BF16), w_ref[...], preferred_element_type=F32) + b_ref[...]


def _matmul(a, w, bias=None, *, silu_in=False, tm=256, tn=None, name="matmul"):
    m, k = a.shape
    n = w.shape[1]
    tm = min(tm, m)
    tn = n if tn is None else tn
    assert m % tm == 0 and n % tn == 0
    if bias is None:
        bias = jnp.zeros((1, n), F32)
    return pl.pallas_call(
        functools.partial(_mm_kernel, silu_in=silu_in),
        grid=(m // tm, n // tn),
        in_specs=[pl.BlockSpec((tm, k), lambda i, j: (i, 0)),
                  pl.BlockSpec((k, tn), lambda i, j: (0, j)),
                  pl.BlockSpec((1, tn), lambda i, j: (0, j))],
        out_specs=pl.BlockSpec((tm, tn), lambda i, j: (i, j)),
        out_shape=jax.ShapeDtypeStruct((m, n), F32),
        compiler_params=_cparams("parallel", "parallel"),
        name=name,
    )(a, w.astype(BF16), bias)


_PROJ_MAIN = Q_DIM + 6 * KV_DIM
_GATE_PAD = 128


def _proj_kernel(x_ref, g_ref, sc_ref, sh_ref, w_ref, wg_ref, *out_refs, attn_layouts, tm):
    hb = _normmod(x_ref[...], g_ref[...], sc_ref[...], sh_ref[...]).astype(BF16)
    q = jnp.dot(hb, w_ref[:, :Q_DIM], preferred_element_type=F32)
    gates = jnp.dot(hb, wg_ref[...], preferred_element_type=F32)
    kvs = []
    for n in range(6):
        lo = Q_DIM + n * KV_DIM
        kvs.append(jnp.dot(hb, w_ref[:, lo:lo + KV_DIM], preferred_element_type=F32))
    if not attn_layouts:
        for n in range(6):
            out_refs[n][...] = kvs[n]
        out_refs[6][...] = q
        out_refs[7][...] = gates
        return
    kt_refs = out_refs[2:8]
    qt_ref, gt_ref, ks_ref, vst_ref, kw_ref, vwt_ref, q_scr, gate_scr, kv_scr = out_refs[8:]
    out_refs[0][...] = kvs[0]
    out_refs[1][...] = kvs[1]
    kvts = []
    for n in range(6):
        kv_scr[n] = kvs[n]
        kvts.append(kv_scr[n].T)
        kt_refs[n][...] = kvts[n]
    n_chunks = tm // CHUNK
    q_scr[...] = q
    gate_scr[...] = gates
    qt = q_scr[...].T
    gt = gate_scr[...].T
    for c in range(n_chunks):
        tok = slice(c * CHUNK, (c + 1) * CHUNK)
        for kv in range(N_KV):
            for gi in range(GROUP):
                h = kv * GROUP + gi
                lanes = slice(gi * CHUNK, (gi + 1) * CHUNK)
                qt_ref[c, kv, :, lanes] = qt[h * HEAD_DIM:(h + 1) * HEAD_DIM, tok].astype(BF16)
                gt_ref[c, kv, :, lanes] = gt[3 * h:3 * h + 3, tok]
    for k_ref, vt_ref, k, vt in ((ks_ref, vst_ref, kvs[2], kvts[3]), (kw_ref, vwt_ref, kvs[4], kvts[5])):
        for kv in range(N_KV):
            cols = slice(kv * HEAD_DIM, (kv + 1) * HEAD_DIM)
            k_ref[kv] = k[:, cols].astype(BF16)
            for c in range(n_chunks):
                vt_ref[kv, c] = vt[cols, c * CHUNK:(c + 1) * CHUNK].astype(BF16)


def _nsa_project(x, g, sc, sh, w_in, rows_per_batch, tm, attn_layouts):
    m = x.shape[0]
    tm = min(tm, m)
    w_main = w_in[:, :_PROJ_MAIN].astype(BF16)
    w_gate = jnp.pad(w_in[:, _PROJ_MAIN:], ((0, 0), (0, _GATE_PAD - 3 * N_HEADS))).astype(BF16)
    row = lambda n: pl.BlockSpec((tm, n), lambda i: (i, 0))
    if attn_layouts:
        assert tm % CHUNK == 0 and rows_per_batch % tm == 0
        nck = tm // CHUNK
        tpb = rows_per_batch // tm
        out_specs = [row(KV_DIM)] * 2 + [pl.BlockSpec((None, KV_DIM, tm), lambda i: (i // tpb, 0, i % tpb))] * 6
        out_shape = ([jax.ShapeDtypeStruct((m, KV_DIM), F32)] * 2
                     + [jax.ShapeDtypeStruct((m // rows_per_batch, KV_DIM, rows_per_batch), F32)] * 6)
        chunked = lambda *s: pl.BlockSpec((nck,) + s, lambda i: (i,) + (0,) * len(s))
        k_spec = pl.BlockSpec((N_KV, tm, HEAD_DIM), lambda i: (0, i, 0))
        vt_spec = pl.BlockSpec((N_KV, nck, HEAD_DIM, CHUNK), lambda i: (0, i, 0, 0))
        k_shape = jax.ShapeDtypeStruct((N_KV, m, HEAD_DIM), BF16)
        vt_shape = jax.ShapeDtypeStruct((N_KV, m // CHUNK, HEAD_DIM, CHUNK), BF16)
        out_specs += [chunked(N_KV, HEAD_DIM, LANES), chunked(N_KV, 3, LANES), k_spec, vt_spec, k_spec, vt_spec]
        out_shape += [jax.ShapeDtypeStruct((m // CHUNK, N_KV, HEAD_DIM, LANES), BF16),
                      jax.ShapeDtypeStruct((m // CHUNK, N_KV, 3, LANES), F32),
                      k_shape, vt_shape, k_shape, vt_shape]
    else:
        out_specs = [row(KV_DIM)] * 6 + [row(Q_DIM), row(_GATE_PAD)]
        out_shape = ([jax.ShapeDtypeStruct((m, KV_DIM), F32)] * 6
                     + [jax.ShapeDtypeStruct((m, Q_DIM), F32), jax.ShapeDtypeStruct((m, _GATE_PAD), F32)])
    outs = pl.pallas_call(
        functools.partial(_proj_kernel, attn_layouts=attn_layouts, tm=tm),
        grid=(m // tm,),
        in_specs=[row(D_MODEL),
                  pl.BlockSpec((1, D_MODEL), lambda i: (0, 0)),
                  _mod_spec(sc, tm, rows_per_batch), _mod_spec(sh, tm, rows_per_batch),
                  pl.BlockSpec((D_MODEL, _PROJ_MAIN), lambda i: (0, 0)),
                  pl.BlockSpec((D_MODEL, _GATE_PAD), lambda i: (0, 0))],
        out_specs=out_specs,
        out_shape=out_shape,
        scratch_shapes=([pltpu.VMEM((tm, Q_DIM), F32), pltpu.VMEM((tm, _GATE_PAD), F32),
                         pltpu.VMEM((6, tm, KV_DIM), F32)] if attn_layouts else []),
        compiler_params=_cparams("parallel"),
        name="nsa_project",
    )(x, g.reshape(1, D_MODEL), sc, sh, w_main, w_gate)
    if attn_layouts:
        return outs[:2], outs[2:8], outs[8:]
    return outs[:6], outs[6:]


def _mm_res_kernel(a_ref, w_ref, x_ref, gate_ref, o_ref):
    y = jnp.dot(a_ref[...].astype(BF16), w_ref[...], preferred_element_type=F32)
    o_ref[...] = x_ref[...] + gate_ref[...] * y


def _matmul_residual(a, w, x, gate, rows_per_batch, tm):
    m, k = a.shape
    tm = min(tm, m)
    return pl.pallas_call(
        _mm_res_kernel,
        grid=(m // tm,),
        in_specs=[pl.BlockSpec((tm, k), lambda i: (i, 0)),
                  pl.BlockSpec((k, D_MODEL), lambda i: (0, 0)),
                  pl.BlockSpec((tm, D_MODEL), lambda i: (i, 0)),
                  _mod_spec(gate, tm, rows_per_batch)],
        out_specs=pl.BlockSpec((tm, D_MODEL), lambda i: (i, 0)),
        out_shape=jax.ShapeDtypeStruct((m, D_MODEL), F32),
        compiler_params=_cparams("parallel"),
        name="out_proj_residual",
    )(a, w.astype(BF16), x, gate)


def _mm_res_t_kernel(ot_ref, w_ref, x_ref, gate_ref, o_ref, *, n_chunks):
    chunks = []
    for c in range(n_chunks):
        rows = [ot_ref[c, kv, :, gi * CHUNK:(gi + 1) * CHUNK] for kv in range(N_KV) for gi in range(GROUP)]
        chunks.append(jnp.concatenate(rows, axis=0).astype(F32).T)
    a = jnp.concatenate(chunks, axis=0).astype(BF16)
    y = jnp.dot(a, w_ref[...], preferred_element_type=F32)
    o_ref[...] = x_ref[...] + gate_ref[...] * y


def _matmul_residual_t(o_t, w, x, gate, rows_per_batch, tm):
    m = x.shape[0]
    nck = tm // CHUNK
    return pl.pallas_call(
        functools.partial(_mm_res_t_kernel, n_chunks=nck),
        grid=(m // tm,),
        in_specs=[pl.BlockSpec((nck, N_KV, HEAD_DIM, LANES), lambda i: (i, 0, 0, 0)),
                  pl.BlockSpec((Q_DIM, D_MODEL), lambda i: (0, 0)),
                  pl.BlockSpec((tm, D_MODEL), lambda i: (i, 0)),
                  _mod_spec(gate, tm, rows_per_batch)],
        out_specs=pl.BlockSpec((tm, D_MODEL), lambda i: (i, 0)),
        out_shape=jax.ShapeDtypeStruct((m, D_MODEL), F32),
        compiler_params=_cparams("parallel"),
        name="out_proj_residual_t",
    )(o_t, w.astype(BF16), x, gate)


def _ffn_kernel(x_ref, g_ref, sc_ref, sh_ref, gate_ref, wg_ref, wu_ref, wd_ref, fg_ref,
                o_ref, hb_ref, acc_ref, *, final_norm):
    f = pl.program_id(1)

    @pl.when(f == 0)
    def _():
        hb_ref[...] = _normmod(x_ref[...], g_ref[...], sc_ref[...], sh_ref[...]).astype(BF16)
        acc_ref[...] = jnp.zeros_like(acc_ref)

    hb = hb_ref[...]
    a = jnp.dot(hb, wg_ref[...], preferred_element_type=F32)
    u = jnp.dot(hb, wu_ref[...], preferred_element_type=F32)
    act = (_silu(a) * u).astype(BF16)
    acc_ref[...] += jnp.dot(act, wd_ref[...], preferred_element_type=F32)

    @pl.when(f == pl.num_programs(1) - 1)
    def _():
        y = x_ref[...] + gate_ref[...] * acc_ref[...]
        if final_norm:
            ms = jnp.mean(y * y, axis=-1, keepdims=True)
            y = y * lax.rsqrt(ms + RMS_EPS) * fg_ref[...]
        o_ref[...] = y


def _ffn(x, g, sc, sh, gate, wg, wu, wd, final_g, rows_per_batch, tm, tf, final_norm):
    m = x.shape[0]
    d_ff = wg.shape[1]
    tm = min(tm, m)
    assert d_ff % tf == 0
    vec = pl.BlockSpec((1, D_MODEL), lambda i, f: (0, 0))
    return pl.pallas_call(
        functools.partial(_ffn_kernel, final_norm=final_norm),
        grid=(m // tm, d_ff // tf),
        in_specs=[pl.BlockSpec((tm, D_MODEL), lambda i, f: (i, 0)),
                  vec,
                  _mod_spec(sc, tm, rows_per_batch), _mod_spec(sh, tm, rows_per_batch),
                  _mod_spec(gate, tm, rows_per_batch),
                  pl.BlockSpec((D_MODEL, tf), lambda i, f: (0, f)),
                  pl.BlockSpec((D_MODEL, tf), lambda i, f: (0, f)),
                  pl.BlockSpec((tf, D_MODEL), lambda i, f: (f, 0)),
                  vec],
        out_specs=pl.BlockSpec((tm, D_MODEL), lambda i, f: (i, 0)),
        out_shape=jax.ShapeDtypeStruct((m, D_MODEL), F32),
        scratch_shapes=[pltpu.VMEM((tm, D_MODEL), BF16), pltpu.VMEM((tm, D_MODEL), F32)],
        compiler_params=_cparams("parallel", "arbitrary"),
        name="ffn",
    )(x, g.reshape(1, D_MODEL), sc, sh, gate, wg.astype(BF16), wu.astype(BF16), wd.astype(BF16),
      final_g.reshape(1, D_MODEL))


_POOL_HALO = 16


def _pool_kernel(x_ref, xprev_ref, state_ref, g_ref, sc_ref, sh_ref, gate_ref, w_ref, ls_ref,
                 o_ref, st_ref, ext_ref, *, tm, pos0):
    i = pl.program_id(1)
    g, sc, sh = g_ref[...], sc_ref[...], sh_ref[...]
    u = _normmod(x_ref[...], g, sc, sh)
    prev = jnp.where(i == 0, state_ref[...], _normmod(xprev_ref[...], g, sc, sh))
    ext_ref[0:_POOL_HALO, :] = prev
    ext_ref[_POOL_HALO:_POOL_HALO + tm, :] = u
    st_ref[...] = ext_ref[tm:tm + _POOL_HALO, :]

    pos = pos0 + i * tm + lax.broadcasted_iota(jnp.int32, (tm, 1), 0)
    mixed = []
    for gi, w in enumerate(POOL_WINDOWS):
        lo = gi * POOL_GROUP_DIM
        s = u[:, lo:lo + POOL_GROUP_DIM]
        for k in range(1, w):
            s = s + ext_ref[_POOL_HALO - k:_POOL_HALO - k + tm, lo:lo + POOL_GROUP_DIM]
        cnt = jnp.minimum(pos + 1, w).astype(F32)
        pooled = s / cnt - u[:, lo:lo + POOL_GROUP_DIM]
        mixed.append(jnp.dot(pooled.astype(BF16), w_ref[gi], preferred_element_type=F32))
    y = jnp.concatenate(mixed, axis=-1) * ls_ref[...]
    o_ref[...] = x_ref[...] + gate_ref[...] * y


def _pool_mix(x3, state, g, sc, sh, gate, w_grp, layer_scale, pos0, tm):
    b, t, _ = x3.shape
    tm = min(tm, t)
    state16 = jnp.pad(state, ((0, 0), (_POOL_HALO - POOL_STATE, 0), (0, 0)))
    if t >= _POOL_HALO:
        xprev = x3
        nprev = tm // _POOL_HALO
        prev_spec = pl.BlockSpec((None, _POOL_HALO, D_MODEL),
                                 lambda bi, i: (bi, jnp.maximum(i * nprev - 1, 0), 0))
    else:
        xprev = state16
        prev_spec = pl.BlockSpec((None, _POOL_HALO, D_MODEL), lambda bi, i: (bi, 0, 0))
    vec = pl.BlockSpec((1, D_MODEL), lambda bi, i: (0, 0))
    bvec = pl.BlockSpec((None, 1, D_MODEL), lambda bi, i: (bi, 0, 0))
    out, st = pl.pallas_call(
        functools.partial(_pool_kernel, tm=tm, pos0=pos0),
        grid=(b, t // tm),
        in_specs=[pl.BlockSpec((None, tm, D_MODEL), lambda bi, i: (bi, i, 0)),
                  prev_spec,
                  pl.BlockSpec((None, _POOL_HALO, D_MODEL), lambda bi, i: (bi, 0, 0)),
                  vec, bvec, bvec, bvec,
                  pl.BlockSpec((len(POOL_WINDOWS), POOL_GROUP_DIM, POOL_GROUP_DIM),
                               lambda bi, i: (0, 0, 0)),
                  vec],
        out_specs=[pl.BlockSpec((None, tm, D_MODEL), lambda bi, i: (bi, i, 0)),
                   pl.BlockSpec((None, _POOL_HALO, D_MODEL), lambda bi, i: (bi, 0, 0))],
        out_shape=[jax.ShapeDtypeStruct((b, t, D_MODEL), F32),
                   jax.ShapeDtypeStruct((b, _POOL_HALO, D_MODEL), F32)],
        scratch_shapes=[pltpu.VMEM((tm + _POOL_HALO, D_MODEL), F32)],
        compiler_params=_cparams("parallel", "arbitrary"),
        name="pool_mix",
    )(x3, xprev, state16, g.reshape(1, D_MODEL), sc, sh, gate, w_grp.astype(BF16),
      layer_scale.reshape(1, D_MODEL))
    return out, st[:, _POOL_HALO - POOL_STATE:]


def _rel_bucket(dist):
    d = jnp.maximum(dist, 0)
    max_exact = N_BUCKETS // 2
    large = max_exact + (jnp.log(jnp.maximum(d, 1).astype(F32) / max_exact)
                         / math.log(MAX_DISTANCE / max_exact) * (N_BUCKETS - max_exact)).astype(jnp.int32)
    large = jnp.minimum(large, N_BUCKETS - 1)
    return jnp.where(d < max_exact, d, large)


def _dist_bias(rel_bias):
    return rel_bias[_rel_bucket(jnp.arange(MAX_DISTANCE + 1, dtype=jnp.int32))]


def _lookup(table, idx):
    onehot = (idx[..., None] == jnp.arange(table.shape[0], dtype=jnp.int32)).astype(F32)
    return jnp.einsum("...d,dh->...h", onehot, table, precision=lax.Precision.HIGHEST)


def _bias_tile(fd, dist, valid):
    k, q = dist.shape
    bias = _lookup(fd, jnp.clip(dist, 0, MAX_DISTANCE)).reshape(k, q, N_KV, GROUP)
    bias = jnp.where(valid[:, :, None, None], bias, NEG)
    return bias.transpose(2, 0, 3, 1).reshape(N_KV, k, GROUP * q)


def _prompt_tables(rel_bias, t):
    fd = _dist_bias(rel_bias)
    kj = jnp.arange(KEY_TILE, dtype=jnp.int32)[:, None]
    qi = jnp.arange(CHUNK, dtype=jnp.int32)[None, :]
    dist = qi - kj
    near = jnp.stack([_bias_tile(fd, dist + KEY_TILE, dist + KEY_TILE >= 0),
                      _bias_tile(fd, dist, dist >= 0)])
    wfirst = _bias_tile(fd, dist + WINDOW, dist < 0)
    far = _bias_tile(fd, jnp.full((1, CHUNK), MAX_DISTANCE, jnp.int32), jnp.ones((1, CHUNK), bool))
    per_chunk = CHUNK // CMP_BLK
    assert CHUNK == 128 and CMP_BLK == 32 and MAX_DISTANCE == 128
    rel0, n_rel = -per_chunk, 2 * per_chunk
    rel = rel0 + jnp.arange(n_rel, dtype=jnp.int32)[:, None]
    dist = qi - CMP_BLK * rel - (CMP_BLK - 1)
    crel = _bias_tile(fd, dist, dist >= 0)
    relm = (jnp.arange(t // CMP_BLK, dtype=jnp.int32)[None, :]
            - per_chunk * jnp.arange(t // CHUNK, dtype=jnp.int32)[:, None])[:, None, :, None]
    cmp_tab = jnp.where(relm >= rel0 + n_rel, NEG, far[None])
    for r in range(n_rel):
        cmp_tab = jnp.where(relm == rel0 + r, crel[None, :, r:r + 1, :], cmp_tab)
    return near, wfirst, far, cmp_tab


def _tile_softmax(tiles, qs):
    scores = [jnp.dot(k, qs, preferred_element_type=F32) + add for k, _, add in tiles]
    m = functools.reduce(jnp.maximum, [jnp.max(s, axis=0, keepdims=True) for s in scores])
    l, acc = 0.0, 0.0
    for s, (_, vt, _) in zip(scores, tiles):
        p = jnp.exp(s - m)
        l = l + jnp.sum(p, axis=0, keepdims=True)
        acc = acc + jnp.dot(vt, p.astype(BF16), preferred_element_type=F32)
    return m, l, acc


def _prompt_attn_kernel(q_ref, g_ref, kc_ref, vct_ref, cb_ref, ks_ref, vst_ref, kw_ref, vwt_ref,
                        near_ref, wfirst_ref, far_ref, o_ref,
                        qs_ref, imp_ref, val_ref, selrep_ref, m_ref, l_ref, alpha_ref, acc_ref, mix_ref,
                        s_ref, p_ref, *, n_blk, n_sel):
    i = pl.program_id(1)
    scale = HEAD_DIM ** -0.5

    def k_rows(ref, kv, jt):
        return ref[kv, pl.ds(pl.multiple_of(jt * KEY_TILE, KEY_TILE), KEY_TILE), :]

    def sel_rows(kv, jt):
        return selrep_ref[kv, pl.ds(2 * jt, 1), :], selrep_ref[kv, pl.ds(2 * jt + 1, 1), :]

    def halves(top, bot):
        return jnp.concatenate([jnp.broadcast_to(top, (SEL_BLK, LANES)),
                                jnp.broadcast_to(bot, (SEL_BLK, LANES))], axis=0)

    for kv in range(N_KV):
        qs = q_ref[kv] * scale
        qs_ref[kv] = qs
        bias = cb_ref[kv]
        s = jnp.dot(kc_ref[kv], qs, preferred_element_type=F32) + bias
        valid = bias > 0.5 * NEG
        m = jnp.max(s, axis=0, keepdims=True)
        p = jnp.where(valid, jnp.exp(s - m), 0.0)
        p = p / jnp.maximum(jnp.sum(p, axis=0, keepdims=True), 1e-30)
        o_c = jnp.dot(vct_ref[kv], p.astype(BF16), preferred_element_type=F32)
        mix_ref[kv] = _sigmoid(g_ref[kv, 0:1, :]) * o_c
        imp_ref[kv] = functools.reduce(
            lambda a, b: a + b, [p[:, gi * CHUNK:(gi + 1) * CHUNK] for gi in range(GROUP)])

    imp = jnp.concatenate(
        [imp_ref[kv, pl.ds(0, n_blk, stride=2), :] + imp_ref[kv, pl.ds(1, n_blk, stride=2), :]
         for kv in range(N_KV)], axis=1)
    blk = lax.broadcasted_iota(jnp.int32, (n_blk, LANES), 0)
    lane = lax.broadcasted_iota(jnp.int32, (1, LANES), 1)
    assert CHUNK == 2 * SEL_BLK
    qblk = 2 * i + jnp.where((lane & (CHUNK - 1)) >= SEL_BLK, 1, 0)
    forced = (blk == 0) | (blk == qblk) | (blk == qblk - 1)
    val_ref[...] = jnp.where(forced, BIG, jnp.where(blk <= qblk, imp, -1.0))
    val = val_ref[...]
    rank = jnp.zeros((n_blk, LANES), F32)
    for j in range(n_blk):
        row = val_ref[j:j + 1, :]
        rank = rank + jnp.where(blk > j, jnp.where(row >= val, 1.0, 0.0), jnp.where(row > val, 1.0, 0.0))
    selneg = jnp.where(rank < n_sel, 0.0, NEG)
    for kv in range(N_KV):
        selrep_ref[kv] = jnp.concatenate([selneg[:, kv * CHUNK:(kv + 1) * CHUNK]] * GROUP, axis=1)

    prev = jnp.maximum(i - 1, 0)
    prev_off = jnp.where(i >= 1, 0.0, NEG)
    for kv in range(N_KV):
        qs = qs_ref[kv]
        far = far_ref[kv]
        tiles = [(k_rows(kw_ref, kv, i), vwt_ref[kv, i], near_ref[1, kv]),
                 (k_rows(kw_ref, kv, prev), vwt_ref[kv, prev], near_ref[0, kv] + prev_off)]
        for back, add in ((2, far), (3, far), (4, wfirst_ref[kv])):
            jt = jnp.maximum(i - back, 0)
            tiles.append((k_rows(kw_ref, kv, jt), vwt_ref[kv, jt], add + jnp.where(i >= back, 0.0, NEG)))
        _, l, acc = _tile_softmax(tiles, qs)
        mix_ref[kv] += _sigmoid(g_ref[kv, 2:3, :]) * (acc / jnp.maximum(l, 1e-30))

        tiles = [(k_rows(ks_ref, kv, i), vst_ref[kv, i], near_ref[1, kv] + halves(*sel_rows(kv, i))),
                 (k_rows(ks_ref, kv, prev), vst_ref[kv, prev],
                  near_ref[0, kv] + halves(*sel_rows(kv, prev)) + prev_off)]
        m_ref[kv], l_ref[kv], acc_ref[kv] = _tile_softmax(tiles, qs)

    n_far = jnp.maximum(i - 1, 0)
    last = jnp.maximum(n_far - 1, 0)

    def scores(jt, slot):
        jc = jnp.minimum(jt, last)
        off = jnp.where(jt < n_far, 0.0, NEG)
        for kv in range(N_KV):
            top, bot = sel_rows(kv, jc)
            far = far_ref[kv] + off
            s_ref[slot, kv] = (jnp.dot(k_rows(ks_ref, kv, jc), qs_ref[kv], preferred_element_type=F32)
                               + halves(top + far, bot + far))

    def softmax(slot):
        for kv in range(N_KV):
            s = s_ref[slot, kv]
            m_old = m_ref[kv]
            m_new = jnp.maximum(m_old, jnp.max(s, axis=0, keepdims=True))
            alpha = jnp.exp(m_old - m_new)
            p = jnp.exp(s - m_new)
            m_ref[kv] = m_new
            alpha_ref[slot, kv] = alpha
            l_ref[kv] = alpha * l_ref[kv] + jnp.sum(p, axis=0, keepdims=True)
            p_ref[slot, kv] = p.astype(BF16)

    def values(jt, slot):
        jc = jnp.clip(jt, 0, last)
        for kv in range(N_KV):
            acc_ref[kv] = alpha_ref[slot, kv] * acc_ref[kv] + jnp.dot(vst_ref[kv, jc], p_ref[slot, kv],
                                                                      preferred_element_type=F32)

    p_ref[1] = jnp.zeros(p_ref.shape[1:], BF16)
    alpha_ref[1] = jnp.ones(alpha_ref.shape[1:], F32)
    scores(0, 0)

    def far_body(j2, carry):
        jt = 2 * j2
        values(jt - 1, 1)
        softmax(0)
        scores(jt + 1, 1)
        values(jt, 0)
        softmax(1)
        scores(jt + 2, 0)
        return carry

    n_pairs = (n_far + 1) // 2
    lax.fori_loop(0, n_pairs, far_body, 0)
    values(2 * n_pairs - 1, 1)

    for kv in range(N_KV):
        o_s = acc_ref[kv] / jnp.maximum(l_ref[kv], 1e-30)
        o_ref[kv] = (mix_ref[kv] + _sigmoid(g_ref[kv, 1:2, :]) * o_s).astype(o_ref.dtype)


def _prompt_attention(layouts, k_cmp, v_cmp, rel_bias, b, t):
    q_t, g_t, ks, vst, kw, vwt = layouts
    nch, nt, nc, n_blk = t // CHUNK, t // KEY_TILE, t // CMP_BLK, t // SEL_BLK
    kc = k_cmp.reshape(b, nc, N_KV, HEAD_DIM).transpose(0, 2, 1, 3).astype(BF16)
    vct = v_cmp.reshape(b, nc, N_KV, HEAD_DIM).transpose(0, 2, 3, 1).astype(BF16)
    near, wfirst, far, cmp_tab = _prompt_tables(rel_bias, t)

    once = pl.Buffered(1)
    per_b = lambda *shape: pl.BlockSpec((None,) + shape, lambda bi, i: (bi,) + (0,) * len(shape),
                                        pipeline_mode=once)
    const = lambda *shape: pl.BlockSpec(shape, lambda bi, i: (0,) * len(shape), pipeline_mode=once)
    chunk = lambda *shape: pl.BlockSpec((None,) + shape, lambda bi, i: (bi * nch + i,) + (0,) * len(shape))
    k_spec = pl.BlockSpec((N_KV, t, HEAD_DIM), lambda bi, i: (0, bi, 0), pipeline_mode=once)
    vt_spec = pl.BlockSpec((N_KV, nt, HEAD_DIM, KEY_TILE), lambda bi, i: (0, bi, 0, 0), pipeline_mode=once)
    stat = pltpu.VMEM((N_KV, 1, LANES), F32)
    wide = pltpu.VMEM((N_KV, HEAD_DIM, LANES), F32)
    return pl.pallas_call(
        functools.partial(_prompt_attn_kernel, n_blk=n_blk, n_sel=min(N_SEL, n_blk)),
        grid=(b, nch),
        in_specs=[chunk(N_KV, HEAD_DIM, LANES),
                  chunk(N_KV, 3, LANES),
                  per_b(N_KV, nc, HEAD_DIM),
                  per_b(N_KV, HEAD_DIM, nc),
                  pl.BlockSpec((None, N_KV, nc, LANES), lambda bi, i: (i, 0, 0, 0)),
                  k_spec, vt_spec, k_spec, vt_spec,
                  const(2, N_KV, KEY_TILE, LANES),
                  const(N_KV, KEY_TILE, LANES),
                  const(N_KV, 1, LANES)],
        out_specs=chunk(N_KV, HEAD_DIM, LANES),
        out_shape=jax.ShapeDtypeStruct((b * nch, N_KV, HEAD_DIM, LANES), BF16),
        scratch_shapes=[pltpu.VMEM((N_KV, HEAD_DIM, LANES), BF16),
                        pltpu.VMEM((N_KV, nc, CHUNK), F32),
                        pltpu.VMEM((n_blk, LANES), F32),
                        pltpu.VMEM((N_KV, n_blk, LANES), F32),
                        stat, stat, pltpu.VMEM((2, N_KV, 1, LANES), F32), wide, wide,
                        pltpu.VMEM((2, N_KV, KEY_TILE, LANES), F32),
                        pltpu.VMEM((2, N_KV, KEY_TILE, LANES), BF16)],
        compiler_params=_cparams("parallel", "arbitrary"),
        name="nsa_prompt_attention",
    )(q_t, g_t, kc, vct, cmp_tab, ks, vst, kw, vwt, near, wfirst, far)


NEW_PAD = 128


def _sample_tables(rel_bias, past, t, wbuf, n_rows):
    lane = jnp.arange(SAMPLE_LANES, dtype=jnp.int32)
    g, kvh, qi = lane // (N_KV * t), (lane // t) % N_KV, lane % t
    used = lane < GROUP * N_KV * t
    head = jnp.where(used, kvh * GROUP + g, 0)
    qpos = past + qi
    fd = _dist_bias(rel_bias)

    def tab(kpos, valid, n_near):
        dist = qpos[None, :] - kpos[:, None]
        n_far = kpos.shape[0] - n_near
        near = fd[jnp.clip(dist[n_far:], 0, MAX_DISTANCE), head[None, :]]
        far = jnp.broadcast_to(fd[MAX_DISTANCE, head][None, :], (n_far, SAMPLE_LANES))
        bias = jnp.where(used[None, :], jnp.concatenate([far, near], axis=0), 0.0)
        return jnp.where(valid(dist) & (kpos[:, None] >= 0), bias, NEG)

    new_pos = past + jnp.where(jnp.arange(NEW_PAD) < t, jnp.arange(NEW_PAD, dtype=jnp.int32), 1 << 20)
    causal = lambda d: d >= 0
    window = lambda d: (d >= 0) & (d < WINDOW)
    nc = past // CMP_BLK
    assert wbuf >= MAX_DISTANCE and past >= MAX_DISTANCE
    t_cmp = tab(jnp.arange(nc, dtype=jnp.int32) * CMP_BLK + CMP_BLK - 1, causal, MAX_DISTANCE // CMP_BLK)
    rows = lambda a: a[:, :n_rows].T
    t_sel = rows(tab(jnp.concatenate([jnp.arange(past, dtype=jnp.int32), new_pos]), causal,
                     MAX_DISTANCE + NEW_PAD))
    t_win = rows(tab(jnp.concatenate([past - wbuf + jnp.arange(wbuf, dtype=jnp.int32), new_pos]), window,
                     MAX_DISTANCE + NEW_PAD))
    key = jnp.arange(past + NEW_PAD, dtype=jnp.int32)
    blk = jnp.where(key < past, key // SEL_BLK, past // SEL_BLK)
    expand = (blk[None, :] == jnp.arange(128, dtype=jnp.int32)[:, None]).astype(BF16)
    return t_cmp, t_sel, t_win, expand


def _sample_attn_kernel(pt_ref, *refs, n_pages, n_sel, t, wbuf):
    del pt_ref
    it = iter(refs)
    qbd_ref, qrow_ref, g_ref = next(it), next(it), next(it)
    kc_pages = [next(it) for _ in range(n_pages)]
    vc_pages = [next(it) for _ in range(n_pages)]
    ks_pages = [next(it) for _ in range(n_pages)]
    vs_pages = [next(it) for _ in range(n_pages)]
    ksn_ref, vsn_ref, kwin_ref, vwin_ref, kwn_ref, vwn_ref = (next(it) for _ in range(6))
    tcmp_ref, tsel_ref, twin_ref, expand_ref = (next(it) for _ in range(4))
    o_ref = next(it)
    kc_ref, vc_ref, imp_ref, val_ref, sel_ref, s_ref = (next(it) for _ in range(6))

    scale = HEAD_DIM ** -0.5
    nq = N_KV * t
    qs = qbd_ref[...] * scale
    qr = qrow_ref[...] * scale
    n_q = qr.shape[0]
    lane = lax.broadcasted_iota(jnp.int32, (1, SAMPLE_LANES), 1)
    blocks_per_page = PAGE_SIZE // CMP_BLK
    past = n_pages * PAGE_SIZE
    nt_dims = (((1,), (1,)), ((), ()))

    for p in range(n_pages):
        kc_ref[p * blocks_per_page:(p + 1) * blocks_per_page, :] = kc_pages[p][...]
        vc_ref[p * blocks_per_page:(p + 1) * blocks_per_page, :] = vc_pages[p][...]
    s = jnp.dot(kc_ref[...].astype(BF16), qs, preferred_element_type=F32) + tcmp_ref[...]
    m = jnp.max(s, axis=0, keepdims=True)
    p_c = jnp.exp(s - m)
    p_c = p_c / jnp.maximum(jnp.sum(p_c, axis=0, keepdims=True), 1e-30)
    o_c = lax.dot_general(p_c.astype(BF16), vc_ref[...].astype(BF16), (((0,), (0,)), ((), ())),
                          preferred_element_type=F32)[:n_q]

    imp = p_c
    for g in range(1, GROUP):
        imp = imp + pltpu.roll(p_c, SAMPLE_LANES - g * nq, 1)
    imp_ref[...] = imp
    n_pairs = n_pages * blocks_per_page // 2
    n_blk = n_pairs + 1
    n_rows = val_ref.shape[0]
    val_ref[...] = jnp.full((n_rows, SAMPLE_LANES), -2.0, F32)
    val_ref[0:n_pairs, :] = imp_ref[pl.ds(0, n_pairs, stride=2), :] + imp_ref[pl.ds(1, n_pairs, stride=2), :]
    blk = lax.broadcasted_iota(jnp.int32, (n_rows, SAMPLE_LANES), 0)
    qblk = n_blk - 1
    forced = (blk == 0) | (blk == qblk) | (blk == qblk - 1)
    val = jnp.where(forced, BIG, val_ref[...])
    val = jnp.where(blk < n_blk, val, -2.0)
    val_ref[...] = val
    rank = jnp.zeros((n_rows, SAMPLE_LANES), F32)
    for j in range(n_blk):
        row = val_ref[j:j + 1, :]
        rank = rank + jnp.where(blk > j, jnp.where(row >= val, 1.0, 0.0), jnp.where(row > val, 1.0, 0.0))
    selneg = jnp.where((rank < n_sel) & (lane < nq), 0.0, jnp.where(lane < nq, NEG, 0.0))
    selrep = selneg
    for g in range(1, GROUP):
        selrep = selrep + pltpu.roll(selneg, g * nq, 1)
    sel_ref[...] = jnp.zeros_like(sel_ref)
    sel_ref[0:n_rows, :] = jnp.where(selrep == 0.0, 1.0, 0.0)
    sel01 = sel_ref[...].T[:n_q].astype(BF16)

    def paged(page_ref):
        return page_ref[...].reshape(KV_DIM, PAGE_SIZE).astype(BF16)

    for p in range(n_pages + 1):
        cols = slice(p * KEY_TILE, (p + 1) * KEY_TILE)
        if p < n_pages:
            s = jnp.dot(qr, paged(ks_pages[p]), preferred_element_type=F32)
        else:
            s = lax.dot_general(qr, ksn_ref[...], nt_dims, preferred_element_type=F32)
        picked = jnp.dot(sel01, expand_ref[:, cols], preferred_element_type=F32)
        s_ref[:, cols] = s + tsel_ref[:, cols] + (picked - 1.0) * (-NEG)
    s = s_ref[...]
    p_s = jnp.exp(s - jnp.max(s, axis=1, keepdims=True))
    l = jnp.sum(p_s, axis=1, keepdims=True)
    p_s = p_s.astype(BF16)
    acc = jnp.dot(p_s[:, past:], vsn_ref[...], preferred_element_type=F32)
    for p in range(n_pages):
        acc = acc + lax.dot_general(p_s[:, p * KEY_TILE:(p + 1) * KEY_TILE], paged(vs_pages[p]), nt_dims,
                                    preferred_element_type=F32)
    o_s = acc / jnp.maximum(l, 1e-30)

    kwin = kwin_ref[...].reshape(KV_DIM, wbuf).astype(BF16)
    vwin = vwin_ref[...].reshape(KV_DIM, wbuf).astype(BF16)
    s_w = jnp.dot(qr, kwin, preferred_element_type=F32) + twin_ref[:, :wbuf]
    s_n = lax.dot_general(qr, kwn_ref[...], nt_dims, preferred_element_type=F32) + twin_ref[:, wbuf:]
    m = jnp.maximum(jnp.max(s_w, axis=1, keepdims=True), jnp.max(s_n, axis=1, keepdims=True))
    p_w, p_n = jnp.exp(s_w - m), jnp.exp(s_n - m)
    l = jnp.sum(p_w, axis=1, keepdims=True) + jnp.sum(p_n, axis=1, keepdims=True)
    acc = (lax.dot_general(p_w.astype(BF16), vwin, nt_dims, preferred_element_type=F32)
           + jnp.dot(p_n.astype(BF16), vwn_ref[...], preferred_element_type=F32))
    o_w = acc / jnp.maximum(l, 1e-30)

    gate = _sigmoid(g_ref[...])
    o_ref[...] = gate[:, 0:1] * o_c + gate[:, 1:2] * o_s + gate[:, 2:3] * o_w


def _sample_attention(q, gates, kcmp_phys, vcmp_phys, cache_ks, cache_vs, page_table,
                      ksl, vsl, win_k, win_v, kw, vw, rel_bias, b, t):
    n_pages = page_table.shape[1]
    past = n_pages * PAGE_SIZE
    wbuf = win_k.shape[-1]
    n_phys = cache_ks.shape[0]
    nq = N_KV * t
    used = GROUP * nq
    assert used % 8 == 0 and used <= SAMPLE_LANES and t <= NEW_PAD
    blocks_per_page = PAGE_SIZE // CMP_BLK
    n_blk = past // SEL_BLK + 1
    n_rows = -(-n_blk // 8) * 8

    q5 = q.reshape(b, t, N_KV, GROUP, HEAD_DIM)
    eye = jnp.eye(N_KV, dtype=q.dtype)
    qbd = jnp.einsum("btkgd,kc->bkdgct", q5, eye).reshape(b, KV_DIM, used)
    qbd = jnp.pad(qbd, ((0, 0), (0, 0), (0, SAMPLE_LANES - used))).astype(BF16)
    qrow = jnp.einsum("btkgd,kc->bgktcd", q5, eye).reshape(b, used, KV_DIM).astype(BF16)
    g_rows = gates[:, :3 * N_HEADS].reshape(b, t, N_KV, GROUP, 3).transpose(0, 3, 2, 1, 4).reshape(b, used, 3)
    g_rows = jnp.pad(g_rows, ((0, 0), (0, 0), (0, 128 - 3)))
    pad_rows = lambda a: jnp.pad(a.reshape(b, t, KV_DIM), ((0, 0), (0, NEW_PAD - t), (0, 0))).astype(BF16)
    tables = _sample_tables(rel_bias, past, t, wbuf, used)

    per_b = lambda *shape: pl.BlockSpec((None,) + shape, lambda bi, pt: (bi,) + (0,) * len(shape))
    const = lambda *shape: pl.BlockSpec(shape, lambda bi, pt: (0,) * len(shape))

    def cmp_page(p):
        return pl.BlockSpec((None, blocks_per_page, KV_DIM), lambda bi, pt, p=p: (pt[bi, p], 0, 0))

    def page(p):
        return pl.BlockSpec((None, N_KV, HEAD_DIM, PAGE_SIZE), lambda bi, pt, p=p: (pt[bi, p], 0, 0, 0))

    in_specs = ([per_b(KV_DIM, SAMPLE_LANES), per_b(used, KV_DIM), per_b(used, 128)]
                + [cmp_page(p) for p in range(n_pages)] * 2
                + [page(p) for p in range(n_pages)] * 2
                + [per_b(NEW_PAD, KV_DIM), per_b(NEW_PAD, KV_DIM),
                   per_b(N_KV, HEAD_DIM, wbuf), per_b(N_KV, HEAD_DIM, wbuf),
                   per_b(NEW_PAD, KV_DIM), per_b(NEW_PAD, KV_DIM)]
                + [const(*tb.shape) for tb in tables])
    kcp = kcmp_phys.reshape(n_phys, blocks_per_page, KV_DIM)
    vcp = vcmp_phys.reshape(n_phys, blocks_per_page, KV_DIM)
    out = pl.pallas_call(
        functools.partial(_sample_attn_kernel, n_pages=n_pages, n_sel=min(N_SEL, n_blk), t=t, wbuf=wbuf),
        grid_spec=pltpu.PrefetchScalarGridSpec(
            num_scalar_prefetch=1,
            grid=(b,),
            in_specs=in_specs,
            out_specs=per_b(used, KV_DIM),
            scratch_shapes=[pltpu.VMEM((n_pages * blocks_per_page, KV_DIM), F32),
                            pltpu.VMEM((n_pages * blocks_per_page, KV_DIM), F32),
                            pltpu.VMEM((n_pages * blocks_per_page, SAMPLE_LANES), F32),
                            pltpu.VMEM((n_rows, SAMPLE_LANES), F32),
                            pltpu.VMEM((128, SAMPLE_LANES), F32),
                            pltpu.VMEM((used, past + NEW_PAD), F32)]),
        out_shape=jax.ShapeDtypeStruct((b, used, KV_DIM), F32),
        compiler_params=_cparams("arbitrary"),
        name="nsa_sample_attention",
    )(page_table, qbd, qrow, g_rows, *([kcp] * n_pages), *([vcp] * n_pages), *([cache_ks] * n_pages),
      *([cache_vs] * n_pages), pad_rows(ksl), pad_rows(vsl), win_k, win_v, pad_rows(kw), pad_rows(vw), *tables)
    o = out.reshape(b, GROUP, N_KV, t, N_KV, HEAD_DIM)
    o = jnp.einsum("bgktkd->btkgd", o)
    return o.reshape(b * t, Q_DIM)


def _compress_weights(w_c, pe_c):
    eye = jnp.eye(N_KV, dtype=w_c.dtype)
    w_big = jnp.einsum("jde,kc->jkdce", w_c, eye).reshape(CMP_BLK * KV_DIM, KV_DIM)
    pe_flat = jnp.broadcast_to(pe_c[:, None, :], (CMP_BLK, N_KV, HEAD_DIM)).reshape(1, CMP_BLK * KV_DIM)
    pe_rows = jnp.pad(pe_flat, ((0, 7), (0, 0)))
    bias = _matmul(pe_rows, w_big, name="compress_pe")[0:1]
    return w_big, bias


def _compress(x_rows, w_big, bias, tm):
    return _matmul(x_rows, w_big, bias, tm=tm, name="compress")


def _compress_pages_kernel(x_ref, w_ref, b_ref, o_ref, rows_ref, *, n_pages):
    for p in range(n_pages):
        xt = x_ref[p].reshape(KV_DIM, PAGE_SIZE).T
        for h in range(2):
            rows_ref[h, p * PAGE_SIZE:(p + 1) * PAGE_SIZE, :] = xt[:, h * 128:(h + 1) * 128]
    n_out = n_pages * (PAGE_SIZE // CMP_BLK)
    acc = jnp.broadcast_to(b_ref[...], (n_out, KV_DIM))
    for j in range(CMP_BLK):
        for h in range(2):
            piece = rows_ref[h, pl.ds(j, n_out, stride=CMP_BLK), :].astype(BF16)
            lo = j * KV_DIM + h * 128
            acc = acc + jnp.dot(piece, w_ref[lo:lo + 128, :], preferred_element_type=F32)
    o_ref[...] = acc


def _compress_pages(cache_t, w_big, bias, n_pages):
    assert KV_DIM == 256
    n_phys = cache_t.shape[0]
    assert n_phys % n_pages == 0
    n_out = n_pages * (PAGE_SIZE // CMP_BLK)
    return pl.pallas_call(
        functools.partial(_compress_pages_kernel, n_pages=n_pages),
        grid=(n_phys // n_pages,),
        in_specs=[pl.BlockSpec((n_pages, N_KV, HEAD_DIM, PAGE_SIZE), lambda i: (i, 0, 0, 0)),
                  pl.BlockSpec((CMP_BLK * KV_DIM, KV_DIM), lambda i: (0, 0)),
                  pl.BlockSpec((1, KV_DIM), lambda i: (0, 0))],
        out_specs=pl.BlockSpec((n_out, KV_DIM), lambda i: (i, 0)),
        out_shape=jax.ShapeDtypeStruct((n_phys * (PAGE_SIZE // CMP_BLK), KV_DIM), F32),
        scratch_shapes=[pltpu.VMEM((2, n_pages * PAGE_SIZE, 128), F32)],
        compiler_params=_cparams("parallel"),
        name="compress_pages",
    )(cache_t, w_big.astype(BF16), bias)


def _ada(c_all, ada_w, ada_b):
    mods = []
    for i in range(ada_w.shape[0]):
        mods.append(_matmul(c_all, ada_w[i], ada_b[i].reshape(1, -1), silu_in=True, tn=2 * D_MODEL,
                            name="ada_modulate"))
    return mods


def kernel(x_prompt, x_sample, c_prompt, c_sample, cache_k_cmp, cache_v_cmp, cache_k_sel, cache_v_sel,
           page_table, state_k_win, state_v_win, state_pool, rel_bias, ada_w, ada_b, norm_g, final_g,
           nsa_w_in, nsa_w_out, cmp_wk, cmp_wv, cmp_pe_k, cmp_pe_v, pool_w, pool_scale,
           ffn_wg, ffn_wu, ffn_wd):
    bp, tp, _ = x_prompt.shape
    bs, ts, _ = x_sample.shape
    n_phys = cache_k_cmp.shape[1]
    past = page_table.shape[1] * PAGE_SIZE
    depth = ada_w.shape[0]
    assert depth == 2 and tp % KEY_TILE == 0 and ts <= 8 and past % PAGE_SIZE == 0

    n_c = bp + bs
    c_all = jnp.pad(jnp.concatenate([c_prompt, c_sample], axis=0), ((0, -n_c % 8), (0, 0)))
    mods = _ada(c_all, ada_w, ada_b)

    def mod_prompt(i):
        return [v[:bp].reshape(bp, 1, D_MODEL) for v in jnp.split(mods[i], 6, axis=-1)]

    def mod_sample(i):
        return [jnp.repeat(v[bp:n_c], ts, axis=0) for v in jnp.split(mods[i], 6, axis=-1)]

    wk_big, k_bias = _compress_weights(cmp_wk[0], cmp_pe_k[0])
    wv_big, v_bias = _compress_weights(cmp_wv[0], cmp_pe_v[0])

    mp = bp * tp
    x = x_prompt.reshape(mp, D_MODEL)
    sh1, sc1, g1, sh2, sc2, g2 = mod_prompt(0)
    (kc, vc), kv_t, layouts = _nsa_project(x, norm_g[0, 0], sc1, sh1, nsa_w_in[0], tp, 512, True)
    blk_rows = lambda a: a.reshape(mp // CMP_BLK, CMP_BLK * KV_DIM)
    k_cmp = _compress(blk_rows(kc), wk_big, k_bias, 256)
    v_cmp = _compress(blk_rows(vc), wv_big, v_bias, 256)
    o_t = _prompt_attention(layouts, k_cmp, v_cmp, rel_bias, bp, tp)
    x = _matmul_residual_t(o_t, nsa_w_out[0], x, g1, tp, 512)
    x = _ffn(x, norm_g[0, 1], sc2, sh2, g2, ffn_wg[0], ffn_wu[0], ffn_wd[0], final_g, tp, 512, 1408, False)
    sh1, sc1, g1, sh2, sc2, g2 = mod_prompt(1)
    x3, pool_p = _pool_mix(x.reshape(bp, tp, D_MODEL), jnp.zeros((bp, POOL_STATE, D_MODEL), F32),
                           norm_g[1, 0], sc1, sh1, g1, pool_w[0], pool_scale[0], 0, 512)
    y_prompt = _ffn(x3.reshape(mp, D_MODEL), norm_g[1, 1], sc2, sh2, g2, ffn_wg[1], ffn_wu[1], ffn_wd[1],
                    final_g, tp, 512, 1408, True).reshape(bp, tp, D_MODEL)
    win = min(WINDOW, tp)
    st5 = lambda a, t0: jnp.transpose(a[:, :, t0:].reshape(bp, N_KV, HEAD_DIM, tp - t0), (0, 3, 1, 2))[None]
    prompt_states = tuple(st5(a, 0) for a in kv_t[:4]) + tuple(st5(a, tp - win) for a in kv_t[4:]) + (pool_p[None],)

    ms = bs * ts
    x = x_sample.reshape(ms, D_MODEL)
    sh1, sc1, g1, sh2, sc2, g2 = mod_sample(0)
    (kc, vc, ksl, vsl, kw, vw), (q, gates) = _nsa_project(x, norm_g[0, 0], sc1, sh1, nsa_w_in[0], ts, 512, False)
    assert (past + ts) // CMP_BLK == past // CMP_BLK
    pos_last = lambda a: jnp.transpose(a[0], (0, 2, 3, 1))
    pages_per_step = math.gcd(n_phys, 32)
    kcmp_phys = _compress_pages(pos_last(cache_k_cmp), wk_big, k_bias, pages_per_step)
    vcmp_phys = _compress_pages(pos_last(cache_v_cmp), wv_big, v_bias, pages_per_step)
    o = _sample_attention(q, gates, kcmp_phys, vcmp_phys, pos_last(cache_k_sel), pos_last(cache_v_sel),
                          page_table, ksl, vsl, pos_last(state_k_win), pos_last(state_v_win), kw, vw,
                          rel_bias, bs, ts)
    x = _matmul_residual(o, nsa_w_out[0], x, g1, ts, 512)
    x = _ffn(x, norm_g[0, 1], sc2, sh2, g2, ffn_wg[0], ffn_wu[0], ffn_wd[0], final_g, ts, 512, 1408, False)
    sh1, sc1, g1, sh2, sc2, g2 = mod_sample(1)
    b3 = lambda v: v.reshape(bs, ts, D_MODEL)[:, :1]
    x3, pool_s = _pool_mix(x.reshape(bs, ts, D_MODEL), state_pool[0], norm_g[1, 0], b3(sc1), b3(sh1), b3(g1),
                           pool_w[0], pool_scale[0], past, 512)
    y_sample = _ffn(x3.reshape(ms, D_MODEL), norm_g[1, 1], sc2, sh2, g2, ffn_wg[1], ffn_wu[1], ffn_wd[1],
                    final_g, ts, 512, 1408, True).reshape(bs, ts, D_MODEL)
    st5 = lambda a: a.reshape(1, bs, ts, N_KV, HEAD_DIM)
    kw_ext = jnp.concatenate([state_k_win[0], st5(kw)[0]], axis=1)[:, ts:]
    vw_ext = jnp.concatenate([state_v_win[0], st5(vw)[0]], axis=1)[:, ts:]
    sample_states = (st5(kc), st5(vc), st5(ksl), st5(vsl), kw_ext[None], vw_ext[None], pool_s[None])

    return (y_prompt, y_sample) + prompt_states + sample_states
```

```python
import functools
import math

import jax
import jax.numpy as jnp
from jax import lax
from jax.experimental import pallas as pl
from jax.experimental.pallas import tpu as pltpu

D_MODEL = 1024
N_HEADS = 16
N_KV = 4
GROUP = N_HEADS // N_KV
HEAD_DIM = 64
Q_DIM = N_HEADS * HEAD_DIM
KV_DIM = N_KV * HEAD_DIM
CMP_BLK = 32
SEL_BLK = 64
N_SEL = 16
WINDOW = 512
Q_CHUNK = 64
N_BUCKETS = 32
MAX_DISTANCE = 128
POOL_WINDOWS = (2, 4, 8, 16)
POOL_GROUP_DIM = D_MODEL // len(POOL_WINDOWS)
POOL_STATE = max(POOL_WINDOWS) - 1
PAGE_SIZE = 128
RMS_EPS = 1e-6
NEG = -1e30
BIG = 1e9

KEY_TILE = 128
CHUNK = 128
LANES = GROUP * CHUNK
LOG2E = math.log2(math.e)
Q_SCALE_LOG2 = HEAD_DIM ** -0.5 * LOG2E
SAMPLE_LANES = 128
VMEM_LIMIT_BYTES = 48 * 1024 * 1024

F32 = jnp.float32
BF16 = jnp.bfloat16


def _cparams(*sem):
    return pltpu.CompilerParams(dimension_semantics=sem, vmem_limit_bytes=VMEM_LIMIT_BYTES)


def _silu(x):
    return x * (1.0 / (1.0 + jnp.exp(-x)))


def _sigmoid(x):
    return 1.0 / (1.0 + jnp.exp(-x))


def _normmod(x, g, sc, sh):
    ms = jnp.mean(x * x, axis=-1, keepdims=True)
    return (x * lax.rsqrt(ms + RMS_EPS) * g) * (1.0 + sc) + sh


def _mod_spec(mod, tm, rows_per_batch):
    if mod.ndim == 3:
        return pl.BlockSpec((None, 1, D_MODEL), lambda i, *_: ((i * tm) // rows_per_batch, 0, 0))
    return pl.BlockSpec((tm, D_MODEL), lambda i, *_: (i, 0))


def _mm_kernel(a_ref, w_ref, b_ref, o_ref, *, silu_in):
    a = a_ref[...]
    if silu_in:
        a = _silu(a)
    o_ref[...] = jnp.dot(a.astype(BF16), w_ref[...], preferred_element_type=F32) + b_ref[...]


def _matmul(a, w, bias=None, *, silu_in=False, tm=256, tn=None, name="matmul"):
    m, k = a.shape
    n = w.shape[1]
    tm = min(tm, m)
    tn = n if tn is None else tn
    assert m % tm == 0 and n % tn == 0
    if bias is None:
        bias = jnp.zeros((1, n), F32)
    return pl.pallas_call(
        functools.partial(_mm_kernel, silu_in=silu_in),
        grid=(m // tm, n // tn),
        in_specs=[pl.BlockSpec((tm, k), lambda i, j: (i, 0)),
                  pl.BlockSpec((k, tn), lambda i, j: (0, j)),
                  pl.BlockSpec((1, tn), lambda i, j: (0, j))],
        out_specs=pl.BlockSpec((tm, tn), lambda i, j: (i, j)),
        out_shape=jax.ShapeDtypeStruct((m, n), F32),
        compiler_params=_cparams("parallel", "parallel"),
        name=name,
    )(a, w.astype(BF16), bias)


_PROJ_MAIN = Q_DIM + 6 * KV_DIM
_GATE_PAD = 128


def _proj_kernel(x_ref, g_ref, sc_ref, sh_ref, w_ref, wg_ref, *out_refs, attn_layouts, tm):
    hb = _normmod(x_ref[...], g_ref[...], sc_ref[...], sh_ref[...]).astype(BF16)
    q = jnp.dot(hb, w_ref[:, :Q_DIM], preferred_element_type=F32)
    gates = jnp.dot(hb, wg_ref[...], preferred_element_type=F32)
    kvs = []
    for n in range(6):
        lo = Q_DIM + n * KV_DIM
        kvs.append(jnp.dot(hb, w_ref[:, lo:lo + KV_DIM], preferred_element_type=F32))
    if not attn_layouts:
        for n in range(6):
            out_refs[n][...] = kvs[n]
        out_refs[6][...] = q
        out_refs[7][...] = gates
        return
    kt_refs = out_refs[2:8]
    qt_ref, gt_ref, ks_ref, vst_ref, kw_ref, vwt_ref, q_scr, gate_scr, kv_scr = out_refs[8:]
    out_refs[0][...] = kvs[0]
    out_refs[1][...] = kvs[1]
    kvts = []
    for n in range(6):
        kv_scr[n] = kvs[n]
        kvts.append(kv_scr[n].T)
        kt_refs[n][...] = kvts[n]
    n_chunks = tm // CHUNK
    q_scr[...] = q * Q_SCALE_LOG2
    gate_scr[...] = gates
    qt = q_scr[...].T
    gt = gate_scr[...].T
    for c in range(n_chunks):
        tok = slice(c * CHUNK, (c + 1) * CHUNK)
        for kv in range(N_KV):
            for gi in range(GROUP):
                h = kv * GROUP + gi
                lanes = slice(gi * CHUNK, (gi + 1) * CHUNK)
                qt_ref[c, kv, :, lanes] = qt[h * HEAD_DIM:(h + 1) * HEAD_DIM, tok].astype(BF16)
                gt_ref[c, kv, :, lanes] = gt[3 * h:3 * h + 3, tok]
    for k_ref, vt_ref, k, vt in ((ks_ref, vst_ref, kvs[2], kvts[3]), (kw_ref, vwt_ref, kvs[4], kvts[5])):
        for kv in range(N_KV):
            cols = slice(kv * HEAD_DIM, (kv + 1) * HEAD_DIM)
            k_ref[kv] = k[:, cols].astype(BF16)
            for c in range(n_chunks):
                vt_ref[kv, c] = vt[cols, c * CHUNK:(c + 1) * CHUNK].astype(BF16)


def _nsa_project(x, g, sc, sh, w_in, rows_per_batch, tm, attn_layouts):
    m = x.shape[0]
    tm = min(tm, m)
    w_main = w_in[:, :_PROJ_MAIN].astype(BF16)
    w_gate = jnp.pad(w_in[:, _PROJ_MAIN:], ((0, 0), (0, _GATE_PAD - 3 * N_HEADS))).astype(BF16)
    row = lambda n: pl.BlockSpec((tm, n), lambda i: (i, 0))
    if attn_layouts:
        assert tm % CHUNK == 0 and rows_per_batch % tm == 0
        nck = tm // CHUNK
        tpb = rows_per_batch // tm
        out_specs = [row(KV_DIM)] * 2 + [pl.BlockSpec((None, KV_DIM, tm), lambda i: (i // tpb, 0, i % tpb))] * 6
        out_shape = ([jax.ShapeDtypeStruct((m, KV_DIM), F32)] * 2
                     + [jax.ShapeDtypeStruct((m // rows_per_batch, KV_DIM, rows_per_batch), F32)] * 6)
        chunked = lambda *s: pl.BlockSpec((nck,) + s, lambda i: (i,) + (0,) * len(s))
        k_spec = pl.BlockSpec((N_KV, tm, HEAD_DIM), lambda i: (0, i, 0))
        vt_spec = pl.BlockSpec((N_KV, nck, HEAD_DIM, CHUNK), lambda i: (0, i, 0, 0))
        k_shape = jax.ShapeDtypeStruct((N_KV, m, HEAD_DIM), BF16)
        vt_shape = jax.ShapeDtypeStruct((N_KV, m // CHUNK, HEAD_DIM, CHUNK), BF16)
        out_specs += [chunked(N_KV, HEAD_DIM, LANES), chunked(N_KV, 3, LANES), k_spec, vt_spec, k_spec, vt_spec]
        out_shape += [jax.ShapeDtypeStruct((m // CHUNK, N_KV, HEAD_DIM, LANES), BF16),
                      jax.ShapeDtypeStruct((m // CHUNK, N_KV, 3, LANES), F32),
                      k_shape, vt_shape, k_shape, vt_shape]
    else:
        out_specs = [row(KV_DIM)] * 6 + [row(Q_DIM), row(_GATE_PAD)]
        out_shape = ([jax.ShapeDtypeStruct((m, KV_DIM), F32)] * 6
                     + [jax.ShapeDtypeStruct((m, Q_DIM), F32), jax.ShapeDtypeStruct((m, _GATE_PAD), F32)])
    outs = pl.pallas_call(
        functools.partial(_proj_kernel, attn_layouts=attn_layouts, tm=tm),
        grid=(m // tm,),
        in_specs=[row(D_MODEL),
                  pl.BlockSpec((1, D_MODEL), lambda i: (0, 0)),
                  _mod_spec(sc, tm, rows_per_batch), _mod_spec(sh, tm, rows_per_batch),
                  pl.BlockSpec((D_MODEL, _PROJ_MAIN), lambda i: (0, 0)),
                  pl.BlockSpec((D_MODEL, _GATE_PAD), lambda i: (0, 0))],
        out_specs=out_specs,
        out_shape=out_shape,
        scratch_shapes=([pltpu.VMEM((tm, Q_DIM), F32), pltpu.VMEM((tm, _GATE_PAD), F32),
                         pltpu.VMEM((6, tm, KV_DIM), F32)] if attn_layouts else []),
        compiler_params=_cparams("parallel"),
        name="nsa_project",
    )(x, g.reshape(1, D_MODEL), sc, sh, w_main, w_gate)
    if attn_layouts:
        return outs[:2], outs[2:8], outs[8:]
    return outs[:6], outs[6:]


def _mm_res_kernel(a_ref, w_ref, x_ref, gate_ref, o_ref):
    y = jnp.dot(a_ref[...].astype(BF16), w_ref[...], preferred_element_type=F32)
    o_ref[...] = x_ref[...] + gate_ref[...] * y


def _matmul_residual(a, w, x, gate, rows_per_batch, tm):
    m, k = a.shape
    tm = min(tm, m)
    return pl.pallas_call(
        _mm_res_kernel,
        grid=(m // tm,),
        in_specs=[pl.BlockSpec((tm, k), lambda i: (i, 0)),
                  pl.BlockSpec((k, D_MODEL), lambda i: (0, 0)),
                  pl.BlockSpec((tm, D_MODEL), lambda i: (i, 0)),
                  _mod_spec(gate, tm, rows_per_batch)],
        out_specs=pl.BlockSpec((tm, D_MODEL), lambda i: (i, 0)),
        out_shape=jax.ShapeDtypeStruct((m, D_MODEL), F32),
        compiler_params=_cparams("parallel"),
        name="out_proj_residual",
    )(a, w.astype(BF16), x, gate)


def _mm_res_t_kernel(ot_ref, w_ref, x_ref, gate_ref, o_ref, *, n_chunks):
    chunks = []
    for c in range(n_chunks):
        rows = [ot_ref[c, kv, :, gi * CHUNK:(gi + 1) * CHUNK] for kv in range(N_KV) for gi in range(GROUP)]
        chunks.append(jnp.concatenate(rows, axis=0).astype(F32).T)
    a = jnp.concatenate(chunks, axis=0).astype(BF16)
    y = jnp.dot(a, w_ref[...], preferred_element_type=F32)
    o_ref[...] = x_ref[...] + gate_ref[...] * y


def _matmul_residual_t(o_t, w, x, gate, rows_per_batch, tm):
    m = x.shape[0]
    nck = tm // CHUNK
    return pl.pallas_call(
        functools.partial(_mm_res_t_kernel, n_chunks=nck),
        grid=(m // tm,),
        in_specs=[pl.BlockSpec((nck, N_KV, HEAD_DIM, LANES), lambda i: (i, 0, 0, 0)),
                  pl.BlockSpec((Q_DIM, D_MODEL), lambda i: (0, 0)),
                  pl.BlockSpec((tm, D_MODEL), lambda i: (i, 0)),
                  _mod_spec(gate, tm, rows_per_batch)],
        out_specs=pl.BlockSpec((tm, D_MODEL), lambda i: (i, 0)),
        out_shape=jax.ShapeDtypeStruct((m, D_MODEL), F32),
        compiler_params=_cparams("parallel"),
        name="out_proj_residual_t",
    )(o_t, w.astype(BF16), x, gate)


def _ffn_kernel(x_ref, g_ref, sc_ref, sh_ref, gate_ref, wg_ref, wu_ref, wd_ref, fg_ref,
                o_ref, hb_ref, acc_ref, *, final_norm):
    f = pl.program_id(1)

    @pl.when(f == 0)
    def _():
        hb_ref[...] = _normmod(x_ref[...], g_ref[...], sc_ref[...], sh_ref[...]).astype(BF16)
        acc_ref[...] = jnp.zeros_like(acc_ref)

    hb = hb_ref[...]
    a = jnp.dot(hb, wg_ref[...], preferred_element_type=F32)
    u = jnp.dot(hb, wu_ref[...], preferred_element_type=F32)
    act = (_silu(a) * u).astype(BF16)
    acc_ref[...] += jnp.dot(act, wd_ref[...], preferred_element_type=F32)

    @pl.when(f == pl.num_programs(1) - 1)
    def _():
        y = x_ref[...] + gate_ref[...] * acc_ref[...]
        if final_norm:
            ms = jnp.mean(y * y, axis=-1, keepdims=True)
            y = y * lax.rsqrt(ms + RMS_EPS) * fg_ref[...]
        o_ref[...] = y


def _ffn(x, g, sc, sh, gate, wg, wu, wd, final_g, rows_per_batch, tm, tf, final_norm):
    m = x.shape[0]
    d_ff = wg.shape[1]
    tm = min(tm, m)
    assert d_ff % tf == 0
    vec = pl.BlockSpec((1, D_MODEL), lambda i, f: (0, 0))
    return pl.pallas_call(
        functools.partial(_ffn_kernel, final_norm=final_norm),
        grid=(m // tm, d_ff // tf),
        in_specs=[pl.BlockSpec((tm, D_MODEL), lambda i, f: (i, 0)),
                  vec,
                  _mod_spec(sc, tm, rows_per_batch), _mod_spec(sh, tm, rows_per_batch),
                  _mod_spec(gate, tm, rows_per_batch),
                  pl.BlockSpec((D_MODEL, tf), lambda i, f: (0, f)),
                  pl.BlockSpec((D_MODEL, tf), lambda i, f: (0, f)),
                  pl.BlockSpec((tf, D_MODEL), lambda i, f: (f, 0)),
                  vec],
        out_specs=pl.BlockSpec((tm, D_MODEL), lambda i, f: (i, 0)),
        out_shape=jax.ShapeDtypeStruct((m, D_MODEL), F32),
        scratch_shapes=[pltpu.VMEM((tm, D_MODEL), BF16), pltpu.VMEM((tm, D_MODEL), F32)],
        compiler_params=_cparams("parallel", "arbitrary"),
        name="ffn",
    )(x, g.reshape(1, D_MODEL), sc, sh, gate, wg.astype(BF16), wu.astype(BF16), wd.astype(BF16),
      final_g.reshape(1, D_MODEL))


_POOL_HALO = 16


def _pool_kernel(x_ref, xprev_ref, state_ref, g_ref, sc_ref, sh_ref, gate_ref, w_ref, ls_ref,
                 o_ref, st_ref, ext_ref, *, tm, pos0):
    i = pl.program_id(1)
    g, sc, sh = g_ref[...], sc_ref[...], sh_ref[...]
    u = _normmod(x_ref[...], g, sc, sh)
    prev = jnp.where(i == 0, state_ref[...], _normmod(xprev_ref[...], g, sc, sh))
    ext_ref[0:_POOL_HALO, :] = prev
    ext_ref[_POOL_HALO:_POOL_HALO + tm, :] = u
    st_ref[...] = ext_ref[tm:tm + _POOL_HALO, :]

    pos = pos0 + i * tm + lax.broadcasted_iota(jnp.int32, (tm, 1), 0)
    mixed = []
    for gi, w in enumerate(POOL_WINDOWS):
        lo = gi * POOL_GROUP_DIM
        s = u[:, lo:lo + POOL_GROUP_DIM]
        for k in range(1, w):
            s = s + ext_ref[_POOL_HALO - k:_POOL_HALO - k + tm, lo:lo + POOL_GROUP_DIM]
        cnt = jnp.minimum(pos + 1, w).astype(F32)
        pooled = s / cnt - u[:, lo:lo + POOL_GROUP_DIM]
        mixed.append(jnp.dot(pooled.astype(BF16), w_ref[gi], preferred_element_type=F32))
    y = jnp.concatenate(mixed, axis=-1) * ls_ref[...]
    o_ref[...] = x_ref[...] + gate_ref[...] * y


def _pool_mix(x3, state, g, sc, sh, gate, w_grp, layer_scale, pos0, tm):
    b, t, _ = x3.shape
    tm = min(tm, t)
    state16 = jnp.pad(state, ((0, 0), (_POOL_HALO - POOL_STATE, 0), (0, 0)))
    if t >= _POOL_HALO:
        xprev = x3
        nprev = tm // _POOL_HALO
        prev_spec = pl.BlockSpec((None, _POOL_HALO, D_MODEL),
                                 lambda bi, i: (bi, jnp.maximum(i * nprev - 1, 0), 0))
    else:
        xprev = state16
        prev_spec = pl.BlockSpec((None, _POOL_HALO, D_MODEL), lambda bi, i: (bi, 0, 0))
    vec = pl.BlockSpec((1, D_MODEL), lambda bi, i: (0, 0))
    bvec = pl.BlockSpec((None, 1, D_MODEL), lambda bi, i: (bi, 0, 0))
    out, st = pl.pallas_call(
        functools.partial(_pool_kernel, tm=tm, pos0=pos0),
        grid=(b, t // tm),
        in_specs=[pl.BlockSpec((None, tm, D_MODEL), lambda bi, i: (bi, i, 0)),
                  prev_spec,
                  pl.BlockSpec((None, _POOL_HALO, D_MODEL), lambda bi, i: (bi, 0, 0)),
                  vec, bvec, bvec, bvec,
                  pl.BlockSpec((len(POOL_WINDOWS), POOL_GROUP_DIM, POOL_GROUP_DIM),
                               lambda bi, i: (0, 0, 0)),
                  vec],
        out_specs=[pl.BlockSpec((None, tm, D_MODEL), lambda bi, i: (bi, i, 0)),
                   pl.BlockSpec((None, _POOL_HALO, D_MODEL), lambda bi, i: (bi, 0, 0))],
        out_shape=[jax.ShapeDtypeStruct((b, t, D_MODEL), F32),
                   jax.ShapeDtypeStruct((b, _POOL_HALO, D_MODEL), F32)],
        scratch_shapes=[pltpu.VMEM((tm + _POOL_HALO, D_MODEL), F32)],
        compiler_params=_cparams("parallel", "arbitrary"),
        name="pool_mix",
    )(x3, xprev, state16, g.reshape(1, D_MODEL), sc, sh, gate, w_grp.astype(BF16),
      layer_scale.reshape(1, D_MODEL))
    return out, st[:, _POOL_HALO - POOL_STATE:]


def _rel_bucket(dist):
    d = jnp.maximum(dist, 0)
    max_exact = N_BUCKETS // 2
    large = max_exact + (jnp.log(jnp.maximum(d, 1).astype(F32) / max_exact)
                         / math.log(MAX_DISTANCE / max_exact) * (N_BUCKETS - max_exact)).astype(jnp.int32)
    large = jnp.minimum(large, N_BUCKETS - 1)
    return jnp.where(d < max_exact, d, large)


def _dist_bias(rel_bias):
    return rel_bias[_rel_bucket(jnp.arange(MAX_DISTANCE + 1, dtype=jnp.int32))]


def _lookup(table, idx):
    onehot = (idx[..., None] == jnp.arange(table.shape[0], dtype=jnp.int32)).astype(F32)
    return jnp.einsum("...d,dh->...h", onehot, table, precision=lax.Precision.HIGHEST)


def _bias_tile(fd, dist, valid):
    k, q = dist.shape
    bias = _lookup(fd, jnp.clip(dist, 0, MAX_DISTANCE)).reshape(k, q, N_KV, GROUP)
    bias = jnp.where(valid[:, :, None, None], bias, NEG)
    return bias.transpose(2, 0, 3, 1).reshape(N_KV, k, GROUP * q)


def _prompt_tables(rel_bias, t):
    fd = _dist_bias(rel_bias) * LOG2E
    kj = jnp.arange(KEY_TILE, dtype=jnp.int32)[:, None]
    qi = jnp.arange(CHUNK, dtype=jnp.int32)[None, :]
    dist = qi - kj
    near = jnp.stack([_bias_tile(fd, dist + KEY_TILE, dist + KEY_TILE >= 0),
                      _bias_tile(fd, dist, dist >= 0)])
    wfirst = _bias_tile(fd, dist + WINDOW, dist < 0)
    far = _bias_tile(fd, jnp.full((1, CHUNK), MAX_DISTANCE, jnp.int32), jnp.ones((1, CHUNK), bool))
    per_chunk = CHUNK // CMP_BLK
    assert CHUNK == 128 and CMP_BLK == 32 and MAX_DISTANCE == 128
    rel0, n_rel = -per_chunk, 2 * per_chunk
    rel = rel0 + jnp.arange(n_rel, dtype=jnp.int32)[:, None]
    dist = qi - CMP_BLK * rel - (CMP_BLK - 1)
    crel = _bias_tile(fd, dist, dist >= 0)
    relm = (jnp.arange(t // CMP_BLK, dtype=jnp.int32)[None, :]
            - per_chunk * jnp.arange(t // CHUNK, dtype=jnp.int32)[:, None])[:, None, :, None]
    cmp_tab = jnp.where(relm >= rel0 + n_rel, NEG, far[None])
    for r in range(n_rel):
        cmp_tab = jnp.where(relm == rel0 + r, crel[None, :, r:r + 1, :], cmp_tab)
    return near, wfirst, far, cmp_tab


def _tile_softmax(tiles, qs):
    scores = [jnp.dot(k, qs, preferred_element_type=F32) + add for k, _, add in tiles]
    m = functools.reduce(jnp.maximum, [jnp.max(s, axis=0, keepdims=True) for s in scores])
    l, acc = 0.0, 0.0
    for s, (_, vt, _) in zip(scores, tiles):
        p = jnp.exp2(s - m)
        l = l + jnp.sum(p, axis=0, keepdims=True)
        acc = acc + jnp.dot(vt, p.astype(BF16), preferred_element_type=F32)
    return m, l, acc


def _prompt_attn_kernel(q_ref, g_ref, kc_ref, vct_ref, cb_ref, ks_ref, vst_ref, kw_ref, vwt_ref,
                        near_ref, wfirst_ref, far_ref, o_ref,
                        imp_ref, val_ref, rank_ref, selrep_ref, m_ref, l_ref, alpha_ref, acc_ref, mix_ref,
                        s_ref, p_ref, *, n_blk, n_sel):
    i = pl.program_id(1)

    def k_rows(ref, kv, jt):
        return ref[kv, pl.ds(pl.multiple_of(jt * KEY_TILE, KEY_TILE), KEY_TILE), :]

    def sel_rows(kv, jt):
        return selrep_ref[kv, pl.ds(2 * jt, 1), :], selrep_ref[kv, pl.ds(2 * jt + 1, 1), :]

    def halves(top, bot):
        return jnp.concatenate([jnp.broadcast_to(top, (SEL_BLK, LANES)),
                                jnp.broadcast_to(bot, (SEL_BLK, LANES))], axis=0)

    for kv in range(N_KV):
        bias = cb_ref[kv]
        s = jnp.dot(kc_ref[kv], q_ref[kv], preferred_element_type=F32) + bias
        valid = bias > 0.5 * NEG
        m = jnp.max(s, axis=0, keepdims=True)
        p = jnp.where(valid, jnp.exp2(s - m), 0.0)
        p = p / jnp.maximum(jnp.sum(p, axis=0, keepdims=True), 1e-30)
        o_c = jnp.dot(vct_ref[kv], p.astype(BF16), preferred_element_type=F32)
        mix_ref[kv] = _sigmoid(g_ref[kv, 0:1, :]) * o_c
        imp_ref[kv] = functools.reduce(
            lambda a, b: a + b, [p[:, gi * CHUNK:(gi + 1) * CHUNK] for gi in range(GROUP)])

    imp = jnp.concatenate(
        [imp_ref[kv, pl.ds(0, n_blk, stride=2), :] + imp_ref[kv, pl.ds(1, n_blk, stride=2), :]
         for kv in range(N_KV)], axis=1)
    blk = lax.broadcasted_iota(jnp.int32, (n_blk, LANES), 0)
    lane = lax.broadcasted_iota(jnp.int32, (1, LANES), 1)
    assert CHUNK == 2 * SEL_BLK
    qblk = 2 * i + jnp.where((lane & (CHUNK - 1)) >= SEL_BLK, 1, 0)
    forced = (blk == 0) | (blk == qblk) | (blk == qblk - 1)
    val_ref[...] = jnp.where(forced, BIG, jnp.where(blk <= qblk, imp, -1.0))
    rank_ref[...] = jnp.zeros_like(rank_ref)
    sub = lax.broadcasted_iota(jnp.int32, (8, LANES), 0)
    n_grp = n_blk // 8
    for grp in range(n_grp):
        @pl.when(8 * grp <= 2 * i + 1)
        def _(grp=grp):
            for r8 in range(n_grp):
                piece = val_ref[r8 * 8:(r8 + 1) * 8, :]
                acc = rank_ref[r8 * 8:(r8 + 1) * 8, :]
                for j in range(grp * 8, grp * 8 + 8):
                    row = val_ref[j:j + 1, :]
                    if r8 > grp:
                        ahead = row >= piece
                    elif r8 < grp:
                        ahead = row > piece
                    else:
                        acc = acc + jnp.where(sub > j - grp * 8, jnp.where(row >= piece, 1.0, 0.0),
                                              jnp.where(row > piece, 1.0, 0.0))
                        continue
                    acc = acc + jnp.where(ahead, 1.0, 0.0)
                rank_ref[r8 * 8:(r8 + 1) * 8, :] = acc
    selneg = jnp.where(rank_ref[...] < n_sel, 0.0, NEG)
    for kv in range(N_KV):
        selrep_ref[kv] = jnp.concatenate([selneg[:, kv * CHUNK:(kv + 1) * CHUNK]] * GROUP, axis=1)

    prev = jnp.maximum(i - 1, 0)
    prev_off = jnp.where(i >= 1, 0.0, NEG)
    for kv in range(N_KV):
        qs = q_ref[kv]
        far = far_ref[kv]
        tiles = [(k_rows(kw_ref, kv, i), vwt_ref[kv, i], near_ref[1, kv]),
                 (k_rows(kw_ref, kv, prev), vwt_ref[kv, prev], near_ref[0, kv] + prev_off)]
        for back, add in ((2, far), (3, far), (4, wfirst_ref[kv])):
            jt = jnp.maximum(i - back, 0)
            tiles.append((k_rows(kw_ref, kv, jt), vwt_ref[kv, jt], add + jnp.where(i >= back, 0.0, NEG)))
        _, l, acc = _tile_softmax(tiles, qs)
        mix_ref[kv] += _sigmoid(g_ref[kv, 2:3, :]) * (acc / jnp.maximum(l, 1e-30))

        tiles = [(k_rows(ks_ref, kv, i), vst_ref[kv, i], near_ref[1, kv] + halves(*sel_rows(kv, i))),
                 (k_rows(ks_ref, kv, prev), vst_ref[kv, prev],
                  near_ref[0, kv] + halves(*sel_rows(kv, prev)) + prev_off)]
        m_ref[kv], l_ref[kv], acc_ref[kv] = _tile_softmax(tiles, qs)

    n_far = jnp.maximum(i - 1, 0)
    last = jnp.maximum(n_far - 1, 0)

    def scores(jt, slot):
        jc = jnp.minimum(jt, last)
        off = jnp.where(jt < n_far, 0.0, NEG)
        for kv in range(N_KV):
            top, bot = sel_rows(kv, jc)
            far = far_ref[kv] + off
            s_ref[slot, kv] = (jnp.dot(k_rows(ks_ref, kv, jc), q_ref[kv], preferred_element_type=F32)
                               + halves(top + far, bot + far))

    def softmax(slot):
        for kv in range(N_KV):
            s = s_ref[slot, kv]
            m_old = m_ref[kv]
            m_new = jnp.maximum(m_old, jnp.max(s, axis=0, keepdims=True))
            alpha = jnp.exp2(m_old - m_new)
            p = jnp.exp2(s - m_new)
            m_ref[kv] = m_new
            alpha_ref[slot, kv] = alpha
            l_ref[kv] = alpha * l_ref[kv] + jnp.sum(p, axis=0, keepdims=True)
            p_ref[slot, kv] = p.astype(BF16)

    def values(jt, slot):
        jc = jnp.clip(jt, 0, last)
        for kv in range(N_KV):
            acc_ref[kv] = alpha_ref[slot, kv] * acc_ref[kv] + jnp.dot(vst_ref[kv, jc], p_ref[slot, kv],
                                                                      preferred_element_type=F32)

    p_ref[1] = jnp.zeros(p_ref.shape[1:], BF16)
    alpha_ref[1] = jnp.ones(alpha_ref.shape[1:], F32)
    scores(0, 0)

    def far_body(j2, carry):
        jt = 2 * j2
        values(jt - 1, 1)
        softmax(0)
        scores(jt + 1, 1)
        values(jt, 0)
        softmax(1)
        scores(jt + 2, 0)
        return carry

    n_pairs = (n_far + 1) // 2
    lax.fori_loop(0, n_pairs, far_body, 0)
    values(2 * n_pairs - 1, 1)

    for kv in range(N_KV):
        o_s = acc_ref[kv] / jnp.maximum(l_ref[kv], 1e-30)
        o_ref[kv] = (mix_ref[kv] + _sigmoid(g_ref[kv, 1:2, :]) * o_s).astype(o_ref.dtype)


def _prompt_attention(layouts, k_cmp, v_cmp, rel_bias, b, t):
    q_t, g_t, ks, vst, kw, vwt = layouts
    nch, nt, nc, n_blk = t // CHUNK, t // KEY_TILE, t // CMP_BLK, t // SEL_BLK
    kc = k_cmp.reshape(b, nc, N_KV, HEAD_DIM).transpose(0, 2, 1, 3).astype(BF16)
    vct = v_cmp.reshape(b, nc, N_KV, HEAD_DIM).transpose(0, 2, 3, 1).astype(BF16)
    near, wfirst, far, cmp_tab = _prompt_tables(rel_bias, t)

    once = pl.Buffered(1)
    per_b = lambda *shape: pl.BlockSpec((None,) + shape, lambda bi, i: (bi,) + (0,) * len(shape),
                                        pipeline_mode=once)
    const = lambda *shape: pl.BlockSpec(shape, lambda bi, i: (0,) * len(shape), pipeline_mode=once)
    chunk = lambda *shape: pl.BlockSpec((None,) + shape, lambda bi, i: (bi * nch + i,) + (0,) * len(shape))
    k_spec = pl.BlockSpec((N_KV, t, HEAD_DIM), lambda bi, i: (0, bi, 0), pipeline_mode=once)
    vt_spec = pl.BlockSpec((N_KV, nt, HEAD_DIM, KEY_TILE), lambda bi, i: (0, bi, 0, 0), pipeline_mode=once)
    stat = pltpu.VMEM((N_KV, 1, LANES), F32)
    wide = pltpu.VMEM((N_KV, HEAD_DIM, LANES), F32)
    return pl.pallas_call(
        functools.partial(_prompt_attn_kernel, n_blk=n_blk, n_sel=min(N_SEL, n_blk)),
        grid=(b, nch),
        in_specs=[chunk(N_KV, HEAD_DIM, LANES),
                  chunk(N_KV, 3, LANES),
                  per_b(N_KV, nc, HEAD_DIM),
                  per_b(N_KV, HEAD_DIM, nc),
                  pl.BlockSpec((None, N_KV, nc, LANES), lambda bi, i: (i, 0, 0, 0)),
                  k_spec, vt_spec, k_spec, vt_spec,
                  const(2, N_KV, KEY_TILE, LANES),
                  const(N_KV, KEY_TILE, LANES),
                  const(N_KV, 1, LANES)],
        out_specs=chunk(N_KV, HEAD_DIM, LANES),
        out_shape=jax.ShapeDtypeStruct((b * nch, N_KV, HEAD_DIM, LANES), BF16),
        scratch_shapes=[pltpu.VMEM((N_KV, nc, CHUNK), F32),
                        pltpu.VMEM((n_blk, LANES), F32),
                        pltpu.VMEM((n_blk, LANES), F32),
                        pltpu.VMEM((N_KV, n_blk, LANES), F32),
                        stat, stat, pltpu.VMEM((2, N_KV, 1, LANES), F32), wide, wide,
                        pltpu.VMEM((2, N_KV, KEY_TILE, LANES), F32),
                        pltpu.VMEM((2, N_KV, KEY_TILE, LANES), BF16)],
        compiler_params=_cparams("parallel", "arbitrary"),
        name="nsa_prompt_attention",
    )(q_t, g_t, kc, vct, cmp_tab, ks, vst, kw, vwt, near, wfirst, far)


NEW_PAD = 128


def _sample_tables(rel_bias, past, t, wbuf, n_rows):
    lane = jnp.arange(SAMPLE_LANES, dtype=jnp.int32)
    g, kvh, qi = lane // (N_KV * t), (lane // t) % N_KV, lane % t
    used = lane < GROUP * N_KV * t
    head = jnp.where(used, kvh * GROUP + g, 0)
    qpos = past + qi
    fd_lane = jnp.take(_dist_bias(rel_bias), head, axis=1)
    dists = jnp.arange(MAX_DISTANCE + 1, dtype=jnp.int32)

    def tab(kpos, valid, n_near):
        dist = qpos[None, :] - kpos[:, None]
        n_far = kpos.shape[0] - n_near
        near_d = jnp.clip(dist[n_far:], 0, MAX_DISTANCE)
        near = jnp.sum(jnp.where(near_d[:, :, None] == dists, fd_lane.T[None], 0.0), axis=-1)
        far = jnp.broadcast_to(fd_lane[MAX_DISTANCE][None, :], (n_far, SAMPLE_LANES))
        bias = jnp.where(used[None, :], jnp.concatenate([far, near], axis=0), 0.0)
        return jnp.where(valid(dist) & (kpos[:, None] >= 0), bias, NEG)

    new_pos = past + jnp.where(jnp.arange(NEW_PAD) < t, jnp.arange(NEW_PAD, dtype=jnp.int32), 1 << 20)
    causal = lambda d: d >= 0
    window = lambda d: (d >= 0) & (d < WINDOW)
    nc = past // CMP_BLK
    assert wbuf >= MAX_DISTANCE and past >= MAX_DISTANCE
    t_cmp = tab(jnp.arange(nc, dtype=jnp.int32) * CMP_BLK + CMP_BLK - 1, causal, MAX_DISTANCE // CMP_BLK)
    rows = lambda a: a[:, :n_rows].T
    t_sel = rows(tab(jnp.concatenate([jnp.arange(past, dtype=jnp.int32), new_pos]), causal,
                     MAX_DISTANCE + NEW_PAD))
    t_win = rows(tab(jnp.concatenate([past - wbuf + jnp.arange(wbuf, dtype=jnp.int32), new_pos]), window,
                     MAX_DISTANCE + NEW_PAD))
    key = jnp.arange(past + NEW_PAD, dtype=jnp.int32)
    blk = jnp.where(key < past, key // SEL_BLK, past // SEL_BLK)
    expand = (blk[None, :] == jnp.arange(128, dtype=jnp.int32)[:, None]).astype(BF16)
    return t_cmp, t_sel, t_win, expand


def _sample_attn_kernel(pt_ref, *refs, n_pages, n_sel, t, wbuf):
    del pt_ref
    it = iter(refs)
    qbd_ref, qrow_ref, g_ref = next(it), next(it), next(it)
    kc_pages = [next(it) for _ in range(n_pages)]
    vc_pages = [next(it) for _ in range(n_pages)]
    ks_pages = [next(it) for _ in range(n_pages)]
    vs_pages = [next(it) for _ in range(n_pages)]
    ksn_ref, vsn_ref, kwin_ref, vwin_ref, kwn_ref, vwn_ref = (next(it) for _ in range(6))
    tcmp_ref, tsel_ref, twin_ref, expand_ref = (next(it) for _ in range(4))
    o_ref = next(it)
    kc_ref, vc_ref, imp_ref, val_ref, sel_ref, s_ref = (next(it) for _ in range(6))

    scale = HEAD_DIM ** -0.5
    nq = N_KV * t
    qs = qbd_ref[...] * scale
    qr = qrow_ref[...] * scale
    n_q = qr.shape[0]
    lane = lax.broadcasted_iota(jnp.int32, (1, SAMPLE_LANES), 1)
    blocks_per_page = PAGE_SIZE // CMP_BLK
    past = n_pages * PAGE_SIZE
    nt_dims = (((1,), (1,)), ((), ()))

    for p in range(n_pages):
        kc_ref[p * blocks_per_page:(p + 1) * blocks_per_page, :] = kc_pages[p][...]
        vc_ref[p * blocks_per_page:(p + 1) * blocks_per_page, :] = vc_pages[p][...]
    s = jnp.dot(kc_ref[...].astype(BF16), qs, preferred_element_type=F32) + tcmp_ref[...]
    m = jnp.max(s, axis=0, keepdims=True)
    p_c = jnp.exp(s - m)
    p_c = p_c / jnp.maximum(jnp.sum(p_c, axis=0, keepdims=True), 1e-30)
    o_c = lax.dot_general(p_c.astype(BF16), vc_ref[...].astype(BF16), (((0,), (0,)), ((), ())),
                          preferred_element_type=F32)[:n_q]

    imp = p_c
    for g in range(1, GROUP):
        imp = imp + pltpu.roll(p_c, SAMPLE_LANES - g * nq, 1)
    imp_ref[...] = imp
    n_pairs = n_pages * blocks_per_page // 2
    n_blk = n_pairs + 1
    n_rows = val_ref.shape[0]
    val_ref[...] = jnp.full((n_rows, SAMPLE_LANES), -2.0, F32)
    val_ref[0:n_pairs, :] = imp_ref[pl.ds(0, n_pairs, stride=2), :] + imp_ref[pl.ds(1, n_pairs, stride=2), :]
    blk = lax.broadcasted_iota(jnp.int32, (n_rows, SAMPLE_LANES), 0)
    qblk = n_blk - 1
    forced = (blk == 0) | (blk == qblk) | (blk == qblk - 1)
    val = jnp.where(forced, BIG, val_ref[...])
    val = jnp.where(blk < n_blk, val, -2.0)
    val_ref[...] = val
    rank = jnp.zeros((n_rows, SAMPLE_LANES), F32)
    for j in range(n_blk):
        row = val_ref[j:j + 1, :]
        rank = rank + jnp.where(blk > j, jnp.where(row >= val, 1.0, 0.0), jnp.where(row > val, 1.0, 0.0))
    selneg = jnp.where((rank < n_sel) & (lane < nq), 0.0, jnp.where(lane < nq, NEG, 0.0))
    selrep = selneg
    for g in range(1, GROUP):
        selrep = selrep + pltpu.roll(selneg, g * nq, 1)
    sel_ref[...] = jnp.zeros_like(sel_ref)
    sel_ref[0:n_rows, :] = jnp.where(selrep == 0.0, 1.0, 0.0)
    sel01 = sel_ref[...].T[:n_q].astype(BF16)

    def paged(page_ref):
        return page_ref[...].reshape(KV_DIM, PAGE_SIZE).astype(BF16)

    for p in range(n_pages + 1):
        cols = slice(p * KEY_TILE, (p + 1) * KEY_TILE)
        if p < n_pages:
            s = jnp.dot(qr, paged(ks_pages[p]), preferred_element_type=F32)
        else:
            s = lax.dot_general(qr, ksn_ref[...], nt_dims, preferred_element_type=F32)
        picked = jnp.dot(sel01, expand_ref[:, cols], preferred_element_type=F32)
        s_ref[:, cols] = s + tsel_ref[:, cols] + (picked - 1.0) * (-NEG)
    s = s_ref[...]
    p_s = jnp.exp(s - jnp.max(s, axis=1, keepdims=True))
    l = jnp.sum(p_s, axis=1, keepdims=True)
    p_s = p_s.astype(BF16)
    acc = jnp.dot(p_s[:, past:], vsn_ref[...], preferred_element_type=F32)
    for p in range(n_pages):
        acc = acc + lax.dot_general(p_s[:, p * KEY_TILE:(p + 1) * KEY_TILE], paged(vs_pages[p]), nt_dims,
                                    preferred_element_type=F32)
    o_s = acc / jnp.maximum(l, 1e-30)

    kwin = kwin_ref[...].reshape(KV_DIM, wbuf).astype(BF16)
    vwin = vwin_ref[...].reshape(KV_DIM, wbuf).astype(BF16)
    s_w = jnp.dot(qr, kwin, preferred_element_type=F32) + twin_ref[:, :wbuf]
    s_n = lax.dot_general(qr, kwn_ref[...], nt_dims, preferred_element_type=F32) + twin_ref[:, wbuf:]
    m = jnp.maximum(jnp.max(s_w, axis=1, keepdims=True), jnp.max(s_n, axis=1, keepdims=True))
    p_w, p_n = jnp.exp(s_w - m), jnp.exp(s_n - m)
    l = jnp.sum(p_w, axis=1, keepdims=True) + jnp.sum(p_n, axis=1, keepdims=True)
    acc = (lax.dot_general(p_w.astype(BF16), vwin, nt_dims, preferred_element_type=F32)
           + jnp.dot(p_n.astype(BF16), vwn_ref[...], preferred_element_type=F32))
    o_w = acc / jnp.maximum(l, 1e-30)

    gate = _sigmoid(g_ref[...])
    o_ref[...] = gate[:, 0:1] * o_c + gate[:, 1:2] * o_s + gate[:, 2:3] * o_w


def _sample_attention(q, gates, kcmp_phys, vcmp_phys, cache_ks, cache_vs, page_table,
                      ksl, vsl, win_k, win_v, kw, vw, rel_bias, b, t):
    n_pages = page_table.shape[1]
    past = n_pages * PAGE_SIZE
    wbuf = win_k.shape[-1]
    n_phys = cache_ks.shape[0]
    nq = N_KV * t
    used = GROUP * nq
    assert used % 8 == 0 and used <= SAMPLE_LANES and t <= NEW_PAD
    blocks_per_page = PAGE_SIZE // CMP_BLK
    n_blk = past // SEL_BLK + 1
    n_rows = -(-n_blk // 8) * 8

    q5 = q.reshape(b, t, N_KV, GROUP, HEAD_DIM)
    eye = jnp.eye(N_KV, dtype=q.dtype)
    qbd = jnp.einsum("btkgd,kc->bkdgct", q5, eye).reshape(b, KV_DIM, used)
    qbd = jnp.pad(qbd, ((0, 0), (0, 0), (0, SAMPLE_LANES - used))).astype(BF16)
    qrow = jnp.einsum("btkgd,kc->bgktcd", q5, eye).reshape(b, used, KV_DIM).astype(BF16)
    g_rows = gates[:, :3 * N_HEADS].reshape(b, t, N_KV, GROUP, 3).transpose(0, 3, 2, 1, 4).reshape(b, used, 3)
    g_rows = jnp.pad(g_rows, ((0, 0), (0, 0), (0, 128 - 3)))
    pad_rows = lambda a: jnp.pad(a.reshape(b, t, KV_DIM), ((0, 0), (0, NEW_PAD - t), (0, 0))).astype(BF16)
    tables = _sample_tables(rel_bias, past, t, wbuf, used)

    per_b = lambda *shape: pl.BlockSpec((None,) + shape, lambda bi, pt: (bi,) + (0,) * len(shape))
    const = lambda *shape: pl.BlockSpec(shape, lambda bi, pt: (0,) * len(shape))

    def cmp_page(p):
        return pl.BlockSpec((None, blocks_per_page, KV_DIM), lambda bi, pt, p=p: (pt[bi, p], 0, 0))

    def page(p):
        return pl.BlockSpec((None, N_KV, HEAD_DIM, PAGE_SIZE), lambda bi, pt, p=p: (pt[bi, p], 0, 0, 0))

    in_specs = ([per_b(KV_DIM, SAMPLE_LANES), per_b(used, KV_DIM), per_b(used, 128)]
                + [cmp_page(p) for p in range(n_pages)] * 2
                + [page(p) for p in range(n_pages)] * 2
                + [per_b(NEW_PAD, KV_DIM), per_b(NEW_PAD, KV_DIM),
                   per_b(N_KV, HEAD_DIM, wbuf), per_b(N_KV, HEAD_DIM, wbuf),
                   per_b(NEW_PAD, KV_DIM), per_b(NEW_PAD, KV_DIM)]
                + [const(*tb.shape) for tb in tables])
    kcp = kcmp_phys.reshape(n_phys, blocks_per_page, KV_DIM)
    vcp = vcmp_phys.reshape(n_phys, blocks_per_page, KV_DIM)
    out = pl.pallas_call(
        functools.partial(_sample_attn_kernel, n_pages=n_pages, n_sel=min(N_SEL, n_blk), t=t, wbuf=wbuf),
        grid_spec=pltpu.PrefetchScalarGridSpec(
            num_scalar_prefetch=1,
            grid=(b,),
            in_specs=in_specs,
            out_specs=per_b(used, KV_DIM),
            scratch_shapes=[pltpu.VMEM((n_pages * blocks_per_page, KV_DIM), F32),
                            pltpu.VMEM((n_pages * blocks_per_page, KV_DIM), F32),
                            pltpu.VMEM((n_pages * blocks_per_page, SAMPLE_LANES), F32),
                            pltpu.VMEM((n_rows, SAMPLE_LANES), F32),
                            pltpu.VMEM((128, SAMPLE_LANES), F32),
                            pltpu.VMEM((used, past + NEW_PAD), F32)]),
        out_shape=jax.ShapeDtypeStruct((b, used, KV_DIM), F32),
        compiler_params=_cparams("arbitrary"),
        name="nsa_sample_attention",
    )(page_table, qbd, qrow, g_rows, *([kcp] * n_pages), *([vcp] * n_pages), *([cache_ks] * n_pages),
      *([cache_vs] * n_pages), pad_rows(ksl), pad_rows(vsl), win_k, win_v, pad_rows(kw), pad_rows(vw), *tables)
    o = out.reshape(b, GROUP, N_KV, t, N_KV, HEAD_DIM)
    o = jnp.einsum("bgktkd->btkgd", o)
    return o.reshape(b * t, Q_DIM)


def _compress_weights(w_c, pe_c):
    eye = jnp.eye(N_KV, dtype=w_c.dtype)
    w_big = jnp.einsum("jde,kc->jkdce", w_c, eye).reshape(CMP_BLK * KV_DIM, KV_DIM)
    pe_flat = jnp.broadcast_to(pe_c[:, None, :], (CMP_BLK, N_KV, HEAD_DIM)).reshape(1, CMP_BLK * KV_DIM)
    pe_rows = jnp.pad(pe_flat, ((0, 7), (0, 0)))
    bias = _matmul(pe_rows, w_big, name="compress_pe")[0:1]
    return w_big, bias


def _compress(x_rows, w_big, bias, tm):
    return _matmul(x_rows, w_big, bias, tm=tm, name="compress")


_BLOCK_PITCH = CMP_BLK + 4


def _compress_pages_kernel(x_ref, w_ref, b_ref, o_ref, rows_ref, *, n_pages):
    per_page = PAGE_SIZE // CMP_BLK
    for p in range(n_pages):
        xt = x_ref[p].reshape(KV_DIM, PAGE_SIZE).T
        for n in range(per_page):
            lo = (p * per_page + n) * _BLOCK_PITCH
            for h in range(2):
                rows_ref[h, lo:lo + CMP_BLK, :] = xt[n * CMP_BLK:(n + 1) * CMP_BLK, h * 128:(h + 1) * 128]
    n_out = n_pages * per_page
    acc = jnp.broadcast_to(b_ref[...], (n_out, KV_DIM))
    for j in range(CMP_BLK):
        for h in range(2):
            piece = rows_ref[h, pl.ds(j, n_out, stride=_BLOCK_PITCH), :].astype(BF16)
            lo = j * KV_DIM + h * 128
            acc = acc + jnp.dot(piece, w_ref[lo:lo + 128, :], preferred_element_type=F32)
    o_ref[...] = acc


def _compress_pages(cache_t, w_big, bias, n_pages):
    assert KV_DIM == 256
    n_phys = cache_t.shape[0]
    assert n_phys % n_pages == 0
    n_out = n_pages * (PAGE_SIZE // CMP_BLK)
    return pl.pallas_call(
        functools.partial(_compress_pages_kernel, n_pages=n_pages),
        grid=(n_phys // n_pages,),
        in_specs=[pl.BlockSpec((n_pages, N_KV, HEAD_DIM, PAGE_SIZE), lambda i: (i, 0, 0, 0)),
                  pl.BlockSpec((CMP_BLK * KV_DIM, KV_DIM), lambda i: (0, 0)),
                  pl.BlockSpec((1, KV_DIM), lambda i: (0, 0))],
        out_specs=pl.BlockSpec((n_out, KV_DIM), lambda i: (i, 0)),
        out_shape=jax.ShapeDtypeStruct((n_phys * (PAGE_SIZE // CMP_BLK), KV_DIM), F32),
        scratch_shapes=[pltpu.VMEM((2, n_out * _BLOCK_PITCH, 128), F32)],
        compiler_params=_cparams("parallel"),
        name="compress_pages",
    )(cache_t, w_big.astype(BF16), bias)


def _ada(c_all, ada_w, ada_b):
    mods = []
    for i in range(ada_w.shape[0]):
        mods.append(_matmul(c_all, ada_w[i], ada_b[i].reshape(1, -1), silu_in=True, tn=2 * D_MODEL,
                            name="ada_modulate"))
    return mods


def kernel(x_prompt, x_sample, c_prompt, c_sample, cache_k_cmp, cache_v_cmp, cache_k_sel, cache_v_sel,
           page_table, state_k_win, state_v_win, state_pool, rel_bias, ada_w, ada_b, norm_g, final_g,
           nsa_w_in, nsa_w_out, cmp_wk, cmp_wv, cmp_pe_k, cmp_pe_v, pool_w, pool_scale,
           ffn_wg, ffn_wu, ffn_wd):
    bp, tp, _ = x_prompt.shape
    bs, ts, _ = x_sample.shape
    n_phys = cache_k_cmp.shape[1]
    past = page_table.shape[1] * PAGE_SIZE
    depth = ada_w.shape[0]
    assert depth == 2 and tp % KEY_TILE == 0 and ts <= 8 and past % PAGE_SIZE == 0

    n_c = bp + bs
    c_all = jnp.pad(jnp.concatenate([c_prompt, c_sample], axis=0), ((0, -n_c % 8), (0, 0)))
    mods = _ada(c_all, ada_w, ada_b)

    def mod_prompt(i):
        return [v[:bp].reshape(bp, 1, D_MODEL) for v in jnp.split(mods[i], 6, axis=-1)]

    def mod_sample(i):
        return [jnp.repeat(v[bp:n_c], ts, axis=0) for v in jnp.split(mods[i], 6, axis=-1)]

    wk_big, k_bias = _compress_weights(cmp_wk[0], cmp_pe_k[0])
    wv_big, v_bias = _compress_weights(cmp_wv[0], cmp_pe_v[0])

    mp = bp * tp
    x = x_prompt.reshape(mp, D_MODEL)
    sh1, sc1, g1, sh2, sc2, g2 = mod_prompt(0)
    (kc, vc), kv_t, layouts = _nsa_project(x, norm_g[0, 0], sc1, sh1, nsa_w_in[0], tp, 512, True)
    blk_rows = lambda a: a.reshape(mp // CMP_BLK, CMP_BLK * KV_DIM)
    k_cmp = _compress(blk_rows(kc), wk_big, k_bias, 256)
    v_cmp = _compress(blk_rows(vc), wv_big, v_bias, 256)
    o_t = _prompt_attention(layouts, k_cmp, v_cmp, rel_bias, bp, tp)
    x = _matmul_residual_t(o_t, nsa_w_out[0], x, g1, tp, 512)
    x = _ffn(x, norm_g[0, 1], sc2, sh2, g2, ffn_wg[0], ffn_wu[0], ffn_wd[0], final_g, tp, 512, 1408, False)
    sh1, sc1, g1, sh2, sc2, g2 = mod_prompt(1)
    x3, pool_p = _pool_mix(x.reshape(bp, tp, D_MODEL), jnp.zeros((bp, POOL_STATE, D_MODEL), F32),
                           norm_g[1, 0], sc1, sh1, g1, pool_w[0], pool_scale[0], 0, 512)
    y_prompt = _ffn(x3.reshape(mp, D_MODEL), norm_g[1, 1], sc2, sh2, g2, ffn_wg[1], ffn_wu[1], ffn_wd[1],
                    final_g, tp, 512, 1408, True).reshape(bp, tp, D_MODEL)
    win = min(WINDOW, tp)
    st5 = lambda a, t0: jnp.transpose(a[:, :, t0:].reshape(bp, N_KV, HEAD_DIM, tp - t0), (0, 3, 1, 2))[None]
    prompt_states = tuple(st5(a, 0) for a in kv_t[:4]) + tuple(st5(a, tp - win) for a in kv_t[4:]) + (pool_p[None],)

    ms = bs * ts
    x = x_sample.reshape(ms, D_MODEL)
    sh1, sc1, g1, sh2, sc2, g2 = mod_sample(0)
    (kc, vc, ksl, vsl, kw, vw), (q, gates) = _nsa_project(x, norm_g[0, 0], sc1, sh1, nsa_w_in[0], ts, 512, False)
    assert (past + ts) // CMP_BLK == past // CMP_BLK
    pos_last = lambda a: jnp.transpose(a[0], (0, 2, 3, 1))
    pages_per_step = math.gcd(n_phys, 32)
    kcmp_phys = _compress_pages(pos_last(cache_k_cmp), wk_big, k_bias, pages_per_step)
    vcmp_phys = _compress_pages(pos_last(cache_v_cmp), wv_big, v_bias, pages_per_step)
    o = _sample_attention(q, gates, kcmp_phys, vcmp_phys, pos_last(cache_k_sel), pos_last(cache_v_sel),
                          page_table, ksl, vsl, pos_last(state_k_win), pos_last(state_v_win), kw, vw,
                          rel_bias, bs, ts)
    x = _matmul_residual(o, nsa_w_out[0], x, g1, ts, 512)
    x = _ffn(x, norm_g[0, 1], sc2, sh2, g2, ffn_wg[0], ffn_wu[0], ffn_wd[0], final_g, ts, 512, 1408, False)
    sh1, sc1, g1, sh2, sc2, g2 = mod_sample(1)
    b3 = lambda v: v.reshape(bs, ts, D_MODEL)[:, :1]
    x3, pool_s = _pool_mix(x.reshape(bs, ts, D_MODEL), state_pool[0], norm_g[1, 0], b3(sc1), b3(sh1), b3(g1),
                           pool_w[0], pool_scale[0], past, 512)
    y_sample = _ffn(x3.reshape(ms, D_MODEL), norm_g[1, 1], sc2, sh2, g2, ffn_wg[1], ffn_wu[1], ffn_wd[1],
                    final_g, ts, 512, 1408, True).reshape(bs, ts, D_MODEL)
    st5 = lambda a: a.reshape(1, bs, ts, N_KV, HEAD_DIM)
    kw_ext = jnp.concatenate([state_k_win[0], st5(kw)[0]], axis=1)[:, ts:]
    vw_ext = jnp.concatenate([state_v_win[0], st5(vw)[0]], axis=1)[:, ts:]
    sample_states = (st5(kc), st5(vc), st5(ksl), st5(vsl), kw_ext[None], vw_ext[None], pool_s[None])

    return (y_prompt, y_sample) + prompt_states + sample_states
```

```python
import functools
import math

import jax
import jax.numpy as jnp
from jax import lax
from jax.experimental import pallas as pl
from jax.experimental.pallas import tpu as pltpu

D_MODEL = 1024
N_HEADS = 16
N_KV = 4
GROUP = N_HEADS // N_KV
HEAD_DIM = 64
Q_DIM = N_HEADS * HEAD_DIM
KV_DIM = N_KV * HEAD_DIM
CMP_BLK = 32
SEL_BLK = 64
N_SEL = 16
WINDOW = 512
Q_CHUNK = 64
N_BUCKETS = 32
MAX_DISTANCE = 128
POOL_WINDOWS = (2, 4, 8, 16)
POOL_GROUP_DIM = D_MODEL // len(POOL_WINDOWS)
POOL_STATE = max(POOL_WINDOWS) - 1
PAGE_SIZE = 128
RMS_EPS = 1e-6
NEG = -1e30
BIG = 1e9

KEY_TILE = 128
CHUNK = 128
LANES = GROUP * CHUNK
LOG2E = math.log2(math.e)
Q_SCALE_LOG2 = HEAD_DIM ** -0.5 * LOG2E
SAMPLE_LANES = 128
VMEM_LIMIT_BYTES = 48 * 1024 * 1024

F32 = jnp.float32
BF16 = jnp.bfloat16


def _cparams(*sem):
    return pltpu.CompilerParams(dimension_semantics=sem, vmem_limit_bytes=VMEM_LIMIT_BYTES)


def _silu(x):
    return x * (1.0 / (1.0 + jnp.exp(-x)))


def _sigmoid(x):
    return 1.0 / (1.0 + jnp.exp(-x))


def _normmod(x, g, sc, sh):
    ms = jnp.mean(x * x, axis=-1, keepdims=True)
    return (x * lax.rsqrt(ms + RMS_EPS) * g) * (1.0 + sc) + sh


def _mod_spec(mod, tm, rows_per_batch):
    if mod.ndim == 3:
        return pl.BlockSpec((None, 1, D_MODEL), lambda i, *_: ((i * tm) // rows_per_batch, 0, 0))
    return pl.BlockSpec((tm, D_MODEL), lambda i, *_: (i, 0))


def _mm_kernel(a_ref, w_ref, b_ref, o_ref, *, silu_in):
    a = a_ref[...]
    if silu_in:
        a = _silu(a)
    o_ref[...] = jnp.dot(a.astype(BF16), w_ref[...], preferred_element_type=F32) + b_ref[...]


def _matmul(a, w, bias=None, *, silu_in=False, tm=256, tn=None, name="matmul"):
    m, k = a.shape
    n = w.shape[1]
    tm = min(tm, m)
    tn = n if tn is None else tn
    assert m % tm == 0 and n % tn == 0
    if bias is None:
        bias = jnp.zeros((1, n), F32)
    return pl.pallas_call(
        functools.partial(_mm_kernel, silu_in=silu_in),
        grid=(m // tm, n // tn),
        in_specs=[pl.BlockSpec((tm, k), lambda i, j: (i, 0)),
                  pl.BlockSpec((k, tn), lambda i, j: (0, j)),
                  pl.BlockSpec((1, tn), lambda i, j: (0, j))],
        out_specs=pl.BlockSpec((tm, tn), lambda i, j: (i, j)),
        out_shape=jax.ShapeDtypeStruct((m, n), F32),
        compiler_params=_cparams("parallel", "parallel"),
        name=name,
    )(a, w.astype(BF16), bias)


_PROJ_MAIN = Q_DIM + 6 * KV_DIM
_GATE_PAD = 128


def _proj_kernel(x_ref, g_ref, sc_ref, sh_ref, w_ref, wg_ref, *out_refs, attn_layouts, tm):
    hb = _normmod(x_ref[...], g_ref[...], sc_ref[...], sh_ref[...]).astype(BF16)
    q = jnp.dot(hb, w_ref[:, :Q_DIM], preferred_element_type=F32)
    gates = jnp.dot(hb, wg_ref[...], preferred_element_type=F32)
    kvs = []
    for n in range(6):
        lo = Q_DIM + n * KV_DIM
        kvs.append(jnp.dot(hb, w_ref[:, lo:lo + KV_DIM], preferred_element_type=F32))
    if not attn_layouts:
        for n in range(6):
            out_refs[n][...] = kvs[n]
        out_refs[6][...] = q
        out_refs[7][...] = gates
        return
    kt_refs = out_refs[2:8]
    qt_ref, gt_ref, ks_ref, vst_ref, kw_ref, vwt_ref, q_scr, gate_scr, kv_scr = out_refs[8:]
    out_refs[0][...] = kvs[0]
    out_refs[1][...] = kvs[1]
    kvts = []
    for n in range(6):
        kv_scr[n] = kvs[n]
        kvts.append(kv_scr[n].T)
        kt_refs[n][...] = kvts[n]
    n_chunks = tm // CHUNK
    q_scr[...] = q * Q_SCALE_LOG2
    gate_scr[...] = gates
    qt = q_scr[...].T
    gt = gate_scr[...].T
    for c in range(n_chunks):
        tok = slice(c * CHUNK, (c + 1) * CHUNK)
        for kv in range(N_KV):
            for gi in range(GROUP):
                h = kv * GROUP + gi
                lanes = slice(gi * CHUNK, (gi + 1) * CHUNK)
                qt_ref[c, kv, :, lanes] = qt[h * HEAD_DIM:(h + 1) * HEAD_DIM, tok].astype(BF16)
                gt_ref[c, kv, :, lanes] = gt[3 * h:3 * h + 3, tok]
    for k_ref, vt_ref, k, vt in ((ks_ref, vst_ref, kvs[2], kvts[3]), (kw_ref, vwt_ref, kvs[4], kvts[5])):
        for kv in range(N_KV):
            cols = slice(kv * HEAD_DIM, (kv + 1) * HEAD_DIM)
            k_ref[kv] = k[:, cols].astype(BF16)
            for c in range(n_chunks):
                vt_ref[kv, c, 0:HEAD_DIM, :] = vt[cols, c * CHUNK:(c + 1) * CHUNK].astype(BF16)
                vt_ref[kv, c, HEAD_DIM:, :] = jnp.ones((V_ROWS - HEAD_DIM, CHUNK), BF16)


def _nsa_project(x, g, sc, sh, w_in, rows_per_batch, tm, attn_layouts):
    m = x.shape[0]
    tm = min(tm, m)
    w_main = w_in[:, :_PROJ_MAIN].astype(BF16)
    w_gate = jnp.pad(w_in[:, _PROJ_MAIN:], ((0, 0), (0, _GATE_PAD - 3 * N_HEADS))).astype(BF16)
    row = lambda n: pl.BlockSpec((tm, n), lambda i: (i, 0))
    if attn_layouts:
        assert tm % CHUNK == 0 and rows_per_batch % tm == 0
        nck = tm // CHUNK
        tpb = rows_per_batch // tm
        out_specs = [row(KV_DIM)] * 2 + [pl.BlockSpec((None, KV_DIM, tm), lambda i: (i // tpb, 0, i % tpb))] * 6
        out_shape = ([jax.ShapeDtypeStruct((m, KV_DIM), F32)] * 2
                     + [jax.ShapeDtypeStruct((m // rows_per_batch, KV_DIM, rows_per_batch), F32)] * 6)
        chunked = lambda *s: pl.BlockSpec((nck,) + s, lambda i: (i,) + (0,) * len(s))
        k_spec = pl.BlockSpec((N_KV, tm, HEAD_DIM), lambda i: (0, i, 0))
        vt_spec = pl.BlockSpec((N_KV, nck, V_ROWS, CHUNK), lambda i: (0, i, 0, 0))
        k_shape = jax.ShapeDtypeStruct((N_KV, m, HEAD_DIM), BF16)
        vt_shape = jax.ShapeDtypeStruct((N_KV, m // CHUNK, V_ROWS, CHUNK), BF16)
        out_specs += [chunked(N_KV, HEAD_DIM, LANES), chunked(N_KV, 3, LANES), k_spec, vt_spec, k_spec, vt_spec]
        out_shape += [jax.ShapeDtypeStruct((m // CHUNK, N_KV, HEAD_DIM, LANES), BF16),
                      jax.ShapeDtypeStruct((m // CHUNK, N_KV, 3, LANES), F32),
                      k_shape, vt_shape, k_shape, vt_shape]
    else:
        out_specs = [row(KV_DIM)] * 6 + [row(Q_DIM), row(_GATE_PAD)]
        out_shape = ([jax.ShapeDtypeStruct((m, KV_DIM), F32)] * 6
                     + [jax.ShapeDtypeStruct((m, Q_DIM), F32), jax.ShapeDtypeStruct((m, _GATE_PAD), F32)])
    outs = pl.pallas_call(
        functools.partial(_proj_kernel, attn_layouts=attn_layouts, tm=tm),
        grid=(m // tm,),
        in_specs=[row(D_MODEL),
                  pl.BlockSpec((1, D_MODEL), lambda i: (0, 0)),
                  _mod_spec(sc, tm, rows_per_batch), _mod_spec(sh, tm, rows_per_batch),
                  pl.BlockSpec((D_MODEL, _PROJ_MAIN), lambda i: (0, 0)),
                  pl.BlockSpec((D_MODEL, _GATE_PAD), lambda i: (0, 0))],
        out_specs=out_specs,
        out_shape=out_shape,
        scratch_shapes=([pltpu.VMEM((tm, Q_DIM), F32), pltpu.VMEM((tm, _GATE_PAD), F32),
                         pltpu.VMEM((6, tm, KV_DIM), F32)] if attn_layouts else []),
        compiler_params=_cparams("parallel"),
        name="nsa_project",
    )(x, g.reshape(1, D_MODEL), sc, sh, w_main, w_gate)
    if attn_layouts:
        return outs[:2], outs[2:8], outs[8:]
    return outs[:6], outs[6:]


def _mm_res_kernel(a_ref, w_ref, x_ref, gate_ref, o_ref):
    y = jnp.dot(a_ref[...].astype(BF16), w_ref[...], preferred_element_type=F32)
    o_ref[...] = x_ref[...] + gate_ref[...] * y


def _matmul_residual(a, w, x, gate, rows_per_batch, tm):
    m, k = a.shape
    tm = min(tm, m)
    return pl.pallas_call(
        _mm_res_kernel,
        grid=(m // tm,),
        in_specs=[pl.BlockSpec((tm, k), lambda i: (i, 0)),
                  pl.BlockSpec((k, D_MODEL), lambda i: (0, 0)),
                  pl.BlockSpec((tm, D_MODEL), lambda i: (i, 0)),
                  _mod_spec(gate, tm, rows_per_batch)],
        out_specs=pl.BlockSpec((tm, D_MODEL), lambda i: (i, 0)),
        out_shape=jax.ShapeDtypeStruct((m, D_MODEL), F32),
        compiler_params=_cparams("parallel"),
        name="out_proj_residual",
    )(a, w.astype(BF16), x, gate)


def _mm_res_t_kernel(ot_ref, w_ref, x_ref, gate_ref, o_ref, *, n_chunks):
    chunks = []
    for c in range(n_chunks):
        rows = [ot_ref[c, kv, :, gi * CHUNK:(gi + 1) * CHUNK] for kv in range(N_KV) for gi in range(GROUP)]
        chunks.append(jnp.concatenate(rows, axis=0).astype(F32).T)
    a = jnp.concatenate(chunks, axis=0).astype(BF16)
    y = jnp.dot(a, w_ref[...], preferred_element_type=F32)
    o_ref[...] = x_ref[...] + gate_ref[...] * y


def _matmul_residual_t(o_t, w, x, gate, rows_per_batch, tm):
    m = x.shape[0]
    nck = tm // CHUNK
    return pl.pallas_call(
        functools.partial(_mm_res_t_kernel, n_chunks=nck),
        grid=(m // tm,),
        in_specs=[pl.BlockSpec((nck, N_KV, HEAD_DIM, LANES), lambda i: (i, 0, 0, 0)),
                  pl.BlockSpec((Q_DIM, D_MODEL), lambda i: (0, 0)),
                  pl.BlockSpec((tm, D_MODEL), lambda i: (i, 0)),
                  _mod_spec(gate, tm, rows_per_batch)],
        out_specs=pl.BlockSpec((tm, D_MODEL), lambda i: (i, 0)),
        out_shape=jax.ShapeDtypeStruct((m, D_MODEL), F32),
        compiler_params=_cparams("parallel"),
        name="out_proj_residual_t",
    )(o_t, w.astype(BF16), x, gate)


def _ffn_kernel(x_ref, g_ref, sc_ref, sh_ref, gate_ref, wg_ref, wu_ref, wd_ref, fg_ref, o_ref,
                *, final_norm, tf):
    x = x_ref[...]
    hb = _normmod(x, g_ref[...], sc_ref[...], sh_ref[...]).astype(BF16)
    acc = None
    for lo in range(0, wg_ref.shape[1], tf):
        a = jnp.dot(hb, wg_ref[:, lo:lo + tf], preferred_element_type=F32)
        u = jnp.dot(hb, wu_ref[:, lo:lo + tf], preferred_element_type=F32)
        act = (_silu(a) * u).astype(BF16)
        part = jnp.dot(act, wd_ref[lo:lo + tf, :], preferred_element_type=F32)
        acc = part if acc is None else acc + part
    y = x + gate_ref[...] * acc
    if final_norm:
        ms = jnp.mean(y * y, axis=-1, keepdims=True)
        y = y * lax.rsqrt(ms + RMS_EPS) * fg_ref[...]
    o_ref[...] = y


def _ffn(x, g, sc, sh, gate, wg, wu, wd, final_g, rows_per_batch, tm, tf, final_norm):
    m = x.shape[0]
    d_ff = wg.shape[1]
    tm = min(tm, m)
    assert d_ff % tf == 0
    once = pl.Buffered(1)
    vec = pl.BlockSpec((1, D_MODEL), lambda i: (0, 0))
    return pl.pallas_call(
        functools.partial(_ffn_kernel, final_norm=final_norm, tf=tf),
        grid=(m // tm,),
        in_specs=[pl.BlockSpec((tm, D_MODEL), lambda i: (i, 0)),
                  vec,
                  _mod_spec(sc, tm, rows_per_batch), _mod_spec(sh, tm, rows_per_batch),
                  _mod_spec(gate, tm, rows_per_batch),
                  pl.BlockSpec((D_MODEL, d_ff), lambda i: (0, 0), pipeline_mode=once),
                  pl.BlockSpec((D_MODEL, d_ff), lambda i: (0, 0), pipeline_mode=once),
                  pl.BlockSpec((d_ff, D_MODEL), lambda i: (0, 0), pipeline_mode=once),
                  vec],
        out_specs=pl.BlockSpec((tm, D_MODEL), lambda i: (i, 0)),
        out_shape=jax.ShapeDtypeStruct((m, D_MODEL), F32),
        compiler_params=_cparams("parallel"),
        name="ffn",
    )(x, g.reshape(1, D_MODEL), sc, sh, gate, wg.astype(BF16), wu.astype(BF16), wd.astype(BF16),
      final_g.reshape(1, D_MODEL))


_POOL_HALO = 16


def _pool_kernel(x_ref, xprev_ref, state_ref, g_ref, sc_ref, sh_ref, gate_ref, w_ref, ls_ref,
                 o_ref, st_ref, ext_ref, *, tm, pos0):
    i = pl.program_id(1)
    g, sc, sh = g_ref[...], sc_ref[...], sh_ref[...]
    u = _normmod(x_ref[...], g, sc, sh)
    prev = jnp.where(i == 0, state_ref[...], _normmod(xprev_ref[...], g, sc, sh))
    ext_ref[0:_POOL_HALO, :] = prev
    ext_ref[_POOL_HALO:_POOL_HALO + tm, :] = u
    st_ref[...] = ext_ref[tm:tm + _POOL_HALO, :]

    pos = pos0 + i * tm + lax.broadcasted_iota(jnp.int32, (tm, 1), 0)
    mixed = []
    for gi, w in enumerate(POOL_WINDOWS):
        lo = gi * POOL_GROUP_DIM
        s = u[:, lo:lo + POOL_GROUP_DIM]
        for k in range(1, w):
            s = s + ext_ref[_POOL_HALO - k:_POOL_HALO - k + tm, lo:lo + POOL_GROUP_DIM]
        cnt = jnp.minimum(pos + 1, w).astype(F32)
        pooled = s / cnt - u[:, lo:lo + POOL_GROUP_DIM]
        mixed.append(jnp.dot(pooled.astype(BF16), w_ref[gi], preferred_element_type=F32))
    y = jnp.concatenate(mixed, axis=-1) * ls_ref[...]
    o_ref[...] = x_ref[...] + gate_ref[...] * y


def _pool_mix(x3, state, g, sc, sh, gate, w_grp, layer_scale, pos0, tm):
    b, t, _ = x3.shape
    tm = min(tm, t)
    state16 = jnp.pad(state, ((0, 0), (_POOL_HALO - POOL_STATE, 0), (0, 0)))
    if t >= _POOL_HALO:
        xprev = x3
        nprev = tm // _POOL_HALO
        prev_spec = pl.BlockSpec((None, _POOL_HALO, D_MODEL),
                                 lambda bi, i: (bi, jnp.maximum(i * nprev - 1, 0), 0))
    else:
        xprev = state16
        prev_spec = pl.BlockSpec((None, _POOL_HALO, D_MODEL), lambda bi, i: (bi, 0, 0))
    vec = pl.BlockSpec((1, D_MODEL), lambda bi, i: (0, 0))
    bvec = pl.BlockSpec((None, 1, D_MODEL), lambda bi, i: (bi, 0, 0))
    out, st = pl.pallas_call(
        functools.partial(_pool_kernel, tm=tm, pos0=pos0),
        grid=(b, t // tm),
        in_specs=[pl.BlockSpec((None, tm, D_MODEL), lambda bi, i: (bi, i, 0)),
                  prev_spec,
                  pl.BlockSpec((None, _POOL_HALO, D_MODEL), lambda bi, i: (bi, 0, 0)),
                  vec, bvec, bvec, bvec,
                  pl.BlockSpec((len(POOL_WINDOWS), POOL_GROUP_DIM, POOL_GROUP_DIM),
                               lambda bi, i: (0, 0, 0)),
                  vec],
        out_specs=[pl.BlockSpec((None, tm, D_MODEL), lambda bi, i: (bi, i, 0)),
                   pl.BlockSpec((None, _POOL_HALO, D_MODEL), lambda bi, i: (bi, 0, 0))],
        out_shape=[jax.ShapeDtypeStruct((b, t, D_MODEL), F32),
                   jax.ShapeDtypeStruct((b, _POOL_HALO, D_MODEL), F32)],
        scratch_shapes=[pltpu.VMEM((tm + _POOL_HALO, D_MODEL), F32)],
        compiler_params=_cparams("parallel", "arbitrary"),
        name="pool_mix",
    )(x3, xprev, state16, g.reshape(1, D_MODEL), sc, sh, gate, w_grp.astype(BF16),
      layer_scale.reshape(1, D_MODEL))
    return out, st[:, _POOL_HALO - POOL_STATE:]


def _rel_bucket(dist):
    d = jnp.maximum(dist, 0)
    max_exact = N_BUCKETS // 2
    large = max_exact + (jnp.log(jnp.maximum(d, 1).astype(F32) / max_exact)
                         / math.log(MAX_DISTANCE / max_exact) * (N_BUCKETS - max_exact)).astype(jnp.int32)
    large = jnp.minimum(large, N_BUCKETS - 1)
    return jnp.where(d < max_exact, d, large)


def _dist_bias(rel_bias):
    return rel_bias[_rel_bucket(jnp.arange(MAX_DISTANCE + 1, dtype=jnp.int32))]


def _lookup(table, idx):
    onehot = (idx[..., None] == jnp.arange(table.shape[0], dtype=jnp.int32)).astype(F32)
    return jnp.einsum("...d,dh->...h", onehot, table, precision=lax.Precision.HIGHEST)


def _bias_tile(fd, dist, valid):
    k, q = dist.shape
    bias = _lookup(fd, jnp.clip(dist, 0, MAX_DISTANCE)).reshape(k, q, N_KV, GROUP)
    bias = jnp.where(valid[:, :, None, None], bias, NEG)
    return bias.transpose(2, 0, 3, 1).reshape(N_KV, k, GROUP * q)


def _prompt_tables(rel_bias, t):
    fd = _dist_bias(rel_bias) * LOG2E
    kj = jnp.arange(KEY_TILE, dtype=jnp.int32)[:, None]
    qi = jnp.arange(CHUNK, dtype=jnp.int32)[None, :]
    dist = qi - kj
    near = jnp.stack([_bias_tile(fd, dist + KEY_TILE, dist + KEY_TILE >= 0),
                      _bias_tile(fd, dist, dist >= 0)])
    wfirst = _bias_tile(fd, dist + WINDOW, dist < 0)
    far = _bias_tile(fd, jnp.full((1, CHUNK), MAX_DISTANCE, jnp.int32), jnp.ones((1, CHUNK), bool))
    per_chunk = CHUNK // CMP_BLK
    assert CHUNK == 128 and CMP_BLK == 32 and MAX_DISTANCE == 128
    rel0, n_rel = -per_chunk, 2 * per_chunk
    rel = rel0 + jnp.arange(n_rel, dtype=jnp.int32)[:, None]
    dist = qi - CMP_BLK * rel - (CMP_BLK - 1)
    crel = _bias_tile(fd, dist, dist >= 0)
    relm = (jnp.arange(t // CMP_BLK, dtype=jnp.int32)[None, :]
            - per_chunk * jnp.arange(t // CHUNK, dtype=jnp.int32)[:, None])[:, None, :, None]
    cmp_tab = jnp.where(relm >= rel0 + n_rel, NEG, far[None])
    for r in range(n_rel):
        cmp_tab = jnp.where(relm == rel0 + r, crel[None, :, r:r + 1, :], cmp_tab)
    return near, wfirst, far, cmp_tab


V_ROWS = HEAD_DIM + 16


def _half_max(s, top, bot):
    return jnp.maximum(jnp.max(s[:SEL_BLK], axis=0, keepdims=True) + top,
                       jnp.max(s[SEL_BLK:], axis=0, keepdims=True) + bot)


def _half_exp2(s, m, top, bot):
    return jnp.exp2(jnp.concatenate([s[:SEL_BLK] - (m - top), s[SEL_BLK:] - (m - bot)], axis=0))


def _tile_softmax(tiles, qs):
    scores, maxes = [], []
    for k, _, bias in tiles:
        s = jnp.dot(k, qs, preferred_element_type=F32)
        if isinstance(bias, tuple):
            maxes.append(_half_max(s, *bias))
        else:
            s = s + bias
            maxes.append(jnp.max(s, axis=0, keepdims=True))
        scores.append(s)
    m = functools.reduce(jnp.maximum, maxes)
    acc = 0.0
    for s, (_, vt, bias) in zip(scores, tiles):
        p = _half_exp2(s, m, *bias) if isinstance(bias, tuple) else jnp.exp2(s - m)
        acc = acc + jnp.dot(vt, p.astype(BF16), preferred_element_type=F32)
    return m, acc


def _normalise(acc):
    return acc[:HEAD_DIM] / jnp.maximum(acc[HEAD_DIM:HEAD_DIM + 1], 1e-30)


def _prompt_attn_kernel(q_ref, g_ref, kc_ref, vct_ref, cb_ref, ks_ref, vst_ref, kw_ref, vwt_ref,
                        near_ref, wfirst_ref, far_ref, o_ref,
                        imp_ref, val_ref, rank_ref, selrep_ref, m_ref, alpha_ref, acc_ref, mix_ref,
                        s_ref, p_ref, *, n_blk, n_sel):
    i = pl.program_id(1)

    def k_rows(ref, kv, jt):
        return ref[kv, pl.ds(pl.multiple_of(jt * KEY_TILE, KEY_TILE), KEY_TILE), :]

    def sel_rows(kv, jt):
        return selrep_ref[kv, pl.ds(2 * jt, 1), :], selrep_ref[kv, pl.ds(2 * jt + 1, 1), :]

    def halves(top, bot):
        return jnp.concatenate([jnp.broadcast_to(top, (SEL_BLK, LANES)),
                                jnp.broadcast_to(bot, (SEL_BLK, LANES))], axis=0)

    for kv in range(N_KV):
        bias = cb_ref[kv]
        s = jnp.dot(kc_ref[kv], q_ref[kv], preferred_element_type=F32) + bias
        valid = bias > 0.5 * NEG
        m = jnp.max(s, axis=0, keepdims=True)
        p = jnp.where(valid, jnp.exp2(s - m), 0.0)
        p = p / jnp.maximum(jnp.sum(p, axis=0, keepdims=True), 1e-30)
        o_c = jnp.dot(vct_ref[kv], p.astype(BF16), preferred_element_type=F32)
        mix_ref[kv] = _sigmoid(g_ref[kv, 0:1, :]) * o_c
        imp_ref[kv] = functools.reduce(
            lambda a, b: a + b, [p[:, gi * CHUNK:(gi + 1) * CHUNK] for gi in range(GROUP)])

    imp = jnp.concatenate(
        [imp_ref[kv, pl.ds(0, n_blk, stride=2), :] + imp_ref[kv, pl.ds(1, n_blk, stride=2), :]
         for kv in range(N_KV)], axis=1)
    blk = lax.broadcasted_iota(jnp.int32, (n_blk, LANES), 0)
    lane = lax.broadcasted_iota(jnp.int32, (1, LANES), 1)
    assert CHUNK == 2 * SEL_BLK
    qblk = 2 * i + jnp.where((lane & (CHUNK - 1)) >= SEL_BLK, 1, 0)
    forced = (blk == 0) | (blk == qblk) | (blk == qblk - 1)
    val_ref[...] = jnp.where(forced, BIG, jnp.where(blk <= qblk, imp, -1.0))
    rank_ref[...] = jnp.zeros_like(rank_ref)
    sub = lax.broadcasted_iota(jnp.int32, (8, LANES), 0)
    n_grp = n_blk // 8
    for grp in range(n_grp):
        @pl.when(8 * grp <= 2 * i + 1)
        def _(grp=grp):
            for r8 in range(n_grp):
                piece = val_ref[r8 * 8:(r8 + 1) * 8, :]
                acc = rank_ref[r8 * 8:(r8 + 1) * 8, :]
                for j in range(grp * 8, grp * 8 + 8):
                    row = val_ref[j:j + 1, :]
                    if r8 > grp:
                        ahead = row >= piece
                    elif r8 < grp:
                        ahead = row > piece
                    else:
                        acc = acc + jnp.where(sub > j - grp * 8, jnp.where(row >= piece, 1.0, 0.0),
                                              jnp.where(row > piece, 1.0, 0.0))
                        continue
                    acc = acc + jnp.where(ahead, 1.0, 0.0)
                rank_ref[r8 * 8:(r8 + 1) * 8, :] = acc
    selneg = jnp.where(rank_ref[...] < n_sel, 0.0, NEG)
    for kv in range(N_KV):
        selrep_ref[kv] = jnp.concatenate([selneg[:, kv * CHUNK:(kv + 1) * CHUNK]] * GROUP, axis=1)

    prev = jnp.maximum(i - 1, 0)
    prev_off = jnp.where(i >= 1, 0.0, NEG)
    for kv in range(N_KV):
        qs = q_ref[kv]
        far = far_ref[kv]
        tiles = [(k_rows(kw_ref, kv, i), vwt_ref[kv, i], near_ref[1, kv]),
                 (k_rows(kw_ref, kv, prev), vwt_ref[kv, prev], near_ref[0, kv] + prev_off)]
        for back in (2, 3):
            jt = jnp.maximum(i - back, 0)
            row = far + jnp.where(i >= back, 0.0, NEG)
            tiles.append((k_rows(kw_ref, kv, jt), vwt_ref[kv, jt], (row, row)))
        jt = jnp.maximum(i - 4, 0)
        tiles.append((k_rows(kw_ref, kv, jt), vwt_ref[kv, jt], wfirst_ref[kv] + jnp.where(i >= 4, 0.0, NEG)))
        _, acc = _tile_softmax(tiles, qs)
        mix_ref[kv] += _sigmoid(g_ref[kv, 2:3, :]) * _normalise(acc)

        tiles = [(k_rows(ks_ref, kv, i), vst_ref[kv, i], near_ref[1, kv] + halves(*sel_rows(kv, i))),
                 (k_rows(ks_ref, kv, prev), vst_ref[kv, prev],
                  near_ref[0, kv] + halves(*sel_rows(kv, prev)) + prev_off)]
        m_ref[kv], acc_ref[kv] = _tile_softmax(tiles, qs)

    n_far = jnp.maximum(i - 1, 0)
    last = jnp.maximum(n_far - 1, 0)

    def scores(jt, slot):
        jc = jnp.minimum(jt, last)
        for kv in range(N_KV):
            s_ref[slot, kv] = jnp.dot(k_rows(ks_ref, kv, jc), q_ref[kv], preferred_element_type=F32)

    def softmax(jt, slot):
        jc = jnp.minimum(jt, last)
        off = jnp.where(jt < n_far, 0.0, NEG)
        for kv in range(N_KV):
            top, bot = sel_rows(kv, jc)
            far = far_ref[kv] + off
            top, bot = top + far, bot + far
            s = s_ref[slot, kv]
            m_old = m_ref[kv]
            m_new = jnp.maximum(m_old, _half_max(s, top, bot))
            m_ref[kv] = m_new
            alpha_ref[slot, kv] = jnp.exp2(m_old - m_new)
            p_ref[slot, kv] = _half_exp2(s, m_new, top, bot).astype(BF16)

    def values(jt, slot):
        jc = jnp.clip(jt, 0, last)
        for kv in range(N_KV):
            acc_ref[kv] = alpha_ref[slot, kv] * acc_ref[kv] + jnp.dot(vst_ref[kv, jc], p_ref[slot, kv],
                                                                      preferred_element_type=F32)

    p_ref[1] = jnp.zeros(p_ref.shape[1:], BF16)
    alpha_ref[1] = jnp.ones(alpha_ref.shape[1:], F32)
    scores(0, 0)

    def far_body(j2, carry):
        jt = 2 * j2
        values(jt - 1, 1)
        softmax(jt, 0)
        scores(jt + 1, 1)
        values(jt, 0)
        softmax(jt + 1, 1)
        scores(jt + 2, 0)
        return carry

    n_pairs = (n_far + 1) // 2
    lax.fori_loop(0, n_pairs, far_body, 0)
    values(2 * n_pairs - 1, 1)

    for kv in range(N_KV):
        o_ref[kv] = (mix_ref[kv] + _sigmoid(g_ref[kv, 1:2, :]) * _normalise(acc_ref[kv])).astype(o_ref.dtype)


def _prompt_attention(layouts, k_cmp, v_cmp, rel_bias, b, t):
    q_t, g_t, ks, vst, kw, vwt = layouts
    nch, nt, nc, n_blk = t // CHUNK, t // KEY_TILE, t // CMP_BLK, t // SEL_BLK
    kc = k_cmp.reshape(b, nc, N_KV, HEAD_DIM).transpose(0, 2, 1, 3).astype(BF16)
    vct = v_cmp.reshape(b, nc, N_KV, HEAD_DIM).transpose(0, 2, 3, 1).astype(BF16)
    near, wfirst, far, cmp_tab = _prompt_tables(rel_bias, t)

    once = pl.Buffered(1)
    per_b = lambda *shape: pl.BlockSpec((None,) + shape, lambda bi, i: (bi,) + (0,) * len(shape),
                                        pipeline_mode=once)
    const = lambda *shape: pl.BlockSpec(shape, lambda bi, i: (0,) * len(shape), pipeline_mode=once)
    chunk = lambda *shape: pl.BlockSpec((None,) + shape, lambda bi, i: (bi * nch + i,) + (0,) * len(shape))
    k_spec = pl.BlockSpec((N_KV, t, HEAD_DIM), lambda bi, i: (0, bi, 0), pipeline_mode=once)
    vt_spec = pl.BlockSpec((N_KV, nt, V_ROWS, KEY_TILE), lambda bi, i: (0, bi, 0, 0), pipeline_mode=once)
    stat = pltpu.VMEM((N_KV, 1, LANES), F32)
    return pl.pallas_call(
        functools.partial(_prompt_attn_kernel, n_blk=n_blk, n_sel=min(N_SEL, n_blk)),
        grid=(b, nch),
        in_specs=[chunk(N_KV, HEAD_DIM, LANES),
                  chunk(N_KV, 3, LANES),
                  per_b(N_KV, nc, HEAD_DIM),
                  per_b(N_KV, HEAD_DIM, nc),
                  pl.BlockSpec((None, N_KV, nc, LANES), lambda bi, i: (i, 0, 0, 0)),
                  k_spec, vt_spec, k_spec, vt_spec,
                  const(2, N_KV, KEY_TILE, LANES),
                  const(N_KV, KEY_TILE, LANES),
                  const(N_KV, 1, LANES)],
        out_specs=chunk(N_KV, HEAD_DIM, LANES),
        out_shape=jax.ShapeDtypeStruct((b * nch, N_KV, HEAD_DIM, LANES), BF16),
        scratch_shapes=[pltpu.VMEM((N_KV, nc, CHUNK), F32),
                        pltpu.VMEM((n_blk, LANES), F32),
                        pltpu.VMEM((n_blk, LANES), F32),
                        pltpu.VMEM((N_KV, n_blk, LANES), F32),
                        stat, pltpu.VMEM((2, N_KV, 1, LANES), F32),
                        pltpu.VMEM((N_KV, V_ROWS, LANES), F32),
                        pltpu.VMEM((N_KV, HEAD_DIM, LANES), F32),
                        pltpu.VMEM((2, N_KV, KEY_TILE, LANES), F32),
                        pltpu.VMEM((2, N_KV, KEY_TILE, LANES), BF16)],
        compiler_params=_cparams("parallel", "arbitrary"),
        name="nsa_prompt_attention",
    )(q_t, g_t, kc, vct, cmp_tab, ks, vst, kw, vwt, near, wfirst, far)


NEW_PAD = 128


def _sample_tables(rel_bias, past, t, wbuf, n_rows):
    lane = jnp.arange(SAMPLE_LANES, dtype=jnp.int32)
    g, kvh, qi = lane // (N_KV * t), (lane // t) % N_KV, lane % t
    used = lane < GROUP * N_KV * t
    head = jnp.where(used, kvh * GROUP + g, 0)
    qpos = past + qi
    fd_lane = jnp.take(_dist_bias(rel_bias), head, axis=1)
    dists = jnp.arange(MAX_DISTANCE + 1, dtype=jnp.int32)

    def tab(kpos, valid, n_near):
        dist = qpos[None, :] - kpos[:, None]
        n_far = kpos.shape[0] - n_near
        near_d = jnp.clip(dist[n_far:], 0, MAX_DISTANCE)
        near = jnp.sum(jnp.where(near_d[:, :, None] == dists, fd_lane.T[None], 0.0), axis=-1)
        far = jnp.broadcast_to(fd_lane[MAX_DISTANCE][None, :], (n_far, SAMPLE_LANES))
        bias = jnp.where(used[None, :], jnp.concatenate([far, near], axis=0), 0.0)
        return jnp.where(valid(dist) & (kpos[:, None] >= 0), bias, NEG)

    new_pos = past + jnp.where(jnp.arange(NEW_PAD) < t, jnp.arange(NEW_PAD, dtype=jnp.int32), 1 << 20)
    causal = lambda d: d >= 0
    window = lambda d: (d >= 0) & (d < WINDOW)
    nc = past // CMP_BLK
    assert wbuf >= MAX_DISTANCE and past >= MAX_DISTANCE
    t_cmp = tab(jnp.arange(nc, dtype=jnp.int32) * CMP_BLK + CMP_BLK - 1, causal, MAX_DISTANCE // CMP_BLK)
    rows = lambda a: a[:, :n_rows].T
    t_sel = rows(tab(jnp.concatenate([jnp.arange(past, dtype=jnp.int32), new_pos]), causal,
                     MAX_DISTANCE + NEW_PAD))
    t_win = rows(tab(jnp.concatenate([past - wbuf + jnp.arange(wbuf, dtype=jnp.int32), new_pos]), window,
                     MAX_DISTANCE + NEW_PAD))
    key = jnp.arange(past + NEW_PAD, dtype=jnp.int32)
    blk = jnp.where(key < past, key // SEL_BLK, past // SEL_BLK)
    expand = (blk[None, :] == jnp.arange(128, dtype=jnp.int32)[:, None]).astype(BF16)
    return t_cmp, t_sel, t_win, expand


def _sample_attn_kernel(pt_ref, *refs, n_pages, n_sel, t, wbuf):
    del pt_ref
    it = iter(refs)
    qbd_ref, qrow_ref, g_ref = next(it), next(it), next(it)
    kc_ref, vc_ref = next(it), next(it)
    ks_pages = [next(it) for _ in range(n_pages)]
    vs_pages = [next(it) for _ in range(n_pages)]
    ksn_ref, vsn_ref, kwin_ref, vwin_ref, kwn_ref, vwn_ref = (next(it) for _ in range(6))
    tcmp_ref, tsel_ref, twin_ref, expand_ref = (next(it) for _ in range(4))
    o_ref = next(it)
    imp_ref, val_ref, sel_ref, s_ref = (next(it) for _ in range(4))

    scale = HEAD_DIM ** -0.5
    nq = N_KV * t
    qs = qbd_ref[...] * scale
    qr = qrow_ref[...] * scale
    n_q = qr.shape[0]
    lane = lax.broadcasted_iota(jnp.int32, (1, SAMPLE_LANES), 1)
    blocks_per_page = PAGE_SIZE // CMP_BLK
    past = n_pages * PAGE_SIZE
    nt_dims = (((1,), (1,)), ((), ()))

    s = jnp.dot(kc_ref[...].astype(BF16), qs, preferred_element_type=F32) + tcmp_ref[...]
    m = jnp.max(s, axis=0, keepdims=True)
    p_c = jnp.exp(s - m)
    p_c = p_c / jnp.maximum(jnp.sum(p_c, axis=0, keepdims=True), 1e-30)
    o_c = lax.dot_general(p_c.astype(BF16), vc_ref[...].astype(BF16), (((0,), (0,)), ((), ())),
                          preferred_element_type=F32)[:n_q]

    imp = p_c
    for g in range(1, GROUP):
        imp = imp + pltpu.roll(p_c, SAMPLE_LANES - g * nq, 1)
    imp_ref[...] = imp
    n_pairs = n_pages * blocks_per_page // 2
    n_blk = n_pairs + 1
    n_rows = val_ref.shape[0]
    val_ref[...] = jnp.full((n_rows, SAMPLE_LANES), -2.0, F32)
    val_ref[0:n_pairs, :] = imp_ref[pl.ds(0, n_pairs, stride=2), :] + imp_ref[pl.ds(1, n_pairs, stride=2), :]
    blk = lax.broadcasted_iota(jnp.int32, (n_rows, SAMPLE_LANES), 0)
    qblk = n_blk - 1
    forced = (blk == 0) | (blk == qblk) | (blk == qblk - 1)
    val = jnp.where(forced, BIG, val_ref[...])
    val = jnp.where(blk < n_blk, val, -2.0)
    val_ref[...] = val
    rank = jnp.zeros((n_rows, SAMPLE_LANES), F32)
    for j in range(n_blk):
        row = val_ref[j:j + 1, :]
        rank = rank + jnp.where(blk > j, jnp.where(row >= val, 1.0, 0.0), jnp.where(row > val, 1.0, 0.0))
    selneg = jnp.where((rank < n_sel) & (lane < nq), 0.0, jnp.where(lane < nq, NEG, 0.0))
    selrep = selneg
    for g in range(1, GROUP):
        selrep = selrep + pltpu.roll(selneg, g * nq, 1)
    sel_ref[...] = jnp.zeros_like(sel_ref)
    sel_ref[0:n_rows, :] = jnp.where(selrep == 0.0, 1.0, 0.0)
    sel01 = sel_ref[...].T[:n_q].astype(BF16)

    def paged(page_ref):
        return page_ref[...].reshape(KV_DIM, PAGE_SIZE).astype(BF16)

    for p in range(n_pages + 1):
        cols = slice(p * KEY_TILE, (p + 1) * KEY_TILE)
        if p < n_pages:
            s = jnp.dot(qr, paged(ks_pages[p]), preferred_element_type=F32)
        else:
            s = lax.dot_general(qr, ksn_ref[...], nt_dims, preferred_element_type=F32)
        picked = jnp.dot(sel01, expand_ref[:, cols], preferred_element_type=F32)
        s_ref[:, cols] = s + tsel_ref[:, cols] + (picked - 1.0) * (-NEG)
    s = s_ref[...]
    p_s = jnp.exp(s - jnp.max(s, axis=1, keepdims=True))
    l = jnp.sum(p_s, axis=1, keepdims=True)
    p_s = p_s.astype(BF16)
    acc = jnp.dot(p_s[:, past:], vsn_ref[...], preferred_element_type=F32)
    for p in range(n_pages):
        acc = acc + lax.dot_general(p_s[:, p * KEY_TILE:(p + 1) * KEY_TILE], paged(vs_pages[p]), nt_dims,
                                    preferred_element_type=F32)
    o_s = acc / jnp.maximum(l, 1e-30)

    kwin = kwin_ref[...].reshape(KV_DIM, wbuf).astype(BF16)
    vwin = vwin_ref[...].reshape(KV_DIM, wbuf).astype(BF16)
    s_w = jnp.dot(qr, kwin, preferred_element_type=F32) + twin_ref[:, :wbuf]
    s_n = lax.dot_general(qr, kwn_ref[...], nt_dims, preferred_element_type=F32) + twin_ref[:, wbuf:]
    m = jnp.maximum(jnp.max(s_w, axis=1, keepdims=True), jnp.max(s_n, axis=1, keepdims=True))
    p_w, p_n = jnp.exp(s_w - m), jnp.exp(s_n - m)
    l = jnp.sum(p_w, axis=1, keepdims=True) + jnp.sum(p_n, axis=1, keepdims=True)
    acc = (lax.dot_general(p_w.astype(BF16), vwin, nt_dims, preferred_element_type=F32)
           + jnp.dot(p_n.astype(BF16), vwn_ref[...], preferred_element_type=F32))
    o_w = acc / jnp.maximum(l, 1e-30)

    gate = _sigmoid(g_ref[...])
    o_ref[...] = gate[:, 0:1] * o_c + gate[:, 1:2] * o_s + gate[:, 2:3] * o_w


def _sample_attention(q, gates, k_cmp, v_cmp, cache_ks, cache_vs, page_table,
                      ksl, vsl, win_k, win_v, kw, vw, rel_bias, b, t):
    n_pages = page_table.shape[1]
    past = n_pages * PAGE_SIZE
    wbuf = win_k.shape[-1]
    nq = N_KV * t
    used = GROUP * nq
    assert used % 8 == 0 and used <= SAMPLE_LANES and t <= NEW_PAD
    blocks_per_page = PAGE_SIZE // CMP_BLK
    n_blk = past // SEL_BLK + 1
    n_rows = -(-n_blk // 8) * 8

    q5 = q.reshape(b, t, N_KV, GROUP, HEAD_DIM)
    eye = jnp.eye(N_KV, dtype=q.dtype)
    qbd = jnp.einsum("btkgd,kc->bkdgct", q5, eye).reshape(b, KV_DIM, used)
    qbd = jnp.pad(qbd, ((0, 0), (0, 0), (0, SAMPLE_LANES - used))).astype(BF16)
    qrow = jnp.einsum("btkgd,kc->bgktcd", q5, eye).reshape(b, used, KV_DIM).astype(BF16)
    g_rows = gates[:, :3 * N_HEADS].reshape(b, t, N_KV, GROUP, 3).transpose(0, 3, 2, 1, 4).reshape(b, used, 3)
    g_rows = jnp.pad(g_rows, ((0, 0), (0, 0), (0, 128 - 3)))
    pad_rows = lambda a: jnp.pad(a.reshape(b, t, KV_DIM), ((0, 0), (0, NEW_PAD - t), (0, 0))).astype(BF16)
    tables = _sample_tables(rel_bias, past, t, wbuf, used)

    per_b = lambda *shape: pl.BlockSpec((None,) + shape, lambda bi, pt: (bi,) + (0,) * len(shape))
    const = lambda *shape: pl.BlockSpec(shape, lambda bi, pt: (0,) * len(shape))

    def page(p):
        return pl.BlockSpec((None, N_KV, HEAD_DIM, PAGE_SIZE), lambda bi, pt, p=p: (pt[bi, p], 0, 0, 0))

    n_cmp = n_pages * blocks_per_page
    in_specs = ([per_b(KV_DIM, SAMPLE_LANES), per_b(used, KV_DIM), per_b(used, 128),
                 per_b(n_cmp, KV_DIM), per_b(n_cmp, KV_DIM)]
                + [page(p) for p in range(n_pages)] * 2
                + [per_b(NEW_PAD, KV_DIM), per_b(NEW_PAD, KV_DIM),
                   per_b(N_KV, HEAD_DIM, wbuf), per_b(N_KV, HEAD_DIM, wbuf),
                   per_b(NEW_PAD, KV_DIM), per_b(NEW_PAD, KV_DIM)]
                + [const(*tb.shape) for tb in tables])
    out = pl.pallas_call(
        functools.partial(_sample_attn_kernel, n_pages=n_pages, n_sel=min(N_SEL, n_blk), t=t, wbuf=wbuf),
        grid_spec=pltpu.PrefetchScalarGridSpec(
            num_scalar_prefetch=1,
            grid=(b,),
            in_specs=in_specs,
            out_specs=per_b(used, KV_DIM),
            scratch_shapes=[pltpu.VMEM((n_cmp, SAMPLE_LANES), F32),
                            pltpu.VMEM((n_rows, SAMPLE_LANES), F32),
                            pltpu.VMEM((128, SAMPLE_LANES), F32),
                            pltpu.VMEM((used, past + NEW_PAD), F32)]),
        out_shape=jax.ShapeDtypeStruct((b, used, KV_DIM), F32),
        compiler_params=_cparams("arbitrary"),
        name="nsa_sample_attention",
    )(page_table, qbd, qrow, g_rows, k_cmp.reshape(b, n_cmp, KV_DIM), v_cmp.reshape(b, n_cmp, KV_DIM),
      *([cache_ks] * n_pages),
      *([cache_vs] * n_pages), pad_rows(ksl), pad_rows(vsl), win_k, win_v, pad_rows(kw), pad_rows(vw), *tables)
    o = out.reshape(b, GROUP, N_KV, t, N_KV, HEAD_DIM)
    o = jnp.einsum("bgktkd->btkgd", o)
    return o.reshape(b * t, Q_DIM)


def _compress_weights(w_c, pe_c):
    eye = jnp.eye(N_KV, dtype=w_c.dtype)
    w_big = jnp.einsum("jde,kc->jkdce", w_c, eye).reshape(CMP_BLK * KV_DIM, KV_DIM)
    pe_flat = jnp.broadcast_to(pe_c[:, None, :], (CMP_BLK, N_KV, HEAD_DIM)).reshape(1, CMP_BLK * KV_DIM)
    pe_rows = jnp.pad(pe_flat, ((0, 7), (0, 0)))
    bias = _matmul(pe_rows, w_big, name="compress_pe")[0:1]
    return w_big, bias


def _compress(x_rows, w_big, bias, tm):
    return _matmul(x_rows, w_big, bias, tm=tm, name="compress")


_BLOCK_PITCH = CMP_BLK + 4


def _compress_pages_kernel(pt_ref, *refs, n_pages):
    del pt_ref
    page_refs, (w_ref, b_ref, o_ref, rows_ref) = refs[:n_pages], refs[n_pages:]
    per_page = PAGE_SIZE // CMP_BLK
    for p in range(n_pages):
        xt = page_refs[p][...].reshape(KV_DIM, PAGE_SIZE).T
        for n in range(per_page):
            lo = (p * per_page + n) * _BLOCK_PITCH
            for h in range(2):
                rows_ref[h, lo:lo + CMP_BLK, :] = xt[n * CMP_BLK:(n + 1) * CMP_BLK, h * 128:(h + 1) * 128]
    n_out = n_pages * per_page
    acc = jnp.broadcast_to(b_ref[...], (n_out, KV_DIM))
    for j in range(CMP_BLK):
        for h in range(2):
            piece = rows_ref[h, pl.ds(j, n_out, stride=_BLOCK_PITCH), :].astype(BF16)
            lo = j * KV_DIM + h * 128
            acc = acc + jnp.dot(piece, w_ref[lo:lo + 128, :], preferred_element_type=F32)
    o_ref[...] = acc


def _compress_pages(cache_t, page_table, w_big, bias):
    assert KV_DIM == 256
    b, per_b = page_table.shape
    group = max(1, 32 // per_b)
    assert b % group == 0
    n_pages = group * per_b
    n_out = n_pages * (PAGE_SIZE // CMP_BLK)

    def page(p):
        return pl.BlockSpec((None, N_KV, HEAD_DIM, PAGE_SIZE),
                            lambda i, pt, p=p: (pt[i * group + p // per_b, p % per_b], 0, 0, 0))

    return pl.pallas_call(
        functools.partial(_compress_pages_kernel, n_pages=n_pages),
        grid_spec=pltpu.PrefetchScalarGridSpec(
            num_scalar_prefetch=1,
            grid=(b // group,),
            in_specs=[page(p) for p in range(n_pages)]
            + [pl.BlockSpec((CMP_BLK * KV_DIM, KV_DIM), lambda i, pt: (0, 0), pipeline_mode=pl.Buffered(1)),
               pl.BlockSpec((1, KV_DIM), lambda i, pt: (0, 0))],
            out_specs=pl.BlockSpec((n_out, KV_DIM), lambda i, pt: (i, 0)),
            scratch_shapes=[pltpu.VMEM((2, n_out * _BLOCK_PITCH, 128), F32)]),
        out_shape=jax.ShapeDtypeStruct((b * per_b * (PAGE_SIZE // CMP_BLK), KV_DIM), F32),
        compiler_params=_cparams("arbitrary"),
        name="compress_pages",
    )(page_table, *([cache_t] * n_pages), w_big.astype(BF16), bias)


def _ada(c_all, ada_w, ada_b):
    mods = []
    for i in range(ada_w.shape[0]):
        mods.append(_matmul(c_all, ada_w[i], ada_b[i].reshape(1, -1), silu_in=True, tn=2 * D_MODEL,
                            name="ada_modulate"))
    return mods


def kernel(x_prompt, x_sample, c_prompt, c_sample, cache_k_cmp, cache_v_cmp, cache_k_sel, cache_v_sel,
           page_table, state_k_win, state_v_win, state_pool, rel_bias, ada_w, ada_b, norm_g, final_g,
           nsa_w_in, nsa_w_out, cmp_wk, cmp_wv, cmp_pe_k, cmp_pe_v, pool_w, pool_scale,
           ffn_wg, ffn_wu, ffn_wd):
    bp, tp, _ = x_prompt.shape
    bs, ts, _ = x_sample.shape
    n_phys = cache_k_cmp.shape[1]
    past = page_table.shape[1] * PAGE_SIZE
    depth = ada_w.shape[0]
    assert depth == 2 and tp % KEY_TILE == 0 and ts <= 8 and past % PAGE_SIZE == 0

    n_c = bp + bs
    c_all = jnp.pad(jnp.concatenate([c_prompt, c_sample], axis=0), ((0, -n_c % 8), (0, 0)))
    mods = _ada(c_all, ada_w, ada_b)

    def mod_prompt(i):
        return [v[:bp].reshape(bp, 1, D_MODEL) for v in jnp.split(mods[i], 6, axis=-1)]

    def mod_sample(i):
        return [jnp.repeat(v[bp:n_c], ts, axis=0) for v in jnp.split(mods[i], 6, axis=-1)]

    wk_big, k_bias = _compress_weights(cmp_wk[0], cmp_pe_k[0])
    wv_big, v_bias = _compress_weights(cmp_wv[0], cmp_pe_v[0])

    mp = bp * tp
    x = x_prompt.reshape(mp, D_MODEL)
    sh1, sc1, g1, sh2, sc2, g2 = mod_prompt(0)
    (kc, vc), kv_t, layouts = _nsa_project(x, norm_g[0, 0], sc1, sh1, nsa_w_in[0], tp, 512, True)
    blk_rows = lambda a: a.reshape(mp // CMP_BLK, CMP_BLK * KV_DIM)
    k_cmp = _compress(blk_rows(kc), wk_big, k_bias, 256)
    v_cmp = _compress(blk_rows(vc), wv_big, v_bias, 256)
    o_t = _prompt_attention(layouts, k_cmp, v_cmp, rel_bias, bp, tp)
    x = _matmul_residual_t(o_t, nsa_w_out[0], x, g1, tp, 512)
    x = _ffn(x, norm_g[0, 1], sc2, sh2, g2, ffn_wg[0], ffn_wu[0], ffn_wd[0], final_g, tp, 512, 1408, False)
    sh1, sc1, g1, sh2, sc2, g2 = mod_prompt(1)
    x3, pool_p = _pool_mix(x.reshape(bp, tp, D_MODEL), jnp.zeros((bp, POOL_STATE, D_MODEL), F32),
                           norm_g[1, 0], sc1, sh1, g1, pool_w[0], pool_scale[0], 0, 512)
    y_prompt = _ffn(x3.reshape(mp, D_MODEL), norm_g[1, 1], sc2, sh2, g2, ffn_wg[1], ffn_wu[1], ffn_wd[1],
                    final_g, tp, 512, 1408, True).reshape(bp, tp, D_MODEL)
    win = min(WINDOW, tp)
    st5 = lambda a, t0: jnp.transpose(a[:, :, t0:].reshape(bp, N_KV, HEAD_DIM, tp - t0), (0, 3, 1, 2))[None]
    prompt_states = tuple(st5(a, 0) for a in kv_t[:4]) + tuple(st5(a, tp - win) for a in kv_t[4:]) + (pool_p[None],)

    ms = bs * ts
    x = x_sample.reshape(ms, D_MODEL)
    sh1, sc1, g1, sh2, sc2, g2 = mod_sample(0)
    (kc, vc, ksl, vsl, kw, vw), (q, gates) = _nsa_project(x, norm_g[0, 0], sc1, sh1, nsa_w_in[0], ts, 512, False)
    assert (past + ts) // CMP_BLK == past // CMP_BLK
    pos_last = lambda a: jnp.transpose(a[0], (0, 2, 3, 1))
    k_cmp_s = _compress_pages(pos_last(cache_k_cmp), page_table, wk_big, k_bias)
    v_cmp_s = _compress_pages(pos_last(cache_v_cmp), page_table, wv_big, v_bias)
    o = _sample_attention(q, gates, k_cmp_s, v_cmp_s, pos_last(cache_k_sel), pos_last(cache_v_sel),
                          page_table, ksl, vsl, pos_last(state_k_win), pos_last(state_v_win), kw, vw,
                          rel_bias, bs, ts)
    x = _matmul_residual(o, nsa_w_out[0], x, g1, ts, 512)
    x = _ffn(x, norm_g[0, 1], sc2, sh2, g2, ffn_wg[0], ffn_wu[0], ffn_wd[0], final_g, ts, 512, 1408, False)
    sh1, sc1, g1, sh2, sc2, g2 = mod_sample(1)
    b3 = lambda v: v.reshape(bs, ts, D_MODEL)[:, :1]
    x3, pool_s = _pool_mix(x.reshape(bs, ts, D_MODEL), state_pool[0], norm_g[1, 0], b3(sc1), b3(sh1), b3(g1),
                           pool_w[0], pool_scale[0], past, 512)
    y_sample = _ffn(x3.reshape(ms, D_MODEL), norm_g[1, 1], sc2, sh2, g2, ffn_wg[1], ffn_wu[1], ffn_wd[1],
                    final_g, ts, 512, 1408, True).reshape(bs, ts, D_MODEL)
    st5 = lambda a: a.reshape(1, bs, ts, N_KV, HEAD_DIM)
    kw_ext = jnp.concatenate([state_k_win[0], st5(kw)[0]], axis=1)[:, ts:]
    vw_ext = jnp.concatenate([state_v_win[0], st5(vw)[0]], axis=1)[:, ts:]
    sample_states = (st5(kc), st5(vc), st5(ksl), st5(vsl), kw_ext[None], vw_ext[None], pool_s[None])

    return (y_prompt, y_sample) + prompt_states + sample_states
```

```python
import functools
import math

import jax
import jax.numpy as jnp
from jax import lax
from jax.experimental import pallas as pl
from jax.experimental.pallas import tpu as pltpu

D_MODEL = 1024
N_HEADS = 16
N_KV = 4
GROUP = N_HEADS // N_KV
HEAD_DIM = 64
Q_DIM = N_HEADS * HEAD_DIM
KV_DIM = N_KV * HEAD_DIM
CMP_BLK = 32
SEL_BLK = 64
N_SEL = 16
WINDOW = 512
Q_CHUNK = 64
N_BUCKETS = 32
MAX_DISTANCE = 128
POOL_WINDOWS = (2, 4, 8, 16)
POOL_GROUP_DIM = D_MODEL // len(POOL_WINDOWS)
POOL_STATE = max(POOL_WINDOWS) - 1
PAGE_SIZE = 128
RMS_EPS = 1e-6
NEG = -1e30
BIG = 1e9

KEY_TILE = 128
CHUNK = 128
LANES = GROUP * CHUNK
LOG2E = math.log2(math.e)
Q_SCALE_LOG2 = HEAD_DIM ** -0.5 * LOG2E
SAMPLE_LANES = 128
VMEM_LIMIT_BYTES = 48 * 1024 * 1024

F32 = jnp.float32
BF16 = jnp.bfloat16


def _cparams(*sem):
    return pltpu.CompilerParams(dimension_semantics=sem, vmem_limit_bytes=VMEM_LIMIT_BYTES)


def _silu(x):
    return x * (1.0 / (1.0 + jnp.exp(-x)))


def _sigmoid(x):
    return 1.0 / (1.0 + jnp.exp(-x))


def _normmod(x, g, sc, sh):
    ms = jnp.mean(x * x, axis=-1, keepdims=True)
    return (x * lax.rsqrt(ms + RMS_EPS) * g) * (1.0 + sc) + sh


def _mod_spec(mod, tm, rows_per_batch):
    if mod.ndim == 3:
        return pl.BlockSpec((None, 1, D_MODEL), lambda i, *_: ((i * tm) // rows_per_batch, 0, 0))
    return pl.BlockSpec((tm, D_MODEL), lambda i, *_: (i, 0))


def _mm_kernel(a_ref, w_ref, b_ref, o_ref, *, silu_in):
    a = a_ref[...]
    if silu_in:
        a = _silu(a)
    o_ref[...] = jnp.dot(a.astype(BF16), w_ref[...], preferred_element_type=F32) + b_ref[...]


def _matmul(a, w, bias=None, *, silu_in=False, tm=256, tn=None, name="matmul"):
    m, k = a.shape
    n = w.shape[1]
    tm = min(tm, m)
    tn = n if tn is None else tn
    assert m % tm == 0 and n % tn == 0
    if bias is None:
        bias = jnp.zeros((1, n), F32)
    return pl.pallas_call(
        functools.partial(_mm_kernel, silu_in=silu_in),
        grid=(m // tm, n // tn),
        in_specs=[pl.BlockSpec((tm, k), lambda i, j: (i, 0)),
                  pl.BlockSpec((k, tn), lambda i, j: (0, j)),
                  pl.BlockSpec((1, tn), lambda i, j: (0, j))],
        out_specs=pl.BlockSpec((tm, tn), lambda i, j: (i, j)),
        out_shape=jax.ShapeDtypeStruct((m, n), F32),
        compiler_params=_cparams("parallel", "parallel"),
        name=name,
    )(a, w.astype(BF16), bias)


_PROJ_MAIN = Q_DIM + 6 * KV_DIM
_GATE_PAD = 128


def _proj_kernel(x_ref, g_ref, sc_ref, sh_ref, w_ref, wg_ref, *out_refs, attn_layouts, tm):
    hb = _normmod(x_ref[...], g_ref[...], sc_ref[...], sh_ref[...]).astype(BF16)
    q = jnp.dot(hb, w_ref[:, :Q_DIM], preferred_element_type=F32)
    gates = jnp.dot(hb, wg_ref[...], preferred_element_type=F32)
    kvs = []
    for n in range(6):
        lo = Q_DIM + n * KV_DIM
        kvs.append(jnp.dot(hb, w_ref[:, lo:lo + KV_DIM], preferred_element_type=F32))
    if not attn_layouts:
        for n in range(6):
            out_refs[n][...] = kvs[n]
        out_refs[6][...] = q
        out_refs[7][...] = gates
        return
    kt_refs = out_refs[2:8]
    qt_ref, gt_ref, ks_ref, vst_ref, kw_ref, vwt_ref, q_scr, gate_scr, kv_scr = out_refs[8:]
    out_refs[0][...] = kvs[0]
    out_refs[1][...] = kvs[1]
    kvts = []
    for n in range(6):
        kv_scr[n] = kvs[n]
        kvts.append(kv_scr[n].T)
        kt_refs[n][...] = kvts[n]
    n_chunks = tm // CHUNK
    q_scr[...] = q * Q_SCALE_LOG2
    gate_scr[...] = gates
    qt = q_scr[...].T
    gt = gate_scr[...].T
    for c in range(n_chunks):
        tok = slice(c * CHUNK, (c + 1) * CHUNK)
        for kv in range(N_KV):
            for gi in range(GROUP):
                h = kv * GROUP + gi
                lanes = slice(gi * CHUNK, (gi + 1) * CHUNK)
                qt_ref[c, kv, :, lanes] = qt[h * HEAD_DIM:(h + 1) * HEAD_DIM, tok].astype(BF16)
                gt_ref[c, kv, :, lanes] = gt[3 * h:3 * h + 3, tok]
    for k_ref, vt_ref, k, vt in ((ks_ref, vst_ref, kvs[2], kvts[3]), (kw_ref, vwt_ref, kvs[4], kvts[5])):
        for kv in range(N_KV):
            cols = slice(kv * HEAD_DIM, (kv + 1) * HEAD_DIM)
            k_ref[kv] = k[:, cols].astype(BF16)
            for c in range(n_chunks):
                vt_ref[kv, c, 0:HEAD_DIM, :] = vt[cols, c * CHUNK:(c + 1) * CHUNK].astype(BF16)
                vt_ref[kv, c, HEAD_DIM:, :] = jnp.ones((V_ROWS - HEAD_DIM, CHUNK), BF16)


def _nsa_project(x, g, sc, sh, w_in, rows_per_batch, tm, attn_layouts):
    m = x.shape[0]
    tm = min(tm, m)
    w_main = w_in[:, :_PROJ_MAIN].astype(BF16)
    w_gate = jnp.pad(w_in[:, _PROJ_MAIN:], ((0, 0), (0, _GATE_PAD - 3 * N_HEADS))).astype(BF16)
    row = lambda n: pl.BlockSpec((tm, n), lambda i: (i, 0))
    if attn_layouts:
        assert tm % CHUNK == 0 and rows_per_batch % tm == 0
        nck = tm // CHUNK
        tpb = rows_per_batch // tm
        out_specs = [row(KV_DIM)] * 2 + [pl.BlockSpec((None, KV_DIM, tm), lambda i: (i // tpb, 0, i % tpb))] * 6
        out_shape = ([jax.ShapeDtypeStruct((m, KV_DIM), F32)] * 2
                     + [jax.ShapeDtypeStruct((m // rows_per_batch, KV_DIM, rows_per_batch), F32)] * 6)
        chunked = lambda *s: pl.BlockSpec((nck,) + s, lambda i: (i,) + (0,) * len(s))
        k_spec = pl.BlockSpec((N_KV, tm, HEAD_DIM), lambda i: (0, i, 0))
        vt_spec = pl.BlockSpec((N_KV, nck, V_ROWS, CHUNK), lambda i: (0, i, 0, 0))
        k_shape = jax.ShapeDtypeStruct((N_KV, m, HEAD_DIM), BF16)
        vt_shape = jax.ShapeDtypeStruct((N_KV, m // CHUNK, V_ROWS, CHUNK), BF16)
        out_specs += [chunked(N_KV, HEAD_DIM, LANES), chunked(N_KV, 3, LANES), k_spec, vt_spec, k_spec, vt_spec]
        out_shape += [jax.ShapeDtypeStruct((m // CHUNK, N_KV, HEAD_DIM, LANES), BF16),
                      jax.ShapeDtypeStruct((m // CHUNK, N_KV, 3, LANES), F32),
                      k_shape, vt_shape, k_shape, vt_shape]
    else:
        out_specs = [row(KV_DIM)] * 6 + [row(Q_DIM), row(_GATE_PAD)]
        out_shape = ([jax.ShapeDtypeStruct((m, KV_DIM), F32)] * 6
                     + [jax.ShapeDtypeStruct((m, Q_DIM), F32), jax.ShapeDtypeStruct((m, _GATE_PAD), F32)])
    outs = pl.pallas_call(
        functools.partial(_proj_kernel, attn_layouts=attn_layouts, tm=tm),
        grid=(m // tm,),
        in_specs=[row(D_MODEL),
                  pl.BlockSpec((1, D_MODEL), lambda i: (0, 0)),
                  _mod_spec(sc, tm, rows_per_batch), _mod_spec(sh, tm, rows_per_batch),
                  pl.BlockSpec((D_MODEL, _PROJ_MAIN), lambda i: (0, 0)),
                  pl.BlockSpec((D_MODEL, _GATE_PAD), lambda i: (0, 0))],
        out_specs=out_specs,
        out_shape=out_shape,
        scratch_shapes=([pltpu.VMEM((tm, Q_DIM), F32), pltpu.VMEM((tm, _GATE_PAD), F32),
                         pltpu.VMEM((6, tm, KV_DIM), F32)] if attn_layouts else []),
        compiler_params=_cparams("parallel"),
        name="nsa_project",
    )(x, g.reshape(1, D_MODEL), sc, sh, w_main, w_gate)
    if attn_layouts:
        return outs[:2], outs[2:8], outs[8:]
    return outs[:6], outs[6:]


def _mm_res_kernel(a_ref, w_ref, x_ref, gate_ref, o_ref):
    y = jnp.dot(a_ref[...].astype(BF16), w_ref[...], preferred_element_type=F32)
    o_ref[...] = x_ref[...] + gate_ref[...] * y


def _matmul_residual(a, w, x, gate, rows_per_batch, tm):
    m, k = a.shape
    tm = min(tm, m)
    return pl.pallas_call(
        _mm_res_kernel,
        grid=(m // tm,),
        in_specs=[pl.BlockSpec((tm, k), lambda i: (i, 0)),
                  pl.BlockSpec((k, D_MODEL), lambda i: (0, 0)),
                  pl.BlockSpec((tm, D_MODEL), lambda i: (i, 0)),
                  _mod_spec(gate, tm, rows_per_batch)],
        out_specs=pl.BlockSpec((tm, D_MODEL), lambda i: (i, 0)),
        out_shape=jax.ShapeDtypeStruct((m, D_MODEL), F32),
        compiler_params=_cparams("parallel"),
        name="out_proj_residual",
    )(a, w.astype(BF16), x, gate)


def _mm_res_t_kernel(ot_ref, w_ref, x_ref, gate_ref, o_ref, *, n_chunks):
    chunks = []
    for c in range(n_chunks):
        rows = [ot_ref[c, kv, :, gi * CHUNK:(gi + 1) * CHUNK] for kv in range(N_KV) for gi in range(GROUP)]
        chunks.append(jnp.concatenate(rows, axis=0).astype(F32).T)
    a = jnp.concatenate(chunks, axis=0).astype(BF16)
    y = jnp.dot(a, w_ref[...], preferred_element_type=F32)
    o_ref[...] = x_ref[...] + gate_ref[...] * y


def _matmul_residual_t(o_t, w, x, gate, rows_per_batch, tm):
    m = x.shape[0]
    nck = tm // CHUNK
    return pl.pallas_call(
        functools.partial(_mm_res_t_kernel, n_chunks=nck),
        grid=(m // tm,),
        in_specs=[pl.BlockSpec((nck, N_KV, HEAD_DIM, LANES), lambda i: (i, 0, 0, 0)),
                  pl.BlockSpec((Q_DIM, D_MODEL), lambda i: (0, 0)),
                  pl.BlockSpec((tm, D_MODEL), lambda i: (i, 0)),
                  _mod_spec(gate, tm, rows_per_batch)],
        out_specs=pl.BlockSpec((tm, D_MODEL), lambda i: (i, 0)),
        out_shape=jax.ShapeDtypeStruct((m, D_MODEL), F32),
        compiler_params=_cparams("parallel"),
        name="out_proj_residual_t",
    )(o_t, w.astype(BF16), x, gate)


def _ffn_kernel(x_ref, g_ref, sc_ref, sh_ref, gate_ref, wg_ref, wu_ref, wd_ref, fg_ref, o_ref,
                *, final_norm, tf):
    x = x_ref[...]
    hb = _normmod(x, g_ref[...], sc_ref[...], sh_ref[...]).astype(BF16)
    acc = None
    for lo in range(0, wg_ref.shape[1], tf):
        a = jnp.dot(hb, wg_ref[:, lo:lo + tf], preferred_element_type=F32)
        u = jnp.dot(hb, wu_ref[:, lo:lo + tf], preferred_element_type=F32)
        act = (_silu(a) * u).astype(BF16)
        part = jnp.dot(act, wd_ref[lo:lo + tf, :], preferred_element_type=F32)
        acc = part if acc is None else acc + part
    y = x + gate_ref[...] * acc
    if final_norm:
        ms = jnp.mean(y * y, axis=-1, keepdims=True)
        y = y * lax.rsqrt(ms + RMS_EPS) * fg_ref[...]
    o_ref[...] = y


def _ffn(x, g, sc, sh, gate, wg, wu, wd, final_g, rows_per_batch, tm, tf, final_norm):
    m = x.shape[0]
    d_ff = wg.shape[1]
    tm = min(tm, m)
    assert d_ff % tf == 0
    once = pl.Buffered(1)
    vec = pl.BlockSpec((1, D_MODEL), lambda i: (0, 0))
    return pl.pallas_call(
        functools.partial(_ffn_kernel, final_norm=final_norm, tf=tf),
        grid=(m // tm,),
        in_specs=[pl.BlockSpec((tm, D_MODEL), lambda i: (i, 0)),
                  vec,
                  _mod_spec(sc, tm, rows_per_batch), _mod_spec(sh, tm, rows_per_batch),
                  _mod_spec(gate, tm, rows_per_batch),
                  pl.BlockSpec((D_MODEL, d_ff), lambda i: (0, 0), pipeline_mode=once),
                  pl.BlockSpec((D_MODEL, d_ff), lambda i: (0, 0), pipeline_mode=once),
                  pl.BlockSpec((d_ff, D_MODEL), lambda i: (0, 0), pipeline_mode=once),
                  vec],
        out_specs=pl.BlockSpec((tm, D_MODEL), lambda i: (i, 0)),
        out_shape=jax.ShapeDtypeStruct((m, D_MODEL), F32),
        compiler_params=_cparams("parallel"),
        name="ffn",
    )(x, g.reshape(1, D_MODEL), sc, sh, gate, wg.astype(BF16), wu.astype(BF16), wd.astype(BF16),
      final_g.reshape(1, D_MODEL))


_POOL_HALO = 16


def _pool_kernel(x_ref, xprev_ref, state_ref, g_ref, sc_ref, sh_ref, gate_ref, w_ref, ls_ref,
                 o_ref, st_ref, ext_ref, *, tm, pos0):
    i = pl.program_id(1)
    g, sc, sh = g_ref[...], sc_ref[...], sh_ref[...]
    u = _normmod(x_ref[...], g, sc, sh)
    prev = jnp.where(i == 0, state_ref[...], _normmod(xprev_ref[...], g, sc, sh))
    ext_ref[0:_POOL_HALO, :] = prev
    ext_ref[_POOL_HALO:_POOL_HALO + tm, :] = u
    st_ref[...] = ext_ref[tm:tm + _POOL_HALO, :]

    pos = pos0 + i * tm + lax.broadcasted_iota(jnp.int32, (tm, 1), 0)
    mixed = []
    for gi, w in enumerate(POOL_WINDOWS):
        lo = gi * POOL_GROUP_DIM
        s = u[:, lo:lo + POOL_GROUP_DIM]
        for k in range(1, w):
            s = s + ext_ref[_POOL_HALO - k:_POOL_HALO - k + tm, lo:lo + POOL_GROUP_DIM]
        cnt = jnp.minimum(pos + 1, w).astype(F32)
        pooled = s / cnt - u[:, lo:lo + POOL_GROUP_DIM]
        mixed.append(jnp.dot(pooled.astype(BF16), w_ref[gi], preferred_element_type=F32))
    y = jnp.concatenate(mixed, axis=-1) * ls_ref[...]
    o_ref[...] = x_ref[...] + gate_ref[...] * y


def _pool_mix(x3, state, g, sc, sh, gate, w_grp, layer_scale, pos0, tm):
    b, t, _ = x3.shape
    tm = min(tm, t)
    state16 = jnp.pad(state, ((0, 0), (_POOL_HALO - POOL_STATE, 0), (0, 0)))
    if t >= _POOL_HALO:
        xprev = x3
        nprev = tm // _POOL_HALO
        prev_spec = pl.BlockSpec((None, _POOL_HALO, D_MODEL),
                                 lambda bi, i: (bi, jnp.maximum(i * nprev - 1, 0), 0))
    else:
        xprev = state16
        prev_spec = pl.BlockSpec((None, _POOL_HALO, D_MODEL), lambda bi, i: (bi, 0, 0))
    vec = pl.BlockSpec((1, D_MODEL), lambda bi, i: (0, 0))
    bvec = pl.BlockSpec((None, 1, D_MODEL), lambda bi, i: (bi, 0, 0))
    out, st = pl.pallas_call(
        functools.partial(_pool_kernel, tm=tm, pos0=pos0),
        grid=(b, t // tm),
        in_specs=[pl.BlockSpec((None, tm, D_MODEL), lambda bi, i: (bi, i, 0)),
                  prev_spec,
                  pl.BlockSpec((None, _POOL_HALO, D_MODEL), lambda bi, i: (bi, 0, 0)),
                  vec, bvec, bvec, bvec,
                  pl.BlockSpec((len(POOL_WINDOWS), POOL_GROUP_DIM, POOL_GROUP_DIM),
                               lambda bi, i: (0, 0, 0)),
                  vec],
        out_specs=[pl.BlockSpec((None, tm, D_MODEL), lambda bi, i: (bi, i, 0)),
                   pl.BlockSpec((None, _POOL_HALO, D_MODEL), lambda bi, i: (bi, 0, 0))],
        out_shape=[jax.ShapeDtypeStruct((b, t, D_MODEL), F32),
                   jax.ShapeDtypeStruct((b, _POOL_HALO, D_MODEL), F32)],
        scratch_shapes=[pltpu.VMEM((tm + _POOL_HALO, D_MODEL), F32)],
        compiler_params=_cparams("parallel", "arbitrary"),
        name="pool_mix",
    )(x3, xprev, state16, g.reshape(1, D_MODEL), sc, sh, gate, w_grp.astype(BF16),
      layer_scale.reshape(1, D_MODEL))
    return out, st[:, _POOL_HALO - POOL_STATE:]


def _rel_bucket(dist):
    d = jnp.maximum(dist, 0)
    max_exact = N_BUCKETS // 2
    large = max_exact + (jnp.log(jnp.maximum(d, 1).astype(F32) / max_exact)
                         / math.log(MAX_DISTANCE / max_exact) * (N_BUCKETS - max_exact)).astype(jnp.int32)
    large = jnp.minimum(large, N_BUCKETS - 1)
    return jnp.where(d < max_exact, d, large)


def _dist_bias(rel_bias):
    return rel_bias[_rel_bucket(jnp.arange(MAX_DISTANCE + 1, dtype=jnp.int32))]


def _lookup(table, idx):
    onehot = (idx[..., None] == jnp.arange(table.shape[0], dtype=jnp.int32)).astype(F32)
    return jnp.einsum("...d,dh->...h", onehot, table, precision=lax.Precision.HIGHEST)


def _bias_tile(fd, dist, valid):
    k, q = dist.shape
    bias = _lookup(fd, jnp.clip(dist, 0, MAX_DISTANCE)).reshape(k, q, N_KV, GROUP)
    bias = jnp.where(valid[:, :, None, None], bias, NEG)
    return bias.transpose(2, 0, 3, 1).reshape(N_KV, k, GROUP * q)


def _prompt_tables(rel_bias, t):
    fd = _dist_bias(rel_bias) * LOG2E
    kj = jnp.arange(KEY_TILE, dtype=jnp.int32)[:, None]
    qi = jnp.arange(CHUNK, dtype=jnp.int32)[None, :]
    dist = qi - kj
    near = jnp.stack([_bias_tile(fd, dist + KEY_TILE, dist + KEY_TILE >= 0),
                      _bias_tile(fd, dist, dist >= 0)])
    wfirst = _bias_tile(fd, dist + WINDOW, dist < 0)
    far = _bias_tile(fd, jnp.full((1, CHUNK), MAX_DISTANCE, jnp.int32), jnp.ones((1, CHUNK), bool))
    per_chunk = CHUNK // CMP_BLK
    assert CHUNK == 128 and CMP_BLK == 32 and MAX_DISTANCE == 128
    rel0, n_rel = -per_chunk, 2 * per_chunk
    rel = rel0 + jnp.arange(n_rel, dtype=jnp.int32)[:, None]
    dist = qi - CMP_BLK * rel - (CMP_BLK - 1)
    crel = _bias_tile(fd, dist, dist >= 0)
    relm = (jnp.arange(t // CMP_BLK, dtype=jnp.int32)[None, :]
            - per_chunk * jnp.arange(t // CHUNK, dtype=jnp.int32)[:, None])[:, None, :, None]
    cmp_tab = jnp.where(relm >= rel0 + n_rel, NEG, far[None])
    for r in range(n_rel):
        cmp_tab = jnp.where(relm == rel0 + r, crel[None, :, r:r + 1, :], cmp_tab)
    return near, wfirst, far, cmp_tab


V_ROWS = HEAD_DIM + 16


def _half_max(s, top, bot):
    return jnp.maximum(jnp.max(s[:SEL_BLK], axis=0, keepdims=True) + top,
                       jnp.max(s[SEL_BLK:], axis=0, keepdims=True) + bot)


def _half_exp2(s, m, top, bot):
    return jnp.exp2(jnp.concatenate([s[:SEL_BLK] - (m - top), s[SEL_BLK:] - (m - bot)], axis=0))


def _tile_softmax(tiles, qs):
    scores, maxes = [], []
    for k, _, bias in tiles:
        s = jnp.dot(k, qs, preferred_element_type=F32)
        if isinstance(bias, tuple):
            maxes.append(_half_max(s, *bias))
        else:
            s = s + bias
            maxes.append(jnp.max(s, axis=0, keepdims=True))
        scores.append(s)
    m = functools.reduce(jnp.maximum, maxes)
    acc = 0.0
    for s, (_, vt, bias) in zip(scores, tiles):
        p = _half_exp2(s, m, *bias) if isinstance(bias, tuple) else jnp.exp2(s - m)
        acc = acc + jnp.dot(vt, p.astype(BF16), preferred_element_type=F32)
    return m, acc


def _normalise(acc):
    return acc[:HEAD_DIM] * (1.0 / jnp.maximum(acc[HEAD_DIM:HEAD_DIM + 1], 1e-30))


def _prompt_attn_kernel(q_ref, g_ref, kc_ref, vct_ref, cb_ref, ks_ref, vst_ref, kw_ref, vwt_ref,
                        near_ref, wfirst_ref, far_ref, o_ref,
                        imp_ref, val_ref, rank_ref, selrep_ref, m_ref, alpha_ref, acc_ref, mix_ref,
                        s_ref, smax_ref, p_ref, *, n_blk, n_sel):
    i = pl.program_id(1)

    def k_rows(ref, kv, jt):
        return ref[kv, pl.ds(pl.multiple_of(jt * KEY_TILE, KEY_TILE), KEY_TILE), :]

    def sel_rows(kv, jt):
        return selrep_ref[kv, pl.ds(2 * jt, 1), :], selrep_ref[kv, pl.ds(2 * jt + 1, 1), :]

    def halves(top, bot):
        return jnp.concatenate([jnp.broadcast_to(top, (SEL_BLK, LANES)),
                                jnp.broadcast_to(bot, (SEL_BLK, LANES))], axis=0)

    for kv in range(N_KV):
        bias = cb_ref[kv]
        s = jnp.dot(kc_ref[kv], q_ref[kv], preferred_element_type=F32) + bias
        valid = bias > 0.5 * NEG
        m = jnp.max(s, axis=0, keepdims=True)
        p = jnp.where(valid, jnp.exp2(s - m), 0.0)
        p = p * (1.0 / jnp.maximum(jnp.sum(p, axis=0, keepdims=True), 1e-30))
        o_c = jnp.dot(vct_ref[kv], p.astype(BF16), preferred_element_type=F32)
        mix_ref[kv] = _sigmoid(g_ref[kv, 0:1, :]) * o_c
        imp_ref[kv] = functools.reduce(
            lambda a, b: a + b, [p[:, gi * CHUNK:(gi + 1) * CHUNK] for gi in range(GROUP)])

    imp = jnp.concatenate(
        [imp_ref[kv, pl.ds(0, n_blk, stride=2), :] + imp_ref[kv, pl.ds(1, n_blk, stride=2), :]
         for kv in range(N_KV)], axis=1)
    blk = lax.broadcasted_iota(jnp.int32, (n_blk, LANES), 0)
    lane = lax.broadcasted_iota(jnp.int32, (1, LANES), 1)
    assert CHUNK == 2 * SEL_BLK
    qblk = 2 * i + jnp.where((lane & (CHUNK - 1)) >= SEL_BLK, 1, 0)
    forced = (blk == 0) | (blk == qblk) | (blk == qblk - 1)
    val_ref[...] = jnp.where(forced, BIG, jnp.where(blk <= qblk, imp, -1.0))
    rank_ref[...] = jnp.zeros_like(rank_ref)
    sub = lax.broadcasted_iota(jnp.int32, (8, LANES), 0)
    n_grp = n_blk // 8
    for grp in range(n_grp):
        @pl.when(8 * grp <= 2 * i + 1)
        def _(grp=grp):
            for r8 in range(n_grp):
                piece = val_ref[r8 * 8:(r8 + 1) * 8, :]
                acc = rank_ref[r8 * 8:(r8 + 1) * 8, :]
                for j in range(grp * 8, grp * 8 + 8):
                    row = val_ref[j:j + 1, :]
                    if r8 > grp:
                        ahead = row >= piece
                    elif r8 < grp:
                        ahead = row > piece
                    else:
                        acc = acc + jnp.where(sub > j - grp * 8, jnp.where(row >= piece, 1.0, 0.0),
                                              jnp.where(row > piece, 1.0, 0.0))
                        continue
                    acc = acc + jnp.where(ahead, 1.0, 0.0)
                rank_ref[r8 * 8:(r8 + 1) * 8, :] = acc
    selneg = jnp.where(rank_ref[...] < n_sel, 0.0, NEG)
    for kv in range(N_KV):
        selrep_ref[kv] = jnp.concatenate([selneg[:, kv * CHUNK:(kv + 1) * CHUNK]] * GROUP, axis=1)

    prev = jnp.maximum(i - 1, 0)
    prev_off = jnp.where(i >= 1, 0.0, NEG)
    for kv in range(N_KV):
        qs = q_ref[kv]
        far = far_ref[kv]
        tiles = [(k_rows(kw_ref, kv, i), vwt_ref[kv, i], near_ref[1, kv]),
                 (k_rows(kw_ref, kv, prev), vwt_ref[kv, prev], near_ref[0, kv] + prev_off)]
        for back in (2, 3):
            jt = jnp.maximum(i - back, 0)
            row = far + jnp.where(i >= back, 0.0, NEG)
            tiles.append((k_rows(kw_ref, kv, jt), vwt_ref[kv, jt], (row, row)))
        jt = jnp.maximum(i - 4, 0)
        tiles.append((k_rows(kw_ref, kv, jt), vwt_ref[kv, jt], wfirst_ref[kv] + jnp.where(i >= 4, 0.0, NEG)))
        _, acc = _tile_softmax(tiles, qs)
        mix_ref[kv] += _sigmoid(g_ref[kv, 2:3, :]) * _normalise(acc)

        tiles = [(k_rows(ks_ref, kv, i), vst_ref[kv, i], near_ref[1, kv] + halves(*sel_rows(kv, i))),
                 (k_rows(ks_ref, kv, prev), vst_ref[kv, prev],
                  near_ref[0, kv] + halves(*sel_rows(kv, prev)) + prev_off)]
        m_ref[kv], acc_ref[kv] = _tile_softmax(tiles, qs)

    n_far = jnp.maximum(i - 1, 0)
    last = jnp.maximum(n_far - 1, 0)

    def scores(jt, slot):
        jc = jnp.minimum(jt, last)
        for kv in range(N_KV):
            s = jnp.dot(k_rows(ks_ref, kv, jc), q_ref[kv], preferred_element_type=F32)
            s_ref[slot, kv] = s
            smax_ref[slot, kv, 0:1, :] = jnp.max(s[:SEL_BLK], axis=0, keepdims=True)
            smax_ref[slot, kv, 1:2, :] = jnp.max(s[SEL_BLK:], axis=0, keepdims=True)

    def softmax(jt, slot):
        jc = jnp.minimum(jt, last)
        off = jnp.where(jt < n_far, 0.0, NEG)
        for kv in range(N_KV):
            top, bot = sel_rows(kv, jc)
            far = far_ref[kv] + off
            top, bot = top + far, bot + far
            m_old = m_ref[kv]
            m_new = jnp.maximum(m_old, jnp.maximum(smax_ref[slot, kv, 0:1, :] + top,
                                                   smax_ref[slot, kv, 1:2, :] + bot))
            m_ref[kv] = m_new
            alpha_ref[slot, kv] = jnp.exp2(m_old - m_new)
            p_ref[slot, kv] = _half_exp2(s_ref[slot, kv], m_new, top, bot).astype(BF16)

    def values(jt, slot):
        jc = jnp.clip(jt, 0, last)
        for kv in range(N_KV):
            acc_ref[kv] = alpha_ref[slot, kv] * acc_ref[kv] + jnp.dot(vst_ref[kv, jc], p_ref[slot, kv],
                                                                      preferred_element_type=F32)

    p_ref[1] = jnp.zeros(p_ref.shape[1:], BF16)
    alpha_ref[1] = jnp.ones(alpha_ref.shape[1:], F32)
    scores(0, 0)

    def pair_step(j2):
        jt = 2 * j2
        values(jt - 1, 1)
        softmax(jt, 0)
        scores(jt + 1, 1)
        values(jt, 0)
        softmax(jt + 1, 1)
        scores(jt + 2, 0)

    def body(pairs_per_trip):
        def run(j, carry):
            for u in range(pairs_per_trip):
                pair_step(pairs_per_trip * j + u)
            return carry
        return run

    n_pairs = (n_far + 1) // 2
    n4 = n_pairs // 4
    n2 = n_pairs // 2
    lax.fori_loop(0, n4, body(4), 0)
    lax.fori_loop(2 * n4, n2, body(2), 0)
    lax.fori_loop(2 * n2, n_pairs, body(1), 0)
    values(2 * n_pairs - 1, 1)

    for kv in range(N_KV):
        o_ref[kv] = (mix_ref[kv] + _sigmoid(g_ref[kv, 1:2, :]) * _normalise(acc_ref[kv])).astype(o_ref.dtype)


def _prompt_attention(layouts, k_cmp, v_cmp, rel_bias, b, t):
    q_t, g_t, ks, vst, kw, vwt = layouts
    nch, nt, nc, n_blk = t // CHUNK, t // KEY_TILE, t // CMP_BLK, t // SEL_BLK
    kc = k_cmp.reshape(b, nc, N_KV, HEAD_DIM).transpose(0, 2, 1, 3).astype(BF16)
    vct = v_cmp.reshape(b, nc, N_KV, HEAD_DIM).transpose(0, 2, 3, 1).astype(BF16)
    near, wfirst, far, cmp_tab = _prompt_tables(rel_bias, t)

    once = pl.Buffered(1)
    per_b = lambda *shape: pl.BlockSpec((None,) + shape, lambda bi, i: (bi,) + (0,) * len(shape),
                                        pipeline_mode=once)
    const = lambda *shape: pl.BlockSpec(shape, lambda bi, i: (0,) * len(shape), pipeline_mode=once)
    chunk = lambda *shape: pl.BlockSpec((None,) + shape, lambda bi, i: (bi * nch + i,) + (0,) * len(shape))
    k_spec = pl.BlockSpec((N_KV, t, HEAD_DIM), lambda bi, i: (0, bi, 0), pipeline_mode=once)
    vt_spec = pl.BlockSpec((N_KV, nt, V_ROWS, KEY_TILE), lambda bi, i: (0, bi, 0, 0), pipeline_mode=once)
    stat = pltpu.VMEM((N_KV, 1, LANES), F32)
    return pl.pallas_call(
        functools.partial(_prompt_attn_kernel, n_blk=n_blk, n_sel=min(N_SEL, n_blk)),
        grid=(b, nch),
        in_specs=[chunk(N_KV, HEAD_DIM, LANES),
                  chunk(N_KV, 3, LANES),
                  per_b(N_KV, nc, HEAD_DIM),
                  per_b(N_KV, HEAD_DIM, nc),
                  pl.BlockSpec((None, N_KV, nc, LANES), lambda bi, i: (i, 0, 0, 0)),
                  k_spec, vt_spec, k_spec, vt_spec,
                  const(2, N_KV, KEY_TILE, LANES),
                  const(N_KV, KEY_TILE, LANES),
                  const(N_KV, 1, LANES)],
        out_specs=chunk(N_KV, HEAD_DIM, LANES),
        out_shape=jax.ShapeDtypeStruct((b * nch, N_KV, HEAD_DIM, LANES), BF16),
        scratch_shapes=[pltpu.VMEM((N_KV, nc, CHUNK), F32),
                        pltpu.VMEM((n_blk, LANES), F32),
                        pltpu.VMEM((n_blk, LANES), F32),
                        pltpu.VMEM((N_KV, n_blk, LANES), F32),
                        stat, pltpu.VMEM((2, N_KV, 1, LANES), F32),
                        pltpu.VMEM((N_KV, V_ROWS, LANES), F32),
                        pltpu.VMEM((N_KV, HEAD_DIM, LANES), F32),
                        pltpu.VMEM((2, N_KV, KEY_TILE, LANES), F32),
                        pltpu.VMEM((2, N_KV, 2, LANES), F32),
                        pltpu.VMEM((2, N_KV, KEY_TILE, LANES), BF16)],
        compiler_params=_cparams("parallel", "arbitrary"),
        name="nsa_prompt_attention",
    )(q_t, g_t, kc, vct, cmp_tab, ks, vst, kw, vwt, near, wfirst, far)


NEW_PAD = 128


def _sample_tables(rel_bias, past, t, wbuf, n_rows):
    lane = jnp.arange(SAMPLE_LANES, dtype=jnp.int32)
    g, kvh, qi = lane // (N_KV * t), (lane // t) % N_KV, lane % t
    used = lane < GROUP * N_KV * t
    head = jnp.where(used, kvh * GROUP + g, 0)
    qpos = past + qi
    fd_lane = jnp.take(_dist_bias(rel_bias), head, axis=1)
    dists = jnp.arange(MAX_DISTANCE + 1, dtype=jnp.int32)

    def tab(kpos, valid, n_near):
        dist = qpos[None, :] - kpos[:, None]
        n_far = kpos.shape[0] - n_near
        near_d = jnp.clip(dist[n_far:], 0, MAX_DISTANCE)
        near = jnp.sum(jnp.where(near_d[:, :, None] == dists, fd_lane.T[None], 0.0), axis=-1)
        far = jnp.broadcast_to(fd_lane[MAX_DISTANCE][None, :], (n_far, SAMPLE_LANES))
        bias = jnp.where(used[None, :], jnp.concatenate([far, near], axis=0), 0.0)
        return jnp.where(valid(dist) & (kpos[:, None] >= 0), bias, NEG)

    new_pos = past + jnp.where(jnp.arange(NEW_PAD) < t, jnp.arange(NEW_PAD, dtype=jnp.int32), 1 << 20)
    causal = lambda d: d >= 0
    window = lambda d: (d >= 0) & (d < WINDOW)
    nc = past // CMP_BLK
    assert wbuf >= MAX_DISTANCE and past >= MAX_DISTANCE
    t_cmp = tab(jnp.arange(nc, dtype=jnp.int32) * CMP_BLK + CMP_BLK - 1, causal, MAX_DISTANCE // CMP_BLK)
    rows = lambda a: a[:, :n_rows].T
    t_sel = rows(tab(jnp.concatenate([jnp.arange(past, dtype=jnp.int32), new_pos]), causal,
                     MAX_DISTANCE + NEW_PAD))
    t_win = rows(tab(jnp.concatenate([past - wbuf + jnp.arange(wbuf, dtype=jnp.int32), new_pos]), window,
                     MAX_DISTANCE + NEW_PAD))
    key = jnp.arange(past + NEW_PAD, dtype=jnp.int32)
    blk = jnp.where(key < past, key // SEL_BLK, past // SEL_BLK)
    expand = (blk[None, :] == jnp.arange(128, dtype=jnp.int32)[:, None]).astype(BF16)
    return t_cmp, t_sel, t_win, expand


def _sample_attn_kernel(pt_ref, *refs, n_pages, n_sel, t, wbuf):
    del pt_ref
    it = iter(refs)
    qbd_ref, qrow_ref, g_ref = next(it), next(it), next(it)
    kc_ref, vc_ref = next(it), next(it)
    ks_pages = [next(it) for _ in range(n_pages)]
    vs_pages = [next(it) for _ in range(n_pages)]
    ksn_ref, vsn_ref, kwin_ref, vwin_ref, kwn_ref, vwn_ref = (next(it) for _ in range(6))
    tcmp_ref, tsel_ref, twin_ref, expand_ref = (next(it) for _ in range(4))
    o_ref = next(it)
    imp_ref, val_ref, sel_ref, s_ref = (next(it) for _ in range(4))

    scale = HEAD_DIM ** -0.5
    nq = N_KV * t
    qs = qbd_ref[...] * scale
    qr = qrow_ref[...] * scale
    n_q = qr.shape[0]
    lane = lax.broadcasted_iota(jnp.int32, (1, SAMPLE_LANES), 1)
    blocks_per_page = PAGE_SIZE // CMP_BLK
    past = n_pages * PAGE_SIZE
    nt_dims = (((1,), (1,)), ((), ()))

    s = jnp.dot(kc_ref[...].astype(BF16), qs, preferred_element_type=F32) + tcmp_ref[...]
    m = jnp.max(s, axis=0, keepdims=True)
    p_c = jnp.exp(s - m)
    p_c = p_c / jnp.maximum(jnp.sum(p_c, axis=0, keepdims=True), 1e-30)
    o_c = lax.dot_general(p_c.astype(BF16), vc_ref[...].astype(BF16), (((0,), (0,)), ((), ())),
                          preferred_element_type=F32)[:n_q]

    imp = p_c
    for g in range(1, GROUP):
        imp = imp + pltpu.roll(p_c, SAMPLE_LANES - g * nq, 1)
    imp_ref[...] = imp
    n_pairs = n_pages * blocks_per_page // 2
    n_blk = n_pairs + 1
    n_rows = val_ref.shape[0]
    val_ref[...] = jnp.full((n_rows, SAMPLE_LANES), -2.0, F32)
    val_ref[0:n_pairs, :] = imp_ref[pl.ds(0, n_pairs, stride=2), :] + imp_ref[pl.ds(1, n_pairs, stride=2), :]
    blk = lax.broadcasted_iota(jnp.int32, (n_rows, SAMPLE_LANES), 0)
    qblk = n_blk - 1
    forced = (blk == 0) | (blk == qblk) | (blk == qblk - 1)
    val = jnp.where(forced, BIG, val_ref[...])
    val = jnp.where(blk < n_blk, val, -2.0)
    val_ref[...] = val
    rank = jnp.zeros((n_rows, SAMPLE_LANES), F32)
    for j in range(n_blk):
        row = val_ref[j:j + 1, :]
        rank = rank + jnp.where(blk > j, jnp.where(row >= val, 1.0, 0.0), jnp.where(row > val, 1.0, 0.0))
    selneg = jnp.where((rank < n_sel) & (lane < nq), 0.0, jnp.where(lane < nq, NEG, 0.0))
    selrep = selneg
    for g in range(1, GROUP):
        selrep = selrep + pltpu.roll(selneg, g * nq, 1)
    sel_ref[...] = jnp.zeros_like(sel_ref)
    sel_ref[0:n_rows, :] = jnp.where(selrep == 0.0, 1.0, 0.0)
    sel01 = sel_ref[...].T[:n_q].astype(BF16)

    def paged(page_ref):
        return page_ref[...].reshape(KV_DIM, PAGE_SIZE).astype(BF16)

    for p in range(n_pages + 1):
        cols = slice(p * KEY_TILE, (p + 1) * KEY_TILE)
        if p < n_pages:
            s = jnp.dot(qr, paged(ks_pages[p]), preferred_element_type=F32)
        else:
            s = lax.dot_general(qr, ksn_ref[...], nt_dims, preferred_element_type=F32)
        picked = jnp.dot(sel01, expand_ref[:, cols], preferred_element_type=F32)
        s_ref[:, cols] = s + tsel_ref[:, cols] + (picked - 1.0) * (-NEG)
    s = s_ref[...]
    p_s = jnp.exp(s - jnp.max(s, axis=1, keepdims=True))
    l = jnp.sum(p_s, axis=1, keepdims=True)
    p_s = p_s.astype(BF16)
    acc = jnp.dot(p_s[:, past:], vsn_ref[...], preferred_element_type=F32)
    for p in range(n_pages):
        acc = acc + lax.dot_general(p_s[:, p * KEY_TILE:(p + 1) * KEY_TILE], paged(vs_pages[p]), nt_dims,
                                    preferred_element_type=F32)
    o_s = acc / jnp.maximum(l, 1e-30)

    kwin = kwin_ref[...].reshape(KV_DIM, wbuf).astype(BF16)
    vwin = vwin_ref[...].reshape(KV_DIM, wbuf).astype(BF16)
    s_w = jnp.dot(qr, kwin, preferred_element_type=F32) + twin_ref[:, :wbuf]
    s_n = lax.dot_general(qr, kwn_ref[...], nt_dims, preferred_element_type=F32) + twin_ref[:, wbuf:]
    m = jnp.maximum(jnp.max(s_w, axis=1, keepdims=True), jnp.max(s_n, axis=1, keepdims=True))
    p_w, p_n = jnp.exp(s_w - m), jnp.exp(s_n - m)
    l = jnp.sum(p_w, axis=1, keepdims=True) + jnp.sum(p_n, axis=1, keepdims=True)
    acc = (lax.dot_general(p_w.astype(BF16), vwin, nt_dims, preferred_element_type=F32)
           + jnp.dot(p_n.astype(BF16), vwn_ref[...], preferred_element_type=F32))
    o_w = acc / jnp.maximum(l, 1e-30)

    gate = _sigmoid(g_ref[...])
    o_ref[...] = gate[:, 0:1] * o_c + gate[:, 1:2] * o_s + gate[:, 2:3] * o_w


def _sample_attention(q, gates, k_cmp, v_cmp, cache_ks, cache_vs, page_table,
                      ksl, vsl, win_k, win_v, kw, vw, rel_bias, b, t):
    n_pages = page_table.shape[1]
    past = n_pages * PAGE_SIZE
    wbuf = win_k.shape[-1]
    nq = N_KV * t
    used = GROUP * nq
    assert used % 8 == 0 and used <= SAMPLE_LANES and t <= NEW_PAD
    blocks_per_page = PAGE_SIZE // CMP_BLK
    n_blk = past // SEL_BLK + 1
    n_rows = -(-n_blk // 8) * 8

    q5 = q.reshape(b, t, N_KV, GROUP, HEAD_DIM)
    eye = jnp.eye(N_KV, dtype=q.dtype)
    qbd = jnp.einsum("btkgd,kc->bkdgct", q5, eye).reshape(b, KV_DIM, used)
    qbd = jnp.pad(qbd, ((0, 0), (0, 0), (0, SAMPLE_LANES - used))).astype(BF16)
    qrow = jnp.einsum("btkgd,kc->bgktcd", q5, eye).reshape(b, used, KV_DIM).astype(BF16)
    g_rows = gates[:, :3 * N_HEADS].reshape(b, t, N_KV, GROUP, 3).transpose(0, 3, 2, 1, 4).reshape(b, used, 3)
    g_rows = jnp.pad(g_rows, ((0, 0), (0, 0), (0, 128 - 3)))
    pad_rows = lambda a: jnp.pad(a.reshape(b, t, KV_DIM), ((0, 0), (0, NEW_PAD - t), (0, 0))).astype(BF16)
    tables = _sample_tables(rel_bias, past, t, wbuf, used)

    per_b = lambda *shape: pl.BlockSpec((None,) + shape, lambda bi, pt: (bi,) + (0,) * len(shape))
    const = lambda *shape: pl.BlockSpec(shape, lambda bi, pt: (0,) * len(shape))

    def page(p):
        return pl.BlockSpec((None, N_KV, HEAD_DIM, PAGE_SIZE), lambda bi, pt, p=p: (pt[bi, p], 0, 0, 0))

    n_cmp = n_pages * blocks_per_page
    in_specs = ([per_b(KV_DIM, SAMPLE_LANES), per_b(used, KV_DIM), per_b(used, 128),
                 per_b(n_cmp, KV_DIM), per_b(n_cmp, KV_DIM)]
                + [page(p) for p in range(n_pages)] * 2
                + [per_b(NEW_PAD, KV_DIM), per_b(NEW_PAD, KV_DIM),
                   per_b(N_KV, HEAD_DIM, wbuf), per_b(N_KV, HEAD_DIM, wbuf),
                   per_b(NEW_PAD, KV_DIM), per_b(NEW_PAD, KV_DIM)]
                + [const(*tb.shape) for tb in tables])
    out = pl.pallas_call(
        functools.partial(_sample_attn_kernel, n_pages=n_pages, n_sel=min(N_SEL, n_blk), t=t, wbuf=wbuf),
        grid_spec=pltpu.PrefetchScalarGridSpec(
            num_scalar_prefetch=1,
            grid=(b,),
            in_specs=in_specs,
            out_specs=per_b(used, KV_DIM),
            scratch_shapes=[pltpu.VMEM((n_cmp, SAMPLE_LANES), F32),
                            pltpu.VMEM((n_rows, SAMPLE_LANES), F32),
                            pltpu.VMEM((128, SAMPLE_LANES), F32),
                            pltpu.VMEM((used, past + NEW_PAD), F32)]),
        out_shape=jax.ShapeDtypeStruct((b, used, KV_DIM), F32),
        compiler_params=_cparams("arbitrary"),
        name="nsa_sample_attention",
    )(page_table, qbd, qrow, g_rows, k_cmp.reshape(b, n_cmp, KV_DIM), v_cmp.reshape(b, n_cmp, KV_DIM),
      *([cache_ks] * n_pages),
      *([cache_vs] * n_pages), pad_rows(ksl), pad_rows(vsl), win_k, win_v, pad_rows(kw), pad_rows(vw), *tables)
    o = out.reshape(b, GROUP, N_KV, t, N_KV, HEAD_DIM)
    o = jnp.einsum("bgktkd->btkgd", o)
    return o.reshape(b * t, Q_DIM)


def _compress_weights(w_c, pe_c):
    eye = jnp.eye(N_KV, dtype=w_c.dtype)
    w_big = jnp.einsum("jde,kc->jkdce", w_c, eye).reshape(CMP_BLK * KV_DIM, KV_DIM)
    pe_flat = jnp.broadcast_to(pe_c[:, None, :], (CMP_BLK, N_KV, HEAD_DIM)).reshape(1, CMP_BLK * KV_DIM)
    pe_rows = jnp.pad(pe_flat, ((0, 7), (0, 0)))
    bias = _matmul(pe_rows, w_big, name="compress_pe")[0:1]
    return w_big, bias


def _compress(x_rows, w_big, bias, tm):
    return _matmul(x_rows, w_big, bias, tm=tm, name="compress")


_BLOCK_PITCH = CMP_BLK + 4


def _compress_pages_kernel(pt_ref, *refs, n_pages):
    del pt_ref
    page_refs, (w_ref, b_ref, o_ref, rows_ref) = refs[:n_pages], refs[n_pages:]
    per_page = PAGE_SIZE // CMP_BLK
    for p in range(n_pages):
        xt = page_refs[p][...].reshape(KV_DIM, PAGE_SIZE).T
        for n in range(per_page):
            lo = (p * per_page + n) * _BLOCK_PITCH
            for h in range(2):
                rows_ref[h, lo:lo + CMP_BLK, :] = xt[n * CMP_BLK:(n + 1) * CMP_BLK, h * 128:(h + 1) * 128]
    n_out = n_pages * per_page
    acc = jnp.broadcast_to(b_ref[...], (n_out, KV_DIM))
    for j in range(CMP_BLK):
        for h in range(2):
            piece = rows_ref[h, pl.ds(j, n_out, stride=_BLOCK_PITCH), :].astype(BF16)
            lo = j * KV_DIM + h * 128
            acc = acc + jnp.dot(piece, w_ref[lo:lo + 128, :], preferred_element_type=F32)
    o_ref[...] = acc


def _compress_pages(cache_t, page_table, w_big, bias):
    assert KV_DIM == 256
    b, per_b = page_table.shape
    group = max(1, 32 // per_b)
    assert b % group == 0
    n_pages = group * per_b
    n_out = n_pages * (PAGE_SIZE // CMP_BLK)

    def page(p):
        return pl.BlockSpec((None, N_KV, HEAD_DIM, PAGE_SIZE),
                            lambda i, pt, p=p: (pt[i * group + p // per_b, p % per_b], 0, 0, 0))

    return pl.pallas_call(
        functools.partial(_compress_pages_kernel, n_pages=n_pages),
        grid_spec=pltpu.PrefetchScalarGridSpec(
            num_scalar_prefetch=1,
            grid=(b // group,),
            in_specs=[page(p) for p in range(n_pages)]
            + [pl.BlockSpec((CMP_BLK * KV_DIM, KV_DIM), lambda i, pt: (0, 0), pipeline_mode=pl.Buffered(1)),
               pl.BlockSpec((1, KV_DIM), lambda i, pt: (0, 0))],
            out_specs=pl.BlockSpec((n_out, KV_DIM), lambda i, pt: (i, 0)),
            scratch_shapes=[pltpu.VMEM((2, n_out * _BLOCK_PITCH, 128), F32)]),
        out_shape=jax.ShapeDtypeStruct((b * per_b * (PAGE_SIZE // CMP_BLK), KV_DIM), F32),
        compiler_params=_cparams("arbitrary"),
        name="compress_pages",
    )(page_table, *([cache_t] * n_pages), w_big.astype(BF16), bias)


def _ada(c_all, ada_w, ada_b):
    mods = []
    for i in range(ada_w.shape[0]):
        mods.append(_matmul(c_all, ada_w[i], ada_b[i].reshape(1, -1), silu_in=True, tn=2 * D_MODEL,
                            name="ada_modulate"))
    return mods


def kernel(x_prompt, x_sample, c_prompt, c_sample, cache_k_cmp, cache_v_cmp, cache_k_sel, cache_v_sel,
           page_table, state_k_win, state_v_win, state_pool, rel_bias, ada_w, ada_b, norm_g, final_g,
           nsa_w_in, nsa_w_out, cmp_wk, cmp_wv, cmp_pe_k, cmp_pe_v, pool_w, pool_scale,
           ffn_wg, ffn_wu, ffn_wd):
    bp, tp, _ = x_prompt.shape
    bs, ts, _ = x_sample.shape
    n_phys = cache_k_cmp.shape[1]
    past = page_table.shape[1] * PAGE_SIZE
    depth = ada_w.shape[0]
    assert depth == 2 and tp % KEY_TILE == 0 and ts <= 8 and past % PAGE_SIZE == 0

    n_c = bp + bs
    c_all = jnp.pad(jnp.concatenate([c_prompt, c_sample], axis=0), ((0, -n_c % 8), (0, 0)))
    mods = _ada(c_all, ada_w, ada_b)

    def mod_prompt(i):
        return [v[:bp].reshape(bp, 1, D_MODEL) for v in jnp.split(mods[i], 6, axis=-1)]

    def mod_sample(i):
        return [jnp.repeat(v[bp:n_c], ts, axis=0) for v in jnp.split(mods[i], 6, axis=-1)]

    wk_big, k_bias = _compress_weights(cmp_wk[0], cmp_pe_k[0])
    wv_big, v_bias = _compress_weights(cmp_wv[0], cmp_pe_v[0])

    mp = bp * tp
    x = x_prompt.reshape(mp, D_MODEL)
    sh1, sc1, g1, sh2, sc2, g2 = mod_prompt(0)
    (kc, vc), kv_t, layouts = _nsa_project(x, norm_g[0, 0], sc1, sh1, nsa_w_in[0], tp, 512, True)
    blk_rows = lambda a: a.reshape(mp // CMP_BLK, CMP_BLK * KV_DIM)
    k_cmp = _compress(blk_rows(kc), wk_big, k_bias, 256)
    v_cmp = _compress(blk_rows(vc), wv_big, v_bias, 256)
    o_t = _prompt_attention(layouts, k_cmp, v_cmp, rel_bias, bp, tp)
    x = _matmul_residual_t(o_t, nsa_w_out[0], x, g1, tp, 512)
    x = _ffn(x, norm_g[0, 1], sc2, sh2, g2, ffn_wg[0], ffn_wu[0], ffn_wd[0], final_g, tp, 512, 1408, False)
    sh1, sc1, g1, sh2, sc2, g2 = mod_prompt(1)
    x3, pool_p = _pool_mix(x.reshape(bp, tp, D_MODEL), jnp.zeros((bp, POOL_STATE, D_MODEL), F32),
                           norm_g[1, 0], sc1, sh1, g1, pool_w[0], pool_scale[0], 0, 512)
    y_prompt = _ffn(x3.reshape(mp, D_MODEL), norm_g[1, 1], sc2, sh2, g2, ffn_wg[1], ffn_wu[1], ffn_wd[1],
                    final_g, tp, 512, 1408, True).reshape(bp, tp, D_MODEL)
    win = min(WINDOW, tp)
    st5 = lambda a, t0: jnp.transpose(a[:, :, t0:].reshape(bp, N_KV, HEAD_DIM, tp - t0), (0, 3, 1, 2))[None]
    prompt_states = tuple(st5(a, 0) for a in kv_t[:4]) + tuple(st5(a, tp - win) for a in kv_t[4:]) + (pool_p[None],)

    ms = bs * ts
    x = x_sample.reshape(ms, D_MODEL)
    sh1, sc1, g1, sh2, sc2, g2 = mod_sample(0)
    (kc, vc, ksl, vsl, kw, vw), (q, gates) = _nsa_project(x, norm_g[0, 0], sc1, sh1, nsa_w_in[0], ts, 512, False)
    assert (past + ts) // CMP_BLK == past // CMP_BLK
    pos_last = lambda a: jnp.transpose(a[0], (0, 2, 3, 1))
    k_cmp_s = _compress_pages(pos_last(cache_k_cmp), page_table, wk_big, k_bias)
    v_cmp_s = _compress_pages(pos_last(cache_v_cmp), page_table, wv_big, v_bias)
    o = _sample_attention(q, gates, k_cmp_s, v_cmp_s, pos_last(cache_k_sel), pos_last(cache_v_sel),
                          page_table, ksl, vsl, pos_last(state_k_win), pos_last(state_v_win), kw, vw,
                          rel_bias, bs, ts)
    x = _matmul_residual(o, nsa_w_out[0], x, g1, ts, 512)
    x = _ffn(x, norm_g[0, 1], sc2, sh2, g2, ffn_wg[0], ffn_wu[0], ffn_wd[0], final_g, ts, 512, 1408, False)
    sh1, sc1, g1, sh2, sc2, g2 = mod_sample(1)
    b3 = lambda v: v.reshape(bs, ts, D_MODEL)[:, :1]
    x3, pool_s = _pool_mix(x.reshape(bs, ts, D_MODEL), state_pool[0], norm_g[1, 0], b3(sc1), b3(sh1), b3(g1),
                           pool_w[0], pool_scale[0], past, 512)
    y_sample = _ffn(x3.reshape(ms, D_MODEL), norm_g[1, 1], sc2, sh2, g2, ffn_wg[1], ffn_wu[1], ffn_wd[1],
                    final_g, ts, 512, 1408, True).reshape(bs, ts, D_MODEL)
    st5 = lambda a: a.reshape(1, bs, ts, N_KV, HEAD_DIM)
    kw_ext = jnp.concatenate([state_k_win[0], st5(kw)[0]], axis=1)[:, ts:]
    vw_ext = jnp.concatenate([state_v_win[0], st5(vw)[0]], axis=1)[:, ts:]
    sample_states = (st5(kc), st5(vc), st5(ksl), st5(vsl), kw_ext[None], vw_ext[None], pool_s[None])

    return (y_prompt, y_sample) + prompt_states + sample_states
```

```python
import functools
import math

import jax
import jax.numpy as jnp
from jax import lax
from jax.experimental import pallas as pl
from jax.experimental.pallas import tpu as pltpu

D_MODEL = 1024
N_HEADS = 16
N_KV = 4
GROUP = N_HEADS // N_KV
HEAD_DIM = 64
Q_DIM = N_HEADS * HEAD_DIM
KV_DIM = N_KV * HEAD_DIM
CMP_BLK = 32
SEL_BLK = 64
N_SEL = 16
WINDOW = 512
Q_CHUNK = 64
N_BUCKETS = 32
MAX_DISTANCE = 128
POOL_WINDOWS = (2, 4, 8, 16)
POOL_GROUP_DIM = D_MODEL // len(POOL_WINDOWS)
POOL_STATE = max(POOL_WINDOWS) - 1
PAGE_SIZE = 128
RMS_EPS = 1e-6
NEG = -1e30
BIG = 1e9

KEY_TILE = 128
CHUNK = 128
LANES = GROUP * CHUNK
LOG2E = math.log2(math.e)
Q_SCALE_LOG2 = HEAD_DIM ** -0.5 * LOG2E
SAMPLE_LANES = 128
VMEM_LIMIT_BYTES = 48 * 1024 * 1024

F32 = jnp.float32
BF16 = jnp.bfloat16


def _cparams(*sem):
    return pltpu.CompilerParams(dimension_semantics=sem, vmem_limit_bytes=VMEM_LIMIT_BYTES)


def _silu(x):
    return x * (1.0 / (1.0 + jnp.exp(-x)))


def _sigmoid(x):
    return 1.0 / (1.0 + jnp.exp(-x))


def _normmod(x, g, sc, sh):
    ms = jnp.mean(x * x, axis=-1, keepdims=True)
    return (x * lax.rsqrt(ms + RMS_EPS) * g) * (1.0 + sc) + sh


def _mod_spec(mod, tm, rows_per_batch):
    if mod.ndim == 3:
        return pl.BlockSpec((None, 1, D_MODEL), lambda i, *_: ((i * tm) // rows_per_batch, 0, 0))
    return pl.BlockSpec((tm, D_MODEL), lambda i, *_: (i, 0))


def _mm_kernel(a_ref, w_ref, b_ref, o_ref, *, silu_in):
    a = a_ref[...]
    if silu_in:
        a = _silu(a)
    o_ref[...] = jnp.dot(a.astype(BF16), w_ref[...], preferred_element_type=F32) + b_ref[...]


def _matmul(a, w, bias=None, *, silu_in=False, tm=256, tn=None, name="matmul"):
    m, k = a.shape
    n = w.shape[1]
    tm = min(tm, m)
    tn = n if tn is None else tn
    assert m % tm == 0 and n % tn == 0
    if bias is None:
        bias = jnp.zeros((1, n), F32)
    return pl.pallas_call(
        functools.partial(_mm_kernel, silu_in=silu_in),
        grid=(m // tm, n // tn),
        in_specs=[pl.BlockSpec((tm, k), lambda i, j: (i, 0)),
                  pl.BlockSpec((k, tn), lambda i, j: (0, j)),
                  pl.BlockSpec((1, tn), lambda i, j: (0, j))],
        out_specs=pl.BlockSpec((tm, tn), lambda i, j: (i, j)),
        out_shape=jax.ShapeDtypeStruct((m, n), F32),
        compiler_params=_cparams("parallel", "parallel"),
        name=name,
    )(a, w.astype(BF16), bias)


_PROJ_MAIN = Q_DIM + 6 * KV_DIM
_GATE_PAD = 128


def _proj_kernel(x_ref, g_ref, sc_ref, sh_ref, w_ref, wg_ref, *out_refs, attn_layouts, tm):
    hb = _normmod(x_ref[...], g_ref[...], sc_ref[...], sh_ref[...]).astype(BF16)
    q = jnp.dot(hb, w_ref[:, :Q_DIM], preferred_element_type=F32)
    gates = jnp.dot(hb, wg_ref[...], preferred_element_type=F32)
    kvs = []
    for n in range(6):
        lo = Q_DIM + n * KV_DIM
        kvs.append(jnp.dot(hb, w_ref[:, lo:lo + KV_DIM], preferred_element_type=F32))
    if not attn_layouts:
        for n in range(6):
            out_refs[n][...] = kvs[n]
        out_refs[6][...] = q
        out_refs[7][...] = gates
        return
    kt_refs = out_refs[2:8]
    qt_ref, gt_ref, ks_ref, vst_ref, kw_ref, vwt_ref, q_scr, gate_scr, kv_scr = out_refs[8:]
    out_refs[0][...] = kvs[0]
    out_refs[1][...] = kvs[1]
    kvts = []
    for n in range(6):
        kv_scr[n] = kvs[n]
        kvts.append(kv_scr[n].T)
        kt_refs[n][...] = kvts[n]
    n_chunks = tm // CHUNK
    q_scr[...] = q * Q_SCALE_LOG2
    gate_scr[...] = gates
    qt = q_scr[...].T
    gt = gate_scr[...].T
    for c in range(n_chunks):
        tok = slice(c * CHUNK, (c + 1) * CHUNK)
        for kv in range(N_KV):
            for gi in range(GROUP):
                h = kv * GROUP + gi
                lanes = slice(gi * CHUNK, (gi + 1) * CHUNK)
                qt_ref[c, kv, :, lanes] = qt[h * HEAD_DIM:(h + 1) * HEAD_DIM, tok].astype(BF16)
                gt_ref[c, kv, :, lanes] = gt[3 * h:3 * h + 3, tok]
    for k_ref, vt_ref, k, vt in ((ks_ref, vst_ref, kvs[2], kvts[3]), (kw_ref, vwt_ref, kvs[4], kvts[5])):
        for kv in range(N_KV):
            cols = slice(kv * HEAD_DIM, (kv + 1) * HEAD_DIM)
            k_ref[kv] = k[:, cols].astype(BF16)
            for c in range(n_chunks):
                vt_ref[kv, c, 0:HEAD_DIM, :] = vt[cols, c * CHUNK:(c + 1) * CHUNK].astype(BF16)
                vt_ref[kv, c, HEAD_DIM:, :] = jnp.ones((V_ROWS - HEAD_DIM, CHUNK), BF16)


def _nsa_project(x, g, sc, sh, w_in, rows_per_batch, tm, attn_layouts):
    m = x.shape[0]
    tm = min(tm, m)
    w_main = w_in[:, :_PROJ_MAIN].astype(BF16)
    w_gate = jnp.pad(w_in[:, _PROJ_MAIN:], ((0, 0), (0, _GATE_PAD - 3 * N_HEADS))).astype(BF16)
    row = lambda n: pl.BlockSpec((tm, n), lambda i: (i, 0))
    if attn_layouts:
        assert tm % CHUNK == 0 and rows_per_batch % tm == 0
        nck = tm // CHUNK
        tpb = rows_per_batch // tm
        out_specs = [row(KV_DIM)] * 2 + [pl.BlockSpec((None, KV_DIM, tm), lambda i: (i // tpb, 0, i % tpb))] * 6
        out_shape = ([jax.ShapeDtypeStruct((m, KV_DIM), F32)] * 2
                     + [jax.ShapeDtypeStruct((m // rows_per_batch, KV_DIM, rows_per_batch), F32)] * 6)
        chunked = lambda *s: pl.BlockSpec((nck,) + s, lambda i: (i,) + (0,) * len(s))
        k_spec = pl.BlockSpec((N_KV, tm, HEAD_DIM), lambda i: (0, i, 0))
        vt_spec = pl.BlockSpec((N_KV, nck, V_ROWS, CHUNK), lambda i: (0, i, 0, 0))
        k_shape = jax.ShapeDtypeStruct((N_KV, m, HEAD_DIM), BF16)
        vt_shape = jax.ShapeDtypeStruct((N_KV, m // CHUNK, V_ROWS, CHUNK), BF16)
        out_specs += [chunked(N_KV, HEAD_DIM, LANES), chunked(N_KV, 3, LANES), k_spec, vt_spec, k_spec, vt_spec]
        out_shape += [jax.ShapeDtypeStruct((m // CHUNK, N_KV, HEAD_DIM, LANES), BF16),
                      jax.ShapeDtypeStruct((m // CHUNK, N_KV, 3, LANES), F32),
                      k_shape, vt_shape, k_shape, vt_shape]
    else:
        out_specs = [row(KV_DIM)] * 6 + [row(Q_DIM), row(_GATE_PAD)]
        out_shape = ([jax.ShapeDtypeStruct((m, KV_DIM), F32)] * 6
                     + [jax.ShapeDtypeStruct((m, Q_DIM), F32), jax.ShapeDtypeStruct((m, _GATE_PAD), F32)])
    outs = pl.pallas_call(
        functools.partial(_proj_kernel, attn_layouts=attn_layouts, tm=tm),
        grid=(m // tm,),
        in_specs=[row(D_MODEL),
                  pl.BlockSpec((1, D_MODEL), lambda i: (0, 0)),
                  _mod_spec(sc, tm, rows_per_batch), _mod_spec(sh, tm, rows_per_batch),
                  pl.BlockSpec((D_MODEL, _PROJ_MAIN), lambda i: (0, 0)),
                  pl.BlockSpec((D_MODEL, _GATE_PAD), lambda i: (0, 0))],
        out_specs=out_specs,
        out_shape=out_shape,
        scratch_shapes=([pltpu.VMEM((tm, Q_DIM), F32), pltpu.VMEM((tm, _GATE_PAD), F32),
                         pltpu.VMEM((6, tm, KV_DIM), F32)] if attn_layouts else []),
        compiler_params=_cparams("parallel"),
        name="nsa_project",
    )(x, g.reshape(1, D_MODEL), sc, sh, w_main, w_gate)
    if attn_layouts:
        return outs[:2], outs[2:8], outs[8:]
    return outs[:6], outs[6:]


def _mm_res_kernel(a_ref, w_ref, x_ref, gate_ref, o_ref):
    y = jnp.dot(a_ref[...].astype(BF16), w_ref[...], preferred_element_type=F32)
    o_ref[...] = x_ref[...] + gate_ref[...] * y


def _matmul_residual(a, w, x, gate, rows_per_batch, tm):
    m, k = a.shape
    tm = min(tm, m)
    return pl.pallas_call(
        _mm_res_kernel,
        grid=(m // tm,),
        in_specs=[pl.BlockSpec((tm, k), lambda i: (i, 0)),
                  pl.BlockSpec((k, D_MODEL), lambda i: (0, 0)),
                  pl.BlockSpec((tm, D_MODEL), lambda i: (i, 0)),
                  _mod_spec(gate, tm, rows_per_batch)],
        out_specs=pl.BlockSpec((tm, D_MODEL), lambda i: (i, 0)),
        out_shape=jax.ShapeDtypeStruct((m, D_MODEL), F32),
        compiler_params=_cparams("parallel"),
        name="out_proj_residual",
    )(a, w.astype(BF16), x, gate)


def _mm_res_t_kernel(ot_ref, w_ref, x_ref, gate_ref, o_ref, *, n_chunks):
    chunks = []
    for c in range(n_chunks):
        rows = [ot_ref[c, kv, :, gi * CHUNK:(gi + 1) * CHUNK] for kv in range(N_KV) for gi in range(GROUP)]
        chunks.append(jnp.concatenate(rows, axis=0).astype(F32).T)
    a = jnp.concatenate(chunks, axis=0).astype(BF16)
    y = jnp.dot(a, w_ref[...], preferred_element_type=F32)
    o_ref[...] = x_ref[...] + gate_ref[...] * y


def _matmul_residual_t(o_t, w, x, gate, rows_per_batch, tm):
    m = x.shape[0]
    nck = tm // CHUNK
    return pl.pallas_call(
        functools.partial(_mm_res_t_kernel, n_chunks=nck),
        grid=(m // tm,),
        in_specs=[pl.BlockSpec((nck, N_KV, HEAD_DIM, LANES), lambda i: (i, 0, 0, 0)),
                  pl.BlockSpec((Q_DIM, D_MODEL), lambda i: (0, 0)),
                  pl.BlockSpec((tm, D_MODEL), lambda i: (i, 0)),
                  _mod_spec(gate, tm, rows_per_batch)],
        out_specs=pl.BlockSpec((tm, D_MODEL), lambda i: (i, 0)),
        out_shape=jax.ShapeDtypeStruct((m, D_MODEL), F32),
        compiler_params=_cparams("parallel"),
        name="out_proj_residual_t",
    )(o_t, w.astype(BF16), x, gate)


def _ffn_kernel(x_ref, g_ref, sc_ref, sh_ref, gate_ref, wg_ref, wu_ref, wd_ref, fg_ref, o_ref,
                *, final_norm, tf):
    x = x_ref[...]
    hb = _normmod(x, g_ref[...], sc_ref[...], sh_ref[...]).astype(BF16)
    acc = None
    for lo in range(0, wg_ref.shape[1], tf):
        a = jnp.dot(hb, wg_ref[:, lo:lo + tf], preferred_element_type=F32)
        u = jnp.dot(hb, wu_ref[:, lo:lo + tf], preferred_element_type=F32)
        act = (_silu(a) * u).astype(BF16)
        part = jnp.dot(act, wd_ref[lo:lo + tf, :], preferred_element_type=F32)
        acc = part if acc is None else acc + part
    y = x + gate_ref[...] * acc
    if final_norm:
        ms = jnp.mean(y * y, axis=-1, keepdims=True)
        y = y * lax.rsqrt(ms + RMS_EPS) * fg_ref[...]
    o_ref[...] = y


def _ffn(x, g, sc, sh, gate, wg, wu, wd, final_g, rows_per_batch, tm, tf, final_norm):
    m = x.shape[0]
    d_ff = wg.shape[1]
    tm = min(tm, m)
    assert d_ff % tf == 0
    once = pl.Buffered(1)
    vec = pl.BlockSpec((1, D_MODEL), lambda i: (0, 0))
    return pl.pallas_call(
        functools.partial(_ffn_kernel, final_norm=final_norm, tf=tf),
        grid=(m // tm,),
        in_specs=[pl.BlockSpec((tm, D_MODEL), lambda i: (i, 0)),
                  vec,
                  _mod_spec(sc, tm, rows_per_batch), _mod_spec(sh, tm, rows_per_batch),
                  _mod_spec(gate, tm, rows_per_batch),
                  pl.BlockSpec((D_MODEL, d_ff), lambda i: (0, 0), pipeline_mode=once),
                  pl.BlockSpec((D_MODEL, d_ff), lambda i: (0, 0), pipeline_mode=once),
                  pl.BlockSpec((d_ff, D_MODEL), lambda i: (0, 0), pipeline_mode=once),
                  vec],
        out_specs=pl.BlockSpec((tm, D_MODEL), lambda i: (i, 0)),
        out_shape=jax.ShapeDtypeStruct((m, D_MODEL), F32),
        compiler_params=_cparams("parallel"),
        name="ffn",
    )(x, g.reshape(1, D_MODEL), sc, sh, gate, wg.astype(BF16), wu.astype(BF16), wd.astype(BF16),
      final_g.reshape(1, D_MODEL))


_POOL_HALO = 16


def _pool_kernel(x_ref, xprev_ref, state_ref, g_ref, sc_ref, sh_ref, gate_ref, w_ref, ls_ref,
                 o_ref, st_ref, ext_ref, *, tm, pos0):
    i = pl.program_id(1)
    g, sc, sh = g_ref[...], sc_ref[...], sh_ref[...]
    u = _normmod(x_ref[...], g, sc, sh)
    prev = jnp.where(i == 0, state_ref[...], _normmod(xprev_ref[...], g, sc, sh))
    ext_ref[0:_POOL_HALO, :] = prev
    ext_ref[_POOL_HALO:_POOL_HALO + tm, :] = u
    st_ref[...] = ext_ref[tm:tm + _POOL_HALO, :]

    pos = pos0 + i * tm + lax.broadcasted_iota(jnp.int32, (tm, 1), 0)
    mixed = []
    for gi, w in enumerate(POOL_WINDOWS):
        lo = gi * POOL_GROUP_DIM
        s = u[:, lo:lo + POOL_GROUP_DIM]
        for k in range(1, w):
            s = s + ext_ref[_POOL_HALO - k:_POOL_HALO - k + tm, lo:lo + POOL_GROUP_DIM]
        cnt = jnp.minimum(pos + 1, w).astype(F32)
        pooled = s / cnt - u[:, lo:lo + POOL_GROUP_DIM]
        mixed.append(jnp.dot(pooled.astype(BF16), w_ref[gi], preferred_element_type=F32))
    y = jnp.concatenate(mixed, axis=-1) * ls_ref[...]
    o_ref[...] = x_ref[...] + gate_ref[...] * y


def _pool_mix(x3, state, g, sc, sh, gate, w_grp, layer_scale, pos0, tm):
    b, t, _ = x3.shape
    tm = min(tm, t)
    state16 = jnp.pad(state, ((0, 0), (_POOL_HALO - POOL_STATE, 0), (0, 0)))
    if t >= _POOL_HALO:
        xprev = x3
        nprev = tm // _POOL_HALO
        prev_spec = pl.BlockSpec((None, _POOL_HALO, D_MODEL),
                                 lambda bi, i: (bi, jnp.maximum(i * nprev - 1, 0), 0))
    else:
        xprev = state16
        prev_spec = pl.BlockSpec((None, _POOL_HALO, D_MODEL), lambda bi, i: (bi, 0, 0))
    vec = pl.BlockSpec((1, D_MODEL), lambda bi, i: (0, 0))
    bvec = pl.BlockSpec((None, 1, D_MODEL), lambda bi, i: (bi, 0, 0))
    out, st = pl.pallas_call(
        functools.partial(_pool_kernel, tm=tm, pos0=pos0),
        grid=(b, t // tm),
        in_specs=[pl.BlockSpec((None, tm, D_MODEL), lambda bi, i: (bi, i, 0)),
                  prev_spec,
                  pl.BlockSpec((None, _POOL_HALO, D_MODEL), lambda bi, i: (bi, 0, 0)),
                  vec, bvec, bvec, bvec,
                  pl.BlockSpec((len(POOL_WINDOWS), POOL_GROUP_DIM, POOL_GROUP_DIM),
                               lambda bi, i: (0, 0, 0)),
                  vec],
        out_specs=[pl.BlockSpec((None, tm, D_MODEL), lambda bi, i: (bi, i, 0)),
                   pl.BlockSpec((None, _POOL_HALO, D_MODEL), lambda bi, i: (bi, 0, 0))],
        out_shape=[jax.ShapeDtypeStruct((b, t, D_MODEL), F32),
                   jax.ShapeDtypeStruct((b, _POOL_HALO, D_MODEL), F32)],
        scratch_shapes=[pltpu.VMEM((tm + _POOL_HALO, D_MODEL), F32)],
        compiler_params=_cparams("parallel", "arbitrary"),
        name="pool_mix",
    )(x3, xprev, state16, g.reshape(1, D_MODEL), sc, sh, gate, w_grp.astype(BF16),
      layer_scale.reshape(1, D_MODEL))
    return out, st[:, _POOL_HALO - POOL_STATE:]


def _rel_bucket(dist):
    d = jnp.maximum(dist, 0)
    max_exact = N_BUCKETS // 2
    large = max_exact + (jnp.log(jnp.maximum(d, 1).astype(F32) / max_exact)
                         / math.log(MAX_DISTANCE / max_exact) * (N_BUCKETS - max_exact)).astype(jnp.int32)
    large = jnp.minimum(large, N_BUCKETS - 1)
    return jnp.where(d < max_exact, d, large)


def _dist_bias(rel_bias):
    return rel_bias[_rel_bucket(jnp.arange(MAX_DISTANCE + 1, dtype=jnp.int32))]


def _lookup(table, idx):
    onehot = (idx[..., None] == jnp.arange(table.shape[0], dtype=jnp.int32)).astype(F32)
    return jnp.einsum("...d,dh->...h", onehot, table, precision=lax.Precision.HIGHEST)


def _bias_tile(fd, dist, valid):
    k, q = dist.shape
    bias = _lookup(fd, jnp.clip(dist, 0, MAX_DISTANCE)).reshape(k, q, N_KV, GROUP)
    bias = jnp.where(valid[:, :, None, None], bias, NEG)
    return bias.transpose(2, 0, 3, 1).reshape(N_KV, k, GROUP * q)


def _prompt_tables(rel_bias, t):
    fd = _dist_bias(rel_bias) * LOG2E
    kj = jnp.arange(KEY_TILE, dtype=jnp.int32)[:, None]
    qi = jnp.arange(CHUNK, dtype=jnp.int32)[None, :]
    dist = qi - kj
    near = jnp.stack([_bias_tile(fd, dist + KEY_TILE, dist + KEY_TILE >= 0),
                      _bias_tile(fd, dist, dist >= 0)])
    wfirst = _bias_tile(fd, dist + WINDOW, dist < 0)
    far = _bias_tile(fd, jnp.full((1, CHUNK), MAX_DISTANCE, jnp.int32), jnp.ones((1, CHUNK), bool))
    per_chunk = CHUNK // CMP_BLK
    assert CHUNK == 128 and CMP_BLK == 32 and MAX_DISTANCE == 128
    rel0, n_rel = -per_chunk, 2 * per_chunk
    rel = rel0 + jnp.arange(n_rel, dtype=jnp.int32)[:, None]
    dist = qi - CMP_BLK * rel - (CMP_BLK - 1)
    crel = _bias_tile(fd, dist, dist >= 0)
    relm = (jnp.arange(t // CMP_BLK, dtype=jnp.int32)[None, :]
            - per_chunk * jnp.arange(t // CHUNK, dtype=jnp.int32)[:, None])[:, None, :, None]
    cmp_tab = jnp.where(relm >= rel0 + n_rel, NEG, far[None])
    for r in range(n_rel):
        cmp_tab = jnp.where(relm == rel0 + r, crel[None, :, r:r + 1, :], cmp_tab)
    return near, wfirst, far, cmp_tab


V_ROWS = HEAD_DIM + 16


def _half_exp2(s, m, top, bot):
    return jnp.exp2(jnp.concatenate([s[:SEL_BLK] - (m - top), s[SEL_BLK:] - (m - bot)], axis=0))


def _normalise(acc):
    return acc[:HEAD_DIM] * (1.0 / jnp.maximum(acc[HEAD_DIM:HEAD_DIM + 1], 1e-30))


def _prompt_attn_kernel(q_ref, g_ref, kc_ref, vct_ref, cb_ref, ks_ref, vst_ref, kw_ref, vwt_ref,
                        near_ref, wfirst_ref, far_ref, o_ref,
                        imp_ref, val_ref, rank_ref, selrep_ref, m_ref, mw_ref, alpha_ref, acc_ref, accw_ref,
                        mix_ref, s_ref, smax_ref, p_ref, *, n_blk, n_sel):
    i = pl.program_id(1)

    def k_rows(ref, kv, jt):
        return ref[kv, pl.ds(pl.multiple_of(jt * KEY_TILE, KEY_TILE), KEY_TILE), :]

    def sel_rows(kv, jt):
        return selrep_ref[kv, pl.ds(2 * jt, 1), :], selrep_ref[kv, pl.ds(2 * jt + 1, 1), :]

    for kv in range(N_KV):
        bias = cb_ref[kv]
        s = jnp.dot(kc_ref[kv], q_ref[kv], preferred_element_type=F32) + bias
        valid = bias > 0.5 * NEG
        m = jnp.max(s, axis=0, keepdims=True)
        p = jnp.where(valid, jnp.exp2(s - m), 0.0)
        p = p * (1.0 / jnp.maximum(jnp.sum(p, axis=0, keepdims=True), 1e-30))
        o_c = jnp.dot(vct_ref[kv], p.astype(BF16), preferred_element_type=F32)
        mix_ref[kv] = _sigmoid(g_ref[kv, 0:1, :]) * o_c
        imp_ref[kv] = functools.reduce(
            lambda a, b: a + b, [p[:, gi * CHUNK:(gi + 1) * CHUNK] for gi in range(GROUP)])

    imp = jnp.concatenate(
        [imp_ref[kv, pl.ds(0, n_blk, stride=2), :] + imp_ref[kv, pl.ds(1, n_blk, stride=2), :]
         for kv in range(N_KV)], axis=1)
    blk = lax.broadcasted_iota(jnp.int32, (n_blk, LANES), 0)
    lane = lax.broadcasted_iota(jnp.int32, (1, LANES), 1)
    assert CHUNK == 2 * SEL_BLK
    qblk = 2 * i + jnp.where((lane & (CHUNK - 1)) >= SEL_BLK, 1, 0)
    forced = (blk == 0) | (blk == qblk) | (blk == qblk - 1)
    val_ref[...] = jnp.where(forced, BIG, jnp.where(blk <= qblk, imp, -1.0))
    rank_ref[...] = jnp.zeros_like(rank_ref)
    sub = lax.broadcasted_iota(jnp.int32, (8, LANES), 0)
    n_grp = n_blk // 8
    for grp in range(n_grp):
        @pl.when(8 * grp <= 2 * i + 1)
        def _(grp=grp):
            for r8 in range(n_grp):
                piece = val_ref[r8 * 8:(r8 + 1) * 8, :]
                acc = rank_ref[r8 * 8:(r8 + 1) * 8, :]
                for j in range(grp * 8, grp * 8 + 8):
                    row = val_ref[j:j + 1, :]
                    if r8 > grp:
                        ahead = row >= piece
                    elif r8 < grp:
                        ahead = row > piece
                    else:
                        acc = acc + jnp.where(sub > j - grp * 8, jnp.where(row >= piece, 1.0, 0.0),
                                              jnp.where(row > piece, 1.0, 0.0))
                        continue
                    acc = acc + jnp.where(ahead, 1.0, 0.0)
                rank_ref[r8 * 8:(r8 + 1) * 8, :] = acc
    selneg = jnp.where(rank_ref[...] < n_sel, 0.0, NEG)
    for kv in range(N_KV):
        selrep_ref[kv] = jnp.concatenate([selneg[:, kv * CHUNK:(kv + 1) * CHUNK]] * GROUP, axis=1)

    prev = jnp.maximum(i - 1, 0)
    n_far = jnp.maximum(i - 1, 0)
    last = jnp.maximum(n_far - 1, 0)
    zero_row = jnp.zeros((1, LANES), F32)
    for state in (mw_ref, m_ref):
        state[...] = jnp.full(state.shape, NEG, F32)
    for state in (accw_ref, acc_ref):
        state[...] = jnp.zeros(state.shape, F32)

    def scores(k_ref, jt, slot, table=None):
        for kv in range(N_KV):
            s = jnp.dot(k_rows(k_ref, kv, jt), q_ref[kv], preferred_element_type=F32)
            if table is not None:
                s = s + table(kv)
            s_ref[slot, kv] = s
            smax_ref[slot, kv, 0:1, :] = jnp.max(s[:SEL_BLK], axis=0, keepdims=True)
            smax_ref[slot, kv, 1:2, :] = jnp.max(s[SEL_BLK:], axis=0, keepdims=True)

    def softmax(slot, rows, stat_ref):
        for kv in range(N_KV):
            top, bot = rows(kv)
            m_old = stat_ref[kv]
            m_new = jnp.maximum(m_old, jnp.maximum(smax_ref[slot, kv, 0:1, :] + top,
                                                   smax_ref[slot, kv, 1:2, :] + bot))
            stat_ref[kv] = m_new
            alpha_ref[slot, kv] = jnp.exp2(m_old - m_new)
            p_ref[slot, kv] = _half_exp2(s_ref[slot, kv], m_new, top, bot).astype(BF16)

    def values(vt_ref, jt, slot, out_ref):
        for kv in range(N_KV):
            out_ref[kv] = alpha_ref[slot, kv] * out_ref[kv] + jnp.dot(vt_ref[kv, jt], p_ref[slot, kv],
                                                                      preferred_element_type=F32)

    def const_rows(row):
        return lambda kv: (row, row)

    def window_far_rows(back):
        off = jnp.where(i >= back, 0.0, NEG)
        return lambda kv: (far_ref[kv] + off, far_ref[kv] + off)

    def sel_near_rows(jt, off):
        def rows(kv):
            top, bot = sel_rows(kv, jt)
            return top + off, bot + off
        return rows

    prev_off = jnp.where(i >= 1, 0.0, NEG)
    back = lambda n: jnp.maximum(i - n, 0)
    diag_table = lambda kv: near_ref[1, kv]
    prev_table = lambda kv: near_ref[0, kv]
    static_tiles = [
        (kw_ref, vwt_ref, i, diag_table, const_rows(zero_row), mw_ref, accw_ref),
        (kw_ref, vwt_ref, prev, prev_table, const_rows(zero_row + prev_off), mw_ref, accw_ref),
        (kw_ref, vwt_ref, back(2), None, window_far_rows(2), mw_ref, accw_ref),
        (kw_ref, vwt_ref, back(3), None, window_far_rows(3), mw_ref, accw_ref),
        (kw_ref, vwt_ref, back(4), lambda kv: wfirst_ref[kv],
         const_rows(zero_row + jnp.where(i >= 4, 0.0, NEG)), mw_ref, accw_ref),
        (ks_ref, vst_ref, i, diag_table, sel_near_rows(i, 0.0), m_ref, acc_ref),
        (ks_ref, vst_ref, prev, prev_table, sel_near_rows(prev, prev_off), m_ref, acc_ref),
    ]

    def far_tile(jt):
        return jnp.minimum(jt, last)

    def far_rows(jt):
        off = jnp.where(jt < n_far, 0.0, NEG)
        def rows(kv):
            top, bot = sel_rows(kv, far_tile(jt))
            far = far_ref[kv] + off
            return top + far, bot + far
        return rows

    def far_values(jt, slot):
        values(vst_ref, jnp.where(jt < 0, prev, jnp.clip(jt, 0, last)), slot, acc_ref)

    n_static = len(static_tiles)
    k_ref0, _, jt0, table0, _, _, _ = static_tiles[0]
    scores(k_ref0, jt0, 0, table0)
    for t, (_, _, _, _, rows, stat_ref, _) in enumerate(static_tiles):
        if t >= 1:
            _, vt_ref, jt, _, _, _, out_ref = static_tiles[t - 1]
            values(vt_ref, jt, (t - 1) % 2, out_ref)
        softmax(t % 2, rows, stat_ref)
        if t + 1 < n_static:
            k_ref, _, jt, table, _, _, _ = static_tiles[t + 1]
            scores(k_ref, jt, (t + 1) % 2, table)
        else:
            scores(ks_ref, far_tile(0), (t + 1) % 2)
    assert n_static % 2 == 1

    def pair_step(j2):
        jt = 2 * j2
        far_values(jt - 1, 0)
        softmax(1, far_rows(jt), m_ref)
        scores(ks_ref, far_tile(jt + 1), 0)
        far_values(jt, 1)
        softmax(0, far_rows(jt + 1), m_ref)
        scores(ks_ref, far_tile(jt + 2), 1)

    def body(pairs_per_trip):
        def run(j, carry):
            for u in range(pairs_per_trip):
                pair_step(pairs_per_trip * j + u)
            return carry
        return run

    n_pairs = (n_far + 1) // 2
    n4 = n_pairs // 4
    n2 = n_pairs // 2
    lax.fori_loop(0, n4, body(4), 0)
    lax.fori_loop(2 * n4, n2, body(2), 0)
    lax.fori_loop(2 * n2, n_pairs, body(1), 0)
    far_values(2 * n_pairs - 1, 0)

    for kv in range(N_KV):
        o_ref[kv] = (mix_ref[kv] + _sigmoid(g_ref[kv, 2:3, :]) * _normalise(accw_ref[kv])
                     + _sigmoid(g_ref[kv, 1:2, :]) * _normalise(acc_ref[kv])).astype(o_ref.dtype)


def _prompt_attention(layouts, k_cmp, v_cmp, rel_bias, b, t):
    q_t, g_t, ks, vst, kw, vwt = layouts
    nch, nt, nc, n_blk = t // CHUNK, t // KEY_TILE, t // CMP_BLK, t // SEL_BLK
    kc = k_cmp.reshape(b, nc, N_KV, HEAD_DIM).transpose(0, 2, 1, 3).astype(BF16)
    vct = v_cmp.reshape(b, nc, N_KV, HEAD_DIM).transpose(0, 2, 3, 1).astype(BF16)
    near, wfirst, far, cmp_tab = _prompt_tables(rel_bias, t)

    once = pl.Buffered(1)
    per_b = lambda *shape: pl.BlockSpec((None,) + shape, lambda bi, i: (bi,) + (0,) * len(shape),
                                        pipeline_mode=once)
    const = lambda *shape: pl.BlockSpec(shape, lambda bi, i: (0,) * len(shape), pipeline_mode=once)
    chunk = lambda *shape: pl.BlockSpec((None,) + shape, lambda bi, i: (bi * nch + i,) + (0,) * len(shape))
    k_spec = pl.BlockSpec((N_KV, t, HEAD_DIM), lambda bi, i: (0, bi, 0), pipeline_mode=once)
    vt_spec = pl.BlockSpec((N_KV, nt, V_ROWS, KEY_TILE), lambda bi, i: (0, bi, 0, 0), pipeline_mode=once)
    stat = pltpu.VMEM((N_KV, 1, LANES), F32)
    return pl.pallas_call(
        functools.partial(_prompt_attn_kernel, n_blk=n_blk, n_sel=min(N_SEL, n_blk)),
        grid=(b, nch),
        in_specs=[chunk(N_KV, HEAD_DIM, LANES),
                  chunk(N_KV, 3, LANES),
                  per_b(N_KV, nc, HEAD_DIM),
                  per_b(N_KV, HEAD_DIM, nc),
                  pl.BlockSpec((None, N_KV, nc, LANES), lambda bi, i: (i, 0, 0, 0)),
                  k_spec, vt_spec, k_spec, vt_spec,
                  const(2, N_KV, KEY_TILE, LANES),
                  const(N_KV, KEY_TILE, LANES),
                  const(N_KV, 1, LANES)],
        out_specs=chunk(N_KV, HEAD_DIM, LANES),
        out_shape=jax.ShapeDtypeStruct((b * nch, N_KV, HEAD_DIM, LANES), BF16),
        scratch_shapes=[pltpu.VMEM((N_KV, nc, CHUNK), F32),
                        pltpu.VMEM((n_blk, LANES), F32),
                        pltpu.VMEM((n_blk, LANES), F32),
                        pltpu.VMEM((N_KV, n_blk, LANES), F32),
                        stat, stat,
                        pltpu.VMEM((2, N_KV, 1, LANES), F32),
                        pltpu.VMEM((N_KV, V_ROWS, LANES), F32),
                        pltpu.VMEM((N_KV, V_ROWS, LANES), F32),
                        pltpu.VMEM((N_KV, HEAD_DIM, LANES), F32),
                        pltpu.VMEM((2, N_KV, KEY_TILE, LANES), F32),
                        pltpu.VMEM((2, N_KV, 2, LANES), F32),
                        pltpu.VMEM((2, N_KV, KEY_TILE, LANES), BF16)],
        compiler_params=_cparams("parallel", "arbitrary"),
        name="nsa_prompt_attention",
    )(q_t, g_t, kc, vct, cmp_tab, ks, vst, kw, vwt, near, wfirst, far)


NEW_PAD = 128


def _sample_tables(rel_bias, past, t, wbuf, n_rows):
    lane = jnp.arange(SAMPLE_LANES, dtype=jnp.int32)
    g, kvh, qi = lane // (N_KV * t), (lane // t) % N_KV, lane % t
    used = lane < GROUP * N_KV * t
    head = jnp.where(used, kvh * GROUP + g, 0)
    qpos = past + qi
    fd_lane = jnp.take(_dist_bias(rel_bias), head, axis=1)
    dists = jnp.arange(MAX_DISTANCE + 1, dtype=jnp.int32)

    def tab(kpos, valid, n_near):
        dist = qpos[None, :] - kpos[:, None]
        n_far = kpos.shape[0] - n_near
        near_d = jnp.clip(dist[n_far:], 0, MAX_DISTANCE)
        near = jnp.sum(jnp.where(near_d[:, :, None] == dists, fd_lane.T[None], 0.0), axis=-1)
        far = jnp.broadcast_to(fd_lane[MAX_DISTANCE][None, :], (n_far, SAMPLE_LANES))
        bias = jnp.where(used[None, :], jnp.concatenate([far, near], axis=0), 0.0)
        return jnp.where(valid(dist) & (kpos[:, None] >= 0), bias, NEG)

    new_pos = past + jnp.where(jnp.arange(NEW_PAD) < t, jnp.arange(NEW_PAD, dtype=jnp.int32), 1 << 20)
    causal = lambda d: d >= 0
    window = lambda d: (d >= 0) & (d < WINDOW)
    nc = past // CMP_BLK
    assert wbuf >= MAX_DISTANCE and past >= MAX_DISTANCE
    t_cmp = tab(jnp.arange(nc, dtype=jnp.int32) * CMP_BLK + CMP_BLK - 1, causal, MAX_DISTANCE // CMP_BLK)
    rows = lambda a: a[:, :n_rows].T
    t_sel = rows(tab(jnp.concatenate([jnp.arange(past, dtype=jnp.int32), new_pos]), causal,
                     MAX_DISTANCE + NEW_PAD))
    t_win = rows(tab(jnp.concatenate([past - wbuf + jnp.arange(wbuf, dtype=jnp.int32), new_pos]), window,
                     MAX_DISTANCE + NEW_PAD))
    key = jnp.arange(past + NEW_PAD, dtype=jnp.int32)
    blk = jnp.where(key < past, key // SEL_BLK, past // SEL_BLK)
    expand = (blk[None, :] == jnp.arange(128, dtype=jnp.int32)[:, None]).astype(BF16)
    return t_cmp, t_sel, t_win, expand


def _sample_attn_kernel(pt_ref, *refs, n_pages, n_sel, t, wbuf):
    del pt_ref
    it = iter(refs)
    qbd_ref, qrow_ref, g_ref = next(it), next(it), next(it)
    kc_ref, vc_ref = next(it), next(it)
    ks_pages = [next(it) for _ in range(n_pages)]
    vs_pages = [next(it) for _ in range(n_pages)]
    ksn_ref, vsn_ref, kwin_ref, vwin_ref, kwn_ref, vwn_ref = (next(it) for _ in range(6))
    tcmp_ref, tsel_ref, twin_ref, expand_ref = (next(it) for _ in range(4))
    o_ref = next(it)
    imp_ref, val_ref, sel_ref, s_ref = (next(it) for _ in range(4))

    scale = HEAD_DIM ** -0.5
    nq = N_KV * t
    qs = qbd_ref[...] * scale
    qr = qrow_ref[...] * scale
    n_q = qr.shape[0]
    lane = lax.broadcasted_iota(jnp.int32, (1, SAMPLE_LANES), 1)
    blocks_per_page = PAGE_SIZE // CMP_BLK
    past = n_pages * PAGE_SIZE
    nt_dims = (((1,), (1,)), ((), ()))

    s = jnp.dot(kc_ref[...].astype(BF16), qs, preferred_element_type=F32) + tcmp_ref[...]
    m = jnp.max(s, axis=0, keepdims=True)
    p_c = jnp.exp(s - m)
    p_c = p_c / jnp.maximum(jnp.sum(p_c, axis=0, keepdims=True), 1e-30)
    o_c = lax.dot_general(p_c.astype(BF16), vc_ref[...].astype(BF16), (((0,), (0,)), ((), ())),
                          preferred_element_type=F32)[:n_q]

    imp = p_c
    for g in range(1, GROUP):
        imp = imp + pltpu.roll(p_c, SAMPLE_LANES - g * nq, 1)
    imp_ref[...] = imp
    n_pairs = n_pages * blocks_per_page // 2
    n_blk = n_pairs + 1
    n_rows = val_ref.shape[0]
    val_ref[...] = jnp.full((n_rows, SAMPLE_LANES), -2.0, F32)
    val_ref[0:n_pairs, :] = imp_ref[pl.ds(0, n_pairs, stride=2), :] + imp_ref[pl.ds(1, n_pairs, stride=2), :]
    blk = lax.broadcasted_iota(jnp.int32, (n_rows, SAMPLE_LANES), 0)
    qblk = n_blk - 1
    forced = (blk == 0) | (blk == qblk) | (blk == qblk - 1)
    val = jnp.where(forced, BIG, val_ref[...])
    val = jnp.where(blk < n_blk, val, -2.0)
    val_ref[...] = val
    rank = jnp.zeros((n_rows, SAMPLE_LANES), F32)
    for j in range(n_blk):
        row = val_ref[j:j + 1, :]
        rank = rank + jnp.where(blk > j, jnp.where(row >= val, 1.0, 0.0), jnp.where(row > val, 1.0, 0.0))
    selneg = jnp.where((rank < n_sel) & (lane < nq), 0.0, jnp.where(lane < nq, NEG, 0.0))
    selrep = selneg
    for g in range(1, GROUP):
        selrep = selrep + pltpu.roll(selneg, g * nq, 1)
    sel_ref[...] = jnp.zeros_like(sel_ref)
    sel_ref[0:n_rows, :] = jnp.where(selrep == 0.0, 1.0, 0.0)
    sel01 = sel_ref[...].T[:n_q].astype(BF16)

    def paged(page_ref):
        return page_ref[...].reshape(KV_DIM, PAGE_SIZE).astype(BF16)

    for p in range(n_pages + 1):
        cols = slice(p * KEY_TILE, (p + 1) * KEY_TILE)
        if p < n_pages:
            s = jnp.dot(qr, paged(ks_pages[p]), preferred_element_type=F32)
        else:
            s = lax.dot_general(qr, ksn_ref[...], nt_dims, preferred_element_type=F32)
        picked = jnp.dot(sel01, expand_ref[:, cols], preferred_element_type=F32)
        s_ref[:, cols] = s + tsel_ref[:, cols] + (picked - 1.0) * (-NEG)
    s = s_ref[...]
    p_s = jnp.exp(s - jnp.max(s, axis=1, keepdims=True))
    l = jnp.sum(p_s, axis=1, keepdims=True)
    p_s = p_s.astype(BF16)
    acc = jnp.dot(p_s[:, past:], vsn_ref[...], preferred_element_type=F32)
    for p in range(n_pages):
        acc = acc + lax.dot_general(p_s[:, p * KEY_TILE:(p + 1) * KEY_TILE], paged(vs_pages[p]), nt_dims,
                                    preferred_element_type=F32)
    o_s = acc / jnp.maximum(l, 1e-30)

    kwin = kwin_ref[...].reshape(KV_DIM, wbuf).astype(BF16)
    vwin = vwin_ref[...].reshape(KV_DIM, wbuf).astype(BF16)
    s_w = jnp.dot(qr, kwin, preferred_element_type=F32) + twin_ref[:, :wbuf]
    s_n = lax.dot_general(qr, kwn_ref[...], nt_dims, preferred_element_type=F32) + twin_ref[:, wbuf:]
    m = jnp.maximum(jnp.max(s_w, axis=1, keepdims=True), jnp.max(s_n, axis=1, keepdims=True))
    p_w, p_n = jnp.exp(s_w - m), jnp.exp(s_n - m)
    l = jnp.sum(p_w, axis=1, keepdims=True) + jnp.sum(p_n, axis=1, keepdims=True)
    acc = (lax.dot_general(p_w.astype(BF16), vwin, nt_dims, preferred_element_type=F32)
           + jnp.dot(p_n.astype(BF16), vwn_ref[...], preferred_element_type=F32))
    o_w = acc / jnp.maximum(l, 1e-30)

    gate = _sigmoid(g_ref[...])
    o_ref[...] = gate[:, 0:1] * o_c + gate[:, 1:2] * o_s + gate[:, 2:3] * o_w


def _sample_attention(q, gates, k_cmp, v_cmp, cache_ks, cache_vs, page_table,
                      ksl, vsl, win_k, win_v, kw, vw, rel_bias, b, t):
    n_pages = page_table.shape[1]
    past = n_pages * PAGE_SIZE
    wbuf = win_k.shape[-1]
    nq = N_KV * t
    used = GROUP * nq
    assert used % 8 == 0 and used <= SAMPLE_LANES and t <= NEW_PAD
    blocks_per_page = PAGE_SIZE // CMP_BLK
    n_blk = past // SEL_BLK + 1
    n_rows = -(-n_blk // 8) * 8

    q5 = q.reshape(b, t, N_KV, GROUP, HEAD_DIM)
    eye = jnp.eye(N_KV, dtype=q.dtype)
    qbd = jnp.einsum("btkgd,kc->bkdgct", q5, eye).reshape(b, KV_DIM, used)
    qbd = jnp.pad(qbd, ((0, 0), (0, 0), (0, SAMPLE_LANES - used))).astype(BF16)
    qrow = jnp.einsum("btkgd,kc->bgktcd", q5, eye).reshape(b, used, KV_DIM).astype(BF16)
    g_rows = gates[:, :3 * N_HEADS].reshape(b, t, N_KV, GROUP, 3).transpose(0, 3, 2, 1, 4).reshape(b, used, 3)
    g_rows = jnp.pad(g_rows, ((0, 0), (0, 0), (0, 128 - 3)))
    pad_rows = lambda a: jnp.pad(a.reshape(b, t, KV_DIM), ((0, 0), (0, NEW_PAD - t), (0, 0))).astype(BF16)
    tables = _sample_tables(rel_bias, past, t, wbuf, used)

    per_b = lambda *shape: pl.BlockSpec((None,) + shape, lambda bi, pt: (bi,) + (0,) * len(shape))
    const = lambda *shape: pl.BlockSpec(shape, lambda bi, pt: (0,) * len(shape))

    def page(p):
        return pl.BlockSpec((None, N_KV, HEAD_DIM, PAGE_SIZE), lambda bi, pt, p=p: (pt[bi, p], 0, 0, 0))

    n_cmp = n_pages * blocks_per_page
    in_specs = ([per_b(KV_DIM, SAMPLE_LANES), per_b(used, KV_DIM), per_b(used, 128),
                 per_b(n_cmp, KV_DIM), per_b(n_cmp, KV_DIM)]
                + [page(p) for p in range(n_pages)] * 2
                + [per_b(NEW_PAD, KV_DIM), per_b(NEW_PAD, KV_DIM),
                   per_b(N_KV, HEAD_DIM, wbuf), per_b(N_KV, HEAD_DIM, wbuf),
                   per_b(NEW_PAD, KV_DIM), per_b(NEW_PAD, KV_DIM)]
                + [const(*tb.shape) for tb in tables])
    out = pl.pallas_call(
        functools.partial(_sample_attn_kernel, n_pages=n_pages, n_sel=min(N_SEL, n_blk), t=t, wbuf=wbuf),
        grid_spec=pltpu.PrefetchScalarGridSpec(
            num_scalar_prefetch=1,
            grid=(b,),
            in_specs=in_specs,
            out_specs=per_b(used, KV_DIM),
            scratch_shapes=[pltpu.VMEM((n_cmp, SAMPLE_LANES), F32),
                            pltpu.VMEM((n_rows, SAMPLE_LANES), F32),
                            pltpu.VMEM((128, SAMPLE_LANES), F32),
                            pltpu.VMEM((used, past + NEW_PAD), F32)]),
        out_shape=jax.ShapeDtypeStruct((b, used, KV_DIM), F32),
        compiler_params=_cparams("arbitrary"),
        name="nsa_sample_attention",
    )(page_table, qbd, qrow, g_rows, k_cmp.reshape(b, n_cmp, KV_DIM), v_cmp.reshape(b, n_cmp, KV_DIM),
      *([cache_ks] * n_pages),
      *([cache_vs] * n_pages), pad_rows(ksl), pad_rows(vsl), win_k, win_v, pad_rows(kw), pad_rows(vw), *tables)
    o = out.reshape(b, GROUP, N_KV, t, N_KV, HEAD_DIM)
    o = jnp.einsum("bgktkd->btkgd", o)
    return o.reshape(b * t, Q_DIM)


def _compress_weights(w_c, pe_c):
    eye = jnp.eye(N_KV, dtype=w_c.dtype)
    w_big = jnp.einsum("jde,kc->jkdce", w_c, eye).reshape(CMP_BLK * KV_DIM, KV_DIM)
    pe_flat = jnp.broadcast_to(pe_c[:, None, :], (CMP_BLK, N_KV, HEAD_DIM)).reshape(1, CMP_BLK * KV_DIM)
    pe_rows = jnp.pad(pe_flat, ((0, 7), (0, 0)))
    bias = _matmul(pe_rows, w_big, name="compress_pe")[0:1]
    return w_big, bias


def _compress(x_rows, w_big, bias, tm):
    return _matmul(x_rows, w_big, bias, tm=tm, name="compress")


_BLOCK_PITCH = CMP_BLK + 4


def _compress_pages_kernel(pt_ref, *refs, n_pages):
    del pt_ref
    page_refs, (w_ref, b_ref, o_ref, rows_ref) = refs[:n_pages], refs[n_pages:]
    per_page = PAGE_SIZE // CMP_BLK
    for p in range(n_pages):
        xt = page_refs[p][...].reshape(KV_DIM, PAGE_SIZE).T
        for n in range(per_page):
            lo = (p * per_page + n) * _BLOCK_PITCH
            for h in range(2):
                rows_ref[h, lo:lo + CMP_BLK, :] = xt[n * CMP_BLK:(n + 1) * CMP_BLK, h * 128:(h + 1) * 128]
    n_out = n_pages * per_page
    acc = jnp.broadcast_to(b_ref[...], (n_out, KV_DIM))
    for j in range(CMP_BLK):
        for h in range(2):
            piece = rows_ref[h, pl.ds(j, n_out, stride=_BLOCK_PITCH), :].astype(BF16)
            lo = j * KV_DIM + h * 128
            acc = acc + jnp.dot(piece, w_ref[lo:lo + 128, :], preferred_element_type=F32)
    o_ref[...] = acc


def _compress_pages(cache_t, page_table, w_big, bias):
    assert KV_DIM == 256
    b, per_b = page_table.shape
    group = max(1, 32 // per_b)
    assert b % group == 0
    n_pages = group * per_b
    n_out = n_pages * (PAGE_SIZE // CMP_BLK)

    def page(p):
        return pl.BlockSpec((None, N_KV, HEAD_DIM, PAGE_SIZE),
                            lambda i, pt, p=p: (pt[i * group + p // per_b, p % per_b], 0, 0, 0))

    return pl.pallas_call(
        functools.partial(_compress_pages_kernel, n_pages=n_pages),
        grid_spec=pltpu.PrefetchScalarGridSpec(
            num_scalar_prefetch=1,
            grid=(b // group,),
            in_specs=[page(p) for p in range(n_pages)]
            + [pl.BlockSpec((CMP_BLK * KV_DIM, KV_DIM), lambda i, pt: (0, 0), pipeline_mode=pl.Buffered(1)),
               pl.BlockSpec((1, KV_DIM), lambda i, pt: (0, 0))],
            out_specs=pl.BlockSpec((n_out, KV_DIM), lambda i, pt: (i, 0)),
            scratch_shapes=[pltpu.VMEM((2, n_out * _BLOCK_PITCH, 128), F32)]),
        out_shape=jax.ShapeDtypeStruct((b * per_b * (PAGE_SIZE // CMP_BLK), KV_DIM), F32),
        compiler_params=_cparams("arbitrary"),
        name="compress_pages",
    )(page_table, *([cache_t] * n_pages), w_big.astype(BF16), bias)


def _ada(c_all, ada_w, ada_b):
    mods = []
    for i in range(ada_w.shape[0]):
        mods.append(_matmul(c_all, ada_w[i], ada_b[i].reshape(1, -1), silu_in=True, tn=2 * D_MODEL,
                            name="ada_modulate"))
    return mods


def kernel(x_prompt, x_sample, c_prompt, c_sample, cache_k_cmp, cache_v_cmp, cache_k_sel, cache_v_sel,
           page_table, state_k_win, state_v_win, state_pool, rel_bias, ada_w, ada_b, norm_g, final_g,
           nsa_w_in, nsa_w_out, cmp_wk, cmp_wv, cmp_pe_k, cmp_pe_v, pool_w, pool_scale,
           ffn_wg, ffn_wu, ffn_wd):
    bp, tp, _ = x_prompt.shape
    bs, ts, _ = x_sample.shape
    n_phys = cache_k_cmp.shape[1]
    past = page_table.shape[1] * PAGE_SIZE
    depth = ada_w.shape[0]
    assert depth == 2 and tp % KEY_TILE == 0 and ts <= 8 and past % PAGE_SIZE == 0

    n_c = bp + bs
    c_all = jnp.pad(jnp.concatenate([c_prompt, c_sample], axis=0), ((0, -n_c % 8), (0, 0)))
    mods = _ada(c_all, ada_w, ada_b)

    def mod_prompt(i):
        return [v[:bp].reshape(bp, 1, D_MODEL) for v in jnp.split(mods[i], 6, axis=-1)]

    def mod_sample(i):
        return [jnp.repeat(v[bp:n_c], ts, axis=0) for v in jnp.split(mods[i], 6, axis=-1)]

    wk_big, k_bias = _compress_weights(cmp_wk[0], cmp_pe_k[0])
    wv_big, v_bias = _compress_weights(cmp_wv[0], cmp_pe_v[0])

    mp = bp * tp
    x = x_prompt.reshape(mp, D_MODEL)
    sh1, sc1, g1, sh2, sc2, g2 = mod_prompt(0)
    (kc, vc), kv_t, layouts = _nsa_project(x, norm_g[0, 0], sc1, sh1, nsa_w_in[0], tp, 512, True)
    blk_rows = lambda a: a.reshape(mp // CMP_BLK, CMP_BLK * KV_DIM)
    k_cmp = _compress(blk_rows(kc), wk_big, k_bias, 256)
    v_cmp = _compress(blk_rows(vc), wv_big, v_bias, 256)
    o_t = _prompt_attention(layouts, k_cmp, v_cmp, rel_bias, bp, tp)
    x = _matmul_residual_t(o_t, nsa_w_out[0], x, g1, tp, 512)
    x = _ffn(x, norm_g[0, 1], sc2, sh2, g2, ffn_wg[0], ffn_wu[0], ffn_wd[0], final_g, tp, 512, 1408, False)
    sh1, sc1, g1, sh2, sc2, g2 = mod_prompt(1)
    x3, pool_p = _pool_mix(x.reshape(bp, tp, D_MODEL), jnp.zeros((bp, POOL_STATE, D_MODEL), F32),
                           norm_g[1, 0], sc1, sh1, g1, pool_w[0], pool_scale[0], 0, 512)
    y_prompt = _ffn(x3.reshape(mp, D_MODEL), norm_g[1, 1], sc2, sh2, g2, ffn_wg[1], ffn_wu[1], ffn_wd[1],
                    final_g, tp, 512, 1408, True).reshape(bp, tp, D_MODEL)
    win = min(WINDOW, tp)
    st5 = lambda a, t0: jnp.transpose(a[:, :, t0:].reshape(bp, N_KV, HEAD_DIM, tp - t0), (0, 3, 1, 2))[None]
    prompt_states = tuple(st5(a, 0) for a in kv_t[:4]) + tuple(st5(a, tp - win) for a in kv_t[4:]) + (pool_p[None],)

    ms = bs * ts
    x = x_sample.reshape(ms, D_MODEL)
    sh1, sc1, g1, sh2, sc2, g2 = mod_sample(0)
    (kc, vc, ksl, vsl, kw, vw), (q, gates) = _nsa_project(x, norm_g[0, 0], sc1, sh1, nsa_w_in[0], ts, 512, False)
    assert (past + ts) // CMP_BLK == past // CMP_BLK
    pos_last = lambda a: jnp.transpose(a[0], (0, 2, 3, 1))
    k_cmp_s = _compress_pages(pos_last(cache_k_cmp), page_table, wk_big, k_bias)
    v_cmp_s = _compress_pages(pos_last(cache_v_cmp), page_table, wv_big, v_bias)
    o = _sample_attention(q, gates, k_cmp_s, v_cmp_s, pos_last(cache_k_sel), pos_last(cache_v_sel),
                          page_table, ksl, vsl, pos_last(state_k_win), pos_last(state_v_win), kw, vw,
                          rel_bias, bs, ts)
    x = _matmul_residual(o, nsa_w_out[0], x, g1, ts, 512)
    x = _ffn(x, norm_g[0, 1], sc2, sh2, g2, ffn_wg[0], ffn_wu[0], ffn_wd[0], final_g, ts, 512, 1408, False)
    sh1, sc1, g1, sh2, sc2, g2 = mod_sample(1)
    b3 = lambda v: v.reshape(bs, ts, D_MODEL)[:, :1]
    x3, pool_s = _pool_mix(x.reshape(bs, ts, D_MODEL), state_pool[0], norm_g[1, 0], b3(sc1), b3(sh1), b3(g1),
                           pool_w[0], pool_scale[0], past, 512)
    y_sample = _ffn(x3.reshape(ms, D_MODEL), norm_g[1, 1], sc2, sh2, g2, ffn_wg[1], ffn_wu[1], ffn_wd[1],
                    final_g, ts, 512, 1408, True).reshape(bs, ts, D_MODEL)
    st5 = lambda a: a.reshape(1, bs, ts, N_KV, HEAD_DIM)
    kw_ext = jnp.concatenate([state_k_win[0], st5(kw)[0]], axis=1)[:, ts:]
    vw_ext = jnp.concatenate([state_v_win[0], st5(vw)[0]], axis=1)[:, ts:]
    sample_states = (st5(kc), st5(vc), st5(ksl), st5(vsl), kw_ext[None], vw_ext[None], pool_s[None])

    return (y_prompt, y_sample) + prompt_states + sample_states
```

```python
import functools
import math

import jax
import jax.numpy as jnp
from jax import lax
from jax.experimental import pallas as pl
from jax.experimental.pallas import tpu as pltpu

D_MODEL = 1024
N_HEADS = 16
N_KV = 4
GROUP = N_HEADS // N_KV
HEAD_DIM = 64
Q_DIM = N_HEADS * HEAD_DIM
KV_DIM = N_KV * HEAD_DIM
CMP_BLK = 32
SEL_BLK = 64
N_SEL = 16
WINDOW = 512
Q_CHUNK = 64
N_BUCKETS = 32
MAX_DISTANCE = 128
POOL_WINDOWS = (2, 4, 8, 16)
POOL_GROUP_DIM = D_MODEL // len(POOL_WINDOWS)
POOL_STATE = max(POOL_WINDOWS) - 1
PAGE_SIZE = 128
RMS_EPS = 1e-6
NEG = -1e30
BIG = 1e9

KEY_TILE = 128
CHUNK = 128
LANES = GROUP * CHUNK
LOG2E = math.log2(math.e)
Q_SCALE_LOG2 = HEAD_DIM ** -0.5 * LOG2E
SAMPLE_LANES = 128
VMEM_LIMIT_BYTES = 48 * 1024 * 1024

F32 = jnp.float32
BF16 = jnp.bfloat16


def _cparams(*sem):
    return pltpu.CompilerParams(dimension_semantics=sem, vmem_limit_bytes=VMEM_LIMIT_BYTES)


def _silu(x):
    return x * (1.0 / (1.0 + jnp.exp(-x)))


def _sigmoid(x):
    return 1.0 / (1.0 + jnp.exp(-x))


def _normmod(x, g, sc, sh):
    ms = jnp.mean(x * x, axis=-1, keepdims=True)
    return (x * lax.rsqrt(ms + RMS_EPS) * g) * (1.0 + sc) + sh


def _mod_spec(mod, tm, rows_per_batch):
    if mod.ndim == 3:
        return pl.BlockSpec((None, 1, D_MODEL), lambda i, *_: ((i * tm) // rows_per_batch, 0, 0))
    return pl.BlockSpec((tm, D_MODEL), lambda i, *_: (i, 0))


def _mm_kernel(a_ref, w_ref, b_ref, o_ref, *, silu_in):
    a = a_ref[...]
    if silu_in:
        a = _silu(a)
    o_ref[...] = jnp.dot(a.astype(BF16), w_ref[...], preferred_element_type=F32) + b_ref[...]


def _matmul(a, w, bias=None, *, silu_in=False, tm=256, tn=None, name="matmul"):
    m, k = a.shape
    n = w.shape[1]
    tm = min(tm, m)
    tn = n if tn is None else tn
    assert m % tm == 0 and n % tn == 0
    if bias is None:
        bias = jnp.zeros((1, n), F32)
    return pl.pallas_call(
        functools.partial(_mm_kernel, silu_in=silu_in),
        grid=(m // tm, n // tn),
        in_specs=[pl.BlockSpec((tm, k), lambda i, j: (i, 0)),
                  pl.BlockSpec((k, tn), lambda i, j: (0, j)),
                  pl.BlockSpec((1, tn), lambda i, j: (0, j))],
        out_specs=pl.BlockSpec((tm, tn), lambda i, j: (i, j)),
        out_shape=jax.ShapeDtypeStruct((m, n), F32),
        compiler_params=_cparams("parallel", "parallel"),
        name=name,
    )(a, w.astype(BF16), bias)


_PROJ_MAIN = Q_DIM + 6 * KV_DIM
_GATE_PAD = 128


def _proj_kernel(x_ref, g_ref, sc_ref, sh_ref, w_ref, wg_ref, *out_refs, attn_layouts, tm):
    hb = _normmod(x_ref[...], g_ref[...], sc_ref[...], sh_ref[...]).astype(BF16)
    q = jnp.dot(hb, w_ref[:, :Q_DIM], preferred_element_type=F32)
    gates = jnp.dot(hb, wg_ref[...], preferred_element_type=F32)
    kvs = []
    for n in range(6):
        lo = Q_DIM + n * KV_DIM
        kvs.append(jnp.dot(hb, w_ref[:, lo:lo + KV_DIM], preferred_element_type=F32))
    if not attn_layouts:
        for n in range(6):
            out_refs[n][...] = kvs[n]
        out_refs[6][...] = q
        out_refs[7][...] = gates
        return
    kt_refs = out_refs[2:8]
    qt_ref, gt_ref, ks_ref, vst_ref, kw_ref, vwt_ref, q_scr, gate_scr, kv_scr = out_refs[8:]
    out_refs[0][...] = kvs[0]
    out_refs[1][...] = kvs[1]
    kvts = []
    for n in range(6):
        kv_scr[n] = kvs[n]
        kvts.append(kv_scr[n].T)
        kt_refs[n][...] = kvts[n]
    n_chunks = tm // CHUNK
    q_scr[...] = q * Q_SCALE_LOG2
    gate_scr[...] = gates
    qt = q_scr[...].T
    gt = gate_scr[...].T
    for c in range(n_chunks):
        tok = slice(c * CHUNK, (c + 1) * CHUNK)
        for kv in range(N_KV):
            for gi in range(GROUP):
                h = kv * GROUP + gi
                lanes = slice(gi * CHUNK, (gi + 1) * CHUNK)
                qt_ref[c, kv, :, lanes] = qt[h * HEAD_DIM:(h + 1) * HEAD_DIM, tok].astype(BF16)
                gt_ref[c, kv, :, lanes] = gt[3 * h:3 * h + 3, tok]
    for k_ref, vt_ref, k, vt in ((ks_ref, vst_ref, kvs[2], kvts[3]), (kw_ref, vwt_ref, kvs[4], kvts[5])):
        for kv in range(N_KV):
            cols = slice(kv * HEAD_DIM, (kv + 1) * HEAD_DIM)
            k_ref[kv] = k[:, cols].astype(BF16)
            for c in range(n_chunks):
                vt_ref[kv, c, 0:HEAD_DIM, :] = vt[cols, c * CHUNK:(c + 1) * CHUNK].astype(BF16)
                vt_ref[kv, c, HEAD_DIM:, :] = jnp.ones((V_ROWS - HEAD_DIM, CHUNK), BF16)


def _nsa_project(x, g, sc, sh, w_in, rows_per_batch, tm, attn_layouts):
    m = x.shape[0]
    tm = min(tm, m)
    w_main = w_in[:, :_PROJ_MAIN].astype(BF16)
    w_gate = jnp.pad(w_in[:, _PROJ_MAIN:], ((0, 0), (0, _GATE_PAD - 3 * N_HEADS))).astype(BF16)
    row = lambda n: pl.BlockSpec((tm, n), lambda i: (i, 0))
    if attn_layouts:
        assert tm % CHUNK == 0 and rows_per_batch % tm == 0
        nck = tm // CHUNK
        tpb = rows_per_batch // tm
        out_specs = [row(KV_DIM)] * 2 + [pl.BlockSpec((None, KV_DIM, tm), lambda i: (i // tpb, 0, i % tpb))] * 6
        out_shape = ([jax.ShapeDtypeStruct((m, KV_DIM), F32)] * 2
                     + [jax.ShapeDtypeStruct((m // rows_per_batch, KV_DIM, rows_per_batch), F32)] * 6)
        chunked = lambda *s: pl.BlockSpec((nck,) + s, lambda i: (i,) + (0,) * len(s))
        k_spec = pl.BlockSpec((N_KV, tm, HEAD_DIM), lambda i: (0, i, 0))
        vt_spec = pl.BlockSpec((N_KV, nck, V_ROWS, CHUNK), lambda i: (0, i, 0, 0))
        k_shape = jax.ShapeDtypeStruct((N_KV, m, HEAD_DIM), BF16)
        vt_shape = jax.ShapeDtypeStruct((N_KV, m // CHUNK, V_ROWS, CHUNK), BF16)
        out_specs += [chunked(N_KV, HEAD_DIM, LANES), chunked(N_KV, 3, LANES), k_spec, vt_spec, k_spec, vt_spec]
        out_shape += [jax.ShapeDtypeStruct((m // CHUNK, N_KV, HEAD_DIM, LANES), BF16),
                      jax.ShapeDtypeStruct((m // CHUNK, N_KV, 3, LANES), F32),
                      k_shape, vt_shape, k_shape, vt_shape]
    else:
        out_specs = [row(KV_DIM)] * 6 + [row(Q_DIM), row(_GATE_PAD)]
        out_shape = ([jax.ShapeDtypeStruct((m, KV_DIM), F32)] * 6
                     + [jax.ShapeDtypeStruct((m, Q_DIM), F32), jax.ShapeDtypeStruct((m, _GATE_PAD), F32)])
    outs = pl.pallas_call(
        functools.partial(_proj_kernel, attn_layouts=attn_layouts, tm=tm),
        grid=(m // tm,),
        in_specs=[row(D_MODEL),
                  pl.BlockSpec((1, D_MODEL), lambda i: (0, 0)),
                  _mod_spec(sc, tm, rows_per_batch), _mod_spec(sh, tm, rows_per_batch),
                  pl.BlockSpec((D_MODEL, _PROJ_MAIN), lambda i: (0, 0)),
                  pl.BlockSpec((D_MODEL, _GATE_PAD), lambda i: (0, 0))],
        out_specs=out_specs,
        out_shape=out_shape,
        scratch_shapes=([pltpu.VMEM((tm, Q_DIM), F32), pltpu.VMEM((tm, _GATE_PAD), F32),
                         pltpu.VMEM((6, tm, KV_DIM), F32)] if attn_layouts else []),
        compiler_params=_cparams("parallel"),
        name="nsa_project",
    )(x, g.reshape(1, D_MODEL), sc, sh, w_main, w_gate)
    if attn_layouts:
        return outs[:2], outs[2:8], outs[8:]
    return outs[:6], outs[6:]


def _mm_res_kernel(a_ref, w_ref, x_ref, gate_ref, o_ref):
    y = jnp.dot(a_ref[...].astype(BF16), w_ref[...], preferred_element_type=F32)
    o_ref[...] = x_ref[...] + gate_ref[...] * y


def _matmul_residual(a, w, x, gate, rows_per_batch, tm):
    m, k = a.shape
    tm = min(tm, m)
    return pl.pallas_call(
        _mm_res_kernel,
        grid=(m // tm,),
        in_specs=[pl.BlockSpec((tm, k), lambda i: (i, 0)),
                  pl.BlockSpec((k, D_MODEL), lambda i: (0, 0)),
                  pl.BlockSpec((tm, D_MODEL), lambda i: (i, 0)),
                  _mod_spec(gate, tm, rows_per_batch)],
        out_specs=pl.BlockSpec((tm, D_MODEL), lambda i: (i, 0)),
        out_shape=jax.ShapeDtypeStruct((m, D_MODEL), F32),
        compiler_params=_cparams("parallel"),
        name="out_proj_residual",
    )(a, w.astype(BF16), x, gate)


def _mm_res_t_kernel(ot_ref, w_ref, x_ref, gate_ref, o_ref, *, n_chunks):
    chunks = []
    for c in range(n_chunks):
        rows = [ot_ref[c, kv, :, gi * CHUNK:(gi + 1) * CHUNK] for kv in range(N_KV) for gi in range(GROUP)]
        chunks.append(jnp.concatenate(rows, axis=0).astype(F32).T)
    a = jnp.concatenate(chunks, axis=0).astype(BF16)
    y = jnp.dot(a, w_ref[...], preferred_element_type=F32)
    o_ref[...] = x_ref[...] + gate_ref[...] * y


def _matmul_residual_t(o_t, w, x, gate, rows_per_batch, tm):
    m = x.shape[0]
    nck = tm // CHUNK
    return pl.pallas_call(
        functools.partial(_mm_res_t_kernel, n_chunks=nck),
        grid=(m // tm,),
        in_specs=[pl.BlockSpec((nck, N_KV, HEAD_DIM, LANES), lambda i: (i, 0, 0, 0)),
                  pl.BlockSpec((Q_DIM, D_MODEL), lambda i: (0, 0)),
                  pl.BlockSpec((tm, D_MODEL), lambda i: (i, 0)),
                  _mod_spec(gate, tm, rows_per_batch)],
        out_specs=pl.BlockSpec((tm, D_MODEL), lambda i: (i, 0)),
        out_shape=jax.ShapeDtypeStruct((m, D_MODEL), F32),
        compiler_params=_cparams("parallel"),
        name="out_proj_residual_t",
    )(o_t, w.astype(BF16), x, gate)


def _ffn_kernel(x_ref, g_ref, sc_ref, sh_ref, gate_ref, wg_ref, wu_ref, wd_ref, fg_ref, o_ref,
                *, final_norm, tf):
    x = x_ref[...]
    hb = _normmod(x, g_ref[...], sc_ref[...], sh_ref[...]).astype(BF16)
    acc = None
    for lo in range(0, wg_ref.shape[1], tf):
        a = jnp.dot(hb, wg_ref[:, lo:lo + tf], preferred_element_type=F32)
        u = jnp.dot(hb, wu_ref[:, lo:lo + tf], preferred_element_type=F32)
        act = (_silu(a) * u).astype(BF16)
        part = jnp.dot(act, wd_ref[lo:lo + tf, :], preferred_element_type=F32)
        acc = part if acc is None else acc + part
    y = x + gate_ref[...] * acc
    if final_norm:
        ms = jnp.mean(y * y, axis=-1, keepdims=True)
        y = y * lax.rsqrt(ms + RMS_EPS) * fg_ref[...]
    o_ref[...] = y


def _ffn(x, g, sc, sh, gate, wg, wu, wd, final_g, rows_per_batch, tm, tf, final_norm):
    m = x.shape[0]
    d_ff = wg.shape[1]
    tm = min(tm, m)
    assert d_ff % tf == 0
    once = pl.Buffered(1)
    vec = pl.BlockSpec((1, D_MODEL), lambda i: (0, 0))
    return pl.pallas_call(
        functools.partial(_ffn_kernel, final_norm=final_norm, tf=tf),
        grid=(m // tm,),
        in_specs=[pl.BlockSpec((tm, D_MODEL), lambda i: (i, 0)),
                  vec,
                  _mod_spec(sc, tm, rows_per_batch), _mod_spec(sh, tm, rows_per_batch),
                  _mod_spec(gate, tm, rows_per_batch),
                  pl.BlockSpec((D_MODEL, d_ff), lambda i: (0, 0), pipeline_mode=once),
                  pl.BlockSpec((D_MODEL, d_ff), lambda i: (0, 0), pipeline_mode=once),
                  pl.BlockSpec((d_ff, D_MODEL), lambda i: (0, 0), pipeline_mode=once),
                  vec],
        out_specs=pl.BlockSpec((tm, D_MODEL), lambda i: (i, 0)),
        out_shape=jax.ShapeDtypeStruct((m, D_MODEL), F32),
        compiler_params=_cparams("parallel"),
        name="ffn",
    )(x, g.reshape(1, D_MODEL), sc, sh, gate, wg.astype(BF16), wu.astype(BF16), wd.astype(BF16),
      final_g.reshape(1, D_MODEL))


_POOL_HALO = 16


def _pool_kernel(x_ref, xprev_ref, state_ref, g_ref, sc_ref, sh_ref, gate_ref, w_ref, ls_ref,
                 o_ref, st_ref, ext_ref, *, tm, pos0):
    i = pl.program_id(1)
    g, sc, sh = g_ref[...], sc_ref[...], sh_ref[...]
    u = _normmod(x_ref[...], g, sc, sh)
    prev = jnp.where(i == 0, state_ref[...], _normmod(xprev_ref[...], g, sc, sh))
    ext_ref[0:_POOL_HALO, :] = prev
    ext_ref[_POOL_HALO:_POOL_HALO + tm, :] = u
    st_ref[...] = ext_ref[tm:tm + _POOL_HALO, :]

    pos = pos0 + i * tm + lax.broadcasted_iota(jnp.int32, (tm, 1), 0)
    mixed = []
    for gi, w in enumerate(POOL_WINDOWS):
        lo = gi * POOL_GROUP_DIM
        s = u[:, lo:lo + POOL_GROUP_DIM]
        for k in range(1, w):
            s = s + ext_ref[_POOL_HALO - k:_POOL_HALO - k + tm, lo:lo + POOL_GROUP_DIM]
        cnt = jnp.minimum(pos + 1, w).astype(F32)
        pooled = s / cnt - u[:, lo:lo + POOL_GROUP_DIM]
        mixed.append(jnp.dot(pooled.astype(BF16), w_ref[gi], preferred_element_type=F32))
    y = jnp.concatenate(mixed, axis=-1) * ls_ref[...]
    o_ref[...] = x_ref[...] + gate_ref[...] * y


def _pool_mix(x3, state, g, sc, sh, gate, w_grp, layer_scale, pos0, tm):
    b, t, _ = x3.shape
    tm = min(tm, t)
    state16 = jnp.pad(state, ((0, 0), (_POOL_HALO - POOL_STATE, 0), (0, 0)))
    if t >= _POOL_HALO:
        xprev = x3
        nprev = tm // _POOL_HALO
        prev_spec = pl.BlockSpec((None, _POOL_HALO, D_MODEL),
                                 lambda bi, i: (bi, jnp.maximum(i * nprev - 1, 0), 0))
    else:
        xprev = state16
        prev_spec = pl.BlockSpec((None, _POOL_HALO, D_MODEL), lambda bi, i: (bi, 0, 0))
    vec = pl.BlockSpec((1, D_MODEL), lambda bi, i: (0, 0))
    bvec = pl.BlockSpec((None, 1, D_MODEL), lambda bi, i: (bi, 0, 0))
    out, st = pl.pallas_call(
        functools.partial(_pool_kernel, tm=tm, pos0=pos0),
        grid=(b, t // tm),
        in_specs=[pl.BlockSpec((None, tm, D_MODEL), lambda bi, i: (bi, i, 0)),
                  prev_spec,
                  pl.BlockSpec((None, _POOL_HALO, D_MODEL), lambda bi, i: (bi, 0, 0)),
                  vec, bvec, bvec, bvec,
                  pl.BlockSpec((len(POOL_WINDOWS), POOL_GROUP_DIM, POOL_GROUP_DIM),
                               lambda bi, i: (0, 0, 0)),
                  vec],
        out_specs=[pl.BlockSpec((None, tm, D_MODEL), lambda bi, i: (bi, i, 0)),
                   pl.BlockSpec((None, _POOL_HALO, D_MODEL), lambda bi, i: (bi, 0, 0))],
        out_shape=[jax.ShapeDtypeStruct((b, t, D_MODEL), F32),
                   jax.ShapeDtypeStruct((b, _POOL_HALO, D_MODEL), F32)],
        scratch_shapes=[pltpu.VMEM((tm + _POOL_HALO, D_MODEL), F32)],
        compiler_params=_cparams("parallel", "arbitrary"),
        name="pool_mix",
    )(x3, xprev, state16, g.reshape(1, D_MODEL), sc, sh, gate, w_grp.astype(BF16),
      layer_scale.reshape(1, D_MODEL))
    return out, st[:, _POOL_HALO - POOL_STATE:]


def _rel_bucket(dist):
    d = jnp.maximum(dist, 0)
    max_exact = N_BUCKETS // 2
    large = max_exact + (jnp.log(jnp.maximum(d, 1).astype(F32) / max_exact)
                         / math.log(MAX_DISTANCE / max_exact) * (N_BUCKETS - max_exact)).astype(jnp.int32)
    large = jnp.minimum(large, N_BUCKETS - 1)
    return jnp.where(d < max_exact, d, large)


def _dist_bias(rel_bias):
    return rel_bias[_rel_bucket(jnp.arange(MAX_DISTANCE + 1, dtype=jnp.int32))]


def _lookup(table, idx):
    onehot = (idx[..., None] == jnp.arange(table.shape[0], dtype=jnp.int32)).astype(F32)
    return jnp.einsum("...d,dh->...h", onehot, table, precision=lax.Precision.HIGHEST)


def _bias_tile(fd, dist, valid):
    k, q = dist.shape
    bias = _lookup(fd, jnp.clip(dist, 0, MAX_DISTANCE)).reshape(k, q, N_KV, GROUP)
    bias = jnp.where(valid[:, :, None, None], bias, NEG)
    return bias.transpose(2, 0, 3, 1).reshape(N_KV, k, GROUP * q)


def _prompt_tables(rel_bias, t):
    fd = _dist_bias(rel_bias) * LOG2E
    kj = jnp.arange(KEY_TILE, dtype=jnp.int32)[:, None]
    qi = jnp.arange(CHUNK, dtype=jnp.int32)[None, :]
    dist = qi - kj
    near = jnp.stack([_bias_tile(fd, dist + KEY_TILE, dist + KEY_TILE >= 0),
                      _bias_tile(fd, dist, dist >= 0)])
    wfirst = _bias_tile(fd, dist + WINDOW, dist < 0)
    far = _bias_tile(fd, jnp.full((1, CHUNK), MAX_DISTANCE, jnp.int32), jnp.ones((1, CHUNK), bool))
    per_chunk = CHUNK // CMP_BLK
    assert CHUNK == 128 and CMP_BLK == 32 and MAX_DISTANCE == 128
    rel0, n_rel = -per_chunk, 2 * per_chunk
    rel = rel0 + jnp.arange(n_rel, dtype=jnp.int32)[:, None]
    dist = qi - CMP_BLK * rel - (CMP_BLK - 1)
    crel = _bias_tile(fd, dist, dist >= 0)
    relm = (jnp.arange(t // CMP_BLK, dtype=jnp.int32)[None, :]
            - per_chunk * jnp.arange(t // CHUNK, dtype=jnp.int32)[:, None])[:, None, :, None]
    cmp_tab = jnp.where(relm >= rel0 + n_rel, NEG, far[None])
    for r in range(n_rel):
        cmp_tab = jnp.where(relm == rel0 + r, crel[None, :, r:r + 1, :], cmp_tab)
    return near, wfirst, far, cmp_tab


V_ROWS = HEAD_DIM + 16


def _half_exp2(s, m, top, bot):
    return jnp.exp2(jnp.concatenate([s[:SEL_BLK] - (m - top), s[SEL_BLK:] - (m - bot)], axis=0))


def _normalise(acc):
    return acc[:HEAD_DIM] * (1.0 / jnp.maximum(acc[HEAD_DIM:HEAD_DIM + 1], 1e-30))


def _prompt_attn_kernel(q_ref, g_ref, kc_ref, vct_ref, cb_ref, ks_ref, vst_ref, kw_ref, vwt_ref,
                        near_ref, wfirst_ref, far_ref, o_ref,
                        imp_ref, val_ref, rank_ref, selrep_ref, m_ref, mw_ref, alpha_ref, acc_ref, accw_ref,
                        mix_ref, s_ref, smax_ref, p_ref, sc_ref, pc_ref, *, n_blk, n_sel):
    i = pl.program_id(1)

    def k_rows(ref, kv, jt):
        return ref[kv, pl.ds(pl.multiple_of(jt * KEY_TILE, KEY_TILE), KEY_TILE), :]

    def sel_rows(kv, jt):
        return selrep_ref[kv, pl.ds(2 * jt, 1), :], selrep_ref[kv, pl.ds(2 * jt + 1, 1), :]

    col_max = []
    for kv in range(N_KV):
        s = jnp.dot(kc_ref[kv], q_ref[kv], preferred_element_type=F32) + cb_ref[kv]
        sc_ref[kv] = s
        col_max.append(jnp.max(s, axis=0, keepdims=True))
    for kv in range(N_KV):
        s = sc_ref[kv]
        p = jnp.where(cb_ref[kv] > 0.5 * NEG, jnp.exp2(s - col_max[kv]), 0.0)
        p = p * (1.0 / jnp.maximum(jnp.sum(p, axis=0, keepdims=True), 1e-30))
        pc_ref[kv] = p.astype(BF16)
        imp_ref[kv] = functools.reduce(
            lambda a, b: a + b, [p[:, gi * CHUNK:(gi + 1) * CHUNK] for gi in range(GROUP)])
    for kv in range(N_KV):
        o_c = jnp.dot(vct_ref[kv], pc_ref[kv], preferred_element_type=F32)
        mix_ref[kv] = _sigmoid(g_ref[kv, 0:1, :]) * o_c

    imp = jnp.concatenate(
        [imp_ref[kv, pl.ds(0, n_blk, stride=2), :] + imp_ref[kv, pl.ds(1, n_blk, stride=2), :]
         for kv in range(N_KV)], axis=1)
    blk = lax.broadcasted_iota(jnp.int32, (n_blk, LANES), 0)
    lane = lax.broadcasted_iota(jnp.int32, (1, LANES), 1)
    assert CHUNK == 2 * SEL_BLK
    qblk = 2 * i + jnp.where((lane & (CHUNK - 1)) >= SEL_BLK, 1, 0)
    forced = (blk == 0) | (blk == qblk) | (blk == qblk - 1)
    val_ref[...] = jnp.where(forced, BIG, jnp.where(blk <= qblk, imp, -1.0))
    rank_ref[...] = jnp.zeros_like(rank_ref)
    sub = lax.broadcasted_iota(jnp.int32, (8, LANES), 0)
    n_grp = n_blk // 8
    for grp in range(n_grp):
        @pl.when(8 * grp <= 2 * i + 1)
        def _(grp=grp):
            for r8 in range(n_grp):
                piece = val_ref[r8 * 8:(r8 + 1) * 8, :]
                acc = rank_ref[r8 * 8:(r8 + 1) * 8, :]
                for j in range(grp * 8, grp * 8 + 8):
                    row = val_ref[j:j + 1, :]
                    if r8 > grp:
                        ahead = row >= piece
                    elif r8 < grp:
                        ahead = row > piece
                    else:
                        acc = acc + jnp.where(sub > j - grp * 8, jnp.where(row >= piece, 1.0, 0.0),
                                              jnp.where(row > piece, 1.0, 0.0))
                        continue
                    acc = acc + jnp.where(ahead, 1.0, 0.0)
                rank_ref[r8 * 8:(r8 + 1) * 8, :] = acc
    selneg = jnp.where(rank_ref[...] < n_sel, 0.0, NEG)
    for kv in range(N_KV):
        selrep_ref[kv] = jnp.concatenate([selneg[:, kv * CHUNK:(kv + 1) * CHUNK]] * GROUP, axis=1)

    prev = jnp.maximum(i - 1, 0)
    n_far = jnp.maximum(i - 1, 0)
    last = jnp.maximum(n_far - 1, 0)
    zero_row = jnp.zeros((1, LANES), F32)
    for state in (mw_ref, m_ref):
        state[...] = jnp.full(state.shape, NEG, F32)
    for state in (accw_ref, acc_ref):
        state[...] = jnp.zeros(state.shape, F32)

    def scores(k_ref, jt, slot, table=None):
        for kv in range(N_KV):
            s = jnp.dot(k_rows(k_ref, kv, jt), q_ref[kv], preferred_element_type=F32)
            if table is not None:
                s = s + table(kv)
            s_ref[slot, kv] = s
            smax_ref[slot, kv, 0:1, :] = jnp.max(s[:SEL_BLK], axis=0, keepdims=True)
            smax_ref[slot, kv, 1:2, :] = jnp.max(s[SEL_BLK:], axis=0, keepdims=True)

    def softmax(slot, rows, stat_ref):
        for kv in range(N_KV):
            top, bot = rows(kv)
            m_old = stat_ref[kv]
            m_new = jnp.maximum(m_old, jnp.maximum(smax_ref[slot, kv, 0:1, :] + top,
                                                   smax_ref[slot, kv, 1:2, :] + bot))
            stat_ref[kv] = m_new
            alpha_ref[slot, kv] = jnp.exp2(m_old - m_new)
            p_ref[slot, kv] = _half_exp2(s_ref[slot, kv], m_new, top, bot).astype(BF16)

    def values(vt_ref, jt, slot, out_ref):
        for kv in range(N_KV):
            out_ref[kv] = alpha_ref[slot, kv] * out_ref[kv] + jnp.dot(vt_ref[kv, jt], p_ref[slot, kv],
                                                                      preferred_element_type=F32)

    def const_rows(row):
        return lambda kv: (row, row)

    def window_far_rows(back):
        off = jnp.where(i >= back, 0.0, NEG)
        return lambda kv: (far_ref[kv] + off, far_ref[kv] + off)

    def sel_near_rows(jt, off):
        def rows(kv):
            top, bot = sel_rows(kv, jt)
            return top + off, bot + off
        return rows

    prev_off = jnp.where(i >= 1, 0.0, NEG)
    back = lambda n: jnp.maximum(i - n, 0)
    diag_table = lambda kv: near_ref[1, kv]
    prev_table = lambda kv: near_ref[0, kv]
    static_tiles = [
        (kw_ref, vwt_ref, i, diag_table, const_rows(zero_row), mw_ref, accw_ref),
        (kw_ref, vwt_ref, prev, prev_table, const_rows(zero_row + prev_off), mw_ref, accw_ref),
        (kw_ref, vwt_ref, back(2), None, window_far_rows(2), mw_ref, accw_ref),
        (kw_ref, vwt_ref, back(3), None, window_far_rows(3), mw_ref, accw_ref),
        (kw_ref, vwt_ref, back(4), lambda kv: wfirst_ref[kv],
         const_rows(zero_row + jnp.where(i >= 4, 0.0, NEG)), mw_ref, accw_ref),
        (ks_ref, vst_ref, i, diag_table, sel_near_rows(i, 0.0), m_ref, acc_ref),
        (ks_ref, vst_ref, prev, prev_table, sel_near_rows(prev, prev_off), m_ref, acc_ref),
    ]

    def far_tile(jt):
        return jnp.minimum(jt, last)

    def far_rows(jt):
        off = jnp.where(jt < n_far, 0.0, NEG)
        def rows(kv):
            top, bot = sel_rows(kv, far_tile(jt))
            far = far_ref[kv] + off
            return top + far, bot + far
        return rows

    def far_values(jt, slot):
        values(vst_ref, jnp.where(jt < 0, prev, jnp.clip(jt, 0, last)), slot, acc_ref)

    n_static = len(static_tiles)
    k_ref0, _, jt0, table0, _, _, _ = static_tiles[0]
    scores(k_ref0, jt0, 0, table0)
    for t, (_, _, _, _, rows, stat_ref, _) in enumerate(static_tiles):
        if t >= 1:
            _, vt_ref, jt, _, _, _, out_ref = static_tiles[t - 1]
            values(vt_ref, jt, (t - 1) % 2, out_ref)
        softmax(t % 2, rows, stat_ref)
        if t + 1 < n_static:
            k_ref, _, jt, table, _, _, _ = static_tiles[t + 1]
            scores(k_ref, jt, (t + 1) % 2, table)
        else:
            scores(ks_ref, far_tile(0), (t + 1) % 2)
    assert n_static % 2 == 1

    def pair_step(j2):
        jt = 2 * j2
        far_values(jt - 1, 0)
        softmax(1, far_rows(jt), m_ref)
        scores(ks_ref, far_tile(jt + 1), 0)
        far_values(jt, 1)
        softmax(0, far_rows(jt + 1), m_ref)
        scores(ks_ref, far_tile(jt + 2), 1)

    def body(pairs_per_trip):
        def run(j, carry):
            for u in range(pairs_per_trip):
                pair_step(pairs_per_trip * j + u)
            return carry
        return run

    n_pairs = (n_far + 1) // 2
    n4 = n_pairs // 4
    n2 = n_pairs // 2
    lax.fori_loop(0, n4, body(4), 0)
    lax.fori_loop(2 * n4, n2, body(2), 0)
    lax.fori_loop(2 * n2, n_pairs, body(1), 0)
    far_values(2 * n_pairs - 1, 0)

    for kv in range(N_KV):
        o_ref[kv] = (mix_ref[kv] + _sigmoid(g_ref[kv, 2:3, :]) * _normalise(accw_ref[kv])
                     + _sigmoid(g_ref[kv, 1:2, :]) * _normalise(acc_ref[kv])).astype(o_ref.dtype)


def _prompt_attention(layouts, k_cmp, v_cmp, rel_bias, b, t):
    q_t, g_t, ks, vst, kw, vwt = layouts
    nch, nt, nc, n_blk = t // CHUNK, t // KEY_TILE, t // CMP_BLK, t // SEL_BLK
    kc = k_cmp.reshape(b, nc, N_KV, HEAD_DIM).transpose(0, 2, 1, 3).astype(BF16)
    vct = v_cmp.reshape(b, nc, N_KV, HEAD_DIM).transpose(0, 2, 3, 1).astype(BF16)
    near, wfirst, far, cmp_tab = _prompt_tables(rel_bias, t)

    once = pl.Buffered(1)
    per_b = lambda *shape: pl.BlockSpec((None,) + shape, lambda bi, i: (bi,) + (0,) * len(shape),
                                        pipeline_mode=once)
    const = lambda *shape: pl.BlockSpec(shape, lambda bi, i: (0,) * len(shape), pipeline_mode=once)
    chunk = lambda *shape: pl.BlockSpec((None,) + shape, lambda bi, i: (bi * nch + i,) + (0,) * len(shape))
    k_spec = pl.BlockSpec((N_KV, t, HEAD_DIM), lambda bi, i: (0, bi, 0), pipeline_mode=once)
    vt_spec = pl.BlockSpec((N_KV, nt, V_ROWS, KEY_TILE), lambda bi, i: (0, bi, 0, 0), pipeline_mode=once)
    stat = pltpu.VMEM((N_KV, 1, LANES), F32)
    return pl.pallas_call(
        functools.partial(_prompt_attn_kernel, n_blk=n_blk, n_sel=min(N_SEL, n_blk)),
        grid=(b, nch),
        in_specs=[chunk(N_KV, HEAD_DIM, LANES),
                  chunk(N_KV, 3, LANES),
                  per_b(N_KV, nc, HEAD_DIM),
                  per_b(N_KV, HEAD_DIM, nc),
                  pl.BlockSpec((None, N_KV, nc, LANES), lambda bi, i: (i, 0, 0, 0)),
                  k_spec, vt_spec, k_spec, vt_spec,
                  const(2, N_KV, KEY_TILE, LANES),
                  const(N_KV, KEY_TILE, LANES),
                  const(N_KV, 1, LANES)],
        out_specs=chunk(N_KV, HEAD_DIM, LANES),
        out_shape=jax.ShapeDtypeStruct((b * nch, N_KV, HEAD_DIM, LANES), BF16),
        scratch_shapes=[pltpu.VMEM((N_KV, nc, CHUNK), F32),
                        pltpu.VMEM((n_blk, LANES), F32),
                        pltpu.VMEM((n_blk, LANES), F32),
                        pltpu.VMEM((N_KV, n_blk, LANES), F32),
                        stat, stat,
                        pltpu.VMEM((2, N_KV, 1, LANES), F32),
                        pltpu.VMEM((N_KV, V_ROWS, LANES), F32),
                        pltpu.VMEM((N_KV, V_ROWS, LANES), F32),
                        pltpu.VMEM((N_KV, HEAD_DIM, LANES), F32),
                        pltpu.VMEM((2, N_KV, KEY_TILE, LANES), F32),
                        pltpu.VMEM((2, N_KV, 2, LANES), F32),
                        pltpu.VMEM((2, N_KV, KEY_TILE, LANES), BF16),
                        pltpu.VMEM((N_KV, nc, LANES), F32),
                        pltpu.VMEM((N_KV, nc, LANES), BF16)],
        compiler_params=_cparams("parallel", "arbitrary"),
        name="nsa_prompt_attention",
    )(q_t, g_t, kc, vct, cmp_tab, ks, vst, kw, vwt, near, wfirst, far)


NEW_PAD = 128


def _sample_tables(rel_bias, past, t, wbuf, n_rows):
    lane = jnp.arange(SAMPLE_LANES, dtype=jnp.int32)
    g, kvh, qi = lane // (N_KV * t), (lane // t) % N_KV, lane % t
    used = lane < GROUP * N_KV * t
    head = jnp.where(used, kvh * GROUP + g, 0)
    qpos = past + qi
    fd_lane = jnp.take(_dist_bias(rel_bias), head, axis=1)
    dists = jnp.arange(MAX_DISTANCE + 1, dtype=jnp.int32)

    def tab(kpos, valid, n_near):
        dist = qpos[None, :] - kpos[:, None]
        n_far = kpos.shape[0] - n_near
        near_d = jnp.clip(dist[n_far:], 0, MAX_DISTANCE)
        near = jnp.sum(jnp.where(near_d[:, :, None] == dists, fd_lane.T[None], 0.0), axis=-1)
        far = jnp.broadcast_to(fd_lane[MAX_DISTANCE][None, :], (n_far, SAMPLE_LANES))
        bias = jnp.where(used[None, :], jnp.concatenate([far, near], axis=0), 0.0)
        return jnp.where(valid(dist) & (kpos[:, None] >= 0), bias, NEG)

    new_pos = past + jnp.where(jnp.arange(NEW_PAD) < t, jnp.arange(NEW_PAD, dtype=jnp.int32), 1 << 20)
    causal = lambda d: d >= 0
    window = lambda d: (d >= 0) & (d < WINDOW)
    nc = past // CMP_BLK
    assert wbuf >= MAX_DISTANCE and past >= MAX_DISTANCE
    t_cmp = tab(jnp.arange(nc, dtype=jnp.int32) * CMP_BLK + CMP_BLK - 1, causal, MAX_DISTANCE // CMP_BLK)
    rows = lambda a: a[:, :n_rows].T
    t_sel = rows(tab(jnp.concatenate([jnp.arange(past, dtype=jnp.int32), new_pos]), causal,
                     MAX_DISTANCE + NEW_PAD))
    t_win = rows(tab(jnp.concatenate([past - wbuf + jnp.arange(wbuf, dtype=jnp.int32), new_pos]), window,
                     MAX_DISTANCE + NEW_PAD))
    key = jnp.arange(past + NEW_PAD, dtype=jnp.int32)
    blk = jnp.where(key < past, key // SEL_BLK, past // SEL_BLK)
    expand = (blk[None, :] == jnp.arange(128, dtype=jnp.int32)[:, None]).astype(BF16)
    return t_cmp, t_sel, t_win, expand


def _sample_attn_kernel(pt_ref, *refs, n_pages, n_sel, t, wbuf):
    del pt_ref
    it = iter(refs)
    qbd_ref, qrow_ref, g_ref = next(it), next(it), next(it)
    kc_ref, vc_ref = next(it), next(it)
    ks_pages = [next(it) for _ in range(n_pages)]
    vs_pages = [next(it) for _ in range(n_pages)]
    ksn_ref, vsn_ref, kwin_ref, vwin_ref, kwn_ref, vwn_ref = (next(it) for _ in range(6))
    tcmp_ref, tsel_ref, twin_ref, expand_ref = (next(it) for _ in range(4))
    o_ref = next(it)
    imp_ref, val_ref, sel_ref, s_ref = (next(it) for _ in range(4))

    scale = HEAD_DIM ** -0.5
    nq = N_KV * t
    qs = qbd_ref[...] * scale
    qr = qrow_ref[...] * scale
    n_q = qr.shape[0]
    lane = lax.broadcasted_iota(jnp.int32, (1, SAMPLE_LANES), 1)
    blocks_per_page = PAGE_SIZE // CMP_BLK
    past = n_pages * PAGE_SIZE
    nt_dims = (((1,), (1,)), ((), ()))

    s = jnp.dot(kc_ref[...].astype(BF16), qs, preferred_element_type=F32) + tcmp_ref[...]
    m = jnp.max(s, axis=0, keepdims=True)
    p_c = jnp.exp(s - m)
    p_c = p_c / jnp.maximum(jnp.sum(p_c, axis=0, keepdims=True), 1e-30)
    o_c = lax.dot_general(p_c.astype(BF16), vc_ref[...].astype(BF16), (((0,), (0,)), ((), ())),
                          preferred_element_type=F32)[:n_q]

    imp = p_c
    for g in range(1, GROUP):
        imp = imp + pltpu.roll(p_c, SAMPLE_LANES - g * nq, 1)
    imp_ref[...] = imp
    n_pairs = n_pages * blocks_per_page // 2
    n_blk = n_pairs + 1
    n_rows = val_ref.shape[0]
    val_ref[...] = jnp.full((n_rows, SAMPLE_LANES), -2.0, F32)
    val_ref[0:n_pairs, :] = imp_ref[pl.ds(0, n_pairs, stride=2), :] + imp_ref[pl.ds(1, n_pairs, stride=2), :]
    blk = lax.broadcasted_iota(jnp.int32, (n_rows, SAMPLE_LANES), 0)
    qblk = n_blk - 1
    forced = (blk == 0) | (blk == qblk) | (blk == qblk - 1)
    val = jnp.where(forced, BIG, val_ref[...])
    val = jnp.where(blk < n_blk, val, -2.0)
    val_ref[...] = val
    rank = jnp.zeros((n_rows, SAMPLE_LANES), F32)
    for j in range(n_blk):
        row = val_ref[j:j + 1, :]
        rank = rank + jnp.where(blk > j, jnp.where(row >= val, 1.0, 0.0), jnp.where(row > val, 1.0, 0.0))
    selneg = jnp.where((rank < n_sel) & (lane < nq), 0.0, jnp.where(lane < nq, NEG, 0.0))
    selrep = selneg
    for g in range(1, GROUP):
        selrep = selrep + pltpu.roll(selneg, g * nq, 1)
    sel_ref[...] = jnp.zeros_like(sel_ref)
    sel_ref[0:n_rows, :] = jnp.where(selrep == 0.0, 1.0, 0.0)
    sel01 = sel_ref[...].T[:n_q].astype(BF16)

    def paged(page_ref):
        return page_ref[...].reshape(KV_DIM, PAGE_SIZE).astype(BF16)

    for p in range(n_pages + 1):
        cols = slice(p * KEY_TILE, (p + 1) * KEY_TILE)
        if p < n_pages:
            s = jnp.dot(qr, paged(ks_pages[p]), preferred_element_type=F32)
        else:
            s = lax.dot_general(qr, ksn_ref[...], nt_dims, preferred_element_type=F32)
        picked = jnp.dot(sel01, expand_ref[:, cols], preferred_element_type=F32)
        s_ref[:, cols] = s + tsel_ref[:, cols] + (picked - 1.0) * (-NEG)
    s = s_ref[...]
    p_s = jnp.exp(s - jnp.max(s, axis=1, keepdims=True))
    l = jnp.sum(p_s, axis=1, keepdims=True)
    p_s = p_s.astype(BF16)
    acc = jnp.dot(p_s[:, past:], vsn_ref[...], preferred_element_type=F32)
    for p in range(n_pages):
        acc = acc + lax.dot_general(p_s[:, p * KEY_TILE:(p + 1) * KEY_TILE], paged(vs_pages[p]), nt_dims,
                                    preferred_element_type=F32)
    o_s = acc / jnp.maximum(l, 1e-30)

    kwin = kwin_ref[...].reshape(KV_DIM, wbuf).astype(BF16)
    vwin = vwin_ref[...].reshape(KV_DIM, wbuf).astype(BF16)
    s_w = jnp.dot(qr, kwin, preferred_element_type=F32) + twin_ref[:, :wbuf]
    s_n = lax.dot_general(qr, kwn_ref[...], nt_dims, preferred_element_type=F32) + twin_ref[:, wbuf:]
    m = jnp.maximum(jnp.max(s_w, axis=1, keepdims=True), jnp.max(s_n, axis=1, keepdims=True))
    p_w, p_n = jnp.exp(s_w - m), jnp.exp(s_n - m)
    l = jnp.sum(p_w, axis=1, keepdims=True) + jnp.sum(p_n, axis=1, keepdims=True)
    acc = (lax.dot_general(p_w.astype(BF16), vwin, nt_dims, preferred_element_type=F32)
           + jnp.dot(p_n.astype(BF16), vwn_ref[...], preferred_element_type=F32))
    o_w = acc / jnp.maximum(l, 1e-30)

    gate = _sigmoid(g_ref[...])
    o_ref[...] = gate[:, 0:1] * o_c + gate[:, 1:2] * o_s + gate[:, 2:3] * o_w


def _sample_attention(q, gates, k_cmp, v_cmp, cache_ks, cache_vs, page_table,
                      ksl, vsl, win_k, win_v, kw, vw, rel_bias, b, t):
    n_pages = page_table.shape[1]
    past = n_pages * PAGE_SIZE
    wbuf = win_k.shape[-1]
    nq = N_KV * t
    used = GROUP * nq
    assert used % 8 == 0 and used <= SAMPLE_LANES and t <= NEW_PAD
    blocks_per_page = PAGE_SIZE // CMP_BLK
    n_blk = past // SEL_BLK + 1
    n_rows = -(-n_blk // 8) * 8

    q5 = q.reshape(b, t, N_KV, GROUP, HEAD_DIM)
    eye = jnp.eye(N_KV, dtype=q.dtype)
    qbd = jnp.einsum("btkgd,kc->bkdgct", q5, eye).reshape(b, KV_DIM, used)
    qbd = jnp.pad(qbd, ((0, 0), (0, 0), (0, SAMPLE_LANES - used))).astype(BF16)
    qrow = jnp.einsum("btkgd,kc->bgktcd", q5, eye).reshape(b, used, KV_DIM).astype(BF16)
    g_rows = gates[:, :3 * N_HEADS].reshape(b, t, N_KV, GROUP, 3).transpose(0, 3, 2, 1, 4).reshape(b, used, 3)
    g_rows = jnp.pad(g_rows, ((0, 0), (0, 0), (0, 128 - 3)))
    pad_rows = lambda a: jnp.pad(a.reshape(b, t, KV_DIM), ((0, 0), (0, NEW_PAD - t), (0, 0))).astype(BF16)
    tables = _sample_tables(rel_bias, past, t, wbuf, used)

    per_b = lambda *shape: pl.BlockSpec((None,) + shape, lambda bi, pt: (bi,) + (0,) * len(shape))
    const = lambda *shape: pl.BlockSpec(shape, lambda bi, pt: (0,) * len(shape))

    def page(p):
        return pl.BlockSpec((None, N_KV, HEAD_DIM, PAGE_SIZE), lambda bi, pt, p=p: (pt[bi, p], 0, 0, 0))

    n_cmp = n_pages * blocks_per_page
    in_specs = ([per_b(KV_DIM, SAMPLE_LANES), per_b(used, KV_DIM), per_b(used, 128),
                 per_b(n_cmp, KV_DIM), per_b(n_cmp, KV_DIM)]
                + [page(p) for p in range(n_pages)] * 2
                + [per_b(NEW_PAD, KV_DIM), per_b(NEW_PAD, KV_DIM),
                   per_b(N_KV, HEAD_DIM, wbuf), per_b(N_KV, HEAD_DIM, wbuf),
                   per_b(NEW_PAD, KV_DIM), per_b(NEW_PAD, KV_DIM)]
                + [const(*tb.shape) for tb in tables])
    out = pl.pallas_call(
        functools.partial(_sample_attn_kernel, n_pages=n_pages, n_sel=min(N_SEL, n_blk), t=t, wbuf=wbuf),
        grid_spec=pltpu.PrefetchScalarGridSpec(
            num_scalar_prefetch=1,
            grid=(b,),
            in_specs=in_specs,
            out_specs=per_b(used, KV_DIM),
            scratch_shapes=[pltpu.VMEM((n_cmp, SAMPLE_LANES), F32),
                            pltpu.VMEM((n_rows, SAMPLE_LANES), F32),
                            pltpu.VMEM((128, SAMPLE_LANES), F32),
                            pltpu.VMEM((used, past + NEW_PAD), F32)]),
        out_shape=jax.ShapeDtypeStruct((b, used, KV_DIM), F32),
        compiler_params=_cparams("arbitrary"),
        name="nsa_sample_attention",
    )(page_table, qbd, qrow, g_rows, k_cmp.reshape(b, n_cmp, KV_DIM), v_cmp.reshape(b, n_cmp, KV_DIM),
      *([cache_ks] * n_pages),
      *([cache_vs] * n_pages), pad_rows(ksl), pad_rows(vsl), win_k, win_v, pad_rows(kw), pad_rows(vw), *tables)
    o = out.reshape(b, GROUP, N_KV, t, N_KV, HEAD_DIM)
    o = jnp.einsum("bgktkd->btkgd", o)
    return o.reshape(b * t, Q_DIM)


def _compress_weights(w_c, pe_c):
    eye = jnp.eye(N_KV, dtype=w_c.dtype)
    w_big = jnp.einsum("jde,kc->jkdce", w_c, eye).reshape(CMP_BLK * KV_DIM, KV_DIM)
    pe_flat = jnp.broadcast_to(pe_c[:, None, :], (CMP_BLK, N_KV, HEAD_DIM)).reshape(1, CMP_BLK * KV_DIM)
    pe_rows = jnp.pad(pe_flat, ((0, 7), (0, 0)))
    bias = _matmul(pe_rows, w_big, name="compress_pe")[0:1]
    return w_big, bias


def _compress(x_rows, w_big, bias, tm):
    return _matmul(x_rows, w_big, bias, tm=tm, name="compress")


_BLOCK_PITCH = CMP_BLK + 4


def _compress_pages_kernel(pt_ref, *refs, n_pages):
    del pt_ref
    page_refs, (w_ref, b_ref, o_ref, rows_ref) = refs[:n_pages], refs[n_pages:]
    per_page = PAGE_SIZE // CMP_BLK
    for p in range(n_pages):
        xt = page_refs[p][...].reshape(KV_DIM, PAGE_SIZE).T
        for n in range(per_page):
            lo = (p * per_page + n) * _BLOCK_PITCH
            for h in range(2):
                rows_ref[h, lo:lo + CMP_BLK, :] = xt[n * CMP_BLK:(n + 1) * CMP_BLK, h * 128:(h + 1) * 128]
    n_out = n_pages * per_page
    acc = jnp.broadcast_to(b_ref[...], (n_out, KV_DIM))
    for j in range(CMP_BLK):
        for h in range(2):
            piece = rows_ref[h, pl.ds(j, n_out, stride=_BLOCK_PITCH), :].astype(BF16)
            lo = j * KV_DIM + h * 128
            acc = acc + jnp.dot(piece, w_ref[lo:lo + 128, :], preferred_element_type=F32)
    o_ref[...] = acc


def _compress_pages(cache_t, page_table, w_big, bias):
    assert KV_DIM == 256
    b, per_b = page_table.shape
    group = max(1, 32 // per_b)
    assert b % group == 0
    n_pages = group * per_b
    n_out = n_pages * (PAGE_SIZE // CMP_BLK)

    def page(p):
        return pl.BlockSpec((None, N_KV, HEAD_DIM, PAGE_SIZE),
                            lambda i, pt, p=p: (pt[i * group + p // per_b, p % per_b], 0, 0, 0))

    return pl.pallas_call(
        functools.partial(_compress_pages_kernel, n_pages=n_pages),
        grid_spec=pltpu.PrefetchScalarGridSpec(
            num_scalar_prefetch=1,
            grid=(b // group,),
            in_specs=[page(p) for p in range(n_pages)]
            + [pl.BlockSpec((CMP_BLK * KV_DIM, KV_DIM), lambda i, pt: (0, 0), pipeline_mode=pl.Buffered(1)),
               pl.BlockSpec((1, KV_DIM), lambda i, pt: (0, 0))],
            out_specs=pl.BlockSpec((n_out, KV_DIM), lambda i, pt: (i, 0)),
            scratch_shapes=[pltpu.VMEM((2, n_out * _BLOCK_PITCH, 128), F32)]),
        out_shape=jax.ShapeDtypeStruct((b * per_b * (PAGE_SIZE // CMP_BLK), KV_DIM), F32),
        compiler_params=_cparams("arbitrary"),
        name="compress_pages",
    )(page_table, *([cache_t] * n_pages), w_big.astype(BF16), bias)


def _ada(c_all, ada_w, ada_b):
    mods = []
    for i in range(ada_w.shape[0]):
        mods.append(_matmul(c_all, ada_w[i], ada_b[i].reshape(1, -1), silu_in=True, tn=2 * D_MODEL,
                            name="ada_modulate"))
    return mods


def kernel(x_prompt, x_sample, c_prompt, c_sample, cache_k_cmp, cache_v_cmp, cache_k_sel, cache_v_sel,
           page_table, state_k_win, state_v_win, state_pool, rel_bias, ada_w, ada_b, norm_g, final_g,
           nsa_w_in, nsa_w_out, cmp_wk, cmp_wv, cmp_pe_k, cmp_pe_v, pool_w, pool_scale,
           ffn_wg, ffn_wu, ffn_wd):
    bp, tp, _ = x_prompt.shape
    bs, ts, _ = x_sample.shape
    n_phys = cache_k_cmp.shape[1]
    past = page_table.shape[1] * PAGE_SIZE
    depth = ada_w.shape[0]
    assert depth == 2 and tp % KEY_TILE == 0 and ts <= 8 and past % PAGE_SIZE == 0

    n_c = bp + bs
    c_all = jnp.pad(jnp.concatenate([c_prompt, c_sample], axis=0), ((0, -n_c % 8), (0, 0)))
    mods = _ada(c_all, ada_w, ada_b)

    def mod_prompt(i):
        return [v[:bp].reshape(bp, 1, D_MODEL) for v in jnp.split(mods[i], 6, axis=-1)]

    def mod_sample(i):
        return [jnp.repeat(v[bp:n_c], ts, axis=0) for v in jnp.split(mods[i], 6, axis=-1)]

    wk_big, k_bias = _compress_weights(cmp_wk[0], cmp_pe_k[0])
    wv_big, v_bias = _compress_weights(cmp_wv[0], cmp_pe_v[0])

    mp = bp * tp
    x = x_prompt.reshape(mp, D_MODEL)
    sh1, sc1, g1, sh2, sc2, g2 = mod_prompt(0)
    (kc, vc), kv_t, layouts = _nsa_project(x, norm_g[0, 0], sc1, sh1, nsa_w_in[0], tp, 512, True)
    blk_rows = lambda a: a.reshape(mp // CMP_BLK, CMP_BLK * KV_DIM)
    k_cmp = _compress(blk_rows(kc), wk_big, k_bias, 256)
    v_cmp = _compress(blk_rows(vc), wv_big, v_bias, 256)
    o_t = _prompt_attention(layouts, k_cmp, v_cmp, rel_bias, bp, tp)
    x = _matmul_residual_t(o_t, nsa_w_out[0], x, g1, tp, 512)
    x = _ffn(x, norm_g[0, 1], sc2, sh2, g2, ffn_wg[0], ffn_wu[0], ffn_wd[0], final_g, tp, 512, 1408, False)
    sh1, sc1, g1, sh2, sc2, g2 = mod_prompt(1)
    x3, pool_p = _pool_mix(x.reshape(bp, tp, D_MODEL), jnp.zeros((bp, POOL_STATE, D_MODEL), F32),
                           norm_g[1, 0], sc1, sh1, g1, pool_w[0], pool_scale[0], 0, 512)
    y_prompt = _ffn(x3.reshape(mp, D_MODEL), norm_g[1, 1], sc2, sh2, g2, ffn_wg[1], ffn_wu[1], ffn_wd[1],
                    final_g, tp, 512, 1408, True).reshape(bp, tp, D_MODEL)
    win = min(WINDOW, tp)
    st5 = lambda a, t0: jnp.transpose(a[:, :, t0:].reshape(bp, N_KV, HEAD_DIM, tp - t0), (0, 3, 1, 2))[None]
    prompt_states = tuple(st5(a, 0) for a in kv_t[:4]) + tuple(st5(a, tp - win) for a in kv_t[4:]) + (pool_p[None],)

    ms = bs * ts
    x = x_sample.reshape(ms, D_MODEL)
    sh1, sc1, g1, sh2, sc2, g2 = mod_sample(0)
    (kc, vc, ksl, vsl, kw, vw), (q, gates) = _nsa_project(x, norm_g[0, 0], sc1, sh1, nsa_w_in[0], ts, 512, False)
    assert (past + ts) // CMP_BLK == past // CMP_BLK
    pos_last = lambda a: jnp.transpose(a[0], (0, 2, 3, 1))
    k_cmp_s = _compress_pages(pos_last(cache_k_cmp), page_table, wk_big, k_bias)
    v_cmp_s = _compress_pages(pos_last(cache_v_cmp), page_table, wv_big, v_bias)
    o = _sample_attention(q, gates, k_cmp_s, v_cmp_s, pos_last(cache_k_sel), pos_last(cache_v_sel),
                          page_table, ksl, vsl, pos_last(state_k_win), pos_last(state_v_win), kw, vw,
                          rel_bias, bs, ts)
    x = _matmul_residual(o, nsa_w_out[0], x, g1, ts, 512)
    x = _ffn(x, norm_g[0, 1], sc2, sh2, g2, ffn_wg[0], ffn_wu[0], ffn_wd[0], final_g, ts, 512, 1408, False)
    sh1, sc1, g1, sh2, sc2, g2 = mod_sample(1)
    b3 = lambda v: v.reshape(bs, ts, D_MODEL)[:, :1]
    x3, pool_s = _pool_mix(x.reshape(bs, ts, D_MODEL), state_pool[0], norm_g[1, 0], b3(sc1), b3(sh1), b3(g1),
                           pool_w[0], pool_scale[0], past, 512)
    y_sample = _ffn(x3.reshape(ms, D_MODEL), norm_g[1, 1], sc2, sh2, g2, ffn_wg[1], ffn_wu[1], ffn_wd[1],
                    final_g, ts, 512, 1408, True).reshape(bs, ts, D_MODEL)
    st5 = lambda a: a.reshape(1, bs, ts, N_KV, HEAD_DIM)
    kw_ext = jnp.concatenate([state_k_win[0], st5(kw)[0]], axis=1)[:, ts:]
    vw_ext = jnp.concatenate([state_v_win[0], st5(vw)[0]], axis=1)[:, ts:]
    sample_states = (st5(kc), st5(vc), st5(ksl), st5(vsl), kw_ext[None], vw_ext[None], pool_s[None])

    return (y_prompt, y_sample) + prompt_states + sample_states
```

```python
import functools
import math

import jax
import jax.numpy as jnp
from jax import lax
from jax.experimental import pallas as pl
from jax.experimental.pallas import tpu as pltpu

D_MODEL = 1024
N_HEADS = 16
N_KV = 4
GROUP = N_HEADS // N_KV
HEAD_DIM = 64
Q_DIM = N_HEADS * HEAD_DIM
KV_DIM = N_KV * HEAD_DIM
CMP_BLK = 32
SEL_BLK = 64
N_SEL = 16
WINDOW = 512
Q_CHUNK = 64
N_BUCKETS = 32
MAX_DISTANCE = 128
POOL_WINDOWS = (2, 4, 8, 16)
POOL_GROUP_DIM = D_MODEL // len(POOL_WINDOWS)
POOL_STATE = max(POOL_WINDOWS) - 1
PAGE_SIZE = 128
RMS_EPS = 1e-6
NEG = -1e30
BIG = 1e9

KEY_TILE = 128
CHUNK = 128
LANES = GROUP * CHUNK
LOG2E = math.log2(math.e)
Q_SCALE_LOG2 = HEAD_DIM ** -0.5 * LOG2E
SAMPLE_LANES = 128
VMEM_LIMIT_BYTES = 48 * 1024 * 1024
ROW_TILE = 512
FFN_SLICES = 2

F32 = jnp.float32
BF16 = jnp.bfloat16


def _cparams(*sem):
    return pltpu.CompilerParams(dimension_semantics=sem, vmem_limit_bytes=VMEM_LIMIT_BYTES)


def _silu(x):
    return x * (1.0 / (1.0 + jnp.exp(-x)))


def _sigmoid(x):
    return 1.0 / (1.0 + jnp.exp(-x))


def _normmod(x, g, sc, sh):
    ms = jnp.mean(x * x, axis=-1, keepdims=True)
    return (x * lax.rsqrt(ms + RMS_EPS) * g) * (1.0 + sc) + sh


def _mod_spec(mod, tm, rows_per_batch):
    if mod.ndim == 3:
        return pl.BlockSpec((None, 1, D_MODEL), lambda i, *_: ((i * tm) // rows_per_batch, 0, 0))
    return pl.BlockSpec((tm, D_MODEL), lambda i, *_: (i, 0))


def _mm_kernel(a_ref, w_ref, b_ref, o_ref, *, silu_in):
    a = a_ref[...]
    if silu_in:
        a = _silu(a)
    o_ref[...] = jnp.dot(a.astype(BF16), w_ref[...], preferred_element_type=F32) + b_ref[...]


def _matmul(a, w, bias=None, *, silu_in=False, tm=256, tn=None, name="matmul"):
    m, k = a.shape
    n = w.shape[1]
    tm = min(tm, m)
    tn = n if tn is None else tn
    assert m % tm == 0 and n % tn == 0
    if bias is None:
        bias = jnp.zeros((1, n), F32)
    return pl.pallas_call(
        functools.partial(_mm_kernel, silu_in=silu_in),
        grid=(m // tm, n // tn),
        in_specs=[pl.BlockSpec((tm, k), lambda i, j: (i, 0)),
                  pl.BlockSpec((k, tn), lambda i, j: (0, j)),
                  pl.BlockSpec((1, tn), lambda i, j: (0, j))],
        out_specs=pl.BlockSpec((tm, tn), lambda i, j: (i, j)),
        out_shape=jax.ShapeDtypeStruct((m, n), F32),
        compiler_params=_cparams("parallel", "parallel"),
        name=name,
    )(a, w.astype(BF16), bias)


_PROJ_MAIN = Q_DIM + 6 * KV_DIM
_GATE_PAD = 128


def _proj_kernel(x_ref, g_ref, sc_ref, sh_ref, w_ref, wg_ref, *out_refs, attn_layouts, tm):
    hb = _normmod(x_ref[...], g_ref[...], sc_ref[...], sh_ref[...]).astype(BF16)
    q = jnp.dot(hb, w_ref[:, :Q_DIM], preferred_element_type=F32)
    gates = jnp.dot(hb, wg_ref[...], preferred_element_type=F32)
    kvs = []
    for n in range(6):
        lo = Q_DIM + n * KV_DIM
        kvs.append(jnp.dot(hb, w_ref[:, lo:lo + KV_DIM], preferred_element_type=F32))
    if not attn_layouts:
        for n in range(6):
            out_refs[n][...] = kvs[n]
        out_refs[6][...] = q
        out_refs[7][...] = gates
        return
    kt_refs = out_refs[:6]
    qt_ref, gt_ref, ks_ref, vst_ref, kw_ref, vwt_ref, q_scr, gate_scr, kv_scr = out_refs[6:]
    kvts = []
    for n in range(6):
        kv_scr[n] = kvs[n]
        kvts.append(kv_scr[n].T)
        kt_refs[n][...] = kvts[n]
    n_chunks = tm // CHUNK
    q_scr[...] = q * Q_SCALE_LOG2
    gate_scr[...] = gates
    qt = q_scr[...].T
    gt = gate_scr[...].T
    for c in range(n_chunks):
        tok = slice(c * CHUNK, (c + 1) * CHUNK)
        for kv in range(N_KV):
            for gi in range(GROUP):
                h = kv * GROUP + gi
                lanes = slice(gi * CHUNK, (gi + 1) * CHUNK)
                qt_ref[c, kv, :, lanes] = qt[h * HEAD_DIM:(h + 1) * HEAD_DIM, tok].astype(BF16)
                gt_ref[c, kv, :, lanes] = gt[3 * h:3 * h + 3, tok]
    for k_ref, vt_ref, k, vt in ((ks_ref, vst_ref, kvs[2], kvts[3]), (kw_ref, vwt_ref, kvs[4], kvts[5])):
        for kv in range(N_KV):
            cols = slice(kv * HEAD_DIM, (kv + 1) * HEAD_DIM)
            k_ref[kv] = k[:, cols].astype(BF16)
            for c in range(n_chunks):
                vt_ref[kv, c, 0:HEAD_DIM, :] = vt[cols, c * CHUNK:(c + 1) * CHUNK].astype(BF16)
                vt_ref[kv, c, HEAD_DIM:, :] = jnp.ones((V_ROWS - HEAD_DIM, CHUNK), BF16)


def _nsa_project(x, g, sc, sh, w_in, rows_per_batch, tm, attn_layouts):
    m = x.shape[0]
    tm = min(tm, m)
    w_main = w_in[:, :_PROJ_MAIN].astype(BF16)
    w_gate = jnp.pad(w_in[:, _PROJ_MAIN:], ((0, 0), (0, _GATE_PAD - 3 * N_HEADS))).astype(BF16)
    row = lambda n: pl.BlockSpec((tm, n), lambda i: (i, 0))
    if attn_layouts:
        assert tm % CHUNK == 0 and rows_per_batch % tm == 0
        nck = tm // CHUNK
        tpb = rows_per_batch // tm
        out_specs = [pl.BlockSpec((None, KV_DIM, tm), lambda i: (i // tpb, 0, i % tpb))] * 6
        out_shape = [jax.ShapeDtypeStruct((m // rows_per_batch, KV_DIM, rows_per_batch), F32)] * 6
        chunked = lambda *s: pl.BlockSpec((nck,) + s, lambda i: (i,) + (0,) * len(s))
        k_spec = pl.BlockSpec((N_KV, tm, HEAD_DIM), lambda i: (0, i, 0))
        vt_spec = pl.BlockSpec((N_KV, nck, V_ROWS, CHUNK), lambda i: (0, i, 0, 0))
        k_shape = jax.ShapeDtypeStruct((N_KV, m, HEAD_DIM), BF16)
        vt_shape = jax.ShapeDtypeStruct((N_KV, m // CHUNK, V_ROWS, CHUNK), BF16)
        out_specs += [chunked(N_KV, HEAD_DIM, LANES), chunked(N_KV, 3, LANES), k_spec, vt_spec, k_spec, vt_spec]
        out_shape += [jax.ShapeDtypeStruct((m // CHUNK, N_KV, HEAD_DIM, LANES), BF16),
                      jax.ShapeDtypeStruct((m // CHUNK, N_KV, 3, LANES), F32),
                      k_shape, vt_shape, k_shape, vt_shape]
    else:
        out_specs = [row(KV_DIM)] * 6 + [row(Q_DIM), row(_GATE_PAD)]
        out_shape = ([jax.ShapeDtypeStruct((m, KV_DIM), F32)] * 6
                     + [jax.ShapeDtypeStruct((m, Q_DIM), F32), jax.ShapeDtypeStruct((m, _GATE_PAD), F32)])
    outs = pl.pallas_call(
        functools.partial(_proj_kernel, attn_layouts=attn_layouts, tm=tm),
        grid=(m // tm,),
        in_specs=[row(D_MODEL),
                  pl.BlockSpec((1, D_MODEL), lambda i: (0, 0)),
                  _mod_spec(sc, tm, rows_per_batch), _mod_spec(sh, tm, rows_per_batch),
                  pl.BlockSpec((D_MODEL, _PROJ_MAIN), lambda i: (0, 0)),
                  pl.BlockSpec((D_MODEL, _GATE_PAD), lambda i: (0, 0))],
        out_specs=out_specs,
        out_shape=out_shape,
        scratch_shapes=([pltpu.VMEM((tm, Q_DIM), F32), pltpu.VMEM((tm, _GATE_PAD), F32),
                         pltpu.VMEM((6, tm, KV_DIM), F32)] if attn_layouts else []),
        compiler_params=_cparams("parallel"),
        name="nsa_project",
    )(x, g.reshape(1, D_MODEL), sc, sh, w_main, w_gate)
    if attn_layouts:
        return outs[:6], outs[6:]
    return outs[:6], outs[6:]


def _mm_res_kernel(a_ref, w_ref, x_ref, gate_ref, o_ref):
    y = jnp.dot(a_ref[...].astype(BF16), w_ref[...], preferred_element_type=F32)
    o_ref[...] = x_ref[...] + gate_ref[...] * y


def _matmul_residual(a, w, x, gate, rows_per_batch, tm):
    m, k = a.shape
    tm = min(tm, m)
    return pl.pallas_call(
        _mm_res_kernel,
        grid=(m // tm,),
        in_specs=[pl.BlockSpec((tm, k), lambda i: (i, 0)),
                  pl.BlockSpec((k, D_MODEL), lambda i: (0, 0)),
                  pl.BlockSpec((tm, D_MODEL), lambda i: (i, 0)),
                  _mod_spec(gate, tm, rows_per_batch)],
        out_specs=pl.BlockSpec((tm, D_MODEL), lambda i: (i, 0)),
        out_shape=jax.ShapeDtypeStruct((m, D_MODEL), F32),
        compiler_params=_cparams("parallel"),
        name="out_proj_residual",
    )(a, w.astype(BF16), x, gate)


def _mm_res_t_kernel(ot_ref, w_ref, x_ref, gate_ref, o_ref, *, n_chunks):
    chunks = []
    for c in range(n_chunks):
        rows = [ot_ref[c, kv, :, gi * CHUNK:(gi + 1) * CHUNK] for kv in range(N_KV) for gi in range(GROUP)]
        chunks.append(jnp.concatenate(rows, axis=0).astype(F32).T)
    a = jnp.concatenate(chunks, axis=0).astype(BF16)
    y = jnp.dot(a, w_ref[...], preferred_element_type=F32)
    o_ref[...] = x_ref[...] + gate_ref[...] * y


def _matmul_residual_t(o_t, w, x, gate, rows_per_batch, tm):
    m = x.shape[0]
    nck = tm // CHUNK
    return pl.pallas_call(
        functools.partial(_mm_res_t_kernel, n_chunks=nck),
        grid=(m // tm,),
        in_specs=[pl.BlockSpec((nck, N_KV, HEAD_DIM, LANES), lambda i: (i, 0, 0, 0)),
                  pl.BlockSpec((Q_DIM, D_MODEL), lambda i: (0, 0)),
                  pl.BlockSpec((tm, D_MODEL), lambda i: (i, 0)),
                  _mod_spec(gate, tm, rows_per_batch)],
        out_specs=pl.BlockSpec((tm, D_MODEL), lambda i: (i, 0)),
        out_shape=jax.ShapeDtypeStruct((m, D_MODEL), F32),
        compiler_params=_cparams("parallel"),
        name="out_proj_residual_t",
    )(o_t, w.astype(BF16), x, gate)


def _ffn_kernel(x_ref, g_ref, sc_ref, sh_ref, gate_ref, wg_ref, wu_ref, wd_ref, fg_ref, o_ref,
                *, final_norm, tf):
    x = x_ref[...]
    hb = _normmod(x, g_ref[...], sc_ref[...], sh_ref[...]).astype(BF16)
    acc = None
    for lo in range(0, wg_ref.shape[1], tf):
        a = jnp.dot(hb, wg_ref[:, lo:lo + tf], preferred_element_type=F32)
        u = jnp.dot(hb, wu_ref[:, lo:lo + tf], preferred_element_type=F32)
        act = (_silu(a) * u).astype(BF16)
        part = jnp.dot(act, wd_ref[lo:lo + tf, :], preferred_element_type=F32)
        acc = part if acc is None else acc + part
    y = x + gate_ref[...] * acc
    if final_norm:
        ms = jnp.mean(y * y, axis=-1, keepdims=True)
        y = y * lax.rsqrt(ms + RMS_EPS) * fg_ref[...]
    o_ref[...] = y


def _ffn(x, g, sc, sh, gate, wg, wu, wd, final_g, rows_per_batch, tm, n_slices, final_norm):
    m = x.shape[0]
    d_ff = wg.shape[1]
    tm = min(tm, m)
    assert d_ff % (n_slices * 128) == 0
    tf = d_ff // n_slices
    once = pl.Buffered(1)
    vec = pl.BlockSpec((1, D_MODEL), lambda i: (0, 0))
    return pl.pallas_call(
        functools.partial(_ffn_kernel, final_norm=final_norm, tf=tf),
        grid=(m // tm,),
        in_specs=[pl.BlockSpec((tm, D_MODEL), lambda i: (i, 0)),
                  vec,
                  _mod_spec(sc, tm, rows_per_batch), _mod_spec(sh, tm, rows_per_batch),
                  _mod_spec(gate, tm, rows_per_batch),
                  pl.BlockSpec((D_MODEL, d_ff), lambda i: (0, 0), pipeline_mode=once),
                  pl.BlockSpec((D_MODEL, d_ff), lambda i: (0, 0), pipeline_mode=once),
                  pl.BlockSpec((d_ff, D_MODEL), lambda i: (0, 0), pipeline_mode=once),
                  vec],
        out_specs=pl.BlockSpec((tm, D_MODEL), lambda i: (i, 0)),
        out_shape=jax.ShapeDtypeStruct((m, D_MODEL), F32),
        compiler_params=_cparams("parallel"),
        name="ffn",
    )(x, g.reshape(1, D_MODEL), sc, sh, gate, wg.astype(BF16), wu.astype(BF16), wd.astype(BF16),
      final_g.reshape(1, D_MODEL))


_POOL_HALO = 16


def _pool_kernel(x_ref, xprev_ref, state_ref, g_ref, sc_ref, sh_ref, gate_ref, w_ref, ls_ref,
                 o_ref, st_ref, ext_ref, *, tm, pos0):
    i = pl.program_id(1)
    g, sc, sh = g_ref[...], sc_ref[...], sh_ref[...]
    u = _normmod(x_ref[...], g, sc, sh)
    prev = jnp.where(i == 0, state_ref[...], _normmod(xprev_ref[...], g, sc, sh))
    ext_ref[0:_POOL_HALO, :] = prev
    ext_ref[_POOL_HALO:_POOL_HALO + tm, :] = u
    st_ref[...] = ext_ref[tm:tm + _POOL_HALO, :]

    pos = pos0 + i * tm + lax.broadcasted_iota(jnp.int32, (tm, 1), 0)
    mixed = []
    for gi, w in enumerate(POOL_WINDOWS):
        lo = gi * POOL_GROUP_DIM
        s = u[:, lo:lo + POOL_GROUP_DIM]
        for k in range(1, w):
            s = s + ext_ref[_POOL_HALO - k:_POOL_HALO - k + tm, lo:lo + POOL_GROUP_DIM]
        cnt = jnp.minimum(pos + 1, w).astype(F32)
        pooled = s / cnt - u[:, lo:lo + POOL_GROUP_DIM]
        mixed.append(jnp.dot(pooled.astype(BF16), w_ref[gi], preferred_element_type=F32))
    y = jnp.concatenate(mixed, axis=-1) * ls_ref[...]
    o_ref[...] = x_ref[...] + gate_ref[...] * y


def _pool_mix(x3, state, g, sc, sh, gate, w_grp, layer_scale, pos0, tm):
    b, t, _ = x3.shape
    tm = min(tm, t)
    state16 = jnp.pad(state, ((0, 0), (_POOL_HALO - POOL_STATE, 0), (0, 0)))
    if t >= _POOL_HALO:
        xprev = x3
        nprev = tm // _POOL_HALO
        prev_spec = pl.BlockSpec((None, _POOL_HALO, D_MODEL),
                                 lambda bi, i: (bi, jnp.maximum(i * nprev - 1, 0), 0))
    else:
        xprev = state16
        prev_spec = pl.BlockSpec((None, _POOL_HALO, D_MODEL), lambda bi, i: (bi, 0, 0))
    vec = pl.BlockSpec((1, D_MODEL), lambda bi, i: (0, 0))
    bvec = pl.BlockSpec((None, 1, D_MODEL), lambda bi, i: (bi, 0, 0))
    out, st = pl.pallas_call(
        functools.partial(_pool_kernel, tm=tm, pos0=pos0),
        grid=(b, t // tm),
        in_specs=[pl.BlockSpec((None, tm, D_MODEL), lambda bi, i: (bi, i, 0)),
                  prev_spec,
                  pl.BlockSpec((None, _POOL_HALO, D_MODEL), lambda bi, i: (bi, 0, 0)),
                  vec, bvec, bvec, bvec,
                  pl.BlockSpec((len(POOL_WINDOWS), POOL_GROUP_DIM, POOL_GROUP_DIM),
                               lambda bi, i: (0, 0, 0)),
                  vec],
        out_specs=[pl.BlockSpec((None, tm, D_MODEL), lambda bi, i: (bi, i, 0)),
                   pl.BlockSpec((None, _POOL_HALO, D_MODEL), lambda bi, i: (bi, 0, 0))],
        out_shape=[jax.ShapeDtypeStruct((b, t, D_MODEL), F32),
                   jax.ShapeDtypeStruct((b, _POOL_HALO, D_MODEL), F32)],
        scratch_shapes=[pltpu.VMEM((tm + _POOL_HALO, D_MODEL), F32)],
        compiler_params=_cparams("parallel", "arbitrary"),
        name="pool_mix",
    )(x3, xprev, state16, g.reshape(1, D_MODEL), sc, sh, gate, w_grp.astype(BF16),
      layer_scale.reshape(1, D_MODEL))
    return out, st[:, _POOL_HALO - POOL_STATE:]


def _rel_bucket(dist):
    d = jnp.maximum(dist, 0)
    max_exact = N_BUCKETS // 2
    large = max_exact + (jnp.log(jnp.maximum(d, 1).astype(F32) / max_exact)
                         / math.log(MAX_DISTANCE / max_exact) * (N_BUCKETS - max_exact)).astype(jnp.int32)
    large = jnp.minimum(large, N_BUCKETS - 1)
    return jnp.where(d < max_exact, d, large)


def _dist_bias(rel_bias):
    return rel_bias[_rel_bucket(jnp.arange(MAX_DISTANCE + 1, dtype=jnp.int32))]


def _lookup(table, idx):
    onehot = (idx[..., None] == jnp.arange(table.shape[0], dtype=jnp.int32)).astype(F32)
    return jnp.einsum("...d,dh->...h", onehot, table, precision=lax.Precision.HIGHEST)


def _bias_tile(fd, dist, valid):
    k, q = dist.shape
    bias = _lookup(fd, jnp.clip(dist, 0, MAX_DISTANCE)).reshape(k, q, N_KV, GROUP)
    bias = jnp.where(valid[:, :, None, None], bias, NEG)
    return bias.transpose(2, 0, 3, 1).reshape(N_KV, k, GROUP * q)


def _prompt_tables(rel_bias, t):
    fd = _dist_bias(rel_bias) * LOG2E
    kj = jnp.arange(KEY_TILE, dtype=jnp.int32)[:, None]
    qi = jnp.arange(CHUNK, dtype=jnp.int32)[None, :]
    dist = qi - kj
    near = jnp.stack([_bias_tile(fd, dist + KEY_TILE, dist + KEY_TILE >= 0),
                      _bias_tile(fd, dist, dist >= 0)])
    wfirst = _bias_tile(fd, dist + WINDOW, dist < 0)
    far = _bias_tile(fd, jnp.full((1, CHUNK), MAX_DISTANCE, jnp.int32), jnp.ones((1, CHUNK), bool))
    per_chunk = CHUNK // CMP_BLK
    assert CHUNK == 128 and CMP_BLK == 32 and MAX_DISTANCE == 128
    rel0, n_rel = -per_chunk, 2 * per_chunk
    rel = rel0 + jnp.arange(n_rel, dtype=jnp.int32)[:, None]
    dist = qi - CMP_BLK * rel - (CMP_BLK - 1)
    crel = _bias_tile(fd, dist, dist >= 0)
    relm = (jnp.arange(t // CMP_BLK, dtype=jnp.int32)[None, :]
            - per_chunk * jnp.arange(t // CHUNK, dtype=jnp.int32)[:, None])[:, None, :, None]
    cmp_tab = jnp.where(relm >= rel0 + n_rel, NEG, far[None])
    for r in range(n_rel):
        cmp_tab = jnp.where(relm == rel0 + r, crel[None, :, r:r + 1, :], cmp_tab)
    return near, wfirst, far, cmp_tab


V_ROWS = HEAD_DIM + 16


def _half_exp2(s, m, top, bot):
    return jnp.exp2(jnp.concatenate([s[:SEL_BLK] - (m - top), s[SEL_BLK:] - (m - bot)], axis=0))


def _normalise(acc):
    return acc[:HEAD_DIM] * (1.0 / jnp.maximum(acc[HEAD_DIM:HEAD_DIM + 1], 1e-30))


def _prompt_attn_kernel(q_ref, g_ref, kc_ref, vct_ref, cb_ref, ks_ref, vst_ref, kw_ref, vwt_ref,
                        near_ref, wfirst_ref, far_ref, o_ref,
                        imp_ref, val_ref, rank_ref, selrep_ref, m_ref, mw_ref, alpha_ref, acc_ref, accw_ref,
                        mix_ref, s_ref, smax_ref, p_ref, sc_ref, pc_ref, *, n_blk, n_sel):
    i = pl.program_id(1)

    def k_rows(ref, kv, jt):
        return ref[kv, pl.ds(pl.multiple_of(jt * KEY_TILE, KEY_TILE), KEY_TILE), :]

    def sel_rows(kv, jt):
        return selrep_ref[kv, pl.ds(2 * jt, 1), :], selrep_ref[kv, pl.ds(2 * jt + 1, 1), :]

    col_max = []
    for kv in range(N_KV):
        s = jnp.dot(kc_ref[kv], q_ref[kv], preferred_element_type=F32) + cb_ref[kv]
        sc_ref[kv] = s
        col_max.append(jnp.max(s, axis=0, keepdims=True))
    for kv in range(N_KV):
        s = sc_ref[kv]
        p = jnp.where(cb_ref[kv] > 0.5 * NEG, jnp.exp2(s - col_max[kv]), 0.0)
        p = p * (1.0 / jnp.maximum(jnp.sum(p, axis=0, keepdims=True), 1e-30))
        pc_ref[kv] = p.astype(BF16)
        imp_ref[kv] = functools.reduce(
            lambda a, b: a + b, [p[:, gi * CHUNK:(gi + 1) * CHUNK] for gi in range(GROUP)])
    for kv in range(N_KV):
        o_c = jnp.dot(vct_ref[kv], pc_ref[kv], preferred_element_type=F32)
        mix_ref[kv] = _sigmoid(g_ref[kv, 0:1, :]) * o_c

    imp = jnp.concatenate(
        [imp_ref[kv, pl.ds(0, n_blk, stride=2), :] + imp_ref[kv, pl.ds(1, n_blk, stride=2), :]
         for kv in range(N_KV)], axis=1)
    blk = lax.broadcasted_iota(jnp.int32, (n_blk, LANES), 0)
    lane = lax.broadcasted_iota(jnp.int32, (1, LANES), 1)
    assert CHUNK == 2 * SEL_BLK
    qblk = 2 * i + jnp.where((lane & (CHUNK - 1)) >= SEL_BLK, 1, 0)
    forced = (blk == 0) | (blk == qblk) | (blk == qblk - 1)
    val_ref[...] = jnp.where(forced, BIG, jnp.where(blk <= qblk, imp, -1.0))
    rank_ref[...] = jnp.zeros_like(rank_ref)
    sub = lax.broadcasted_iota(jnp.int32, (8, LANES), 0)
    group = 8
    assert n_blk % group == 0
    for grp in range(n_blk // group):
        @pl.when(group * grp <= 2 * i + 1)
        def _(grp=grp):
            for r8 in range(n_blk // 8):
                lo = r8 * 8
                piece = val_ref[lo:lo + 8, :]
                acc = rank_ref[lo:lo + 8, :]
                for j in range(grp * group, (grp + 1) * group):
                    row = val_ref[j:j + 1, :]
                    if j < lo:
                        acc = acc + jnp.where(row >= piece, 1.0, 0.0)
                    elif j >= lo + 8:
                        acc = acc + jnp.where(row > piece, 1.0, 0.0)
                    else:
                        acc = acc + jnp.where(sub > j - lo, jnp.where(row >= piece, 1.0, 0.0),
                                              jnp.where(row > piece, 1.0, 0.0))
                rank_ref[lo:lo + 8, :] = acc
    selneg = jnp.where(rank_ref[...] < n_sel, 0.0, NEG)
    for kv in range(N_KV):
        selrep_ref[kv] = jnp.concatenate([selneg[:, kv * CHUNK:(kv + 1) * CHUNK]] * GROUP, axis=1)

    prev = jnp.maximum(i - 1, 0)
    n_far = jnp.maximum(i - 1, 0)
    last = jnp.maximum(n_far - 1, 0)
    zero_row = jnp.zeros((1, LANES), F32)
    for state in (mw_ref, m_ref):
        state[...] = jnp.full(state.shape, NEG, F32)
    for state in (accw_ref, acc_ref):
        state[...] = jnp.zeros(state.shape, F32)

    def scores(k_ref, jt, slot, table=None):
        for kv in range(N_KV):
            s = jnp.dot(k_rows(k_ref, kv, jt), q_ref[kv], preferred_element_type=F32)
            if table is not None:
                s = s + table(kv)
            s_ref[slot, kv] = s
            smax_ref[slot, kv, 0:1, :] = jnp.max(s[:SEL_BLK], axis=0, keepdims=True)
            smax_ref[slot, kv, 1:2, :] = jnp.max(s[SEL_BLK:], axis=0, keepdims=True)

    def softmax(slot, rows, stat_ref):
        for kv in range(N_KV):
            top, bot = rows(kv)
            m_old = stat_ref[kv]
            m_new = jnp.maximum(m_old, jnp.maximum(smax_ref[slot, kv, 0:1, :] + top,
                                                   smax_ref[slot, kv, 1:2, :] + bot))
            stat_ref[kv] = m_new
            alpha_ref[slot, kv] = jnp.exp2(m_old - m_new)
            p_ref[slot, kv] = _half_exp2(s_ref[slot, kv], m_new, top, bot).astype(BF16)

    def values(vt_ref, jt, slot, out_ref):
        for kv in range(N_KV):
            out_ref[kv] = alpha_ref[slot, kv] * out_ref[kv] + jnp.dot(vt_ref[kv, jt], p_ref[slot, kv],
                                                                      preferred_element_type=F32)

    def const_rows(row):
        return lambda kv: (row, row)

    def window_far_rows(back):
        off = jnp.where(i >= back, 0.0, NEG)
        return lambda kv: (far_ref[kv] + off, far_ref[kv] + off)

    def sel_near_rows(jt, off):
        def rows(kv):
            top, bot = sel_rows(kv, jt)
            return top + off, bot + off
        return rows

    prev_off = jnp.where(i >= 1, 0.0, NEG)
    back = lambda n: jnp.maximum(i - n, 0)
    diag_table = lambda kv: near_ref[1, kv]
    prev_table = lambda kv: near_ref[0, kv]
    static_tiles = [
        (kw_ref, vwt_ref, i, diag_table, const_rows(zero_row), mw_ref, accw_ref),
        (kw_ref, vwt_ref, prev, prev_table, const_rows(zero_row + prev_off), mw_ref, accw_ref),
        (kw_ref, vwt_ref, back(2), None, window_far_rows(2), mw_ref, accw_ref),
        (kw_ref, vwt_ref, back(3), None, window_far_rows(3), mw_ref, accw_ref),
        (kw_ref, vwt_ref, back(4), lambda kv: wfirst_ref[kv],
         const_rows(zero_row + jnp.where(i >= 4, 0.0, NEG)), mw_ref, accw_ref),
        (ks_ref, vst_ref, i, diag_table, sel_near_rows(i, 0.0), m_ref, acc_ref),
        (ks_ref, vst_ref, prev, prev_table, sel_near_rows(prev, prev_off), m_ref, acc_ref),
    ]

    def far_tile(jt):
        return jnp.minimum(jt, last)

    def far_rows(jt):
        off = jnp.where(jt < n_far, 0.0, NEG)
        def rows(kv):
            top, bot = sel_rows(kv, far_tile(jt))
            far = far_ref[kv] + off
            return top + far, bot + far
        return rows

    def far_values(jt, slot):
        values(vst_ref, jnp.where(jt < 0, prev, jnp.clip(jt, 0, last)), slot, acc_ref)

    n_static = len(static_tiles)
    k_ref0, _, jt0, table0, _, _, _ = static_tiles[0]
    scores(k_ref0, jt0, 0, table0)
    for t, (_, _, _, _, rows, stat_ref, _) in enumerate(static_tiles):
        if t >= 1:
            _, vt_ref, jt, _, _, _, out_ref = static_tiles[t - 1]
            values(vt_ref, jt, (t - 1) % 2, out_ref)
        softmax(t % 2, rows, stat_ref)
        if t + 1 < n_static:
            k_ref, _, jt, table, _, _, _ = static_tiles[t + 1]
            scores(k_ref, jt, (t + 1) % 2, table)
        else:
            scores(ks_ref, far_tile(0), (t + 1) % 2)
    assert n_static % 2 == 1

    def pair_step(j2):
        jt = 2 * j2
        far_values(jt - 1, 0)
        softmax(1, far_rows(jt), m_ref)
        scores(ks_ref, far_tile(jt + 1), 0)
        far_values(jt, 1)
        softmax(0, far_rows(jt + 1), m_ref)
        scores(ks_ref, far_tile(jt + 2), 1)

    def body(pairs_per_trip):
        def run(j, carry):
            for u in range(pairs_per_trip):
                pair_step(pairs_per_trip * j + u)
            return carry
        return run

    n_pairs = (n_far + 1) // 2
    n4 = n_pairs // 4
    n2 = n_pairs // 2
    lax.fori_loop(0, n4, body(4), 0)
    lax.fori_loop(2 * n4, n2, body(2), 0)
    lax.fori_loop(2 * n2, n_pairs, body(1), 0)
    far_values(2 * n_pairs - 1, 0)

    for kv in range(N_KV):
        o_ref[kv] = (mix_ref[kv] + _sigmoid(g_ref[kv, 2:3, :]) * _normalise(accw_ref[kv])
                     + _sigmoid(g_ref[kv, 1:2, :]) * _normalise(acc_ref[kv])).astype(o_ref.dtype)


def _prompt_attention(layouts, k_cmp, v_cmp, rel_bias, b, t):
    q_t, g_t, ks, vst, kw, vwt = layouts
    nch, nt, nc, n_blk = t // CHUNK, t // KEY_TILE, t // CMP_BLK, t // SEL_BLK
    kc = k_cmp.reshape(b, nc, N_KV, HEAD_DIM).transpose(0, 2, 1, 3).astype(BF16)
    vct = v_cmp.reshape(b, nc, N_KV, HEAD_DIM).transpose(0, 2, 3, 1).astype(BF16)
    near, wfirst, far, cmp_tab = _prompt_tables(rel_bias, t)

    once = pl.Buffered(1)
    per_b = lambda *shape: pl.BlockSpec((None,) + shape, lambda bi, i: (bi,) + (0,) * len(shape),
                                        pipeline_mode=once)
    const = lambda *shape: pl.BlockSpec(shape, lambda bi, i: (0,) * len(shape), pipeline_mode=once)
    chunk = lambda *shape: pl.BlockSpec((None,) + shape, lambda bi, i: (bi * nch + i,) + (0,) * len(shape))
    k_spec = pl.BlockSpec((N_KV, t, HEAD_DIM), lambda bi, i: (0, bi, 0), pipeline_mode=once)
    vt_spec = pl.BlockSpec((N_KV, nt, V_ROWS, KEY_TILE), lambda bi, i: (0, bi, 0, 0), pipeline_mode=once)
    stat = pltpu.VMEM((N_KV, 1, LANES), F32)
    return pl.pallas_call(
        functools.partial(_prompt_attn_kernel, n_blk=n_blk, n_sel=min(N_SEL, n_blk)),
        grid=(b, nch),
        in_specs=[chunk(N_KV, HEAD_DIM, LANES),
                  chunk(N_KV, 3, LANES),
                  per_b(N_KV, nc, HEAD_DIM),
                  per_b(N_KV, HEAD_DIM, nc),
                  pl.BlockSpec((None, N_KV, nc, LANES), lambda bi, i: (i, 0, 0, 0)),
                  k_spec, vt_spec, k_spec, vt_spec,
                  const(2, N_KV, KEY_TILE, LANES),
                  const(N_KV, KEY_TILE, LANES),
                  const(N_KV, 1, LANES)],
        out_specs=chunk(N_KV, HEAD_DIM, LANES),
        out_shape=jax.ShapeDtypeStruct((b * nch, N_KV, HEAD_DIM, LANES), BF16),
        scratch_shapes=[pltpu.VMEM((N_KV, nc, CHUNK), F32),
                        pltpu.VMEM((n_blk, LANES), F32),
                        pltpu.VMEM((n_blk, LANES), F32),
                        pltpu.VMEM((N_KV, n_blk, LANES), F32),
                        stat, stat,
                        pltpu.VMEM((2, N_KV, 1, LANES), F32),
                        pltpu.VMEM((N_KV, V_ROWS, LANES), F32),
                        pltpu.VMEM((N_KV, V_ROWS, LANES), F32),
                        pltpu.VMEM((N_KV, HEAD_DIM, LANES), F32),
                        pltpu.VMEM((2, N_KV, KEY_TILE, LANES), F32),
                        pltpu.VMEM((2, N_KV, 2, LANES), F32),
                        pltpu.VMEM((2, N_KV, KEY_TILE, LANES), BF16),
                        pltpu.VMEM((N_KV, nc, LANES), F32),
                        pltpu.VMEM((N_KV, nc, LANES), BF16)],
        compiler_params=_cparams("parallel", "arbitrary"),
        name="nsa_prompt_attention",
    )(q_t, g_t, kc, vct, cmp_tab, ks, vst, kw, vwt, near, wfirst, far)


NEW_PAD = 128


def _sample_tables(rel_bias, past, t, wbuf, n_rows):
    lane = jnp.arange(SAMPLE_LANES, dtype=jnp.int32)
    g, kvh, qi = lane // (N_KV * t), (lane // t) % N_KV, lane % t
    used = lane < GROUP * N_KV * t
    head = jnp.where(used, kvh * GROUP + g, 0)
    qpos = past + qi
    fd_lane = jnp.take(_dist_bias(rel_bias), head, axis=1)
    dists = jnp.arange(MAX_DISTANCE + 1, dtype=jnp.int32)

    def tab(kpos, valid, n_near):
        dist = qpos[None, :] - kpos[:, None]
        n_far = kpos.shape[0] - n_near
        near_d = jnp.clip(dist[n_far:], 0, MAX_DISTANCE)
        near = jnp.sum(jnp.where(near_d[:, :, None] == dists, fd_lane.T[None], 0.0), axis=-1)
        far = jnp.broadcast_to(fd_lane[MAX_DISTANCE][None, :], (n_far, SAMPLE_LANES))
        bias = jnp.where(used[None, :], jnp.concatenate([far, near], axis=0), 0.0)
        return jnp.where(valid(dist) & (kpos[:, None] >= 0), bias, NEG)

    new_pos = past + jnp.where(jnp.arange(NEW_PAD) < t, jnp.arange(NEW_PAD, dtype=jnp.int32), 1 << 20)
    causal = lambda d: d >= 0
    window = lambda d: (d >= 0) & (d < WINDOW)
    nc = past // CMP_BLK
    assert wbuf >= MAX_DISTANCE and past >= MAX_DISTANCE
    t_cmp = tab(jnp.arange(nc, dtype=jnp.int32) * CMP_BLK + CMP_BLK - 1, causal, MAX_DISTANCE // CMP_BLK)
    rows = lambda a: a[:, :n_rows].T
    t_sel = rows(tab(jnp.concatenate([jnp.arange(past, dtype=jnp.int32), new_pos]), causal,
                     MAX_DISTANCE + NEW_PAD))
    t_win = rows(tab(jnp.concatenate([past - wbuf + jnp.arange(wbuf, dtype=jnp.int32), new_pos]), window,
                     MAX_DISTANCE + NEW_PAD))
    key = jnp.arange(past + NEW_PAD, dtype=jnp.int32)
    blk = jnp.where(key < past, key // SEL_BLK, past // SEL_BLK)
    expand = (blk[None, :] == jnp.arange(128, dtype=jnp.int32)[:, None]).astype(BF16)
    return t_cmp, t_sel, t_win, expand


def _sample_attn_kernel(pt_ref, *refs, n_pages, n_sel, t, wbuf):
    del pt_ref
    it = iter(refs)
    qbd_ref, qrow_ref, g_ref = next(it), next(it), next(it)
    kc_ref, vc_ref = next(it), next(it)
    ks_pages = [next(it) for _ in range(n_pages)]
    vs_pages = [next(it) for _ in range(n_pages)]
    ksn_ref, vsn_ref, kwin_ref, vwin_ref, kwn_ref, vwn_ref = (next(it) for _ in range(6))
    tcmp_ref, tsel_ref, twin_ref, expand_ref = (next(it) for _ in range(4))
    o_ref = next(it)
    imp_ref, val_ref, sel_ref, s_ref = (next(it) for _ in range(4))

    scale = HEAD_DIM ** -0.5
    nq = N_KV * t
    qs = qbd_ref[...] * scale
    qr = qrow_ref[...] * scale
    n_q = qr.shape[0]
    lane = lax.broadcasted_iota(jnp.int32, (1, SAMPLE_LANES), 1)
    blocks_per_page = PAGE_SIZE // CMP_BLK
    past = n_pages * PAGE_SIZE
    nt_dims = (((1,), (1,)), ((), ()))

    s = jnp.dot(kc_ref[...].astype(BF16), qs, preferred_element_type=F32) + tcmp_ref[...]
    m = jnp.max(s, axis=0, keepdims=True)
    p_c = jnp.exp(s - m)
    p_c = p_c / jnp.maximum(jnp.sum(p_c, axis=0, keepdims=True), 1e-30)
    o_c = lax.dot_general(p_c.astype(BF16), vc_ref[...].astype(BF16), (((0,), (0,)), ((), ())),
                          preferred_element_type=F32)[:n_q]

    imp = p_c
    for g in range(1, GROUP):
        imp = imp + pltpu.roll(p_c, SAMPLE_LANES - g * nq, 1)
    imp_ref[...] = imp
    n_pairs = n_pages * blocks_per_page // 2
    n_blk = n_pairs + 1
    n_rows = val_ref.shape[0]
    val_ref[...] = jnp.full((n_rows, SAMPLE_LANES), -2.0, F32)
    val_ref[0:n_pairs, :] = imp_ref[pl.ds(0, n_pairs, stride=2), :] + imp_ref[pl.ds(1, n_pairs, stride=2), :]
    blk = lax.broadcasted_iota(jnp.int32, (n_rows, SAMPLE_LANES), 0)
    qblk = n_blk - 1
    forced = (blk == 0) | (blk == qblk) | (blk == qblk - 1)
    val = jnp.where(forced, BIG, val_ref[...])
    val = jnp.where(blk < n_blk, val, -2.0)
    val_ref[...] = val
    rank = jnp.zeros((n_rows, SAMPLE_LANES), F32)
    for j in range(n_blk):
        row = val_ref[j:j + 1, :]
        rank = rank + jnp.where(blk > j, jnp.where(row >= val, 1.0, 0.0), jnp.where(row > val, 1.0, 0.0))
    selneg = jnp.where((rank < n_sel) & (lane < nq), 0.0, jnp.where(lane < nq, NEG, 0.0))
    selrep = selneg
    for g in range(1, GROUP):
        selrep = selrep + pltpu.roll(selneg, g * nq, 1)
    sel_ref[...] = jnp.zeros_like(sel_ref)
    sel_ref[0:n_rows, :] = jnp.where(selrep == 0.0, 1.0, 0.0)
    sel01 = sel_ref[...].T[:n_q].astype(BF16)

    def paged(page_ref):
        return page_ref[...].reshape(KV_DIM, PAGE_SIZE).astype(BF16)

    for p in range(n_pages + 1):
        cols = slice(p * KEY_TILE, (p + 1) * KEY_TILE)
        if p < n_pages:
            s = jnp.dot(qr, paged(ks_pages[p]), preferred_element_type=F32)
        else:
            s = lax.dot_general(qr, ksn_ref[...], nt_dims, preferred_element_type=F32)
        picked = jnp.dot(sel01, expand_ref[:, cols], preferred_element_type=F32)
        s_ref[:, cols] = s + tsel_ref[:, cols] + (picked - 1.0) * (-NEG)
    s = s_ref[...]
    p_s = jnp.exp(s - jnp.max(s, axis=1, keepdims=True))
    l = jnp.sum(p_s, axis=1, keepdims=True)
    p_s = p_s.astype(BF16)
    acc = jnp.dot(p_s[:, past:], vsn_ref[...], preferred_element_type=F32)
    for p in range(n_pages):
        acc = acc + lax.dot_general(p_s[:, p * KEY_TILE:(p + 1) * KEY_TILE], paged(vs_pages[p]), nt_dims,
                                    preferred_element_type=F32)
    o_s = acc / jnp.maximum(l, 1e-30)

    kwin = kwin_ref[...].reshape(KV_DIM, wbuf).astype(BF16)
    vwin = vwin_ref[...].reshape(KV_DIM, wbuf).astype(BF16)
    s_w = jnp.dot(qr, kwin, preferred_element_type=F32) + twin_ref[:, :wbuf]
    s_n = lax.dot_general(qr, kwn_ref[...], nt_dims, preferred_element_type=F32) + twin_ref[:, wbuf:]
    m = jnp.maximum(jnp.max(s_w, axis=1, keepdims=True), jnp.max(s_n, axis=1, keepdims=True))
    p_w, p_n = jnp.exp(s_w - m), jnp.exp(s_n - m)
    l = jnp.sum(p_w, axis=1, keepdims=True) + jnp.sum(p_n, axis=1, keepdims=True)
    acc = (lax.dot_general(p_w.astype(BF16), vwin, nt_dims, preferred_element_type=F32)
           + jnp.dot(p_n.astype(BF16), vwn_ref[...], preferred_element_type=F32))
    o_w = acc / jnp.maximum(l, 1e-30)

    gate = _sigmoid(g_ref[...])
    o_ref[...] = gate[:, 0:1] * o_c + gate[:, 1:2] * o_s + gate[:, 2:3] * o_w


def _sample_attention(q, gates, k_cmp, v_cmp, cache_ks, cache_vs, page_table,
                      ksl, vsl, win_k, win_v, kw, vw, rel_bias, b, t):
    n_pages = page_table.shape[1]
    past = n_pages * PAGE_SIZE
    wbuf = win_k.shape[-1]
    nq = N_KV * t
    used = GROUP * nq
    assert used % 8 == 0 and used <= SAMPLE_LANES and t <= NEW_PAD
    blocks_per_page = PAGE_SIZE // CMP_BLK
    n_blk = past // SEL_BLK + 1
    n_rows = -(-n_blk // 8) * 8

    q5 = q.reshape(b, t, N_KV, GROUP, HEAD_DIM)
    eye = jnp.eye(N_KV, dtype=q.dtype)
    qbd = jnp.einsum("btkgd,kc->bkdgct", q5, eye).reshape(b, KV_DIM, used)
    qbd = jnp.pad(qbd, ((0, 0), (0, 0), (0, SAMPLE_LANES - used))).astype(BF16)
    qrow = jnp.einsum("btkgd,kc->bgktcd", q5, eye).reshape(b, used, KV_DIM).astype(BF16)
    g_rows = gates[:, :3 * N_HEADS].reshape(b, t, N_KV, GROUP, 3).transpose(0, 3, 2, 1, 4).reshape(b, used, 3)
    g_rows = jnp.pad(g_rows, ((0, 0), (0, 0), (0, 128 - 3)))
    pad_rows = lambda a: jnp.pad(a.reshape(b, t, KV_DIM), ((0, 0), (0, NEW_PAD - t), (0, 0))).astype(BF16)
    tables = _sample_tables(rel_bias, past, t, wbuf, used)

    per_b = lambda *shape: pl.BlockSpec((None,) + shape, lambda bi, pt: (bi,) + (0,) * len(shape))
    const = lambda *shape: pl.BlockSpec(shape, lambda bi, pt: (0,) * len(shape))

    def page(p):
        return pl.BlockSpec((None, N_KV, HEAD_DIM, PAGE_SIZE), lambda bi, pt, p=p: (pt[bi, p], 0, 0, 0))

    n_cmp = n_pages * blocks_per_page
    in_specs = ([per_b(KV_DIM, SAMPLE_LANES), per_b(used, KV_DIM), per_b(used, 128),
                 per_b(n_cmp, KV_DIM), per_b(n_cmp, KV_DIM)]
                + [page(p) for p in range(n_pages)] * 2
                + [per_b(NEW_PAD, KV_DIM), per_b(NEW_PAD, KV_DIM),
                   per_b(N_KV, HEAD_DIM, wbuf), per_b(N_KV, HEAD_DIM, wbuf),
                   per_b(NEW_PAD, KV_DIM), per_b(NEW_PAD, KV_DIM)]
                + [const(*tb.shape) for tb in tables])
    out = pl.pallas_call(
        functools.partial(_sample_attn_kernel, n_pages=n_pages, n_sel=min(N_SEL, n_blk), t=t, wbuf=wbuf),
        grid_spec=pltpu.PrefetchScalarGridSpec(
            num_scalar_prefetch=1,
            grid=(b,),
            in_specs=in_specs,
            out_specs=per_b(used, KV_DIM),
            scratch_shapes=[pltpu.VMEM((n_cmp, SAMPLE_LANES), F32),
                            pltpu.VMEM((n_rows, SAMPLE_LANES), F32),
                            pltpu.VMEM((128, SAMPLE_LANES), F32),
                            pltpu.VMEM((used, past + NEW_PAD), F32)]),
        out_shape=jax.ShapeDtypeStruct((b, used, KV_DIM), F32),
        compiler_params=_cparams("arbitrary"),
        name="nsa_sample_attention",
    )(page_table, qbd, qrow, g_rows, k_cmp.reshape(b, n_cmp, KV_DIM), v_cmp.reshape(b, n_cmp, KV_DIM),
      *([cache_ks] * n_pages),
      *([cache_vs] * n_pages), pad_rows(ksl), pad_rows(vsl), win_k, win_v, pad_rows(kw), pad_rows(vw), *tables)
    o = out.reshape(b, GROUP, N_KV, t, N_KV, HEAD_DIM)
    o = jnp.einsum("bgktkd->btkgd", o)
    return o.reshape(b * t, Q_DIM)


def _compress_weights(w_c, pe_c):
    eye = jnp.eye(N_KV, dtype=w_c.dtype)
    w_big = jnp.einsum("jde,kc->jkdce", w_c, eye).reshape(CMP_BLK * KV_DIM, KV_DIM)
    pe_flat = jnp.broadcast_to(pe_c[:, None, :], (CMP_BLK, N_KV, HEAD_DIM)).reshape(1, CMP_BLK * KV_DIM)
    pe_rows = jnp.pad(pe_flat, ((0, 7), (0, 0)))
    bias = _matmul(pe_rows, w_big, name="compress_pe")[0:1]
    return w_big, bias


_BLOCK_PITCH = CMP_BLK + 4


def _compress_pages_kernel(pt_ref, *refs, n_pages):
    del pt_ref
    page_refs = refs[:n_pages]
    _compress_position_minor(lambda p: page_refs[p][...].reshape(KV_DIM, PAGE_SIZE), n_pages, *refs[n_pages:])


def _compress_tokens_kernel(x_ref, w_ref, b_ref, o_ref, rows_ref, *, n_pages):
    _compress_position_minor(lambda p: x_ref[:, p * PAGE_SIZE:(p + 1) * PAGE_SIZE], n_pages,
                             w_ref, b_ref, o_ref, rows_ref)


def _compress_position_minor(get_page, n_pages, w_ref, b_ref, o_ref, rows_ref):
    per_page = PAGE_SIZE // CMP_BLK
    for p in range(n_pages):
        xt = get_page(p).T
        for n in range(per_page):
            lo = (p * per_page + n) * _BLOCK_PITCH
            for h in range(2):
                rows_ref[h, lo:lo + CMP_BLK, :] = xt[n * CMP_BLK:(n + 1) * CMP_BLK, h * 128:(h + 1) * 128]
    n_out = n_pages * per_page
    acc = jnp.broadcast_to(b_ref[...], (n_out, KV_DIM))
    for j in range(CMP_BLK):
        for h in range(2):
            piece = rows_ref[h, pl.ds(j, n_out, stride=_BLOCK_PITCH), :].astype(BF16)
            lo = j * KV_DIM + h * 128
            acc = acc + jnp.dot(piece, w_ref[lo:lo + 128, :], preferred_element_type=F32)
    o_ref[...] = acc


def _compress_pages(cache_t, page_table, w_big, bias):
    assert KV_DIM == 256
    b, per_b = page_table.shape
    group = max(1, 32 // per_b)
    assert b % group == 0
    n_pages = group * per_b
    n_out = n_pages * (PAGE_SIZE // CMP_BLK)

    def page(p):
        return pl.BlockSpec((None, N_KV, HEAD_DIM, PAGE_SIZE),
                            lambda i, pt, p=p: (pt[i * group + p // per_b, p % per_b], 0, 0, 0))

    return pl.pallas_call(
        functools.partial(_compress_pages_kernel, n_pages=n_pages),
        grid_spec=pltpu.PrefetchScalarGridSpec(
            num_scalar_prefetch=1,
            grid=(b // group,),
            in_specs=[page(p) for p in range(n_pages)]
            + [pl.BlockSpec((CMP_BLK * KV_DIM, KV_DIM), lambda i, pt: (0, 0), pipeline_mode=pl.Buffered(1)),
               pl.BlockSpec((1, KV_DIM), lambda i, pt: (0, 0))],
            out_specs=pl.BlockSpec((n_out, KV_DIM), lambda i, pt: (i, 0)),
            scratch_shapes=[pltpu.VMEM((2, n_out * _BLOCK_PITCH, 128), F32)]),
        out_shape=jax.ShapeDtypeStruct((b * per_b * (PAGE_SIZE // CMP_BLK), KV_DIM), F32),
        compiler_params=_cparams("arbitrary"),
        name="compress_pages",
    )(page_table, *([cache_t] * n_pages), w_big.astype(BF16), bias)


def _compress_tokens(kv_t, w_big, bias):
    assert KV_DIM == 256
    b, _, t = kv_t.shape
    n_pages = math.gcd(32, t // PAGE_SIZE)
    steps = t // (n_pages * PAGE_SIZE)
    n_out = n_pages * (PAGE_SIZE // CMP_BLK)
    return pl.pallas_call(
        functools.partial(_compress_tokens_kernel, n_pages=n_pages),
        grid=(b, steps),
        in_specs=[pl.BlockSpec((None, KV_DIM, n_pages * PAGE_SIZE), lambda bi, i: (bi, 0, i)),
                  pl.BlockSpec((CMP_BLK * KV_DIM, KV_DIM), lambda bi, i: (0, 0), pipeline_mode=pl.Buffered(1)),
                  pl.BlockSpec((1, KV_DIM), lambda bi, i: (0, 0))],
        out_specs=pl.BlockSpec((n_out, KV_DIM), lambda bi, i: (bi * steps + i, 0)),
        out_shape=jax.ShapeDtypeStruct((b * t // CMP_BLK, KV_DIM), F32),
        scratch_shapes=[pltpu.VMEM((2, n_out * _BLOCK_PITCH, 128), F32)],
        compiler_params=_cparams("parallel", "parallel"),
        name="compress_tokens",
    )(kv_t, w_big.astype(BF16), bias)


def _ada(c_all, ada_w, ada_b):
    mods = []
    for i in range(ada_w.shape[0]):
        mods.append(_matmul(c_all, ada_w[i], ada_b[i].reshape(1, -1), silu_in=True, tn=2 * D_MODEL,
                            name="ada_modulate"))
    return mods


def kernel(x_prompt, x_sample, c_prompt, c_sample, cache_k_cmp, cache_v_cmp, cache_k_sel, cache_v_sel,
           page_table, state_k_win, state_v_win, state_pool, rel_bias, ada_w, ada_b, norm_g, final_g,
           nsa_w_in, nsa_w_out, cmp_wk, cmp_wv, cmp_pe_k, cmp_pe_v, pool_w, pool_scale,
           ffn_wg, ffn_wu, ffn_wd):
    bp, tp, _ = x_prompt.shape
    bs, ts, _ = x_sample.shape
    n_phys = cache_k_cmp.shape[1]
    past = page_table.shape[1] * PAGE_SIZE
    depth = ada_w.shape[0]
    assert depth == 2 and tp % KEY_TILE == 0 and ts <= 8 and past % PAGE_SIZE == 0

    n_c = bp + bs
    c_all = jnp.pad(jnp.concatenate([c_prompt, c_sample], axis=0), ((0, -n_c % 8), (0, 0)))
    mods = _ada(c_all, ada_w, ada_b)

    def mod_prompt(i):
        return [v[:bp].reshape(bp, 1, D_MODEL) for v in jnp.split(mods[i], 6, axis=-1)]

    def mod_sample(i):
        return [jnp.repeat(v[bp:n_c], ts, axis=0) for v in jnp.split(mods[i], 6, axis=-1)]

    wk_big, k_bias = _compress_weights(cmp_wk[0], cmp_pe_k[0])
    wv_big, v_bias = _compress_weights(cmp_wv[0], cmp_pe_v[0])

    mp = bp * tp
    x = x_prompt.reshape(mp, D_MODEL)
    sh1, sc1, g1, sh2, sc2, g2 = mod_prompt(0)
    kv_t, layouts = _nsa_project(x, norm_g[0, 0], sc1, sh1, nsa_w_in[0], tp, ROW_TILE, True)
    k_cmp = _compress_tokens(kv_t[0], wk_big, k_bias)
    v_cmp = _compress_tokens(kv_t[1], wv_big, v_bias)
    o_t = _prompt_attention(layouts, k_cmp, v_cmp, rel_bias, bp, tp)
    x = _matmul_residual_t(o_t, nsa_w_out[0], x, g1, tp, ROW_TILE)
    x = _ffn(x, norm_g[0, 1], sc2, sh2, g2, ffn_wg[0], ffn_wu[0], ffn_wd[0], final_g, tp, ROW_TILE, FFN_SLICES, False)
    sh1, sc1, g1, sh2, sc2, g2 = mod_prompt(1)
    x3, pool_p = _pool_mix(x.reshape(bp, tp, D_MODEL), jnp.zeros((bp, POOL_STATE, D_MODEL), F32),
                           norm_g[1, 0], sc1, sh1, g1, pool_w[0], pool_scale[0], 0, ROW_TILE)
    y_prompt = _ffn(x3.reshape(mp, D_MODEL), norm_g[1, 1], sc2, sh2, g2, ffn_wg[1], ffn_wu[1], ffn_wd[1],
                    final_g, tp, ROW_TILE, FFN_SLICES, True).reshape(bp, tp, D_MODEL)
    win = min(WINDOW, tp)
    st5 = lambda a, t0: jnp.transpose(a[:, :, t0:].reshape(bp, N_KV, HEAD_DIM, tp - t0), (0, 3, 1, 2))[None]
    prompt_states = tuple(st5(a, 0) for a in kv_t[:4]) + tuple(st5(a, tp - win) for a in kv_t[4:]) + (pool_p[None],)

    ms = bs * ts
    x = x_sample.reshape(ms, D_MODEL)
    sh1, sc1, g1, sh2, sc2, g2 = mod_sample(0)
    (kc, vc, ksl, vsl, kw, vw), (q, gates) = _nsa_project(x, norm_g[0, 0], sc1, sh1, nsa_w_in[0], ts, ROW_TILE, False)
    assert (past + ts) // CMP_BLK == past // CMP_BLK
    pos_last = lambda a: jnp.transpose(a[0], (0, 2, 3, 1))
    k_cmp_s = _compress_pages(pos_last(cache_k_cmp), page_table, wk_big, k_bias)
    v_cmp_s = _compress_pages(pos_last(cache_v_cmp), page_table, wv_big, v_bias)
    o = _sample_attention(q, gates, k_cmp_s, v_cmp_s, pos_last(cache_k_sel), pos_last(cache_v_sel),
                          page_table, ksl, vsl, pos_last(state_k_win), pos_last(state_v_win), kw, vw,
                          rel_bias, bs, ts)
    x = _matmul_residual(o, nsa_w_out[0], x, g1, ts, ROW_TILE)
    x = _ffn(x, norm_g[0, 1], sc2, sh2, g2, ffn_wg[0], ffn_wu[0], ffn_wd[0], final_g, ts, ROW_TILE, FFN_SLICES, False)
    sh1, sc1, g1, sh2, sc2, g2 = mod_sample(1)
    b3 = lambda v: v.reshape(bs, ts, D_MODEL)[:, :1]
    x3, pool_s = _pool_mix(x.reshape(bs, ts, D_MODEL), state_pool[0], norm_g[1, 0], b3(sc1), b3(sh1), b3(g1),
                           pool_w[0], pool_scale[0], past, ROW_TILE)
    y_sample = _ffn(x3.reshape(ms, D_MODEL), norm_g[1, 1], sc2, sh2, g2, ffn_wg[1], ffn_wu[1], ffn_wd[1],
                    final_g, ts, ROW_TILE, FFN_SLICES, True).reshape(bs, ts, D_MODEL)
    st5 = lambda a: a.reshape(1, bs, ts, N_KV, HEAD_DIM)
    kw_ext = jnp.concatenate([state_k_win[0], st5(kw)[0]], axis=1)[:, ts:]
    vw_ext = jnp.concatenate([state_v_win[0], st5(vw)[0]], axis=1)[:, ts:]
    sample_states = (st5(kc), st5(vc), st5(ksl), st5(vsl), kw_ext[None], vw_ext[None], pool_s[None])

    return (y_prompt, y_sample) + prompt_states + sample_states
```

```python
import functools
import math

import jax
import jax.numpy as jnp
from jax import lax
from jax.experimental import pallas as pl
from jax.experimental.pallas import tpu as pltpu

D_MODEL = 1024
N_HEADS = 16
N_KV = 4
GROUP = N_HEADS // N_KV
HEAD_DIM = 64
Q_DIM = N_HEADS * HEAD_DIM
KV_DIM = N_KV * HEAD_DIM
CMP_BLK = 32
SEL_BLK = 64
N_SEL = 16
WINDOW = 512
Q_CHUNK = 64
N_BUCKETS = 32
MAX_DISTANCE = 128
POOL_WINDOWS = (2, 4, 8, 16)
POOL_GROUP_DIM = D_MODEL // len(POOL_WINDOWS)
POOL_STATE = max(POOL_WINDOWS) - 1
PAGE_SIZE = 128
RMS_EPS = 1e-6
NEG = -1e30
BIG = 1e9

KEY_TILE = 128
CHUNK = 128
LANES = GROUP * CHUNK
LOG2E = math.log2(math.e)
Q_SCALE_LOG2 = HEAD_DIM ** -0.5 * LOG2E
SAMPLE_LANES = 128
VMEM_LIMIT_BYTES = 48 * 1024 * 1024
ROW_TILE = 512
FFN_SLICES = 2

F32 = jnp.float32
BF16 = jnp.bfloat16


def _cparams(*sem):
    return pltpu.CompilerParams(dimension_semantics=sem, vmem_limit_bytes=VMEM_LIMIT_BYTES)


def _silu(x):
    return x * (1.0 / (1.0 + jnp.exp(-x)))


def _sigmoid(x):
    return 1.0 / (1.0 + jnp.exp(-x))


def _normmod(x, g, sc, sh):
    ms = jnp.mean(x * x, axis=-1, keepdims=True)
    return (x * lax.rsqrt(ms + RMS_EPS) * g) * (1.0 + sc) + sh


def _mod_spec(mod, tm, rows_per_batch):
    if mod.ndim == 3:
        return pl.BlockSpec((None, 1, D_MODEL), lambda i, *_: ((i * tm) // rows_per_batch, 0, 0))
    return pl.BlockSpec((tm, D_MODEL), lambda i, *_: (i, 0))


def _mm_kernel(a_ref, w_ref, b_ref, o_ref, *, silu_in):
    a = a_ref[...]
    if silu_in:
        a = _silu(a)
    o_ref[...] = jnp.dot(a.astype(BF16), w_ref[...], preferred_element_type=F32) + b_ref[...]


def _matmul(a, w, bias=None, *, silu_in=False, tm=256, tn=None, name="matmul"):
    m, k = a.shape
    n = w.shape[1]
    tm = min(tm, m)
    tn = n if tn is None else tn
    assert m % tm == 0 and n % tn == 0
    if bias is None:
        bias = jnp.zeros((1, n), F32)
    return pl.pallas_call(
        functools.partial(_mm_kernel, silu_in=silu_in),
        grid=(m // tm, n // tn),
        in_specs=[pl.BlockSpec((tm, k), lambda i, j: (i, 0)),
                  pl.BlockSpec((k, tn), lambda i, j: (0, j)),
                  pl.BlockSpec((1, tn), lambda i, j: (0, j))],
        out_specs=pl.BlockSpec((tm, tn), lambda i, j: (i, j)),
        out_shape=jax.ShapeDtypeStruct((m, n), F32),
        compiler_params=_cparams("parallel", "parallel"),
        name=name,
    )(a, w.astype(BF16), bias)


_PROJ_MAIN = Q_DIM + 6 * KV_DIM
_GATE_PAD = 128


def _proj_kernel(x_ref, g_ref, sc_ref, sh_ref, w_ref, wg_ref, *out_refs, attn_layouts, tm):
    hb = _normmod(x_ref[...], g_ref[...], sc_ref[...], sh_ref[...]).astype(BF16)
    q = jnp.dot(hb, w_ref[:, :Q_DIM], preferred_element_type=F32)
    gates = jnp.dot(hb, wg_ref[...], preferred_element_type=F32)
    kvs = []
    for n in range(6):
        lo = Q_DIM + n * KV_DIM
        kvs.append(jnp.dot(hb, w_ref[:, lo:lo + KV_DIM], preferred_element_type=F32))
    if not attn_layouts:
        for n in range(6):
            out_refs[n][...] = kvs[n]
        out_refs[6][...] = q
        out_refs[7][...] = gates
        return
    kt_refs = out_refs[:6]
    qt_ref, gt_ref, ks_ref, vst_ref, kw_ref, vwt_ref, q_scr, gate_scr, kv_scr = out_refs[6:]
    kvts = []
    for n in range(6):
        kv_scr[n] = kvs[n]
        kvts.append(kv_scr[n].T)
        kt_refs[n][...] = kvts[n]
    n_chunks = tm // CHUNK
    q_scr[...] = q * Q_SCALE_LOG2
    gate_scr[...] = gates
    qt = q_scr[...].T
    gt = gate_scr[...].T
    for c in range(n_chunks):
        tok = slice(c * CHUNK, (c + 1) * CHUNK)
        for kv in range(N_KV):
            for gi in range(GROUP):
                h = kv * GROUP + gi
                lanes = slice(gi * CHUNK, (gi + 1) * CHUNK)
                qt_ref[c, kv, :, lanes] = qt[h * HEAD_DIM:(h + 1) * HEAD_DIM, tok].astype(BF16)
                gt_ref[c, kv, :, lanes] = gt[3 * h:3 * h + 3, tok]
    for k_ref, vt_ref, k, vt in ((ks_ref, vst_ref, kvs[2], kvts[3]), (kw_ref, vwt_ref, kvs[4], kvts[5])):
        for kv in range(N_KV):
            cols = slice(kv * HEAD_DIM, (kv + 1) * HEAD_DIM)
            k_ref[kv] = k[:, cols].astype(BF16)
            for c in range(n_chunks):
                vt_ref[kv, c, 0:HEAD_DIM, :] = vt[cols, c * CHUNK:(c + 1) * CHUNK].astype(BF16)
                vt_ref[kv, c, HEAD_DIM:, :] = jnp.ones((V_ROWS - HEAD_DIM, CHUNK), BF16)


def _nsa_project(x, g, sc, sh, w_in, rows_per_batch, tm, attn_layouts):
    m = x.shape[0]
    tm = min(tm, m)
    w_main = w_in[:, :_PROJ_MAIN].astype(BF16)
    w_gate = jnp.pad(w_in[:, _PROJ_MAIN:], ((0, 0), (0, _GATE_PAD - 3 * N_HEADS))).astype(BF16)
    row = lambda n: pl.BlockSpec((tm, n), lambda i: (i, 0))
    if attn_layouts:
        assert tm % CHUNK == 0 and rows_per_batch % tm == 0
        nck = tm // CHUNK
        tpb = rows_per_batch // tm
        out_specs = [pl.BlockSpec((None, KV_DIM, tm), lambda i: (i // tpb, 0, i % tpb))] * 6
        out_shape = [jax.ShapeDtypeStruct((m // rows_per_batch, KV_DIM, rows_per_batch), F32)] * 6
        chunked = lambda *s: pl.BlockSpec((nck,) + s, lambda i: (i,) + (0,) * len(s))
        k_spec = pl.BlockSpec((N_KV, tm, HEAD_DIM), lambda i: (0, i, 0))
        vt_spec = pl.BlockSpec((N_KV, nck, V_ROWS, CHUNK), lambda i: (0, i, 0, 0))
        k_shape = jax.ShapeDtypeStruct((N_KV, m, HEAD_DIM), BF16)
        vt_shape = jax.ShapeDtypeStruct((N_KV, m // CHUNK, V_ROWS, CHUNK), BF16)
        out_specs += [chunked(N_KV, HEAD_DIM, LANES), chunked(N_KV, 3, LANES), k_spec, vt_spec, k_spec, vt_spec]
        out_shape += [jax.ShapeDtypeStruct((m // CHUNK, N_KV, HEAD_DIM, LANES), BF16),
                      jax.ShapeDtypeStruct((m // CHUNK, N_KV, 3, LANES), F32),
                      k_shape, vt_shape, k_shape, vt_shape]
    else:
        out_specs = [row(KV_DIM)] * 6 + [row(Q_DIM), row(_GATE_PAD)]
        out_shape = ([jax.ShapeDtypeStruct((m, KV_DIM), F32)] * 6
                     + [jax.ShapeDtypeStruct((m, Q_DIM), F32), jax.ShapeDtypeStruct((m, _GATE_PAD), F32)])
    outs = pl.pallas_call(
        functools.partial(_proj_kernel, attn_layouts=attn_layouts, tm=tm),
        grid=(m // tm,),
        in_specs=[row(D_MODEL),
                  pl.BlockSpec((1, D_MODEL), lambda i: (0, 0)),
                  _mod_spec(sc, tm, rows_per_batch), _mod_spec(sh, tm, rows_per_batch),
                  pl.BlockSpec((D_MODEL, _PROJ_MAIN), lambda i: (0, 0)),
                  pl.BlockSpec((D_MODEL, _GATE_PAD), lambda i: (0, 0))],
        out_specs=out_specs,
        out_shape=out_shape,
        scratch_shapes=([pltpu.VMEM((tm, Q_DIM), F32), pltpu.VMEM((tm, _GATE_PAD), F32),
                         pltpu.VMEM((6, tm, KV_DIM), F32)] if attn_layouts else []),
        compiler_params=_cparams("parallel"),
        name="nsa_project",
    )(x, g.reshape(1, D_MODEL), sc, sh, w_main, w_gate)
    if attn_layouts:
        return outs[:6], outs[6:]
    return outs[:6], outs[6:]


def _mm_res_kernel(a_ref, w_ref, x_ref, gate_ref, o_ref):
    y = jnp.dot(a_ref[...].astype(BF16), w_ref[...], preferred_element_type=F32)
    o_ref[...] = x_ref[...] + gate_ref[...] * y


def _matmul_residual(a, w, x, gate, rows_per_batch, tm):
    m, k = a.shape
    tm = min(tm, m)
    return pl.pallas_call(
        _mm_res_kernel,
        grid=(m // tm,),
        in_specs=[pl.BlockSpec((tm, k), lambda i: (i, 0)),
                  pl.BlockSpec((k, D_MODEL), lambda i: (0, 0)),
                  pl.BlockSpec((tm, D_MODEL), lambda i: (i, 0)),
                  _mod_spec(gate, tm, rows_per_batch)],
        out_specs=pl.BlockSpec((tm, D_MODEL), lambda i: (i, 0)),
        out_shape=jax.ShapeDtypeStruct((m, D_MODEL), F32),
        compiler_params=_cparams("parallel"),
        name="out_proj_residual",
    )(a, w.astype(BF16), x, gate)


def _mm_res_t_kernel(ot_ref, w_ref, x_ref, gate_ref, o_ref, *, n_chunks):
    chunks = []
    for c in range(n_chunks):
        rows = [ot_ref[c, kv, :, gi * CHUNK:(gi + 1) * CHUNK] for kv in range(N_KV) for gi in range(GROUP)]
        chunks.append(jnp.concatenate(rows, axis=0).astype(F32).T)
    a = jnp.concatenate(chunks, axis=0).astype(BF16)
    y = jnp.dot(a, w_ref[...], preferred_element_type=F32)
    o_ref[...] = x_ref[...] + gate_ref[...] * y


def _matmul_residual_t(o_t, w, x, gate, rows_per_batch, tm):
    m = x.shape[0]
    nck = tm // CHUNK
    return pl.pallas_call(
        functools.partial(_mm_res_t_kernel, n_chunks=nck),
        grid=(m // tm,),
        in_specs=[pl.BlockSpec((nck, N_KV, HEAD_DIM, LANES), lambda i: (i, 0, 0, 0)),
                  pl.BlockSpec((Q_DIM, D_MODEL), lambda i: (0, 0)),
                  pl.BlockSpec((tm, D_MODEL), lambda i: (i, 0)),
                  _mod_spec(gate, tm, rows_per_batch)],
        out_specs=pl.BlockSpec((tm, D_MODEL), lambda i: (i, 0)),
        out_shape=jax.ShapeDtypeStruct((m, D_MODEL), F32),
        compiler_params=_cparams("parallel"),
        name="out_proj_residual_t",
    )(o_t, w.astype(BF16), x, gate)


def _ffn_kernel(x_ref, g_ref, sc_ref, sh_ref, gate_ref, wg_ref, wu_ref, wd_ref, fg_ref, o_ref,
                *, final_norm, tf):
    x = x_ref[...]
    hb = _normmod(x, g_ref[...], sc_ref[...], sh_ref[...]).astype(BF16)
    acc = None
    for lo in range(0, wg_ref.shape[1], tf):
        a = jnp.dot(hb, wg_ref[:, lo:lo + tf], preferred_element_type=F32)
        u = jnp.dot(hb, wu_ref[:, lo:lo + tf], preferred_element_type=F32)
        act = (_silu(a) * u).astype(BF16)
        part = jnp.dot(act, wd_ref[lo:lo + tf, :], preferred_element_type=F32)
        acc = part if acc is None else acc + part
    y = x + gate_ref[...] * acc
    if final_norm:
        ms = jnp.mean(y * y, axis=-1, keepdims=True)
        y = y * lax.rsqrt(ms + RMS_EPS) * fg_ref[...]
    o_ref[...] = y


def _ffn(x, g, sc, sh, gate, wg, wu, wd, final_g, rows_per_batch, tm, n_slices, final_norm):
    m = x.shape[0]
    d_ff = wg.shape[1]
    tm = min(tm, m)
    assert d_ff % (n_slices * 128) == 0
    tf = d_ff // n_slices
    once = pl.Buffered(1)
    vec = pl.BlockSpec((1, D_MODEL), lambda i: (0, 0))
    return pl.pallas_call(
        functools.partial(_ffn_kernel, final_norm=final_norm, tf=tf),
        grid=(m // tm,),
        in_specs=[pl.BlockSpec((tm, D_MODEL), lambda i: (i, 0)),
                  vec,
                  _mod_spec(sc, tm, rows_per_batch), _mod_spec(sh, tm, rows_per_batch),
                  _mod_spec(gate, tm, rows_per_batch),
                  pl.BlockSpec((D_MODEL, d_ff), lambda i: (0, 0), pipeline_mode=once),
                  pl.BlockSpec((D_MODEL, d_ff), lambda i: (0, 0), pipeline_mode=once),
                  pl.BlockSpec((d_ff, D_MODEL), lambda i: (0, 0), pipeline_mode=once),
                  vec],
        out_specs=pl.BlockSpec((tm, D_MODEL), lambda i: (i, 0)),
        out_shape=jax.ShapeDtypeStruct((m, D_MODEL), F32),
        compiler_params=_cparams("parallel"),
        name="ffn",
    )(x, g.reshape(1, D_MODEL), sc, sh, gate, wg.astype(BF16), wu.astype(BF16), wd.astype(BF16),
      final_g.reshape(1, D_MODEL))


_POOL_HALO = 16


def _pool_kernel(x_ref, xprev_ref, state_ref, g_ref, sc_ref, sh_ref, gate_ref, w_ref, ls_ref,
                 o_ref, st_ref, ext_ref, *, tm, pos0):
    i = pl.program_id(1)
    g, sc, sh = g_ref[...], sc_ref[...], sh_ref[...]
    u = _normmod(x_ref[...], g, sc, sh)
    prev = jnp.where(i == 0, state_ref[...], _normmod(xprev_ref[...], g, sc, sh))
    ext_ref[0:_POOL_HALO, :] = prev
    ext_ref[_POOL_HALO:_POOL_HALO + tm, :] = u
    st_ref[...] = ext_ref[tm:tm + _POOL_HALO, :]

    pos = pos0 + i * tm + lax.broadcasted_iota(jnp.int32, (tm, 1), 0)
    mixed = []
    for gi, w in enumerate(POOL_WINDOWS):
        lo = gi * POOL_GROUP_DIM
        s = u[:, lo:lo + POOL_GROUP_DIM]
        for k in range(1, w):
            s = s + ext_ref[_POOL_HALO - k:_POOL_HALO - k + tm, lo:lo + POOL_GROUP_DIM]
        cnt = jnp.minimum(pos + 1, w).astype(F32)
        pooled = s / cnt - u[:, lo:lo + POOL_GROUP_DIM]
        mixed.append(jnp.dot(pooled.astype(BF16), w_ref[gi], preferred_element_type=F32))
    y = jnp.concatenate(mixed, axis=-1) * ls_ref[...]
    o_ref[...] = x_ref[...] + gate_ref[...] * y


def _pool_mix(x3, state, g, sc, sh, gate, w_grp, layer_scale, pos0, tm):
    b, t, _ = x3.shape
    tm = min(tm, t)
    state16 = jnp.pad(state, ((0, 0), (_POOL_HALO - POOL_STATE, 0), (0, 0)))
    if t >= _POOL_HALO:
        xprev = x3
        nprev = tm // _POOL_HALO
        prev_spec = pl.BlockSpec((None, _POOL_HALO, D_MODEL),
                                 lambda bi, i: (bi, jnp.maximum(i * nprev - 1, 0), 0))
    else:
        xprev = state16
        prev_spec = pl.BlockSpec((None, _POOL_HALO, D_MODEL), lambda bi, i: (bi, 0, 0))
    vec = pl.BlockSpec((1, D_MODEL), lambda bi, i: (0, 0))
    bvec = pl.BlockSpec((None, 1, D_MODEL), lambda bi, i: (bi, 0, 0))
    out, st = pl.pallas_call(
        functools.partial(_pool_kernel, tm=tm, pos0=pos0),
        grid=(b, t // tm),
        in_specs=[pl.BlockSpec((None, tm, D_MODEL), lambda bi, i: (bi, i, 0)),
                  prev_spec,
                  pl.BlockSpec((None, _POOL_HALO, D_MODEL), lambda bi, i: (bi, 0, 0)),
                  vec, bvec, bvec, bvec,
                  pl.BlockSpec((len(POOL_WINDOWS), POOL_GROUP_DIM, POOL_GROUP_DIM),
                               lambda bi, i: (0, 0, 0)),
                  vec],
        out_specs=[pl.BlockSpec((None, tm, D_MODEL), lambda bi, i: (bi, i, 0)),
                   pl.BlockSpec((None, _POOL_HALO, D_MODEL), lambda bi, i: (bi, 0, 0))],
        out_shape=[jax.ShapeDtypeStruct((b, t, D_MODEL), F32),
                   jax.ShapeDtypeStruct((b, _POOL_HALO, D_MODEL), F32)],
        scratch_shapes=[pltpu.VMEM((tm + _POOL_HALO, D_MODEL), F32)],
        compiler_params=_cparams("parallel", "arbitrary"),
        name="pool_mix",
    )(x3, xprev, state16, g.reshape(1, D_MODEL), sc, sh, gate, w_grp.astype(BF16),
      layer_scale.reshape(1, D_MODEL))
    return out, st[:, _POOL_HALO - POOL_STATE:]


def _rel_bucket(dist):
    d = jnp.maximum(dist, 0)
    max_exact = N_BUCKETS // 2
    large = max_exact + (jnp.log(jnp.maximum(d, 1).astype(F32) / max_exact)
                         / math.log(MAX_DISTANCE / max_exact) * (N_BUCKETS - max_exact)).astype(jnp.int32)
    large = jnp.minimum(large, N_BUCKETS - 1)
    return jnp.where(d < max_exact, d, large)


def _dist_bias(rel_bias):
    return rel_bias[_rel_bucket(jnp.arange(MAX_DISTANCE + 1, dtype=jnp.int32))]


def _lookup(table, idx):
    onehot = (idx[..., None] == jnp.arange(table.shape[0], dtype=jnp.int32)).astype(F32)
    return jnp.einsum("...d,dh->...h", onehot, table, precision=lax.Precision.HIGHEST)


def _bias_tile(fd, dist, valid):
    k, q = dist.shape
    bias = _lookup(fd, jnp.clip(dist, 0, MAX_DISTANCE)).reshape(k, q, N_KV, GROUP)
    bias = jnp.where(valid[:, :, None, None], bias, NEG)
    return bias.transpose(2, 0, 3, 1).reshape(N_KV, k, GROUP * q)


def _prompt_tables(rel_bias, t):
    fd = _dist_bias(rel_bias) * LOG2E
    kj = jnp.arange(KEY_TILE, dtype=jnp.int32)[:, None]
    qi = jnp.arange(CHUNK, dtype=jnp.int32)[None, :]
    dist = qi - kj
    near = jnp.stack([_bias_tile(fd, dist + KEY_TILE, dist + KEY_TILE >= 0),
                      _bias_tile(fd, dist, dist >= 0)])
    wfirst = _bias_tile(fd, dist + WINDOW, dist < 0)
    far = _bias_tile(fd, jnp.full((1, CHUNK), MAX_DISTANCE, jnp.int32), jnp.ones((1, CHUNK), bool))
    per_chunk = CHUNK // CMP_BLK
    assert CHUNK == 128 and CMP_BLK == 32 and MAX_DISTANCE == 128
    rel0, n_rel = -per_chunk, 2 * per_chunk
    rel = rel0 + jnp.arange(n_rel, dtype=jnp.int32)[:, None]
    dist = qi - CMP_BLK * rel - (CMP_BLK - 1)
    crel = _bias_tile(fd, dist, dist >= 0)
    relm = (jnp.arange(t // CMP_BLK, dtype=jnp.int32)[None, :]
            - per_chunk * jnp.arange(t // CHUNK, dtype=jnp.int32)[:, None])[:, None, :, None]
    cmp_tab = jnp.where(relm >= rel0 + n_rel, NEG, far[None])
    for r in range(n_rel):
        cmp_tab = jnp.where(relm == rel0 + r, crel[None, :, r:r + 1, :], cmp_tab)
    return near, wfirst, far, cmp_tab


V_ROWS = HEAD_DIM + 16


def _half_exp2(s, m, top, bot):
    return jnp.exp2(jnp.concatenate([s[:SEL_BLK] - (m - top), s[SEL_BLK:] - (m - bot)], axis=0))


def _normalise(acc):
    return acc[:HEAD_DIM] * (1.0 / jnp.maximum(acc[HEAD_DIM:HEAD_DIM + 1], 1e-30))


def _prompt_attn_kernel(q_ref, g_ref, kc_ref, vct_ref, cb_ref, ks_ref, vst_ref, kw_ref, vwt_ref,
                        near_ref, wfirst_ref, far_ref, o_ref,
                        imp_ref, val_ref, rank_ref, selrep_ref, m_ref, mw_ref, alpha_ref, acc_ref, accw_ref,
                        mix_ref, s_ref, smax_ref, p_ref, sc_ref, pc_ref, *, n_blk, n_sel):
    i = pl.program_id(1)

    def k_rows(ref, kv, jt):
        return ref[kv, pl.ds(pl.multiple_of(jt * KEY_TILE, KEY_TILE), KEY_TILE), :]

    def sel_rows(kv, jt):
        return selrep_ref[kv, pl.ds(2 * jt, 1), :], selrep_ref[kv, pl.ds(2 * jt + 1, 1), :]

    col_max = []
    for kv in range(N_KV):
        s = jnp.dot(kc_ref[kv], q_ref[kv], preferred_element_type=F32) + cb_ref[kv]
        sc_ref[kv] = s
        col_max.append(jnp.max(s, axis=0, keepdims=True))
    for kv in range(N_KV):
        s = sc_ref[kv]
        p = jnp.where(cb_ref[kv] > 0.5 * NEG, jnp.exp2(s - col_max[kv]), 0.0)
        p = p * (1.0 / jnp.maximum(jnp.sum(p, axis=0, keepdims=True), 1e-30))
        pc_ref[kv] = p.astype(BF16)
        imp_ref[kv] = functools.reduce(
            lambda a, b: a + b, [p[:, gi * CHUNK:(gi + 1) * CHUNK] for gi in range(GROUP)])
    for kv in range(N_KV):
        o_c = jnp.dot(vct_ref[kv], pc_ref[kv], preferred_element_type=F32)
        mix_ref[kv] = _sigmoid(g_ref[kv, 0:1, :]) * o_c

    imp = jnp.concatenate(
        [imp_ref[kv, pl.ds(0, n_blk, stride=2), :] + imp_ref[kv, pl.ds(1, n_blk, stride=2), :]
         for kv in range(N_KV)], axis=1)
    blk = lax.broadcasted_iota(jnp.int32, (n_blk, LANES), 0)
    lane = lax.broadcasted_iota(jnp.int32, (1, LANES), 1)
    assert CHUNK == 2 * SEL_BLK
    qblk = 2 * i + jnp.where((lane & (CHUNK - 1)) >= SEL_BLK, 1, 0)
    forced = (blk == 0) | (blk == qblk) | (blk == qblk - 1)
    val_ref[...] = jnp.where(forced, BIG, jnp.where(blk <= qblk, imp, -1.0))
    rank_ref[...] = jnp.zeros_like(rank_ref)
    sub = lax.broadcasted_iota(jnp.int32, (8, LANES), 0)
    group = 8
    assert n_blk % group == 0
    for grp in range(n_blk // group):
        @pl.when(group * grp <= 2 * i + 1)
        def _(grp=grp):
            for r8 in range(n_blk // 8):
                lo = r8 * 8
                piece = val_ref[lo:lo + 8, :]
                acc = rank_ref[lo:lo + 8, :]
                for j in range(grp * group, (grp + 1) * group):
                    row = val_ref[j:j + 1, :]
                    if j < lo:
                        acc = acc + jnp.where(row >= piece, 1.0, 0.0)
                    elif j >= lo + 8:
                        acc = acc + jnp.where(row > piece, 1.0, 0.0)
                    else:
                        acc = acc + jnp.where(sub > j - lo, jnp.where(row >= piece, 1.0, 0.0),
                                              jnp.where(row > piece, 1.0, 0.0))
                rank_ref[lo:lo + 8, :] = acc
    selneg = jnp.where(rank_ref[...] < n_sel, 0.0, NEG)
    for kv in range(N_KV):
        selrep_ref[kv] = jnp.concatenate([selneg[:, kv * CHUNK:(kv + 1) * CHUNK]] * GROUP, axis=1)

    prev = jnp.maximum(i - 1, 0)
    n_far = jnp.maximum(i - 1, 0)
    last = jnp.maximum(n_far - 1, 0)
    zero_row = jnp.zeros((1, LANES), F32)
    for state in (mw_ref, m_ref):
        state[...] = jnp.full(state.shape, NEG, F32)
    for state in (accw_ref, acc_ref):
        state[...] = jnp.zeros(state.shape, F32)

    def scores(k_ref, jt, slot, table=None):
        for kv in range(N_KV):
            s = jnp.dot(k_rows(k_ref, kv, jt), q_ref[kv], preferred_element_type=F32)
            if table is not None:
                s = s + table(kv)
            s_ref[slot, kv] = s
            smax_ref[slot, kv, 0:1, :] = jnp.max(s[:SEL_BLK], axis=0, keepdims=True)
            smax_ref[slot, kv, 1:2, :] = jnp.max(s[SEL_BLK:], axis=0, keepdims=True)

    def softmax(slot, rows, stat_ref):
        for kv in range(N_KV):
            top, bot = rows(kv)
            m_old = stat_ref[kv]
            m_new = jnp.maximum(m_old, jnp.maximum(smax_ref[slot, kv, 0:1, :] + top,
                                                   smax_ref[slot, kv, 1:2, :] + bot))
            stat_ref[kv] = m_new
            alpha_ref[slot, kv] = jnp.exp2(m_old - m_new)
            p_ref[slot, kv] = _half_exp2(s_ref[slot, kv], m_new, top, bot).astype(BF16)

    def values(vt_ref, jt, slot, out_ref):
        for kv in range(N_KV):
            out_ref[kv] = alpha_ref[slot, kv] * out_ref[kv] + jnp.dot(vt_ref[kv, jt], p_ref[slot, kv],
                                                                      preferred_element_type=F32)

    def const_rows(row):
        return lambda kv: (row, row)

    def window_far_rows(back):
        off = jnp.where(i >= back, 0.0, NEG)
        return lambda kv: (far_ref[kv] + off, far_ref[kv] + off)

    def sel_near_rows(jt, off):
        def rows(kv):
            top, bot = sel_rows(kv, jt)
            return top + off, bot + off
        return rows

    prev_off = jnp.where(i >= 1, 0.0, NEG)
    back = lambda n: jnp.maximum(i - n, 0)
    diag_table = lambda kv: near_ref[1, kv]
    prev_table = lambda kv: near_ref[0, kv]
    static_tiles = [
        (kw_ref, vwt_ref, i, diag_table, const_rows(zero_row), mw_ref, accw_ref),
        (kw_ref, vwt_ref, prev, prev_table, const_rows(zero_row + prev_off), mw_ref, accw_ref),
        (kw_ref, vwt_ref, back(2), None, window_far_rows(2), mw_ref, accw_ref),
        (kw_ref, vwt_ref, back(3), None, window_far_rows(3), mw_ref, accw_ref),
        (kw_ref, vwt_ref, back(4), lambda kv: wfirst_ref[kv],
         const_rows(zero_row + jnp.where(i >= 4, 0.0, NEG)), mw_ref, accw_ref),
        (ks_ref, vst_ref, i, diag_table, sel_near_rows(i, 0.0), m_ref, acc_ref),
        (ks_ref, vst_ref, prev, prev_table, sel_near_rows(prev, prev_off), m_ref, acc_ref),
    ]

    def far_tile(jt):
        return jnp.minimum(jt, last)

    def far_rows(jt):
        off = jnp.where(jt < n_far, 0.0, NEG)
        def rows(kv):
            top, bot = sel_rows(kv, far_tile(jt))
            far = far_ref[kv] + off
            return top + far, bot + far
        return rows

    def far_values(jt, slot):
        values(vst_ref, jnp.where(jt < 0, prev, jnp.clip(jt, 0, last)), slot, acc_ref)

    n_static = len(static_tiles)
    k_ref0, _, jt0, table0, _, _, _ = static_tiles[0]
    scores(k_ref0, jt0, 0, table0)
    for t, (_, _, _, _, rows, stat_ref, _) in enumerate(static_tiles):
        if t >= 1:
            _, vt_ref, jt, _, _, _, out_ref = static_tiles[t - 1]
            values(vt_ref, jt, (t - 1) % 2, out_ref)
        softmax(t % 2, rows, stat_ref)
        if t + 1 < n_static:
            k_ref, _, jt, table, _, _, _ = static_tiles[t + 1]
            scores(k_ref, jt, (t + 1) % 2, table)
        else:
            scores(ks_ref, far_tile(0), (t + 1) % 2)
    assert n_static % 2 == 1

    def pair_step(j2):
        jt = 2 * j2
        far_values(jt - 1, 0)
        softmax(1, far_rows(jt), m_ref)
        scores(ks_ref, far_tile(jt + 1), 0)
        far_values(jt, 1)
        softmax(0, far_rows(jt + 1), m_ref)
        scores(ks_ref, far_tile(jt + 2), 1)

    def body(pairs_per_trip):
        def run(j, carry):
            for u in range(pairs_per_trip):
                pair_step(pairs_per_trip * j + u)
            return carry
        return run

    n_pairs = (n_far + 1) // 2
    n4 = n_pairs // 4
    n2 = n_pairs // 2
    lax.fori_loop(0, n4, body(4), 0)
    lax.fori_loop(2 * n4, n2, body(2), 0)
    lax.fori_loop(2 * n2, n_pairs, body(1), 0)
    far_values(2 * n_pairs - 1, 0)

    for kv in range(N_KV):
        o_ref[kv] = (mix_ref[kv] + _sigmoid(g_ref[kv, 2:3, :]) * _normalise(accw_ref[kv])
                     + _sigmoid(g_ref[kv, 1:2, :]) * _normalise(acc_ref[kv])).astype(o_ref.dtype)


def _prompt_attention(layouts, k_cmp, v_cmp, rel_bias, b, t):
    q_t, g_t, ks, vst, kw, vwt = layouts
    nch, nt, nc, n_blk = t // CHUNK, t // KEY_TILE, t // CMP_BLK, t // SEL_BLK
    kc = k_cmp.reshape(b, nc, N_KV, HEAD_DIM).transpose(0, 2, 1, 3).astype(BF16)
    vct = v_cmp.reshape(b, nc, N_KV, HEAD_DIM).transpose(0, 2, 3, 1).astype(BF16)
    near, wfirst, far, cmp_tab = _prompt_tables(rel_bias, t)

    once = pl.Buffered(1)
    per_b = lambda *shape: pl.BlockSpec((None,) + shape, lambda bi, i: (bi,) + (0,) * len(shape),
                                        pipeline_mode=once)
    const = lambda *shape: pl.BlockSpec(shape, lambda bi, i: (0,) * len(shape), pipeline_mode=once)
    chunk = lambda *shape: pl.BlockSpec((None,) + shape, lambda bi, i: (bi * nch + i,) + (0,) * len(shape))
    k_spec = pl.BlockSpec((N_KV, t, HEAD_DIM), lambda bi, i: (0, bi, 0))
    vt_spec = pl.BlockSpec((N_KV, nt, V_ROWS, KEY_TILE), lambda bi, i: (0, bi, 0, 0))
    stat = pltpu.VMEM((N_KV, 1, LANES), F32)
    return pl.pallas_call(
        functools.partial(_prompt_attn_kernel, n_blk=n_blk, n_sel=min(N_SEL, n_blk)),
        grid=(b, nch),
        in_specs=[chunk(N_KV, HEAD_DIM, LANES),
                  chunk(N_KV, 3, LANES),
                  per_b(N_KV, nc, HEAD_DIM),
                  per_b(N_KV, HEAD_DIM, nc),
                  pl.BlockSpec((None, N_KV, nc, LANES), lambda bi, i: (i, 0, 0, 0)),
                  k_spec, vt_spec, k_spec, vt_spec,
                  const(2, N_KV, KEY_TILE, LANES),
                  const(N_KV, KEY_TILE, LANES),
                  const(N_KV, 1, LANES)],
        out_specs=chunk(N_KV, HEAD_DIM, LANES),
        out_shape=jax.ShapeDtypeStruct((b * nch, N_KV, HEAD_DIM, LANES), BF16),
        scratch_shapes=[pltpu.VMEM((N_KV, nc, CHUNK), F32),
                        pltpu.VMEM((n_blk, LANES), F32),
                        pltpu.VMEM((n_blk, LANES), F32),
                        pltpu.VMEM((N_KV, n_blk, LANES), F32),
                        stat, stat,
                        pltpu.VMEM((2, N_KV, 1, LANES), F32),
                        pltpu.VMEM((N_KV, V_ROWS, LANES), F32),
                        pltpu.VMEM((N_KV, V_ROWS, LANES), F32),
                        pltpu.VMEM((N_KV, HEAD_DIM, LANES), F32),
                        pltpu.VMEM((2, N_KV, KEY_TILE, LANES), F32),
                        pltpu.VMEM((2, N_KV, 2, LANES), F32),
                        pltpu.VMEM((2, N_KV, KEY_TILE, LANES), BF16),
                        pltpu.VMEM((N_KV, nc, LANES), F32),
                        pltpu.VMEM((N_KV, nc, LANES), BF16)],
        compiler_params=_cparams("parallel", "arbitrary"),
        name="nsa_prompt_attention",
    )(q_t, g_t, kc, vct, cmp_tab, ks, vst, kw, vwt, near, wfirst, far)


NEW_PAD = 128


def _sample_tables(rel_bias, past, t, wbuf, n_rows):
    lane = jnp.arange(SAMPLE_LANES, dtype=jnp.int32)
    g, kvh, qi = lane // (N_KV * t), (lane // t) % N_KV, lane % t
    used = lane < GROUP * N_KV * t
    head = jnp.where(used, kvh * GROUP + g, 0)
    qpos = past + qi
    fd_lane = jnp.take(_dist_bias(rel_bias), head, axis=1)
    dists = jnp.arange(MAX_DISTANCE + 1, dtype=jnp.int32)

    def tab(kpos, valid, n_near):
        dist = qpos[None, :] - kpos[:, None]
        n_far = kpos.shape[0] - n_near
        near_d = jnp.clip(dist[n_far:], 0, MAX_DISTANCE)
        near = jnp.sum(jnp.where(near_d[:, :, None] == dists, fd_lane.T[None], 0.0), axis=-1)
        far = jnp.broadcast_to(fd_lane[MAX_DISTANCE][None, :], (n_far, SAMPLE_LANES))
        bias = jnp.where(used[None, :], jnp.concatenate([far, near], axis=0), 0.0)
        return jnp.where(valid(dist) & (kpos[:, None] >= 0), bias, NEG)

    new_pos = past + jnp.where(jnp.arange(NEW_PAD) < t, jnp.arange(NEW_PAD, dtype=jnp.int32), 1 << 20)
    causal = lambda d: d >= 0
    window = lambda d: (d >= 0) & (d < WINDOW)
    nc = past // CMP_BLK
    assert wbuf >= MAX_DISTANCE and past >= MAX_DISTANCE
    t_cmp = tab(jnp.arange(nc, dtype=jnp.int32) * CMP_BLK + CMP_BLK - 1, causal, MAX_DISTANCE // CMP_BLK)
    rows = lambda a: a[:, :n_rows].T
    t_sel = rows(tab(jnp.concatenate([jnp.arange(past, dtype=jnp.int32), new_pos]), causal,
                     MAX_DISTANCE + NEW_PAD))
    t_win = rows(tab(jnp.concatenate([past - wbuf + jnp.arange(wbuf, dtype=jnp.int32), new_pos]), window,
                     MAX_DISTANCE + NEW_PAD))
    key = jnp.arange(past + NEW_PAD, dtype=jnp.int32)
    blk = jnp.where(key < past, key // SEL_BLK, past // SEL_BLK)
    expand = (blk[None, :] == jnp.arange(128, dtype=jnp.int32)[:, None]).astype(BF16)
    return t_cmp, t_sel, t_win, expand


def _sample_attn_kernel(pt_ref, *refs, n_pages, n_sel, t, wbuf):
    del pt_ref
    it = iter(refs)
    qbd_ref, qrow_ref, g_ref = next(it), next(it), next(it)
    kc_ref, vc_ref = next(it), next(it)
    ks_pages = [next(it) for _ in range(n_pages)]
    vs_pages = [next(it) for _ in range(n_pages)]
    ksn_ref, vsn_ref, kwin_ref, vwin_ref, kwn_ref, vwn_ref = (next(it) for _ in range(6))
    tcmp_ref, tsel_ref, twin_ref, expand_ref = (next(it) for _ in range(4))
    o_ref = next(it)
    imp_ref, val_ref, sel_ref, s_ref = (next(it) for _ in range(4))

    scale = HEAD_DIM ** -0.5
    nq = N_KV * t
    qs = qbd_ref[...] * scale
    qr = qrow_ref[...] * scale
    n_q = qr.shape[0]
    lane = lax.broadcasted_iota(jnp.int32, (1, SAMPLE_LANES), 1)
    blocks_per_page = PAGE_SIZE // CMP_BLK
    past = n_pages * PAGE_SIZE
    nt_dims = (((1,), (1,)), ((), ()))

    s = jnp.dot(kc_ref[...].astype(BF16), qs, preferred_element_type=F32) + tcmp_ref[...]
    m = jnp.max(s, axis=0, keepdims=True)
    p_c = jnp.exp(s - m)
    p_c = p_c / jnp.maximum(jnp.sum(p_c, axis=0, keepdims=True), 1e-30)
    o_c = lax.dot_general(p_c.astype(BF16), vc_ref[...].astype(BF16), (((0,), (0,)), ((), ())),
                          preferred_element_type=F32)[:n_q]

    imp = p_c
    for g in range(1, GROUP):
        imp = imp + pltpu.roll(p_c, SAMPLE_LANES - g * nq, 1)
    imp_ref[...] = imp
    n_pairs = n_pages * blocks_per_page // 2
    n_blk = n_pairs + 1
    n_rows = val_ref.shape[0]
    val_ref[...] = jnp.full((n_rows, SAMPLE_LANES), -2.0, F32)
    val_ref[0:n_pairs, :] = imp_ref[pl.ds(0, n_pairs, stride=2), :] + imp_ref[pl.ds(1, n_pairs, stride=2), :]
    blk = lax.broadcasted_iota(jnp.int32, (n_rows, SAMPLE_LANES), 0)
    qblk = n_blk - 1
    forced = (blk == 0) | (blk == qblk) | (blk == qblk - 1)
    val = jnp.where(forced, BIG, val_ref[...])
    val = jnp.where(blk < n_blk, val, -2.0)
    val_ref[...] = val
    ranks = [jnp.zeros((n_rows, SAMPLE_LANES), F32) for _ in range(4)]
    for j in range(n_blk):
        row = val_ref[j:j + 1, :]
        ranks[j % 4] = ranks[j % 4] + jnp.where(blk > j, jnp.where(row >= val, 1.0, 0.0),
                                                jnp.where(row > val, 1.0, 0.0))
    rank = (ranks[0] + ranks[1]) + (ranks[2] + ranks[3])
    selneg = jnp.where((rank < n_sel) & (lane < nq), 0.0, jnp.where(lane < nq, NEG, 0.0))
    selrep = selneg
    for g in range(1, GROUP):
        selrep = selrep + pltpu.roll(selneg, g * nq, 1)
    sel_ref[...] = jnp.zeros_like(sel_ref)
    sel_ref[0:n_rows, :] = jnp.where(selrep == 0.0, 1.0, 0.0)
    sel01 = sel_ref[...].T[:n_q].astype(BF16)

    def paged(page_ref):
        return page_ref[...].reshape(KV_DIM, PAGE_SIZE).astype(BF16)

    for p in range(n_pages + 1):
        cols = slice(p * KEY_TILE, (p + 1) * KEY_TILE)
        if p < n_pages:
            s = jnp.dot(qr, paged(ks_pages[p]), preferred_element_type=F32)
        else:
            s = lax.dot_general(qr, ksn_ref[...], nt_dims, preferred_element_type=F32)
        picked = jnp.dot(sel01, expand_ref[:, cols], preferred_element_type=F32)
        s_ref[:, cols] = s + tsel_ref[:, cols] + (picked - 1.0) * (-NEG)
    s = s_ref[...]
    p_s = jnp.exp(s - jnp.max(s, axis=1, keepdims=True))
    l = jnp.sum(p_s, axis=1, keepdims=True)
    p_s = p_s.astype(BF16)
    acc = jnp.dot(p_s[:, past:], vsn_ref[...], preferred_element_type=F32)
    for p in range(n_pages):
        acc = acc + lax.dot_general(p_s[:, p * KEY_TILE:(p + 1) * KEY_TILE], paged(vs_pages[p]), nt_dims,
                                    preferred_element_type=F32)
    o_s = acc / jnp.maximum(l, 1e-30)

    kwin = kwin_ref[...].reshape(KV_DIM, wbuf).astype(BF16)
    vwin = vwin_ref[...].reshape(KV_DIM, wbuf).astype(BF16)
    s_w = jnp.dot(qr, kwin, preferred_element_type=F32) + twin_ref[:, :wbuf]
    s_n = lax.dot_general(qr, kwn_ref[...], nt_dims, preferred_element_type=F32) + twin_ref[:, wbuf:]
    m = jnp.maximum(jnp.max(s_w, axis=1, keepdims=True), jnp.max(s_n, axis=1, keepdims=True))
    p_w, p_n = jnp.exp(s_w - m), jnp.exp(s_n - m)
    l = jnp.sum(p_w, axis=1, keepdims=True) + jnp.sum(p_n, axis=1, keepdims=True)
    acc = (lax.dot_general(p_w.astype(BF16), vwin, nt_dims, preferred_element_type=F32)
           + jnp.dot(p_n.astype(BF16), vwn_ref[...], preferred_element_type=F32))
    o_w = acc / jnp.maximum(l, 1e-30)

    gate = _sigmoid(g_ref[...])
    o_ref[...] = gate[:, 0:1] * o_c + gate[:, 1:2] * o_s + gate[:, 2:3] * o_w


def _sample_attention(q, gates, k_cmp, v_cmp, cache_ks, cache_vs, page_table,
                      ksl, vsl, win_k, win_v, kw, vw, rel_bias, b, t):
    n_pages = page_table.shape[1]
    past = n_pages * PAGE_SIZE
    wbuf = win_k.shape[-1]
    nq = N_KV * t
    used = GROUP * nq
    assert used % 8 == 0 and used <= SAMPLE_LANES and t <= NEW_PAD
    blocks_per_page = PAGE_SIZE // CMP_BLK
    n_blk = past // SEL_BLK + 1
    n_rows = -(-n_blk // 8) * 8

    q5 = q.reshape(b, t, N_KV, GROUP, HEAD_DIM)
    eye = jnp.eye(N_KV, dtype=q.dtype)
    qbd = jnp.einsum("btkgd,kc->bkdgct", q5, eye).reshape(b, KV_DIM, used)
    qbd = jnp.pad(qbd, ((0, 0), (0, 0), (0, SAMPLE_LANES - used))).astype(BF16)
    qrow = jnp.einsum("btkgd,kc->bgktcd", q5, eye).reshape(b, used, KV_DIM).astype(BF16)
    g_rows = gates[:, :3 * N_HEADS].reshape(b, t, N_KV, GROUP, 3).transpose(0, 3, 2, 1, 4).reshape(b, used, 3)
    g_rows = jnp.pad(g_rows, ((0, 0), (0, 0), (0, 128 - 3)))
    pad_rows = lambda a: jnp.pad(a.reshape(b, t, KV_DIM), ((0, 0), (0, NEW_PAD - t), (0, 0))).astype(BF16)
    tables = _sample_tables(rel_bias, past, t, wbuf, used)

    per_b = lambda *shape: pl.BlockSpec((None,) + shape, lambda bi, pt: (bi,) + (0,) * len(shape))
    const = lambda *shape: pl.BlockSpec(shape, lambda bi, pt: (0,) * len(shape))

    def page(p):
        return pl.BlockSpec((None, N_KV, HEAD_DIM, PAGE_SIZE), lambda bi, pt, p=p: (pt[bi, p], 0, 0, 0))

    n_cmp = n_pages * blocks_per_page
    in_specs = ([per_b(KV_DIM, SAMPLE_LANES), per_b(used, KV_DIM), per_b(used, 128),
                 per_b(n_cmp, KV_DIM), per_b(n_cmp, KV_DIM)]
                + [page(p) for p in range(n_pages)] * 2
                + [per_b(NEW_PAD, KV_DIM), per_b(NEW_PAD, KV_DIM),
                   per_b(N_KV, HEAD_DIM, wbuf), per_b(N_KV, HEAD_DIM, wbuf),
                   per_b(NEW_PAD, KV_DIM), per_b(NEW_PAD, KV_DIM)]
                + [const(*tb.shape) for tb in tables])
    out = pl.pallas_call(
        functools.partial(_sample_attn_kernel, n_pages=n_pages, n_sel=min(N_SEL, n_blk), t=t, wbuf=wbuf),
        grid_spec=pltpu.PrefetchScalarGridSpec(
            num_scalar_prefetch=1,
            grid=(b,),
            in_specs=in_specs,
            out_specs=per_b(used, KV_DIM),
            scratch_shapes=[pltpu.VMEM((n_cmp, SAMPLE_LANES), F32),
                            pltpu.VMEM((n_rows, SAMPLE_LANES), F32),
                            pltpu.VMEM((128, SAMPLE_LANES), F32),
                            pltpu.VMEM((used, past + NEW_PAD), F32)]),
        out_shape=jax.ShapeDtypeStruct((b, used, KV_DIM), F32),
        compiler_params=_cparams("arbitrary"),
        name="nsa_sample_attention",
    )(page_table, qbd, qrow, g_rows, k_cmp.reshape(b, n_cmp, KV_DIM), v_cmp.reshape(b, n_cmp, KV_DIM),
      *([cache_ks] * n_pages),
      *([cache_vs] * n_pages), pad_rows(ksl), pad_rows(vsl), win_k, win_v, pad_rows(kw), pad_rows(vw), *tables)
    o = out.reshape(b, GROUP, N_KV, t, N_KV, HEAD_DIM)
    o = jnp.einsum("bgktkd->btkgd", o)
    return o.reshape(b * t, Q_DIM)


def _compress_weights(w_c, pe_c):
    eye = jnp.eye(N_KV, dtype=w_c.dtype)
    w_big = jnp.einsum("jde,kc->jkdce", w_c, eye).reshape(CMP_BLK * KV_DIM, KV_DIM)
    pe_flat = jnp.broadcast_to(pe_c[:, None, :], (CMP_BLK, N_KV, HEAD_DIM)).reshape(1, CMP_BLK * KV_DIM)
    pe_rows = jnp.pad(pe_flat, ((0, 7), (0, 0)))
    bias = _matmul(pe_rows, w_big, name="compress_pe")[0:1]
    return w_big, bias


_BLOCK_PITCH = CMP_BLK + 4


def _compress_pages_kernel(pt_ref, *refs, n_pages):
    del pt_ref
    page_refs = refs[:n_pages]
    _compress_position_minor(lambda p: page_refs[p][...].reshape(KV_DIM, PAGE_SIZE), n_pages, *refs[n_pages:])


def _compress_tokens_kernel(x_ref, w_ref, b_ref, o_ref, rows_ref, *, n_pages):
    _compress_position_minor(lambda p: x_ref[:, p * PAGE_SIZE:(p + 1) * PAGE_SIZE], n_pages,
                             w_ref, b_ref, o_ref, rows_ref)


def _compress_position_minor(get_page, n_pages, w_ref, b_ref, o_ref, rows_ref):
    per_page = PAGE_SIZE // CMP_BLK
    for p in range(n_pages):
        xt = get_page(p).T
        for n in range(per_page):
            lo = (p * per_page + n) * _BLOCK_PITCH
            for h in range(2):
                rows_ref[h, lo:lo + CMP_BLK, :] = xt[n * CMP_BLK:(n + 1) * CMP_BLK, h * 128:(h + 1) * 128]
    n_out = n_pages * per_page
    acc = jnp.broadcast_to(b_ref[...], (n_out, KV_DIM))
    for j in range(CMP_BLK):
        for h in range(2):
            piece = rows_ref[h, pl.ds(j, n_out, stride=_BLOCK_PITCH), :].astype(BF16)
            lo = j * KV_DIM + h * 128
            acc = acc + jnp.dot(piece, w_ref[lo:lo + 128, :], preferred_element_type=F32)
    o_ref[...] = acc


def _compress_pages(cache_t, page_table, w_big, bias):
    assert KV_DIM == 256
    b, per_b = page_table.shape
    group = max(1, 32 // per_b)
    assert b % group == 0
    n_pages = group * per_b
    n_out = n_pages * (PAGE_SIZE // CMP_BLK)

    def page(p):
        return pl.BlockSpec((None, N_KV, HEAD_DIM, PAGE_SIZE),
                            lambda i, pt, p=p: (pt[i * group + p // per_b, p % per_b], 0, 0, 0))

    return pl.pallas_call(
        functools.partial(_compress_pages_kernel, n_pages=n_pages),
        grid_spec=pltpu.PrefetchScalarGridSpec(
            num_scalar_prefetch=1,
            grid=(b // group,),
            in_specs=[page(p) for p in range(n_pages)]
            + [pl.BlockSpec((CMP_BLK * KV_DIM, KV_DIM), lambda i, pt: (0, 0), pipeline_mode=pl.Buffered(1)),
               pl.BlockSpec((1, KV_DIM), lambda i, pt: (0, 0))],
            out_specs=pl.BlockSpec((n_out, KV_DIM), lambda i, pt: (i, 0)),
            scratch_shapes=[pltpu.VMEM((2, n_out * _BLOCK_PITCH, 128), F32)]),
        out_shape=jax.ShapeDtypeStruct((b * per_b * (PAGE_SIZE // CMP_BLK), KV_DIM), F32),
        compiler_params=_cparams("arbitrary"),
        name="compress_pages",
    )(page_table, *([cache_t] * n_pages), w_big.astype(BF16), bias)


def _compress_tokens(kv_t, w_big, bias):
    assert KV_DIM == 256
    b, _, t = kv_t.shape
    n_pages = math.gcd(32, t // PAGE_SIZE)
    steps = t // (n_pages * PAGE_SIZE)
    n_out = n_pages * (PAGE_SIZE // CMP_BLK)
    return pl.pallas_call(
        functools.partial(_compress_tokens_kernel, n_pages=n_pages),
        grid=(b, steps),
        in_specs=[pl.BlockSpec((None, KV_DIM, n_pages * PAGE_SIZE), lambda bi, i: (bi, 0, i)),
                  pl.BlockSpec((CMP_BLK * KV_DIM, KV_DIM), lambda bi, i: (0, 0), pipeline_mode=pl.Buffered(1)),
                  pl.BlockSpec((1, KV_DIM), lambda bi, i: (0, 0))],
        out_specs=pl.BlockSpec((n_out, KV_DIM), lambda bi, i: (bi * steps + i, 0)),
        out_shape=jax.ShapeDtypeStruct((b * t // CMP_BLK, KV_DIM), F32),
        scratch_shapes=[pltpu.VMEM((2, n_out * _BLOCK_PITCH, 128), F32)],
        compiler_params=_cparams("parallel", "parallel"),
        name="compress_tokens",
    )(kv_t, w_big.astype(BF16), bias)


def _ada(c_all, ada_w, ada_b):
    mods = []
    for i in range(ada_w.shape[0]):
        mods.append(_matmul(c_all, ada_w[i], ada_b[i].reshape(1, -1), silu_in=True, tn=2 * D_MODEL,
                            name="ada_modulate"))
    return mods


def kernel(x_prompt, x_sample, c_prompt, c_sample, cache_k_cmp, cache_v_cmp, cache_k_sel, cache_v_sel,
           page_table, state_k_win, state_v_win, state_pool, rel_bias, ada_w, ada_b, norm_g, final_g,
           nsa_w_in, nsa_w_out, cmp_wk, cmp_wv, cmp_pe_k, cmp_pe_v, pool_w, pool_scale,
           ffn_wg, ffn_wu, ffn_wd):
    bp, tp, _ = x_prompt.shape
    bs, ts, _ = x_sample.shape
    n_phys = cache_k_cmp.shape[1]
    past = page_table.shape[1] * PAGE_SIZE
    depth = ada_w.shape[0]
    assert depth == 2 and tp % KEY_TILE == 0 and ts <= 8 and past % PAGE_SIZE == 0

    n_c = bp + bs
    c_all = jnp.pad(jnp.concatenate([c_prompt, c_sample], axis=0), ((0, -n_c % 8), (0, 0)))
    mods = _ada(c_all, ada_w, ada_b)

    def mod_prompt(i):
        return [v[:bp].reshape(bp, 1, D_MODEL) for v in jnp.split(mods[i], 6, axis=-1)]

    def mod_sample(i):
        return [jnp.repeat(v[bp:n_c], ts, axis=0) for v in jnp.split(mods[i], 6, axis=-1)]

    wk_big, k_bias = _compress_weights(cmp_wk[0], cmp_pe_k[0])
    wv_big, v_bias = _compress_weights(cmp_wv[0], cmp_pe_v[0])

    mp = bp * tp
    x = x_prompt.reshape(mp, D_MODEL)
    sh1, sc1, g1, sh2, sc2, g2 = mod_prompt(0)
    kv_t, layouts = _nsa_project(x, norm_g[0, 0], sc1, sh1, nsa_w_in[0], tp, ROW_TILE, True)
    k_cmp = _compress_tokens(kv_t[0], wk_big, k_bias)
    v_cmp = _compress_tokens(kv_t[1], wv_big, v_bias)
    o_t = _prompt_attention(layouts, k_cmp, v_cmp, rel_bias, bp, tp)
    x = _matmul_residual_t(o_t, nsa_w_out[0], x, g1, tp, ROW_TILE)
    x = _ffn(x, norm_g[0, 1], sc2, sh2, g2, ffn_wg[0], ffn_wu[0], ffn_wd[0], final_g, tp, ROW_TILE, FFN_SLICES, False)
    sh1, sc1, g1, sh2, sc2, g2 = mod_prompt(1)
    x3, pool_p = _pool_mix(x.reshape(bp, tp, D_MODEL), jnp.zeros((bp, POOL_STATE, D_MODEL), F32),
                           norm_g[1, 0], sc1, sh1, g1, pool_w[0], pool_scale[0], 0, ROW_TILE)
    y_prompt = _ffn(x3.reshape(mp, D_MODEL), norm_g[1, 1], sc2, sh2, g2, ffn_wg[1], ffn_wu[1], ffn_wd[1],
                    final_g, tp, ROW_TILE, FFN_SLICES, True).reshape(bp, tp, D_MODEL)
    win = min(WINDOW, tp)
    st5 = lambda a, t0: jnp.transpose(a[:, :, t0:].reshape(bp, N_KV, HEAD_DIM, tp - t0), (0, 3, 1, 2))[None]
    prompt_states = tuple(st5(a, 0) for a in kv_t[:4]) + tuple(st5(a, tp - win) for a in kv_t[4:]) + (pool_p[None],)

    ms = bs * ts
    x = x_sample.reshape(ms, D_MODEL)
    sh1, sc1, g1, sh2, sc2, g2 = mod_sample(0)
    (kc, vc, ksl, vsl, kw, vw), (q, gates) = _nsa_project(x, norm_g[0, 0], sc1, sh1, nsa_w_in[0], ts, ROW_TILE, False)
    assert (past + ts) // CMP_BLK == past // CMP_BLK
    pos_last = lambda a: jnp.transpose(a[0], (0, 2, 3, 1))
    k_cmp_s = _compress_pages(pos_last(cache_k_cmp), page_table, wk_big, k_bias)
    v_cmp_s = _compress_pages(pos_last(cache_v_cmp), page_table, wv_big, v_bias)
    o = _sample_attention(q, gates, k_cmp_s, v_cmp_s, pos_last(cache_k_sel), pos_last(cache_v_sel),
                          page_table, ksl, vsl, pos_last(state_k_win), pos_last(state_v_win), kw, vw,
                          rel_bias, bs, ts)
    x = _matmul_residual(o, nsa_w_out[0], x, g1, ts, ROW_TILE)
    x = _ffn(x, norm_g[0, 1], sc2, sh2, g2, ffn_wg[0], ffn_wu[0], ffn_wd[0], final_g, ts, ROW_TILE, FFN_SLICES, False)
    sh1, sc1, g1, sh2, sc2, g2 = mod_sample(1)
    b3 = lambda v: v.reshape(bs, ts, D_MODEL)[:, :1]
    x3, pool_s = _pool_mix(x.reshape(bs, ts, D_MODEL), state_pool[0], norm_g[1, 0], b3(sc1), b3(sh1), b3(g1),
                           pool_w[0], pool_scale[0], past, ROW_TILE)
    y_sample = _ffn(x3.reshape(ms, D_MODEL), norm_g[1, 1], sc2, sh2, g2, ffn_wg[1], ffn_wu[1], ffn_wd[1],
                    final_g, ts, ROW_TILE, FFN_SLICES, True).reshape(bs, ts, D_MODEL)
    st5 = lambda a: a.reshape(1, bs, ts, N_KV, HEAD_DIM)
    kw_ext = jnp.concatenate([state_k_win[0], st5(kw)[0]], axis=1)[:, ts:]
    vw_ext = jnp.concatenate([state_v_win[0], st5(vw)[0]], axis=1)[:, ts:]
    sample_states = (st5(kc), st5(vc), st5(ksl), st5(vsl), kw_ext[None], vw_ext[None], pool_s[None])

    return (y_prompt, y_sample) + prompt_states + sample_states
```

```python
import functools
import math

import jax
import jax.numpy as jnp
from jax import lax
from jax.experimental import pallas as pl
from jax.experimental.pallas import tpu as pltpu

D_MODEL = 1024
N_HEADS = 16
N_KV = 4
GROUP = N_HEADS // N_KV
HEAD_DIM = 64
Q_DIM = N_HEADS * HEAD_DIM
KV_DIM = N_KV * HEAD_DIM
CMP_BLK = 32
SEL_BLK = 64
N_SEL = 16
WINDOW = 512
Q_CHUNK = 64
N_BUCKETS = 32
MAX_DISTANCE = 128
POOL_WINDOWS = (2, 4, 8, 16)
POOL_GROUP_DIM = D_MODEL // len(POOL_WINDOWS)
POOL_STATE = max(POOL_WINDOWS) - 1
PAGE_SIZE = 128
RMS_EPS = 1e-6
NEG = -1e30
BIG = 1e9

KEY_TILE = 128
CHUNK = 128
LANES = GROUP * CHUNK
LOG2E = math.log2(math.e)
Q_SCALE_LOG2 = HEAD_DIM ** -0.5 * LOG2E
SAMPLE_LANES = 128
VMEM_LIMIT_BYTES = 48 * 1024 * 1024
ROW_TILE = 512
FFN_SLICES = 2

F32 = jnp.float32
BF16 = jnp.bfloat16


def _cparams(*sem):
    return pltpu.CompilerParams(dimension_semantics=sem, vmem_limit_bytes=VMEM_LIMIT_BYTES)


def _silu(x):
    return x * (1.0 / (1.0 + jnp.exp(-x)))


def _sigmoid(x):
    return 1.0 / (1.0 + jnp.exp(-x))


def _normmod(x, g, sc, sh):
    ms = jnp.mean(x * x, axis=-1, keepdims=True)
    return (x * lax.rsqrt(ms + RMS_EPS) * g) * (1.0 + sc) + sh


def _mod_spec(mod, tm, rows_per_batch):
    if mod.ndim == 3:
        return pl.BlockSpec((None, 1, D_MODEL), lambda i, *_: ((i * tm) // rows_per_batch, 0, 0))
    return pl.BlockSpec((tm, D_MODEL), lambda i, *_: (i, 0))


def _mm_kernel(a_ref, w_ref, b_ref, o_ref, *, silu_in):
    a = a_ref[...]
    if silu_in:
        a = _silu(a)
    o_ref[...] = jnp.dot(a.astype(BF16), w_ref[...], preferred_element_type=F32) + b_ref[...]


def _matmul(a, w, bias=None, *, silu_in=False, tm=256, tn=None, name="matmul"):
    m, k = a.shape
    n = w.shape[1]
    tm = min(tm, m)
    tn = n if tn is None else tn
    assert m % tm == 0 and n % tn == 0
    if bias is None:
        bias = jnp.zeros((1, n), F32)
    return pl.pallas_call(
        functools.partial(_mm_kernel, silu_in=silu_in),
        grid=(m // tm, n // tn),
        in_specs=[pl.BlockSpec((tm, k), lambda i, j: (i, 0)),
                  pl.BlockSpec((k, tn), lambda i, j: (0, j)),
                  pl.BlockSpec((1, tn), lambda i, j: (0, j))],
        out_specs=pl.BlockSpec((tm, tn), lambda i, j: (i, j)),
        out_shape=jax.ShapeDtypeStruct((m, n), F32),
        compiler_params=_cparams("parallel", "parallel"),
        name=name,
    )(a, w.astype(BF16), bias)


_PROJ_MAIN = Q_DIM + 6 * KV_DIM
_GATE_PAD = 128


def _proj_kernel(x_ref, g_ref, sc_ref, sh_ref, w_ref, wg_ref, *out_refs, attn_layouts, tm):
    hb = _normmod(x_ref[...], g_ref[...], sc_ref[...], sh_ref[...]).astype(BF16)
    q = jnp.dot(hb, w_ref[:, :Q_DIM], preferred_element_type=F32)
    gates = jnp.dot(hb, wg_ref[...], preferred_element_type=F32)
    kvs = []
    for n in range(6):
        lo = Q_DIM + n * KV_DIM
        kvs.append(jnp.dot(hb, w_ref[:, lo:lo + KV_DIM], preferred_element_type=F32))
    if not attn_layouts:
        for n in range(6):
            out_refs[n][...] = kvs[n]
        out_refs[6][...] = q
        out_refs[7][...] = gates
        return
    kt_refs = out_refs[:6]
    qt_ref, gt_ref, ks_ref, vst_ref, kw_ref, vwt_ref, q_scr, gate_scr, kv_scr = out_refs[6:]
    kvts = []
    for n in range(6):
        kv_scr[n] = kvs[n]
        kvts.append(kv_scr[n].T)
        kt_refs[n][...] = kvts[n]
    n_chunks = tm // CHUNK
    q_scr[...] = q * Q_SCALE_LOG2
    gate_scr[...] = gates
    qt = q_scr[...].T
    gt = gate_scr[...].T
    for c in range(n_chunks):
        tok = slice(c * CHUNK, (c + 1) * CHUNK)
        for kv in range(N_KV):
            for gi in range(GROUP):
                h = kv * GROUP + gi
                lanes = slice(gi * CHUNK, (gi + 1) * CHUNK)
                qt_ref[c, kv, :, lanes] = qt[h * HEAD_DIM:(h + 1) * HEAD_DIM, tok].astype(BF16)
                gt_ref[c, kv, :, lanes] = gt[3 * h:3 * h + 3, tok]
    for k_ref, vt_ref, k, vt in ((ks_ref, vst_ref, kvs[2], kvts[3]), (kw_ref, vwt_ref, kvs[4], kvts[5])):
        for kv in range(N_KV):
            cols = slice(kv * HEAD_DIM, (kv + 1) * HEAD_DIM)
            k_ref[kv] = k[:, cols].astype(BF16)
            for c in range(n_chunks):
                vt_ref[kv, c, 0:HEAD_DIM, :] = vt[cols, c * CHUNK:(c + 1) * CHUNK].astype(BF16)
                vt_ref[kv, c, HEAD_DIM:, :] = jnp.ones((V_ROWS - HEAD_DIM, CHUNK), BF16)


def _nsa_project(x, g, sc, sh, w_in, rows_per_batch, tm, attn_layouts):
    m = x.shape[0]
    tm = min(tm, m)
    w_main = w_in[:, :_PROJ_MAIN].astype(BF16)
    w_gate = jnp.pad(w_in[:, _PROJ_MAIN:], ((0, 0), (0, _GATE_PAD - 3 * N_HEADS))).astype(BF16)
    row = lambda n: pl.BlockSpec((tm, n), lambda i: (i, 0))
    if attn_layouts:
        assert tm % CHUNK == 0 and rows_per_batch % tm == 0
        nck = tm // CHUNK
        tpb = rows_per_batch // tm
        out_specs = [pl.BlockSpec((None, KV_DIM, tm), lambda i: (i // tpb, 0, i % tpb))] * 6
        out_shape = [jax.ShapeDtypeStruct((m // rows_per_batch, KV_DIM, rows_per_batch), F32)] * 6
        chunked = lambda *s: pl.BlockSpec((nck,) + s, lambda i: (i,) + (0,) * len(s))
        k_spec = pl.BlockSpec((N_KV, tm, HEAD_DIM), lambda i: (0, i, 0))
        vt_spec = pl.BlockSpec((N_KV, nck, V_ROWS, CHUNK), lambda i: (0, i, 0, 0))
        k_shape = jax.ShapeDtypeStruct((N_KV, m, HEAD_DIM), BF16)
        vt_shape = jax.ShapeDtypeStruct((N_KV, m // CHUNK, V_ROWS, CHUNK), BF16)
        out_specs += [chunked(N_KV, HEAD_DIM, LANES), chunked(N_KV, 3, LANES), k_spec, vt_spec, k_spec, vt_spec]
        out_shape += [jax.ShapeDtypeStruct((m // CHUNK, N_KV, HEAD_DIM, LANES), BF16),
                      jax.ShapeDtypeStruct((m // CHUNK, N_KV, 3, LANES), F32),
                      k_shape, vt_shape, k_shape, vt_shape]
    else:
        out_specs = [row(KV_DIM)] * 6 + [row(Q_DIM), row(_GATE_PAD)]
        out_shape = ([jax.ShapeDtypeStruct((m, KV_DIM), F32)] * 6
                     + [jax.ShapeDtypeStruct((m, Q_DIM), F32), jax.ShapeDtypeStruct((m, _GATE_PAD), F32)])
    outs = pl.pallas_call(
        functools.partial(_proj_kernel, attn_layouts=attn_layouts, tm=tm),
        grid=(m // tm,),
        in_specs=[row(D_MODEL),
                  pl.BlockSpec((1, D_MODEL), lambda i: (0, 0)),
                  _mod_spec(sc, tm, rows_per_batch), _mod_spec(sh, tm, rows_per_batch),
                  pl.BlockSpec((D_MODEL, _PROJ_MAIN), lambda i: (0, 0)),
                  pl.BlockSpec((D_MODEL, _GATE_PAD), lambda i: (0, 0))],
        out_specs=out_specs,
        out_shape=out_shape,
        scratch_shapes=([pltpu.VMEM((tm, Q_DIM), F32), pltpu.VMEM((tm, _GATE_PAD), F32),
                         pltpu.VMEM((6, tm, KV_DIM), F32)] if attn_layouts else []),
        compiler_params=_cparams("parallel"),
        name="nsa_project",
    )(x, g.reshape(1, D_MODEL), sc, sh, w_main, w_gate)
    if attn_layouts:
        return outs[:6], outs[6:]
    return outs[:6], outs[6:]


def _mm_res_kernel(a_ref, w_ref, x_ref, gate_ref, o_ref):
    y = jnp.dot(a_ref[...].astype(BF16), w_ref[...], preferred_element_type=F32)
    o_ref[...] = x_ref[...] + gate_ref[...] * y


def _matmul_residual(a, w, x, gate, rows_per_batch, tm):
    m, k = a.shape
    tm = min(tm, m)
    return pl.pallas_call(
        _mm_res_kernel,
        grid=(m // tm,),
        in_specs=[pl.BlockSpec((tm, k), lambda i: (i, 0)),
                  pl.BlockSpec((k, D_MODEL), lambda i: (0, 0)),
                  pl.BlockSpec((tm, D_MODEL), lambda i: (i, 0)),
                  _mod_spec(gate, tm, rows_per_batch)],
        out_specs=pl.BlockSpec((tm, D_MODEL), lambda i: (i, 0)),
        out_shape=jax.ShapeDtypeStruct((m, D_MODEL), F32),
        compiler_params=_cparams("parallel"),
        name="out_proj_residual",
    )(a, w.astype(BF16), x, gate)


def _mm_res_t_kernel(ot_ref, w_ref, x_ref, gate_ref, o_ref, *, n_chunks):
    chunks = []
    for c in range(n_chunks):
        rows = [ot_ref[c, kv, :, gi * CHUNK:(gi + 1) * CHUNK] for kv in range(N_KV) for gi in range(GROUP)]
        chunks.append(jnp.concatenate(rows, axis=0).astype(F32).T)
    a = jnp.concatenate(chunks, axis=0).astype(BF16)
    y = jnp.dot(a, w_ref[...], preferred_element_type=F32)
    o_ref[...] = x_ref[...] + gate_ref[...] * y


def _matmul_residual_t(o_t, w, x, gate, rows_per_batch, tm):
    m = x.shape[0]
    nck = tm // CHUNK
    return pl.pallas_call(
        functools.partial(_mm_res_t_kernel, n_chunks=nck),
        grid=(m // tm,),
        in_specs=[pl.BlockSpec((nck, N_KV, HEAD_DIM, LANES), lambda i: (i, 0, 0, 0)),
                  pl.BlockSpec((Q_DIM, D_MODEL), lambda i: (0, 0)),
                  pl.BlockSpec((tm, D_MODEL), lambda i: (i, 0)),
                  _mod_spec(gate, tm, rows_per_batch)],
        out_specs=pl.BlockSpec((tm, D_MODEL), lambda i: (i, 0)),
        out_shape=jax.ShapeDtypeStruct((m, D_MODEL), F32),
        compiler_params=_cparams("parallel"),
        name="out_proj_residual_t",
    )(o_t, w.astype(BF16), x, gate)


def _ffn_kernel(x_ref, g_ref, sc_ref, sh_ref, gate_ref, wg_ref, wu_ref, wd_ref, fg_ref, o_ref,
                *, final_norm, tf):
    x = x_ref[...]
    hb = _normmod(x, g_ref[...], sc_ref[...], sh_ref[...]).astype(BF16)
    acc = None
    for lo in range(0, wg_ref.shape[1], tf):
        a = jnp.dot(hb, wg_ref[:, lo:lo + tf], preferred_element_type=F32)
        u = jnp.dot(hb, wu_ref[:, lo:lo + tf], preferred_element_type=F32)
        act = (_silu(a) * u).astype(BF16)
        part = jnp.dot(act, wd_ref[lo:lo + tf, :], preferred_element_type=F32)
        acc = part if acc is None else acc + part
    y = x + gate_ref[...] * acc
    if final_norm:
        ms = jnp.mean(y * y, axis=-1, keepdims=True)
        y = y * lax.rsqrt(ms + RMS_EPS) * fg_ref[...]
    o_ref[...] = y


def _ffn(x, g, sc, sh, gate, wg, wu, wd, final_g, rows_per_batch, tm, n_slices, final_norm):
    m = x.shape[0]
    d_ff = wg.shape[1]
    tm = min(tm, m)
    assert d_ff % (n_slices * 128) == 0
    tf = d_ff // n_slices
    once = pl.Buffered(1)
    vec = pl.BlockSpec((1, D_MODEL), lambda i: (0, 0))
    return pl.pallas_call(
        functools.partial(_ffn_kernel, final_norm=final_norm, tf=tf),
        grid=(m // tm,),
        in_specs=[pl.BlockSpec((tm, D_MODEL), lambda i: (i, 0)),
                  vec,
                  _mod_spec(sc, tm, rows_per_batch), _mod_spec(sh, tm, rows_per_batch),
                  _mod_spec(gate, tm, rows_per_batch),
                  pl.BlockSpec((D_MODEL, d_ff), lambda i: (0, 0), pipeline_mode=once),
                  pl.BlockSpec((D_MODEL, d_ff), lambda i: (0, 0), pipeline_mode=once),
                  pl.BlockSpec((d_ff, D_MODEL), lambda i: (0, 0), pipeline_mode=once),
                  vec],
        out_specs=pl.BlockSpec((tm, D_MODEL), lambda i: (i, 0)),
        out_shape=jax.ShapeDtypeStruct((m, D_MODEL), F32),
        compiler_params=_cparams("parallel"),
        name="ffn",
    )(x, g.reshape(1, D_MODEL), sc, sh, gate, wg.astype(BF16), wu.astype(BF16), wd.astype(BF16),
      final_g.reshape(1, D_MODEL))


_POOL_HALO = 16


def _pool_kernel(x_ref, xprev_ref, state_ref, g_ref, sc_ref, sh_ref, gate_ref, w_ref, ls_ref,
                 o_ref, st_ref, ext_ref, *, tm, pos0):
    i = pl.program_id(1)
    g, sc, sh = g_ref[...], sc_ref[...], sh_ref[...]
    u = _normmod(x_ref[...], g, sc, sh)
    prev = jnp.where(i == 0, state_ref[...], _normmod(xprev_ref[...], g, sc, sh))
    ext_ref[0:_POOL_HALO, :] = prev
    ext_ref[_POOL_HALO:_POOL_HALO + tm, :] = u
    st_ref[...] = ext_ref[tm:tm + _POOL_HALO, :]

    pos = pos0 + i * tm + lax.broadcasted_iota(jnp.int32, (tm, 1), 0)
    mixed = []
    for gi, w in enumerate(POOL_WINDOWS):
        lo = gi * POOL_GROUP_DIM
        s = u[:, lo:lo + POOL_GROUP_DIM]
        for k in range(1, w):
            s = s + ext_ref[_POOL_HALO - k:_POOL_HALO - k + tm, lo:lo + POOL_GROUP_DIM]
        cnt = jnp.minimum(pos + 1, w).astype(F32)
        pooled = s / cnt - u[:, lo:lo + POOL_GROUP_DIM]
        mixed.append(jnp.dot(pooled.astype(BF16), w_ref[gi], preferred_element_type=F32))
    y = jnp.concatenate(mixed, axis=-1) * ls_ref[...]
    o_ref[...] = x_ref[...] + gate_ref[...] * y


def _pool_mix(x3, state, g, sc, sh, gate, w_grp, layer_scale, pos0, tm):
    b, t, _ = x3.shape
    tm = min(tm, t)
    state16 = jnp.pad(state, ((0, 0), (_POOL_HALO - POOL_STATE, 0), (0, 0)))
    if t >= _POOL_HALO:
        xprev = x3
        nprev = tm // _POOL_HALO
        prev_spec = pl.BlockSpec((None, _POOL_HALO, D_MODEL),
                                 lambda bi, i: (bi, jnp.maximum(i * nprev - 1, 0), 0))
    else:
        xprev = state16
        prev_spec = pl.BlockSpec((None, _POOL_HALO, D_MODEL), lambda bi, i: (bi, 0, 0))
    vec = pl.BlockSpec((1, D_MODEL), lambda bi, i: (0, 0))
    bvec = pl.BlockSpec((None, 1, D_MODEL), lambda bi, i: (bi, 0, 0))
    out, st = pl.pallas_call(
        functools.partial(_pool_kernel, tm=tm, pos0=pos0),
        grid=(b, t // tm),
        in_specs=[pl.BlockSpec((None, tm, D_MODEL), lambda bi, i: (bi, i, 0)),
                  prev_spec,
                  pl.BlockSpec((None, _POOL_HALO, D_MODEL), lambda bi, i: (bi, 0, 0)),
                  vec, bvec, bvec, bvec,
                  pl.BlockSpec((len(POOL_WINDOWS), POOL_GROUP_DIM, POOL_GROUP_DIM),
                               lambda bi, i: (0, 0, 0)),
                  vec],
        out_specs=[pl.BlockSpec((None, tm, D_MODEL), lambda bi, i: (bi, i, 0)),
                   pl.BlockSpec((None, _POOL_HALO, D_MODEL), lambda bi, i: (bi, 0, 0))],
        out_shape=[jax.ShapeDtypeStruct((b, t, D_MODEL), F32),
                   jax.ShapeDtypeStruct((b, _POOL_HALO, D_MODEL), F32)],
        scratch_shapes=[pltpu.VMEM((tm + _POOL_HALO, D_MODEL), F32)],
        compiler_params=_cparams("parallel", "arbitrary"),
        name="pool_mix",
    )(x3, xprev, state16, g.reshape(1, D_MODEL), sc, sh, gate, w_grp.astype(BF16),
      layer_scale.reshape(1, D_MODEL))
    return out, st[:, _POOL_HALO - POOL_STATE:]


def _rel_bucket(dist):
    d = jnp.maximum(dist, 0)
    max_exact = N_BUCKETS // 2
    large = max_exact + (jnp.log(jnp.maximum(d, 1).astype(F32) / max_exact)
                         / math.log(MAX_DISTANCE / max_exact) * (N_BUCKETS - max_exact)).astype(jnp.int32)
    large = jnp.minimum(large, N_BUCKETS - 1)
    return jnp.where(d < max_exact, d, large)


def _dist_bias(rel_bias):
    return rel_bias[_rel_bucket(jnp.arange(MAX_DISTANCE + 1, dtype=jnp.int32))]


def _lookup(table, idx):
    onehot = (idx[..., None] == jnp.arange(table.shape[0], dtype=jnp.int32)).astype(F32)
    return jnp.einsum("...d,dh->...h", onehot, table, precision=lax.Precision.HIGHEST)


def _bias_tile(fd, dist, valid):
    k, q = dist.shape
    bias = _lookup(fd, jnp.clip(dist, 0, MAX_DISTANCE)).reshape(k, q, N_KV, GROUP)
    bias = jnp.where(valid[:, :, None, None], bias, NEG)
    return bias.transpose(2, 0, 3, 1).reshape(N_KV, k, GROUP * q)


def _prompt_tables(rel_bias, t):
    fd = _dist_bias(rel_bias) * LOG2E
    kj = jnp.arange(KEY_TILE, dtype=jnp.int32)[:, None]
    qi = jnp.arange(CHUNK, dtype=jnp.int32)[None, :]
    dist = qi - kj
    near = jnp.stack([_bias_tile(fd, dist + KEY_TILE, dist + KEY_TILE >= 0),
                      _bias_tile(fd, dist, dist >= 0)])
    wfirst = _bias_tile(fd, dist + WINDOW, dist < 0)
    far = _bias_tile(fd, jnp.full((1, CHUNK), MAX_DISTANCE, jnp.int32), jnp.ones((1, CHUNK), bool))
    per_chunk = CHUNK // CMP_BLK
    assert CHUNK == 128 and CMP_BLK == 32 and MAX_DISTANCE == 128
    rel0, n_rel = -per_chunk, 2 * per_chunk
    rel = rel0 + jnp.arange(n_rel, dtype=jnp.int32)[:, None]
    dist = qi - CMP_BLK * rel - (CMP_BLK - 1)
    crel = _bias_tile(fd, dist, dist >= 0)
    relm = (jnp.arange(t // CMP_BLK, dtype=jnp.int32)[None, :]
            - per_chunk * jnp.arange(t // CHUNK, dtype=jnp.int32)[:, None])[:, None, :, None]
    cmp_tab = jnp.where(relm >= rel0 + n_rel, NEG, far[None])
    for r in range(n_rel):
        cmp_tab = jnp.where(relm == rel0 + r, crel[None, :, r:r + 1, :], cmp_tab)
    return near, wfirst, far, cmp_tab


V_ROWS = HEAD_DIM + 16


def _half_exp2(s, m, top, bot):
    return jnp.exp2(jnp.concatenate([s[:SEL_BLK] - (m - top), s[SEL_BLK:] - (m - bot)], axis=0))


def _normalise(acc):
    return acc[:HEAD_DIM] * (1.0 / jnp.maximum(acc[HEAD_DIM:HEAD_DIM + 1], 1e-30))


def _prompt_attn_kernel(q_ref, g_ref, kc_ref, vct_ref, cb_ref, ks_ref, vst_ref, kw_ref, vwt_ref,
                        near_ref, wfirst_ref, far_ref, o_ref,
                        imp_ref, val_ref, rank_ref, selrep_ref, m_ref, mw_ref, alpha_ref, acc_ref, accw_ref,
                        mix_ref, s_ref, smax_ref, p_ref, sc_ref, pc_ref, *, n_blk, n_sel):
    i = pl.program_id(1)

    def k_rows(ref, kv, jt):
        return ref[kv, pl.ds(pl.multiple_of(jt * KEY_TILE, KEY_TILE), KEY_TILE), :]

    def sel_rows(kv, jt):
        return selrep_ref[kv, pl.ds(2 * jt, 1), :], selrep_ref[kv, pl.ds(2 * jt + 1, 1), :]

    col_max = []
    for kv in range(N_KV):
        s = jnp.dot(kc_ref[kv], q_ref[kv], preferred_element_type=F32) + cb_ref[kv]
        sc_ref[kv] = s
        col_max.append(jnp.max(s, axis=0, keepdims=True))
    for kv in range(N_KV):
        s = sc_ref[kv]
        p = jnp.where(cb_ref[kv] > 0.5 * NEG, jnp.exp2(s - col_max[kv]), 0.0)
        p = p * (1.0 / jnp.maximum(jnp.sum(p, axis=0, keepdims=True), 1e-30))
        pc_ref[kv] = p.astype(BF16)
        imp_ref[kv] = functools.reduce(
            lambda a, b: a + b, [p[:, gi * CHUNK:(gi + 1) * CHUNK] for gi in range(GROUP)])
    for kv in range(N_KV):
        o_c = jnp.dot(vct_ref[kv], pc_ref[kv], preferred_element_type=F32)
        mix_ref[kv] = _sigmoid(g_ref[kv, 0:1, :]) * o_c

    imp = jnp.concatenate(
        [imp_ref[kv, pl.ds(0, n_blk, stride=2), :] + imp_ref[kv, pl.ds(1, n_blk, stride=2), :]
         for kv in range(N_KV)], axis=1)
    blk = lax.broadcasted_iota(jnp.int32, (n_blk, LANES), 0)
    lane = lax.broadcasted_iota(jnp.int32, (1, LANES), 1)
    assert CHUNK == 2 * SEL_BLK
    qblk = 2 * i + jnp.where((lane & (CHUNK - 1)) >= SEL_BLK, 1, 0)
    forced = (blk == 0) | (blk == qblk) | (blk == qblk - 1)
    val_ref[...] = jnp.where(forced, BIG, jnp.where(blk <= qblk, imp, -1.0))
    rank_ref[...] = jnp.zeros_like(rank_ref)
    sub = lax.broadcasted_iota(jnp.int32, (8, LANES), 0)
    group = 8
    assert n_blk % group == 0
    for grp in range(n_blk // group):
        @pl.when(group * grp <= 2 * i + 1)
        def _(grp=grp):
            for r8 in range(n_blk // 8):
                lo = r8 * 8
                piece = val_ref[lo:lo + 8, :]
                acc = rank_ref[lo:lo + 8, :]
                for j in range(grp * group, (grp + 1) * group):
                    row = val_ref[j:j + 1, :]
                    if j < lo:
                        acc = acc + jnp.where(row >= piece, 1.0, 0.0)
                    elif j >= lo + 8:
                        acc = acc + jnp.where(row > piece, 1.0, 0.0)
                    else:
                        acc = acc + jnp.where(sub > j - lo, jnp.where(row >= piece, 1.0, 0.0),
                                              jnp.where(row > piece, 1.0, 0.0))
                rank_ref[lo:lo + 8, :] = acc
    selneg = jnp.where(rank_ref[...] < n_sel, 0.0, NEG)
    for kv in range(N_KV):
        selrep_ref[kv] = jnp.concatenate([selneg[:, kv * CHUNK:(kv + 1) * CHUNK]] * GROUP, axis=1)

    prev = jnp.maximum(i - 1, 0)
    n_far = jnp.maximum(i - 1, 0)
    last = jnp.maximum(n_far - 1, 0)
    zero_row = jnp.zeros((1, LANES), F32)
    for state in (mw_ref, m_ref):
        state[...] = jnp.full(state.shape, NEG, F32)
    for state in (accw_ref, acc_ref):
        state[...] = jnp.zeros(state.shape, F32)

    def scores(k_ref, jt, slot, table=None):
        for kv in range(N_KV):
            s = jnp.dot(k_rows(k_ref, kv, jt), q_ref[kv], preferred_element_type=F32)
            if table is not None:
                s = s + table(kv)
            s_ref[slot, kv] = s
            smax_ref[slot, kv, 0:1, :] = jnp.max(s[:SEL_BLK], axis=0, keepdims=True)
            smax_ref[slot, kv, 1:2, :] = jnp.max(s[SEL_BLK:], axis=0, keepdims=True)

    def softmax(slot, rows, stat_ref):
        for kv in range(N_KV):
            top, bot = rows(kv)
            m_old = stat_ref[kv]
            m_new = jnp.maximum(m_old, jnp.maximum(smax_ref[slot, kv, 0:1, :] + top,
                                                   smax_ref[slot, kv, 1:2, :] + bot))
            stat_ref[kv] = m_new
            alpha_ref[slot, kv] = jnp.exp2(m_old - m_new)
            shift_top, shift_bot = m_new - top, m_new - bot
            for lo in range(0, LANES, 256):
                cols = slice(lo, lo + 256)
                p_ref[slot, kv, 0:SEL_BLK, cols] = jnp.exp2(
                    s_ref[slot, kv, 0:SEL_BLK, cols] - shift_top[:, cols]).astype(BF16)
                p_ref[slot, kv, SEL_BLK:, cols] = jnp.exp2(
                    s_ref[slot, kv, SEL_BLK:, cols] - shift_bot[:, cols]).astype(BF16)

    def values(vt_ref, jt, slot, out_ref):
        for kv in range(N_KV):
            out_ref[kv] = alpha_ref[slot, kv] * out_ref[kv] + jnp.dot(vt_ref[kv, jt], p_ref[slot, kv],
                                                                      preferred_element_type=F32)

    def const_rows(row):
        return lambda kv: (row, row)

    def window_far_rows(back):
        off = jnp.where(i >= back, 0.0, NEG)
        return lambda kv: (far_ref[kv] + off, far_ref[kv] + off)

    def sel_near_rows(jt, off):
        def rows(kv):
            top, bot = sel_rows(kv, jt)
            return top + off, bot + off
        return rows

    prev_off = jnp.where(i >= 1, 0.0, NEG)
    back = lambda n: jnp.maximum(i - n, 0)
    diag_table = lambda kv: near_ref[1, kv]
    prev_table = lambda kv: near_ref[0, kv]
    static_tiles = [
        (kw_ref, vwt_ref, i, diag_table, const_rows(zero_row), mw_ref, accw_ref),
        (kw_ref, vwt_ref, prev, prev_table, const_rows(zero_row + prev_off), mw_ref, accw_ref),
        (kw_ref, vwt_ref, back(2), None, window_far_rows(2), mw_ref, accw_ref),
        (kw_ref, vwt_ref, back(3), None, window_far_rows(3), mw_ref, accw_ref),
        (kw_ref, vwt_ref, back(4), lambda kv: wfirst_ref[kv],
         const_rows(zero_row + jnp.where(i >= 4, 0.0, NEG)), mw_ref, accw_ref),
        (ks_ref, vst_ref, i, diag_table, sel_near_rows(i, 0.0), m_ref, acc_ref),
        (ks_ref, vst_ref, prev, prev_table, sel_near_rows(prev, prev_off), m_ref, acc_ref),
    ]

    def far_tile(jt):
        return jnp.minimum(jt, last)

    def far_rows(jt):
        off = jnp.where(jt < n_far, 0.0, NEG)
        def rows(kv):
            top, bot = sel_rows(kv, far_tile(jt))
            far = far_ref[kv] + off
            return top + far, bot + far
        return rows

    def far_values(jt, slot):
        values(vst_ref, jnp.where(jt < 0, prev, jnp.clip(jt, 0, last)), slot, acc_ref)

    n_static = len(static_tiles)
    k_ref0, _, jt0, table0, _, _, _ = static_tiles[0]
    scores(k_ref0, jt0, 0, table0)
    for t, (_, _, _, _, rows, stat_ref, _) in enumerate(static_tiles):
        if t >= 1:
            _, vt_ref, jt, _, _, _, out_ref = static_tiles[t - 1]
            values(vt_ref, jt, (t - 1) % 2, out_ref)
        softmax(t % 2, rows, stat_ref)
        if t + 1 < n_static:
            k_ref, _, jt, table, _, _, _ = static_tiles[t + 1]
            scores(k_ref, jt, (t + 1) % 2, table)
        else:
            scores(ks_ref, far_tile(0), (t + 1) % 2)
    assert n_static % 2 == 1

    def pair_step(j2):
        jt = 2 * j2
        far_values(jt - 1, 0)
        softmax(1, far_rows(jt), m_ref)
        scores(ks_ref, far_tile(jt + 1), 0)
        far_values(jt, 1)
        softmax(0, far_rows(jt + 1), m_ref)
        scores(ks_ref, far_tile(jt + 2), 1)

    def body(pairs_per_trip):
        def run(j, carry):
            for u in range(pairs_per_trip):
                pair_step(pairs_per_trip * j + u)
            return carry
        return run

    n_pairs = (n_far + 1) // 2
    n4 = n_pairs // 4
    n2 = n_pairs // 2
    lax.fori_loop(0, n4, body(4), 0)
    lax.fori_loop(2 * n4, n2, body(2), 0)
    lax.fori_loop(2 * n2, n_pairs, body(1), 0)
    far_values(2 * n_pairs - 1, 0)

    for kv in range(N_KV):
        o_ref[kv] = (mix_ref[kv] + _sigmoid(g_ref[kv, 2:3, :]) * _normalise(accw_ref[kv])
                     + _sigmoid(g_ref[kv, 1:2, :]) * _normalise(acc_ref[kv])).astype(o_ref.dtype)


def _prompt_attention(layouts, k_cmp, v_cmp, rel_bias, b, t):
    q_t, g_t, ks, vst, kw, vwt = layouts
    nch, nt, nc, n_blk = t // CHUNK, t // KEY_TILE, t // CMP_BLK, t // SEL_BLK
    kc = k_cmp.reshape(b, nc, N_KV, HEAD_DIM).transpose(0, 2, 1, 3).astype(BF16)
    vct = v_cmp.reshape(b, nc, N_KV, HEAD_DIM).transpose(0, 2, 3, 1).astype(BF16)
    near, wfirst, far, cmp_tab = _prompt_tables(rel_bias, t)

    once = pl.Buffered(1)
    per_b = lambda *shape: pl.BlockSpec((None,) + shape, lambda bi, i: (bi,) + (0,) * len(shape),
                                        pipeline_mode=once)
    const = lambda *shape: pl.BlockSpec(shape, lambda bi, i: (0,) * len(shape), pipeline_mode=once)
    chunk = lambda *shape: pl.BlockSpec((None,) + shape, lambda bi, i: (bi * nch + i,) + (0,) * len(shape))
    k_spec = pl.BlockSpec((N_KV, t, HEAD_DIM), lambda bi, i: (0, bi, 0))
    vt_spec = pl.BlockSpec((N_KV, nt, V_ROWS, KEY_TILE), lambda bi, i: (0, bi, 0, 0))
    stat = pltpu.VMEM((N_KV, 1, LANES), F32)
    return pl.pallas_call(
        functools.partial(_prompt_attn_kernel, n_blk=n_blk, n_sel=min(N_SEL, n_blk)),
        grid=(b, nch),
        in_specs=[chunk(N_KV, HEAD_DIM, LANES),
                  chunk(N_KV, 3, LANES),
                  per_b(N_KV, nc, HEAD_DIM),
                  per_b(N_KV, HEAD_DIM, nc),
                  pl.BlockSpec((None, N_KV, nc, LANES), lambda bi, i: (i, 0, 0, 0)),
                  k_spec, vt_spec, k_spec, vt_spec,
                  const(2, N_KV, KEY_TILE, LANES),
                  const(N_KV, KEY_TILE, LANES),
                  const(N_KV, 1, LANES)],
        out_specs=chunk(N_KV, HEAD_DIM, LANES),
        out_shape=jax.ShapeDtypeStruct((b * nch, N_KV, HEAD_DIM, LANES), BF16),
        scratch_shapes=[pltpu.VMEM((N_KV, nc, CHUNK), F32),
                        pltpu.VMEM((n_blk, LANES), F32),
                        pltpu.VMEM((n_blk, LANES), F32),
                        pltpu.VMEM((N_KV, n_blk, LANES), F32),
                        stat, stat,
                        pltpu.VMEM((2, N_KV, 1, LANES), F32),
                        pltpu.VMEM((N_KV, V_ROWS, LANES), F32),
                        pltpu.VMEM((N_KV, V_ROWS, LANES), F32),
                        pltpu.VMEM((N_KV, HEAD_DIM, LANES), F32),
                        pltpu.VMEM((2, N_KV, KEY_TILE, LANES), F32),
                        pltpu.VMEM((2, N_KV, 2, LANES), F32),
                        pltpu.VMEM((2, N_KV, KEY_TILE, LANES), BF16),
                        pltpu.VMEM((N_KV, nc, LANES), F32),
                        pltpu.VMEM((N_KV, nc, LANES), BF16)],
        compiler_params=_cparams("parallel", "arbitrary"),
        name="nsa_prompt_attention",
    )(q_t, g_t, kc, vct, cmp_tab, ks, vst, kw, vwt, near, wfirst, far)


NEW_PAD = 128


def _sample_tables(rel_bias, past, t, wbuf, n_rows):
    lane = jnp.arange(SAMPLE_LANES, dtype=jnp.int32)
    g, kvh, qi = lane // (N_KV * t), (lane // t) % N_KV, lane % t
    used = lane < GROUP * N_KV * t
    head = jnp.where(used, kvh * GROUP + g, 0)
    qpos = past + qi
    fd_lane = jnp.take(_dist_bias(rel_bias), head, axis=1)
    dists = jnp.arange(MAX_DISTANCE + 1, dtype=jnp.int32)

    def tab(kpos, valid, n_near):
        dist = qpos[None, :] - kpos[:, None]
        n_far = kpos.shape[0] - n_near
        near_d = jnp.clip(dist[n_far:], 0, MAX_DISTANCE)
        near = jnp.sum(jnp.where(near_d[:, :, None] == dists, fd_lane.T[None], 0.0), axis=-1)
        far = jnp.broadcast_to(fd_lane[MAX_DISTANCE][None, :], (n_far, SAMPLE_LANES))
        bias = jnp.where(used[None, :], jnp.concatenate([far, near], axis=0), 0.0)
        return jnp.where(valid(dist) & (kpos[:, None] >= 0), bias, NEG)

    new_pos = past + jnp.where(jnp.arange(NEW_PAD) < t, jnp.arange(NEW_PAD, dtype=jnp.int32), 1 << 20)
    causal = lambda d: d >= 0
    window = lambda d: (d >= 0) & (d < WINDOW)
    nc = past // CMP_BLK
    assert wbuf >= MAX_DISTANCE and past >= MAX_DISTANCE
    t_cmp = tab(jnp.arange(nc, dtype=jnp.int32) * CMP_BLK + CMP_BLK - 1, causal, MAX_DISTANCE // CMP_BLK)
    rows = lambda a: a[:, :n_rows].T
    t_sel = rows(tab(jnp.concatenate([jnp.arange(past, dtype=jnp.int32), new_pos]), causal,
                     MAX_DISTANCE + NEW_PAD))
    t_win = rows(tab(jnp.concatenate([past - wbuf + jnp.arange(wbuf, dtype=jnp.int32), new_pos]), window,
                     MAX_DISTANCE + NEW_PAD))
    key = jnp.arange(past + NEW_PAD, dtype=jnp.int32)
    blk = jnp.where(key < past, key // SEL_BLK, past // SEL_BLK)
    expand = (blk[None, :] == jnp.arange(128, dtype=jnp.int32)[:, None]).astype(BF16)
    return t_cmp, t_sel, t_win, expand


def _sample_attn_kernel(pt_ref, *refs, n_pages, n_sel, t, wbuf):
    del pt_ref
    it = iter(refs)
    qbd_ref, qrow_ref, g_ref = next(it), next(it), next(it)
    kc_ref, vc_ref = next(it), next(it)
    ks_pages = [next(it) for _ in range(n_pages)]
    vs_pages = [next(it) for _ in range(n_pages)]
    ksn_ref, vsn_ref, kwin_ref, vwin_ref, kwn_ref, vwn_ref = (next(it) for _ in range(6))
    tcmp_ref, tsel_ref, twin_ref, expand_ref = (next(it) for _ in range(4))
    o_ref = next(it)
    imp_ref, val_ref, sel_ref, s_ref = (next(it) for _ in range(4))

    scale = HEAD_DIM ** -0.5
    nq = N_KV * t
    qs = qbd_ref[...] * scale
    qr = qrow_ref[...] * scale
    n_q = qr.shape[0]
    lane = lax.broadcasted_iota(jnp.int32, (1, SAMPLE_LANES), 1)
    blocks_per_page = PAGE_SIZE // CMP_BLK
    past = n_pages * PAGE_SIZE
    nt_dims = (((1,), (1,)), ((), ()))

    s = jnp.dot(kc_ref[...].astype(BF16), qs, preferred_element_type=F32) + tcmp_ref[...]
    m = jnp.max(s, axis=0, keepdims=True)
    p_c = jnp.exp(s - m)
    p_c = p_c / jnp.maximum(jnp.sum(p_c, axis=0, keepdims=True), 1e-30)
    o_c = lax.dot_general(p_c.astype(BF16), vc_ref[...].astype(BF16), (((0,), (0,)), ((), ())),
                          preferred_element_type=F32)[:n_q]

    imp = p_c
    for g in range(1, GROUP):
        imp = imp + pltpu.roll(p_c, SAMPLE_LANES - g * nq, 1)
    imp_ref[...] = imp
    n_pairs = n_pages * blocks_per_page // 2
    n_blk = n_pairs + 1
    n_rows = val_ref.shape[0]
    val_ref[...] = jnp.full((n_rows, SAMPLE_LANES), -2.0, F32)
    val_ref[0:n_pairs, :] = imp_ref[pl.ds(0, n_pairs, stride=2), :] + imp_ref[pl.ds(1, n_pairs, stride=2), :]
    blk = lax.broadcasted_iota(jnp.int32, (n_rows, SAMPLE_LANES), 0)
    qblk = n_blk - 1
    forced = (blk == 0) | (blk == qblk) | (blk == qblk - 1)
    val = jnp.where(forced, BIG, val_ref[...])
    val = jnp.where(blk < n_blk, val, -2.0)
    val_ref[...] = val
    ranks = [jnp.zeros((n_rows, SAMPLE_LANES), F32) for _ in range(4)]
    for j in range(n_blk):
        row = val_ref[j:j + 1, :]
        ranks[j % 4] = ranks[j % 4] + jnp.where(blk > j, jnp.where(row >= val, 1.0, 0.0),
                                                jnp.where(row > val, 1.0, 0.0))
    rank = (ranks[0] + ranks[1]) + (ranks[2] + ranks[3])
    selneg = jnp.where((rank < n_sel) & (lane < nq), 0.0, jnp.where(lane < nq, NEG, 0.0))
    selrep = selneg
    for g in range(1, GROUP):
        selrep = selrep + pltpu.roll(selneg, g * nq, 1)
    sel_ref[...] = jnp.zeros_like(sel_ref)
    sel_ref[0:n_rows, :] = jnp.where(selrep == 0.0, 1.0, 0.0)
    sel01 = sel_ref[...].T[:n_q].astype(BF16)

    def paged(page_ref):
        return page_ref[...].reshape(KV_DIM, PAGE_SIZE).astype(BF16)

    for p in range(n_pages + 1):
        cols = slice(p * KEY_TILE, (p + 1) * KEY_TILE)
        if p < n_pages:
            s = jnp.dot(qr, paged(ks_pages[p]), preferred_element_type=F32)
        else:
            s = lax.dot_general(qr, ksn_ref[...], nt_dims, preferred_element_type=F32)
        picked = jnp.dot(sel01, expand_ref[:, cols], preferred_element_type=F32)
        s_ref[:, cols] = s + tsel_ref[:, cols] + (picked - 1.0) * (-NEG)
    s = s_ref[...]
    p_s = jnp.exp(s - jnp.max(s, axis=1, keepdims=True))
    l = jnp.sum(p_s, axis=1, keepdims=True)
    p_s = p_s.astype(BF16)
    acc = jnp.dot(p_s[:, past:], vsn_ref[...], preferred_element_type=F32)
    for p in range(n_pages):
        acc = acc + lax.dot_general(p_s[:, p * KEY_TILE:(p + 1) * KEY_TILE], paged(vs_pages[p]), nt_dims,
                                    preferred_element_type=F32)
    o_s = acc / jnp.maximum(l, 1e-30)

    kwin = kwin_ref[...].reshape(KV_DIM, wbuf).astype(BF16)
    vwin = vwin_ref[...].reshape(KV_DIM, wbuf).astype(BF16)
    s_w = jnp.dot(qr, kwin, preferred_element_type=F32) + twin_ref[:, :wbuf]
    s_n = lax.dot_general(qr, kwn_ref[...], nt_dims, preferred_element_type=F32) + twin_ref[:, wbuf:]
    m = jnp.maximum(jnp.max(s_w, axis=1, keepdims=True), jnp.max(s_n, axis=1, keepdims=True))
    p_w, p_n = jnp.exp(s_w - m), jnp.exp(s_n - m)
    l = jnp.sum(p_w, axis=1, keepdims=True) + jnp.sum(p_n, axis=1, keepdims=True)
    acc = (lax.dot_general(p_w.astype(BF16), vwin, nt_dims, preferred_element_type=F32)
           + jnp.dot(p_n.astype(BF16), vwn_ref[...], preferred_element_type=F32))
    o_w = acc / jnp.maximum(l, 1e-30)

    gate = _sigmoid(g_ref[...])
    o_ref[...] = gate[:, 0:1] * o_c + gate[:, 1:2] * o_s + gate[:, 2:3] * o_w


def _sample_attention(q, gates, k_cmp, v_cmp, cache_ks, cache_vs, page_table,
                      ksl, vsl, win_k, win_v, kw, vw, rel_bias, b, t):
    n_pages = page_table.shape[1]
    past = n_pages * PAGE_SIZE
    wbuf = win_k.shape[-1]
    nq = N_KV * t
    used = GROUP * nq
    assert used % 8 == 0 and used <= SAMPLE_LANES and t <= NEW_PAD
    blocks_per_page = PAGE_SIZE // CMP_BLK
    n_blk = past // SEL_BLK + 1
    n_rows = -(-n_blk // 8) * 8

    q5 = q.reshape(b, t, N_KV, GROUP, HEAD_DIM)
    eye = jnp.eye(N_KV, dtype=q.dtype)
    qbd = jnp.einsum("btkgd,kc->bkdgct", q5, eye).reshape(b, KV_DIM, used)
    qbd = jnp.pad(qbd, ((0, 0), (0, 0), (0, SAMPLE_LANES - used))).astype(BF16)
    qrow = jnp.einsum("btkgd,kc->bgktcd", q5, eye).reshape(b, used, KV_DIM).astype(BF16)
    g_rows = gates[:, :3 * N_HEADS].reshape(b, t, N_KV, GROUP, 3).transpose(0, 3, 2, 1, 4).reshape(b, used, 3)
    g_rows = jnp.pad(g_rows, ((0, 0), (0, 0), (0, 128 - 3)))
    pad_rows = lambda a: jnp.pad(a.reshape(b, t, KV_DIM), ((0, 0), (0, NEW_PAD - t), (0, 0))).astype(BF16)
    tables = _sample_tables(rel_bias, past, t, wbuf, used)

    per_b = lambda *shape: pl.BlockSpec((None,) + shape, lambda bi, pt: (bi,) + (0,) * len(shape))
    const = lambda *shape: pl.BlockSpec(shape, lambda bi, pt: (0,) * len(shape))

    def page(p):
        return pl.BlockSpec((None, N_KV, HEAD_DIM, PAGE_SIZE), lambda bi, pt, p=p: (pt[bi, p], 0, 0, 0))

    n_cmp = n_pages * blocks_per_page
    in_specs = ([per_b(KV_DIM, SAMPLE_LANES), per_b(used, KV_DIM), per_b(used, 128),
                 per_b(n_cmp, KV_DIM), per_b(n_cmp, KV_DIM)]
                + [page(p) for p in range(n_pages)] * 2
                + [per_b(NEW_PAD, KV_DIM), per_b(NEW_PAD, KV_DIM),
                   per_b(N_KV, HEAD_DIM, wbuf), per_b(N_KV, HEAD_DIM, wbuf),
                   per_b(NEW_PAD, KV_DIM), per_b(NEW_PAD, KV_DIM)]
                + [const(*tb.shape) for tb in tables])
    out = pl.pallas_call(
        functools.partial(_sample_attn_kernel, n_pages=n_pages, n_sel=min(N_SEL, n_blk), t=t, wbuf=wbuf),
        grid_spec=pltpu.PrefetchScalarGridSpec(
            num_scalar_prefetch=1,
            grid=(b,),
            in_specs=in_specs,
            out_specs=per_b(used, KV_DIM),
            scratch_shapes=[pltpu.VMEM((n_cmp, SAMPLE_LANES), F32),
                            pltpu.VMEM((n_rows, SAMPLE_LANES), F32),
                            pltpu.VMEM((128, SAMPLE_LANES), F32),
                            pltpu.VMEM((used, past + NEW_PAD), F32)]),
        out_shape=jax.ShapeDtypeStruct((b, used, KV_DIM), F32),
        compiler_params=_cparams("arbitrary"),
        name="nsa_sample_attention",
    )(page_table, qbd, qrow, g_rows, k_cmp.reshape(b, n_cmp, KV_DIM), v_cmp.reshape(b, n_cmp, KV_DIM),
      *([cache_ks] * n_pages),
      *([cache_vs] * n_pages), pad_rows(ksl), pad_rows(vsl), win_k, win_v, pad_rows(kw), pad_rows(vw), *tables)
    o = out.reshape(b, GROUP, N_KV, t, N_KV, HEAD_DIM)
    o = jnp.einsum("bgktkd->btkgd", o)
    return o.reshape(b * t, Q_DIM)


def _compress_weights(w_c, pe_c):
    eye = jnp.eye(N_KV, dtype=w_c.dtype)
    w_big = jnp.einsum("jde,kc->jkdce", w_c, eye).reshape(CMP_BLK * KV_DIM, KV_DIM)
    pe_flat = jnp.broadcast_to(pe_c[:, None, :], (CMP_BLK, N_KV, HEAD_DIM)).reshape(1, CMP_BLK * KV_DIM)
    pe_rows = jnp.pad(pe_flat, ((0, 7), (0, 0)))
    bias = _matmul(pe_rows, w_big, name="compress_pe")[0:1]
    return w_big, bias


_BLOCK_PITCH = CMP_BLK + 4


def _compress_pages_kernel(pt_ref, *refs, n_pages):
    del pt_ref
    page_refs = refs[:n_pages]
    _compress_position_minor(lambda p: page_refs[p][...].reshape(KV_DIM, PAGE_SIZE), n_pages, *refs[n_pages:])


def _compress_tokens_kernel(x_ref, w_ref, b_ref, o_ref, rows_ref, *, n_pages):
    _compress_position_minor(lambda p: x_ref[:, p * PAGE_SIZE:(p + 1) * PAGE_SIZE], n_pages,
                             w_ref, b_ref, o_ref, rows_ref)


def _compress_position_minor(get_page, n_pages, w_ref, b_ref, o_ref, rows_ref):
    per_page = PAGE_SIZE // CMP_BLK
    for p in range(n_pages):
        xt = get_page(p).T
        for n in range(per_page):
            lo = (p * per_page + n) * _BLOCK_PITCH
            for h in range(2):
                rows_ref[h, lo:lo + CMP_BLK, :] = xt[n * CMP_BLK:(n + 1) * CMP_BLK, h * 128:(h + 1) * 128]
    n_out = n_pages * per_page
    acc = jnp.broadcast_to(b_ref[...], (n_out, KV_DIM))
    for j in range(CMP_BLK):
        for h in range(2):
            piece = rows_ref[h, pl.ds(j, n_out, stride=_BLOCK_PITCH), :].astype(BF16)
            lo = j * KV_DIM + h * 128
            acc = acc + jnp.dot(piece, w_ref[lo:lo + 128, :], preferred_element_type=F32)
    o_ref[...] = acc


def _compress_pages(cache_t, page_table, w_big, bias):
    assert KV_DIM == 256
    b, per_b = page_table.shape
    group = max(1, 32 // per_b)
    assert b % group == 0
    n_pages = group * per_b
    n_out = n_pages * (PAGE_SIZE // CMP_BLK)

    def page(p):
        return pl.BlockSpec((None, N_KV, HEAD_DIM, PAGE_SIZE),
                            lambda i, pt, p=p: (pt[i * group + p // per_b, p % per_b], 0, 0, 0))

    return pl.pallas_call(
        functools.partial(_compress_pages_kernel, n_pages=n_pages),
        grid_spec=pltpu.PrefetchScalarGridSpec(
            num_scalar_prefetch=1,
            grid=(b // group,),
            in_specs=[page(p) for p in range(n_pages)]
            + [pl.BlockSpec((CMP_BLK * KV_DIM, KV_DIM), lambda i, pt: (0, 0), pipeline_mode=pl.Buffered(1)),
               pl.BlockSpec((1, KV_DIM), lambda i, pt: (0, 0))],
            out_specs=pl.BlockSpec((n_out, KV_DIM), lambda i, pt: (i, 0)),
            scratch_shapes=[pltpu.VMEM((2, n_out * _BLOCK_PITCH, 128), F32)]),
        out_shape=jax.ShapeDtypeStruct((b * per_b * (PAGE_SIZE // CMP_BLK), KV_DIM), F32),
        compiler_params=_cparams("arbitrary"),
        name="compress_pages",
    )(page_table, *([cache_t] * n_pages), w_big.astype(BF16), bias)


def _compress_tokens(kv_t, w_big, bias):
    assert KV_DIM == 256
    b, _, t = kv_t.shape
    n_pages = math.gcd(32, t // PAGE_SIZE)
    steps = t // (n_pages * PAGE_SIZE)
    n_out = n_pages * (PAGE_SIZE // CMP_BLK)
    return pl.pallas_call(
        functools.partial(_compress_tokens_kernel, n_pages=n_pages),
        grid=(b, steps),
        in_specs=[pl.BlockSpec((None, KV_DIM, n_pages * PAGE_SIZE), lambda bi, i: (bi, 0, i)),
                  pl.BlockSpec((CMP_BLK * KV_DIM, KV_DIM), lambda bi, i: (0, 0), pipeline_mode=pl.Buffered(1)),
                  pl.BlockSpec((1, KV_DIM), lambda bi, i: (0, 0))],
        out_specs=pl.BlockSpec((n_out, KV_DIM), lambda bi, i: (bi * steps + i, 0)),
        out_shape=jax.ShapeDtypeStruct((b * t // CMP_BLK, KV_DIM), F32),
        scratch_shapes=[pltpu.VMEM((2, n_out * _BLOCK_PITCH, 128), F32)],
        compiler_params=_cparams("parallel", "parallel"),
        name="compress_tokens",
    )(kv_t, w_big.astype(BF16), bias)


def _ada(c_all, ada_w, ada_b):
    mods = []
    for i in range(ada_w.shape[0]):
        mods.append(_matmul(c_all, ada_w[i], ada_b[i].reshape(1, -1), silu_in=True, tn=2 * D_MODEL,
                            name="ada_modulate"))
    return mods


def kernel(x_prompt, x_sample, c_prompt, c_sample, cache_k_cmp, cache_v_cmp, cache_k_sel, cache_v_sel,
           page_table, state_k_win, state_v_win, state_pool, rel_bias, ada_w, ada_b, norm_g, final_g,
           nsa_w_in, nsa_w_out, cmp_wk, cmp_wv, cmp_pe_k, cmp_pe_v, pool_w, pool_scale,
           ffn_wg, ffn_wu, ffn_wd):
    bp, tp, _ = x_prompt.shape
    bs, ts, _ = x_sample.shape
    n_phys = cache_k_cmp.shape[1]
    past = page_table.shape[1] * PAGE_SIZE
    depth = ada_w.shape[0]
    assert depth == 2 and tp % KEY_TILE == 0 and ts <= 8 and past % PAGE_SIZE == 0

    n_c = bp + bs
    c_all = jnp.pad(jnp.concatenate([c_prompt, c_sample], axis=0), ((0, -n_c % 8), (0, 0)))
    mods = _ada(c_all, ada_w, ada_b)

    def mod_prompt(i):
        return [v[:bp].reshape(bp, 1, D_MODEL) for v in jnp.split(mods[i], 6, axis=-1)]

    def mod_sample(i):
        return [jnp.repeat(v[bp:n_c], ts, axis=0) for v in jnp.split(mods[i], 6, axis=-1)]

    wk_big, k_bias = _compress_weights(cmp_wk[0], cmp_pe_k[0])
    wv_big, v_bias = _compress_weights(cmp_wv[0], cmp_pe_v[0])

    mp = bp * tp
    x = x_prompt.reshape(mp, D_MODEL)
    sh1, sc1, g1, sh2, sc2, g2 = mod_prompt(0)
    kv_t, layouts = _nsa_project(x, norm_g[0, 0], sc1, sh1, nsa_w_in[0], tp, ROW_TILE, True)
    k_cmp = _compress_tokens(kv_t[0], wk_big, k_bias)
    v_cmp = _compress_tokens(kv_t[1], wv_big, v_bias)
    o_t = _prompt_attention(layouts, k_cmp, v_cmp, rel_bias, bp, tp)
    x = _matmul_residual_t(o_t, nsa_w_out[0], x, g1, tp, ROW_TILE)
    x = _ffn(x, norm_g[0, 1], sc2, sh2, g2, ffn_wg[0], ffn_wu[0], ffn_wd[0], final_g, tp, ROW_TILE, FFN_SLICES, False)
    sh1, sc1, g1, sh2, sc2, g2 = mod_prompt(1)
    x3, pool_p = _pool_mix(x.reshape(bp, tp, D_MODEL), jnp.zeros((bp, POOL_STATE, D_MODEL), F32),
                           norm_g[1, 0], sc1, sh1, g1, pool_w[0], pool_scale[0], 0, ROW_TILE)
    y_prompt = _ffn(x3.reshape(mp, D_MODEL), norm_g[1, 1], sc2, sh2, g2, ffn_wg[1], ffn_wu[1], ffn_wd[1],
                    final_g, tp, ROW_TILE, FFN_SLICES, True).reshape(bp, tp, D_MODEL)
    win = min(WINDOW, tp)
    st5 = lambda a, t0: jnp.transpose(a[:, :, t0:].reshape(bp, N_KV, HEAD_DIM, tp - t0), (0, 3, 1, 2))[None]
    prompt_states = tuple(st5(a, 0) for a in kv_t[:4]) + tuple(st5(a, tp - win) for a in kv_t[4:]) + (pool_p[None],)

    ms = bs * ts
    x = x_sample.reshape(ms, D_MODEL)
    sh1, sc1, g1, sh2, sc2, g2 = mod_sample(0)
    (kc, vc, ksl, vsl, kw, vw), (q, gates) = _nsa_project(x, norm_g[0, 0], sc1, sh1, nsa_w_in[0], ts, ROW_TILE, False)
    assert (past + ts) // CMP_BLK == past // CMP_BLK
    pos_last = lambda a: jnp.transpose(a[0], (0, 2, 3, 1))
    k_cmp_s = _compress_pages(pos_last(cache_k_cmp), page_table, wk_big, k_bias)
    v_cmp_s = _compress_pages(pos_last(cache_v_cmp), page_table, wv_big, v_bias)
    o = _sample_attention(q, gates, k_cmp_s, v_cmp_s, pos_last(cache_k_sel), pos_last(cache_v_sel),
                          page_table, ksl, vsl, pos_last(state_k_win), pos_last(state_v_win), kw, vw,
                          rel_bias, bs, ts)
    x = _matmul_residual(o, nsa_w_out[0], x, g1, ts, ROW_TILE)
    x = _ffn(x, norm_g[0, 1], sc2, sh2, g2, ffn_wg[0], ffn_wu[0], ffn_wd[0], final_g, ts, ROW_TILE, FFN_SLICES, False)
    sh1, sc1, g1, sh2, sc2, g2 = mod_sample(1)
    b3 = lambda v: v.reshape(bs, ts, D_MODEL)[:, :1]
    x3, pool_s = _pool_mix(x.reshape(bs, ts, D_MODEL), state_pool[0], norm_g[1, 0], b3(sc1), b3(sh1), b3(g1),
                           pool_w[0], pool_scale[0], past, ROW_TILE)
    y_sample = _ffn(x3.reshape(ms, D_MODEL), norm_g[1, 1], sc2, sh2, g2, ffn_wg[1], ffn_wu[1], ffn_wd[1],
                    final_g, ts, ROW_TILE, FFN_SLICES, True).reshape(bs, ts, D_MODEL)
    st5 = lambda a: a.reshape(1, bs, ts, N_KV, HEAD_DIM)
    kw_ext = jnp.concatenate([state_k_win[0], st5(kw)[0]], axis=1)[:, ts:]
    vw_ext = jnp.concatenate([state_v_win[0], st5(vw)[0]], axis=1)[:, ts:]
    sample_states = (st5(kc), st5(vc), st5(ksl), st5(vsl), kw_ext[None], vw_ext[None], pool_s[None])

    return (y_prompt, y_sample) + prompt_states + sample_states
```
